```python
import math
import jax, jax.numpy as jnp
from jax import lax
import numpy as np

D_MODEL = 1024
BATCH = 4
SEQ = 4096
DEPTH = 2
DEC_BATCH = 8
DEC_SEQ = 2048
PAST_LEN = 128

N_HEADS = 16
N_KV_HEADS = 4
HEAD_DIM = D_MODEL // N_HEADS
GROUP = N_HEADS // N_KV_HEADS
ATTN_WIDTH = N_HEADS * HEAD_DIM
KV_WIDTH = N_KV_HEADS * HEAD_DIM
AXIS_DIM = HEAD_DIM // 2
ROPE_THETA = 10000.0
GRID_W = 64
Q_BLOCK = 128
CONV_WIDTH = D_MODEL
CONV_K = 3
D_FF = (7 * D_MODEL) // 2
N_EXPERTS = 8
TOP_K = 2
EPS = 1e-6

_SIZES = (ATTN_WIDTH, KV_WIDTH, KV_WIDTH, CONV_WIDTH, CONV_WIDTH, CONV_WIDTH, D_MODEL, D_MODEL)
_OFFS = tuple(int(v) for v in np.cumsum((0,) + _SIZES))
IN_WIDTH = _OFFS[-1]

kernel_name = "hybrid_gqa_shortconv_moe_encoder"


def rmsnorm(x, g):
    xf = x.astype(jnp.float32)
    y = xf * lax.rsqrt(jnp.mean(xf * xf, axis=-1, keepdims=True) + EPS)
    return (y * g.astype(jnp.float32)).astype(x.dtype)


def axial_rope_tables(n_tokens):
    rows = n_tokens // GRID_W
    row = jnp.repeat(jnp.arange(rows, dtype=jnp.float32), GRID_W)
    col = jnp.tile(jnp.arange(GRID_W, dtype=jnp.float32), rows)
    inv = 1.0 / (ROPE_THETA ** (jnp.arange(0, AXIS_DIM, 2, dtype=jnp.float32) / AXIS_DIM))
    ang_r = row[:, None] * inv[None, :]
    ang_c = col[:, None] * inv[None, :]
    return jnp.cos(ang_r), jnp.sin(ang_r), jnp.cos(ang_c), jnp.sin(ang_c)


def _rotate(x, cos, sin):
    half = AXIS_DIM // 2
    c = cos[None, :, None, :].astype(x.dtype)
    s = sin[None, :, None, :].astype(x.dtype)
    x1, x2 = x[..., :half], x[..., half:]
    return jnp.concatenate([x1 * c - x2 * s, x1 * s + x2 * c], axis=-1)


def apply_axial_rope(x, tables):
    cr, sr, cc, sc = tables
    return jnp.concatenate([_rotate(x[..., :AXIS_DIM], cr, sr),
                            _rotate(x[..., AXIS_DIM:], cc, sc)], axis=-1)


def blocked_gqa(q, k, v):
    b, s = q.shape[0], q.shape[1]
    nb = s // Q_BLOCK
    scale = 1.0 / math.sqrt(HEAD_DIM)
    qb = q.reshape(b, nb, Q_BLOCK, N_KV_HEADS, GROUP, HEAD_DIM).transpose(1, 0, 2, 3, 4, 5)

    def one_block(qblk):
        sc = jnp.einsum('bqkgd,bskd->bkgqs', qblk, k, preferred_element_type=jnp.float32) * scale
        p = jax.nn.softmax(sc, axis=-1).astype(v.dtype)
        return jnp.einsum('bkgqs,bskd->bqkgd', p, v)

    out = lax.map(one_block, qb)
    return out.transpose(1, 0, 2, 3, 4, 5).reshape(b, s, ATTN_WIDTH)


def centred_dwconv3(u, w):
    up = jnp.pad(u, ((0, 0), (1, 1), (0, 0)))
    return w[0] * up[:, :-2] + w[1] * up[:, 1:-1] + w[2] * up[:, 2:]


def swiglu(h, w_gate, w_up, w_down):
    return (jax.nn.silu(h @ w_gate) * (h @ w_up)) @ w_down


def moe_swiglu(h, router_w, w_gate, w_up, w_down):
    b, s, d = h.shape
    t = h.reshape(-1, d)
    logits = jnp.einsum('td,de->te', t, router_w, preferred_element_type=jnp.float32)
    top_vals, top_idx = lax.top_k(logits, TOP_K)
    top_w = jax.nn.softmax(top_vals, axis=-1)
    combine = jnp.sum(jax.nn.one_hot(top_idx, N_EXPERTS, dtype=jnp.float32) * top_w[..., None], axis=1)
    out = jnp.zeros_like(t)
    for e in range(N_EXPERTS):
        out = out + combine[:, e:e + 1].astype(t.dtype) * swiglu(t, w_gate[e], w_up[e], w_down[e])
    return out.reshape(b, s, d)


def trunk(x, norm_mix, w_in, q_norm, k_norm, conv_w, w_oa, w_ob, w_o, norm_ffn,
          ffn_w_gate, ffn_w_up, ffn_w_down, router_w, moe_w_gate, moe_w_up, moe_w_down, final_norm):
    b, s, _ = x.shape
    tables = axial_rope_tables(s)
    for l in range(DEPTH):
        h = rmsnorm(x, norm_mix[l])
        proj = h @ w_in[l]
        q, k, v, cb, cc, cx, ga, gb = [proj[..., _OFFS[i]:_OFFS[i + 1]] for i in range(len(_SIZES))]
        q = apply_axial_rope(rmsnorm(q.reshape(b, s, N_HEADS, HEAD_DIM), q_norm[l]), tables)
        k = apply_axial_rope(rmsnorm(k.reshape(b, s, N_KV_HEADS, HEAD_DIM), k_norm[l]), tables)
        v = v.reshape(b, s, N_KV_HEADS, HEAD_DIM)
        ya = blocked_gqa(q, k, v) @ w_oa[l]
        yb = (cb * centred_dwconv3(cc * cx, conv_w[l])) @ w_ob[l]
        m = jax.nn.sigmoid(ga) * ya + jax.nn.sigmoid(gb) * yb
        x = x + m @ w_o[l]
        h2 = rmsnorm(x, norm_ffn[l])
        if l % 2 == 0:
            j = l // 2
            x = x + swiglu(h2, ffn_w_gate[j], ffn_w_up[j], ffn_w_down[j])
        else:
            j = l // 2
            x = x + moe_swiglu(h2, router_w[j], moe_w_gate[j], moe_w_up[j], moe_w_down[j])
    return rmsnorm(x, final_norm)


def setup_inputs(seed: int = 0) -> dict:
    key = jax.random.key(seed)
    ks = jax.random.split(key, 24)
    n_dense = (DEPTH + 1) // 2
    n_moe = DEPTH // 2
    f32 = jnp.float32

    def nrm(k, shape, fan_in):
        return jax.random.normal(k, shape, f32) * (fan_in ** -0.5)

    def gain(k, shape):
        return 1.0 + 0.02 * jax.random.normal(k, shape, f32)

    return {
        "x_prompt": jax.random.normal(ks[0], (BATCH, SEQ, D_MODEL), f32),
        "x_sample": jax.random.normal(ks[1], (DEC_BATCH, DEC_SEQ, D_MODEL), f32),
        "norm_mix": gain(ks[2], (DEPTH, D_MODEL)),
        "w_in": nrm(ks[3], (DEPTH, D_MODEL, IN_WIDTH), D_MODEL),
        "q_norm": gain(ks[4], (DEPTH, HEAD_DIM)),
        "k_norm": gain(ks[5], (DEPTH, HEAD_DIM)),
        "conv_w": nrm(ks[6], (DEPTH, CONV_K, CONV_WIDTH), CONV_K),
        "w_oa": nrm(ks[7], (DEPTH, ATTN_WIDTH, D_MODEL), ATTN_WIDTH),
        "w_ob": nrm(ks[8], (DEPTH, CONV_WIDTH, D_MODEL), CONV_WIDTH),
        "w_o": nrm(ks[9], (DEPTH, D_MODEL, D_MODEL), D_MODEL),
        "norm_ffn": gain(ks[10], (DEPTH, D_MODEL)),
        "ffn_w_gate": nrm(ks[11], (n_dense, D_MODEL, D_FF), D_MODEL),
        "ffn_w_up": nrm(ks[12], (n_dense, D_MODEL, D_FF), D_MODEL),
        "ffn_w_down": nrm(ks[13], (n_dense, D_FF, D_MODEL), D_FF),
        "router_w": nrm(ks[14], (n_moe, D_MODEL, N_EXPERTS), D_MODEL),
        "moe_w_gate": nrm(ks[15], (n_moe, N_EXPERTS, D_MODEL, D_FF), D_MODEL),
        "moe_w_up": nrm(ks[16], (n_moe, N_EXPERTS, D_MODEL, D_FF), D_MODEL),
        "moe_w_down": nrm(ks[17], (n_moe, N_EXPERTS, D_FF, D_MODEL), D_FF),
        "final_norm": gain(ks[18], (D_MODEL,)),
    }


def reference(x_prompt, x_sample, norm_mix, w_in, q_norm, k_norm, conv_w, w_oa, w_ob, w_o, norm_ffn,
              ffn_w_gate, ffn_w_up, ffn_w_down, router_w, moe_w_gate, moe_w_up, moe_w_down, final_norm):
    y_prompt = trunk(x_prompt, norm_mix, w_in, q_norm, k_norm, conv_w, w_oa, w_ob, w_o, norm_ffn,
                     ffn_w_gate, ffn_w_up, ffn_w_down, router_w, moe_w_gate, moe_w_up, moe_w_down, final_norm)
    y_sample = trunk(x_sample, norm_mix, w_in, q_norm, k_norm, conv_w, w_oa, w_ob, w_o, norm_ffn,
                     ffn_w_gate, ffn_w_up, ffn_w_down, router_w, moe_w_gate, moe_w_up, moe_w_down, final_norm)
    return (y_prompt, y_sample)
```

```python
import functools
import math

import jax
import jax.numpy as jnp
from jax import lax
from jax.experimental import pallas as pl
from jax.experimental.pallas import tpu as pltpu

F32 = jnp.float32
BF16 = jnp.bfloat16

D_MODEL = 1024
N_HEADS = 16
N_KV_HEADS = 4
HEAD_DIM = 64
GROUP = N_HEADS // N_KV_HEADS
KV_WIDTH = N_KV_HEADS * HEAD_DIM
AXIS_DIM = HEAD_DIM // 2
ROPE_THETA = 10000.0
GRID_W = 64
D_FF = 3584
N_EXPERTS = 8
EPS = 1e-6
LANES = 128
QKV_WIDTH = D_MODEL + 2 * KV_WIDTH
REST_WIDTH = 5 * D_MODEL
Q_EXP_WIDTH = N_HEADS * LANES

TM = 512
TM_FFN = 1024
TF = 512
TR = 512
VMEM_LIMIT = 56 * 1024 * 1024


def _params(sem):
    return pltpu.CompilerParams(dimension_semantics=sem, vmem_limit_bytes=VMEM_LIMIT)


def _rms(x, gain):
    return x * lax.rsqrt(jnp.mean(x * x, axis=-1, keepdims=True) + EPS) * gain


def _qkv_kernel(x_ref, g_ref, w_ref, qg_ref, kg_ref, cos_ref, sin_ref, bd_ref,
                q_ref, kt_ref, v_ref):
    tm = x_ref.shape[0]
    h = _rms(x_ref[...], g_ref[...]).astype(BF16)
    p = jnp.dot(h, w_ref[...], preferred_element_type=F32)
    cos = cos_ref[...]
    sin = sin_ref[...]
    bd = bd_ref[...]
    lane = lax.broadcasted_iota(jnp.int32, (tm, LANES), 1)
    first16 = (lane & 31) < 16
    low_half = lane < HEAD_DIM

    def norm_rope(c, gain):
        sq = c * c
        hi = sq.astype(BF16)
        lo = (sq - hi.astype(F32)).astype(BF16)
        ss = (jnp.dot(hi, bd, preferred_element_type=F32)
              + jnp.dot(lo, bd, preferred_element_type=F32))
        n = c * lax.rsqrt(ss * (1.0 / HEAD_DIM) + EPS) * gain
        partner = jnp.where(first16, pltpu.roll(n, LANES - 16, 1), pltpu.roll(n, 16, 1))
        return n * cos + partner * sin

    qg = qg_ref[...]
    scale = 1.0 / math.sqrt(HEAD_DIM)
    for c in range(N_HEADS // 2):
        r = norm_rope(p[:, c * LANES:(c + 1) * LANES], qg) * scale
        swapped = pltpu.roll(r, HEAD_DIM, 1)
        zero = jnp.zeros_like(r)
        if ((2 * c) // GROUP) % 2 == 0:
            even = jnp.where(low_half, r, zero)
            odd = jnp.where(low_half, swapped, zero)
        else:
            even = jnp.where(low_half, zero, swapped)
            odd = jnp.where(low_half, zero, r)
        q_ref[:, (2 * c) * LANES:(2 * c + 1) * LANES] = even.astype(BF16)
        q_ref[:, (2 * c + 1) * LANES:(2 * c + 2) * LANES] = odd.astype(BF16)

    kg = kg_ref[...]
    for c in range(KV_WIDTH // LANES):
        kn = norm_rope(p[:, D_MODEL + c * LANES:D_MODEL + (c + 1) * LANES], kg)
        kt_ref[c * LANES:(c + 1) * LANES, :] = kn.T.astype(BF16)
    v_ref[...] = p[:, D_MODEL + KV_WIDTH:].astype(BF16)


def _rest_kernel(x_ref, g_ref, w_ref, cb_ref, u_ref, ga_ref, gb_ref):
    h = _rms(x_ref[...], g_ref[...]).astype(BF16)
    p = jnp.dot(h, w_ref[...], preferred_element_type=F32)
    cb_ref[...] = p[:, :D_MODEL].astype(BF16)
    u_ref[...] = (p[:, D_MODEL:2 * D_MODEL] * p[:, 2 * D_MODEL:3 * D_MODEL]).astype(BF16)
    ga_ref[...] = p[:, 3 * D_MODEL:4 * D_MODEL].astype(BF16)
    gb_ref[...] = p[:, 4 * D_MODEL:].astype(BF16)


def _attn_kernel(q_ref, kt_ref, v_ref, o_ref):
    tq = q_ref.shape[0]
    lane = lax.broadcasted_iota(jnp.int32, (tq, LANES), 1)
    low_half = lane < HEAD_DIM
    for j in range(N_KV_HEADS):
        pair = j // 2
        kc = kt_ref[pair * LANES:(pair + 1) * LANES, :]
        vc = v_ref[:, pair * LANES:(pair + 1) * LANES]
        q4 = jnp.concatenate(
            [q_ref[:, h * LANES:(h + 1) * LANES] for h in range(GROUP * j, GROUP * (j + 1))],
            axis=0)
        s = jnp.dot(q4, kc, preferred_element_type=F32)
        m = jnp.max(s, axis=-1, keepdims=True)
        p = jnp.exp(s - m)
        l = jnp.sum(p, axis=-1, keepdims=True)
        o = jnp.dot(p.astype(BF16), vc, preferred_element_type=F32) / l
        for a in range(2):
            oe = o[(2 * a) * tq:(2 * a + 1) * tq]
            oo = o[(2 * a + 1) * tq:(2 * a + 2) * tq]
            if j % 2 == 0:
                chunk = jnp.where(low_half, oe, pltpu.roll(oo, HEAD_DIM, 1))
            else:
                chunk = jnp.where(low_half, pltpu.roll(oe, HEAD_DIM, 1), oo)
            c = 2 * j + a
            o_ref[:, c * LANES:(c + 1) * LANES] = chunk.astype(BF16)


def _mix_kernel(attn_ref, cb_ref, u_ref, up_ref, un_ref, ga_ref, gb_ref, x_ref,
                cw_ref, woa_ref, wob_ref, wo_ref, g_ref, rw_ref,
                xo_ref, h2_ref, lg_ref, *, n_prompt_tiles, tiles4, tiles2):
    i = pl.program_id(0)
    tm = x_ref.shape[0]
    is_prompt = i < n_prompt_tiles
    seq_start = jnp.where(is_prompt, i % tiles4 == 0, i % tiles2 == 0)
    seq_end = jnp.where(is_prompt, i % tiles4 == tiles4 - 1, i % tiles2 == tiles2 - 1)

    ya = jnp.dot(attn_ref[...], woa_ref[...], preferred_element_type=F32)

    u = u_ref[...].astype(F32)
    row = lax.broadcasted_iota(jnp.int32, u.shape, 0)
    prev_row = jnp.where(seq_start, 0.0, up_ref[7:8, :].astype(F32))
    next_row = jnp.where(seq_end, 0.0, un_ref[0:1, :].astype(F32))
    u_prev = jnp.where(row == 0, prev_row, pltpu.roll(u, 1, 0))
    u_next = jnp.where(row == tm - 1, next_row, pltpu.roll(u, tm - 1, 0))
    cw = cw_ref[...]
    conv = cw[0:1, :] * u_prev + cw[1:2, :] * u + cw[2:3, :] * u_next
    yb_in = (cb_ref[...].astype(F32) * conv).astype(BF16)
    yb = jnp.dot(yb_in, wob_ref[...], preferred_element_type=F32)

    m = (jax.nn.sigmoid(ga_ref[...].astype(F32)) * ya
         + jax.nn.sigmoid(gb_ref[...].astype(F32)) * yb)
    xn = x_ref[...] + jnp.dot(m.astype(BF16), wo_ref[...], preferred_element_type=F32)
    xo_ref[...] = xn
    h2 = _rms(xn, g_ref[...])
    h2_ref[...] = h2.astype(BF16)
    hi = h2.astype(BF16)
    lo = (h2 - hi.astype(F32)).astype(BF16)
    rw = rw_ref[...]
    rhi = rw.astype(BF16)
    rlo = (rw - rhi.astype(F32)).astype(BF16)
    lg_ref[...] = (jnp.dot(hi, rhi, preferred_element_type=F32)
                   + jnp.dot(hi, rlo, preferred_element_type=F32)
                   + jnp.dot(lo, rhi, preferred_element_type=F32))


def _ffn_kernel(h_ref, x_ref, wg_ref, wu_ref, wd_ref, o_ref, acc_ref):
    j = pl.program_id(1)

    @pl.when(j == 0)
    def _():
        acc_ref[...] = jnp.zeros_like(acc_ref)

    h = h_ref[...]
    g = jnp.dot(h, wg_ref[...], preferred_element_type=F32)
    u = jnp.dot(h, wu_ref[...], preferred_element_type=F32)
    a = (g * jax.nn.sigmoid(g) * u).astype(BF16)
    acc_ref[...] += jnp.dot(a, wd_ref[...], preferred_element_type=F32)

    @pl.when(j == pl.num_programs(1) - 1)
    def _():
        o_ref[...] = x_ref[...] + acc_ref[...]


def _router_kernel(lg_ref, comb_ref):
    lg = lg_ref[...]
    lane = lax.broadcasted_iota(jnp.int32, lg.shape, 1)
    neg = jnp.float32(-jnp.inf)
    l1 = jnp.where(lane < N_EXPERTS, lg, neg)
    m1 = jnp.max(l1, axis=-1, keepdims=True)
    i1 = jnp.min(jnp.where(l1 == m1, lane, LANES), axis=-1, keepdims=True)
    l2 = jnp.where(lane == i1, neg, l1)
    m2 = jnp.max(l2, axis=-1, keepdims=True)
    i2 = jnp.min(jnp.where(l2 == m2, lane, LANES), axis=-1, keepdims=True)
    e = jnp.exp(m2 - m1)
    w1 = 1.0 / (1.0 + e)
    w2 = e / (1.0 + e)
    comb_ref[...] = jnp.where(lane == i1, w1, 0.0) + jnp.where(lane == i2, w2, 0.0)


def _moe_kernel(h_ref, x_ref, comb_ref, wg_ref, wu_ref, wd_ref, fg_ref, o_ref, acc_ref):
    e = pl.program_id(1)
    j = pl.program_id(2)

    @pl.when((e == 0) & (j == 0))
    def _():
        acc_ref[...] = jnp.zeros_like(acc_ref)

    comb = comb_ref[...]
    lane = lax.broadcasted_iota(jnp.int32, comb.shape, 1)
    ce = jnp.sum(jnp.where(lane == e, comb, 0.0), axis=-1, keepdims=True)
    h = h_ref[...]
    g = jnp.dot(h, wg_ref[0], preferred_element_type=F32)
    u = jnp.dot(h, wu_ref[0], preferred_element_type=F32)
    a = (g * jax.nn.sigmoid(g) * u * ce).astype(BF16)
    acc_ref[...] += jnp.dot(a, wd_ref[0], preferred_element_type=F32)

    @pl.when((e == pl.num_programs(1) - 1) & (j == pl.num_programs(2) - 1))
    def _():
        o_ref[...] = _rms(x_ref[...] + acc_ref[...], fg_ref[...])


def _rope_tables(max_seq):
    t = jnp.arange(max_seq, dtype=jnp.int32)
    row = (t // GRID_W).astype(F32)
    col = (t % GRID_W).astype(F32)
    inv = 1.0 / (ROPE_THETA ** (jnp.arange(0, AXIS_DIM, 2, dtype=F32) / AXIS_DIM))
    ar = row[:, None] * inv[None, :]
    ac = col[:, None] * inv[None, :]
    cos64 = jnp.concatenate([jnp.cos(ar), jnp.cos(ar), jnp.cos(ac), jnp.cos(ac)], axis=-1)
    sin64 = jnp.concatenate([-jnp.sin(ar), jnp.sin(ar), -jnp.sin(ac), jnp.sin(ac)], axis=-1)
    return jnp.tile(cos64, (1, 2)), jnp.tile(sin64, (1, 2))


def _qkv_proj(x, gain, w, qg, kg, cos, sin, bd, pos_map):
    t = x.shape[0]
    row = lambda i: (i, 0)
    const = lambda i: (0, 0)
    return pl.pallas_call(
        _qkv_kernel,
        grid=(t // TM,),
        in_specs=[pl.BlockSpec((TM, D_MODEL), row),
                  pl.BlockSpec((1, D_MODEL), const),
                  pl.BlockSpec((D_MODEL, QKV_WIDTH), const),
                  pl.BlockSpec((1, LANES), const),
                  pl.BlockSpec((1, LANES), const),
                  pl.BlockSpec((TM, LANES), pos_map),
                  pl.BlockSpec((TM, LANES), pos_map),
                  pl.BlockSpec((LANES, LANES), const)],
        out_specs=[pl.BlockSpec((TM, Q_EXP_WIDTH), row),
                   pl.BlockSpec((KV_WIDTH, TM), lambda i: (0, i)),
                   pl.BlockSpec((TM, KV_WIDTH), row)],
        out_shape=[jax.ShapeDtypeStruct((t, Q_EXP_WIDTH), BF16),
                   jax.ShapeDtypeStruct((KV_WIDTH, t), BF16),
                   jax.ShapeDtypeStruct((t, KV_WIDTH), BF16)],
        compiler_params=_params(("parallel",)),
        name="qkv_proj",
    )(x, gain, w, qg, kg, cos, sin, bd)


def _rest_proj(x, gain, w):
    t = x.shape[0]
    row = lambda i: (i, 0)
    const = lambda i: (0, 0)
    out = jax.ShapeDtypeStruct((t, D_MODEL), BF16)
    return pl.pallas_call(
        _rest_kernel,
        grid=(t // TM,),
        in_specs=[pl.BlockSpec((TM, D_MODEL), row),
                  pl.BlockSpec((1, D_MODEL), const),
                  pl.BlockSpec((D_MODEL, REST_WIDTH), const)],
        out_specs=[pl.BlockSpec((TM, D_MODEL), row)] * 4,
        out_shape=[out] * 4,
        compiler_params=_params(("parallel",)),
        name="rest_proj",
    )(x, gain, w)


def _attention(q, kt, v, *, row0, n_seq, seq, tq):
    nq = seq // tq
    q0 = row0 // tq
    s0 = row0 // seq
    return pl.pallas_call(
        _attn_kernel,
        grid=(n_seq, nq),
        in_specs=[pl.BlockSpec((tq, Q_EXP_WIDTH), lambda b, i: (q0 + b * nq + i, 0)),
                  pl.BlockSpec((KV_WIDTH, seq), lambda b, i: (0, s0 + b)),
                  pl.BlockSpec((seq, KV_WIDTH), lambda b, i: (s0 + b, 0))],
        out_specs=pl.BlockSpec((tq, D_MODEL), lambda b, i: (b * nq + i, 0)),
        out_shape=jax.ShapeDtypeStruct((n_seq * seq, D_MODEL), BF16),
        compiler_params=_params(("parallel", "parallel")),
        name=f"attention_s{seq}",
    )(q, kt, v)


def _mix(attn, cb, u, ga, gb, x, cw, woa, wob, wo, gain, rw, *, n_prompt, seq_p, seq_s):
    t = x.shape[0]
    row = lambda i: (i, 0)
    const = lambda i: (0, 0)
    sub = TM // 8
    last = t // 8 - 1
    act = pl.BlockSpec((TM, D_MODEL), row)
    wspec = pl.BlockSpec((D_MODEL, D_MODEL), const)
    kern = functools.partial(_mix_kernel, n_prompt_tiles=n_prompt // TM,
                             tiles4=seq_p // TM, tiles2=seq_s // TM)
    return pl.pallas_call(
        kern,
        grid=(t // TM,),
        in_specs=[act, act, act,
                  pl.BlockSpec((8, D_MODEL), lambda i: (jnp.maximum(i * sub - 1, 0), 0)),
                  pl.BlockSpec((8, D_MODEL), lambda i: (jnp.minimum((i + 1) * sub, last), 0)),
                  act, act, act,
                  pl.BlockSpec((3, D_MODEL), const),
                  wspec, wspec, wspec,
                  pl.BlockSpec((1, D_MODEL), const),
                  pl.BlockSpec((D_MODEL, LANES), const)],
        out_specs=[act, act, pl.BlockSpec((TM, LANES), row)],
        out_shape=[jax.ShapeDtypeStruct((t, D_MODEL), F32),
                   jax.ShapeDtypeStruct((t, D_MODEL), BF16),
                   jax.ShapeDtypeStruct((t, LANES), F32)],
        compiler_params=_params(("parallel",)),
        name="mix_proj",
    )(attn, cb, u, u, u, ga, gb, x, cw, woa, wob, wo, gain, rw)


def _ffn(h, x, wg, wu, wd):
    t = x.shape[0]
    row = lambda i, j: (i, 0)
    return pl.pallas_call(
        _ffn_kernel,
        grid=(t // TM_FFN, D_FF // TF),
        in_specs=[pl.BlockSpec((TM_FFN, D_MODEL), row),
                  pl.BlockSpec((TM_FFN, D_MODEL), row),
                  pl.BlockSpec((D_MODEL, TF), lambda i, j: (0, j)),
                  pl.BlockSpec((D_MODEL, TF), lambda i, j: (0, j)),
                  pl.BlockSpec((TF, D_MODEL), lambda i, j: (j, 0))],
        out_specs=pl.BlockSpec((TM_FFN, D_MODEL), row),
        out_shape=jax.ShapeDtypeStruct((t, D_MODEL), F32),
        scratch_shapes=[pltpu.VMEM((TM_FFN, D_MODEL), F32)],
        compiler_params=_params(("parallel", "arbitrary")),
        name="ffn_dense",
    )(h, x, wg, wu, wd)


def _router(logits):
    t = logits.shape[0]
    row = lambda i: (i, 0)
    return pl.pallas_call(
        _router_kernel,
        grid=(t // TR,),
        in_specs=[pl.BlockSpec((TR, LANES), row)],
        out_specs=pl.BlockSpec((TR, LANES), row),
        out_shape=jax.ShapeDtypeStruct((t, LANES), F32),
        compiler_params=_params(("parallel",)),
        name="router",
    )(logits)


def _moe(h, x, comb, wg, wu, wd, fgain):
    t = x.shape[0]
    row = lambda i, e, j: (i, 0)
    return pl.pallas_call(
        _moe_kernel,
        grid=(t // TM_FFN, N_EXPERTS, D_FF // TF),
        in_specs=[pl.BlockSpec((TM_FFN, D_MODEL), row),
                  pl.BlockSpec((TM_FFN, D_MODEL), row),
                  pl.BlockSpec((TM_FFN, LANES), row),
                  pl.BlockSpec((1, D_MODEL, TF), lambda i, e, j: (e, 0, j)),
                  pl.BlockSpec((1, D_MODEL, TF), lambda i, e, j: (e, 0, j)),
                  pl.BlockSpec((1, TF, D_MODEL), lambda i, e, j: (e, j, 0)),
                  pl.BlockSpec((1, D_MODEL), lambda i, e, j: (0, 0))],
        out_specs=pl.BlockSpec((TM_FFN, D_MODEL), row),
        out_shape=jax.ShapeDtypeStruct((t, D_MODEL), F32),
        scratch_shapes=[pltpu.VMEM((TM_FFN, D_MODEL), F32)],
        compiler_params=_params(("parallel", "arbitrary", "arbitrary")),
        name="moe_dense",
    )(h, x, comb, wg, wu, wd, fgain)


def kernel(x_prompt, x_sample, norm_mix, w_in, q_norm, k_norm, conv_w, w_oa, w_ob, w_o, norm_ffn,
           ffn_w_gate, ffn_w_up, ffn_w_down, router_w, moe_w_gate, moe_w_up, moe_w_down, final_norm):
    bp, sp, _ = x_prompt.shape
    bs, ss, _ = x_sample.shape
    n_prompt = bp * sp
    n_sample = bs * ss
    depth = norm_mix.shape[0]
    assert sp % TM == 0 and ss % TM == 0 and n_prompt % TM_FFN == 0 and n_sample % TM_FFN == 0
    assert depth == 2 and ffn_w_gate.shape[0] == 1 and moe_w_gate.shape[0] == 1

    x = jnp.concatenate([x_prompt.reshape(n_prompt, D_MODEL),
                         x_sample.reshape(n_sample, D_MODEL)], axis=0)

    cos, sin = _rope_tables(max(sp, ss))
    n_prompt_tiles, tiles4, tiles2 = n_prompt // TM, sp // TM, ss // TM
    pos_map = lambda i: (jnp.where(i < n_prompt_tiles, i % tiles4, i % tiles2), 0)
    idx = jnp.arange(LANES)
    bd = (idx[:, None] // HEAD_DIM == idx[None, :] // HEAD_DIM).astype(BF16)

    for l in range(depth):
        gain = norm_mix[l][None, :]
        w_l = w_in[l]
        w_qkv = w_l[:, :QKV_WIDTH].astype(BF16)
        w_rest = w_l[:, QKV_WIDTH:].astype(BF16)
        qg = jnp.tile(q_norm[l], 2)[None, :]
        kg = jnp.tile(k_norm[l], 2)[None, :]
        q, kt, v = _qkv_proj(x, gain, w_qkv, qg, kg, cos, sin, bd, pos_map)
        cb, u, ga, gb = _rest_proj(x, gain, w_rest)
        attn_p = _attention(q, kt, v, row0=0, n_seq=bp, seq=sp, tq=128)
        attn_s = _attention(q, kt, v, row0=n_prompt, n_seq=bs, seq=ss, tq=256)
        attn = jnp.concatenate([attn_p, attn_s], axis=0)
        rw = jnp.zeros((D_MODEL, LANES), F32)
        if l % 2 == 1:
            rw = rw.at[:, :N_EXPERTS].set(router_w[l // 2])
        x, h2, logits = _mix(attn, cb, u, ga, gb, x, conv_w[l],
                             w_oa[l].astype(BF16), w_ob[l].astype(BF16), w_o[l].astype(BF16),
                             norm_ffn[l][None, :], rw,
                             n_prompt=n_prompt, seq_p=sp, seq_s=ss)
        j = l // 2
        if l % 2 == 0:
            x = _ffn(h2, x, ffn_w_gate[j].astype(BF16), ffn_w_up[j].astype(BF16),
                     ffn_w_down[j].astype(BF16))
        else:
            comb = _router(logits)
            x = _moe(h2, x, comb, moe_w_gate[j].astype(BF16), moe_w_up[j].astype(BF16),
                     moe_w_down[j].astype(BF16), final_norm[None, :])

    y_prompt = x[:n_prompt].reshape(bp, sp, D_MODEL)
    y_sample = x[n_prompt:].reshape(bs, ss, D_MODEL)
    return (y_prompt, y_sample)
```

```python
import functools
import math

import jax
import jax.numpy as jnp
from jax import lax
from jax.experimental import pallas as pl
from jax.experimental.pallas import tpu as pltpu

F32 = jnp.float32
BF16 = jnp.bfloat16

D_MODEL = 1024
N_HEADS = 16
N_KV_HEADS = 4
HEAD_DIM = 64
GROUP = N_HEADS // N_KV_HEADS
KV_WIDTH = N_KV_HEADS * HEAD_DIM
AXIS_DIM = HEAD_DIM // 2
ROPE_THETA = 10000.0
GRID_W = 64
D_FF = 3584
N_EXPERTS = 8
EPS = 1e-6
LANES = 128
QKV_WIDTH = D_MODEL + 2 * KV_WIDTH
REST_WIDTH = 5 * D_MODEL
Q_EXP_WIDTH = N_HEADS * LANES

TM = 512
TM_FFN = 1024
TF = 512
TR = 512
TM_MOE = 512
VMEM_LIMIT = 56 * 1024 * 1024


def _params(sem):
    return pltpu.CompilerParams(dimension_semantics=sem, vmem_limit_bytes=VMEM_LIMIT)


def _rms(x, gain):
    return x * lax.rsqrt(jnp.mean(x * x, axis=-1, keepdims=True) + EPS) * gain


def _qkv_kernel(x_ref, g_ref, w_ref, qg_ref, kg_ref, cos_ref, sin_ref, bd_ref,
                q_ref, kt_ref, v_ref):
    tm = x_ref.shape[0]
    h = _rms(x_ref[...], g_ref[...]).astype(BF16)
    p = jnp.dot(h, w_ref[...], preferred_element_type=F32)
    cos = cos_ref[...]
    sin = sin_ref[...]
    bd = bd_ref[...]
    lane = lax.broadcasted_iota(jnp.int32, (tm, LANES), 1)
    first16 = (lane & 31) < 16
    low_half = lane < HEAD_DIM

    def norm_rope(c, gain):
        sq = c * c
        hi = sq.astype(BF16)
        lo = (sq - hi.astype(F32)).astype(BF16)
        ss = (jnp.dot(hi, bd, preferred_element_type=F32)
              + jnp.dot(lo, bd, preferred_element_type=F32))
        n = c * lax.rsqrt(ss * (1.0 / HEAD_DIM) + EPS) * gain
        partner = jnp.where(first16, pltpu.roll(n, LANES - 16, 1), pltpu.roll(n, 16, 1))
        return n * cos + partner * sin

    qg = qg_ref[...]
    scale = 1.0 / math.sqrt(HEAD_DIM)
    for c in range(N_HEADS // 2):
        r = norm_rope(p[:, c * LANES:(c + 1) * LANES], qg) * scale
        swapped = pltpu.roll(r, HEAD_DIM, 1)
        zero = jnp.zeros_like(r)
        if ((2 * c) // GROUP) % 2 == 0:
            even = jnp.where(low_half, r, zero)
            odd = jnp.where(low_half, swapped, zero)
        else:
            even = jnp.where(low_half, zero, swapped)
            odd = jnp.where(low_half, zero, r)
        q_ref[:, (2 * c) * LANES:(2 * c + 1) * LANES] = even.astype(BF16)
        q_ref[:, (2 * c + 1) * LANES:(2 * c + 2) * LANES] = odd.astype(BF16)

    kg = kg_ref[...]
    for c in range(KV_WIDTH // LANES):
        kn = norm_rope(p[:, D_MODEL + c * LANES:D_MODEL + (c + 1) * LANES], kg)
        kt_ref[c * LANES:(c + 1) * LANES, :] = kn.T.astype(BF16)
    v_ref[...] = p[:, D_MODEL + KV_WIDTH:].astype(BF16)


def _rest_kernel(x_ref, g_ref, w_ref, cb_ref, u_ref, ga_ref, gb_ref):
    h = _rms(x_ref[...], g_ref[...]).astype(BF16)
    p = jnp.dot(h, w_ref[...], preferred_element_type=F32)
    cb_ref[...] = p[:, :D_MODEL].astype(BF16)
    u_ref[...] = (p[:, D_MODEL:2 * D_MODEL] * p[:, 2 * D_MODEL:3 * D_MODEL]).astype(BF16)
    ga_ref[...] = p[:, 3 * D_MODEL:4 * D_MODEL].astype(BF16)
    gb_ref[...] = p[:, 4 * D_MODEL:].astype(BF16)


def _attn_kernel(q_ref, kt_ref, v_ref, o_ref):
    tq = q_ref.shape[0]
    lane = lax.broadcasted_iota(jnp.int32, (tq, LANES), 1)
    low_half = lane < HEAD_DIM
    for j in range(N_KV_HEADS):
        pair = j // 2
        kc = kt_ref[pair * LANES:(pair + 1) * LANES, :]
        vc = v_ref[:, pair * LANES:(pair + 1) * LANES]
        q4 = jnp.concatenate(
            [q_ref[:, h * LANES:(h + 1) * LANES] for h in range(GROUP * j, GROUP * (j + 1))],
            axis=0)
        s = jnp.dot(q4, kc, preferred_element_type=F32)
        m = jnp.max(s, axis=-1, keepdims=True)
        p = jnp.exp(s - m)
        l = jnp.sum(p, axis=-1, keepdims=True)
        o = jnp.dot(p.astype(BF16), vc, preferred_element_type=F32) / l
        for a in range(2):
            oe = o[(2 * a) * tq:(2 * a + 1) * tq]
            oo = o[(2 * a + 1) * tq:(2 * a + 2) * tq]
            if j % 2 == 0:
                chunk = jnp.where(low_half, oe, pltpu.roll(oo, HEAD_DIM, 1))
            else:
                chunk = jnp.where(low_half, pltpu.roll(oe, HEAD_DIM, 1), oo)
            c = 2 * j + a
            o_ref[:, c * LANES:(c + 1) * LANES] = chunk.astype(BF16)


def _mix_kernel(attn_ref, cb_ref, u_ref, up_ref, un_ref, ga_ref, gb_ref, x_ref,
                cw_ref, woa_ref, wob_ref, wo_ref, g_ref, *rest,
                n_prompt_tiles, tiles4, tiles2, with_router):
    if with_router:
        rw_ref, xo_ref, h2_ref, lg_ref = rest
    else:
        xo_ref, h2_ref = rest
    i = pl.program_id(0)
    tm = x_ref.shape[0]
    is_prompt = i < n_prompt_tiles
    seq_start = jnp.where(is_prompt, i % tiles4 == 0, i % tiles2 == 0)
    seq_end = jnp.where(is_prompt, i % tiles4 == tiles4 - 1, i % tiles2 == tiles2 - 1)

    ya = jnp.dot(attn_ref[...], woa_ref[...], preferred_element_type=F32)

    u = u_ref[...].astype(F32)
    row = lax.broadcasted_iota(jnp.int32, u.shape, 0)
    prev_row = jnp.where(seq_start, 0.0, up_ref[7:8, :].astype(F32))
    next_row = jnp.where(seq_end, 0.0, un_ref[0:1, :].astype(F32))
    u_prev = jnp.where(row == 0, prev_row, pltpu.roll(u, 1, 0))
    u_next = jnp.where(row == tm - 1, next_row, pltpu.roll(u, tm - 1, 0))
    cw = cw_ref[...]
    conv = cw[0:1, :] * u_prev + cw[1:2, :] * u + cw[2:3, :] * u_next
    yb_in = (cb_ref[...].astype(F32) * conv).astype(BF16)
    yb = jnp.dot(yb_in, wob_ref[...], preferred_element_type=F32)

    m = (jax.nn.sigmoid(ga_ref[...].astype(F32)) * ya
         + jax.nn.sigmoid(gb_ref[...].astype(F32)) * yb)
    xn = x_ref[...] + jnp.dot(m.astype(BF16), wo_ref[...], preferred_element_type=F32)
    xo_ref[...] = xn
    h2 = _rms(xn, g_ref[...])
    h2_ref[...] = h2.astype(h2_ref.dtype)
    if not with_router:
        return
    hi = h2.astype(BF16)
    lo = (h2 - hi.astype(F32)).astype(BF16)
    rw = rw_ref[...]
    rhi = rw.astype(BF16)
    rlo = (rw - rhi.astype(F32)).astype(BF16)
    lg_ref[...] = (jnp.dot(hi, rhi, preferred_element_type=F32)
                   + jnp.dot(hi, rlo, preferred_element_type=F32)
                   + jnp.dot(lo, rhi, preferred_element_type=F32))


def _ffn_kernel(h_ref, x_ref, wg_ref, wu_ref, wd_ref, o_ref, acc_ref):
    j = pl.program_id(1)

    @pl.when(j == 0)
    def _():
        acc_ref[...] = jnp.zeros_like(acc_ref)

    h = h_ref[...]
    g = jnp.dot(h, wg_ref[...], preferred_element_type=F32)
    u = jnp.dot(h, wu_ref[...], preferred_element_type=F32)
    a = (g * jax.nn.sigmoid(g) * u).astype(BF16)
    acc_ref[...] += jnp.dot(a, wd_ref[...], preferred_element_type=F32)

    @pl.when(j == pl.num_programs(1) - 1)
    def _():
        o_ref[...] = x_ref[...] + acc_ref[...]


ROUTE_E0, ROUTE_E1, ROUTE_R0, ROUTE_R1, ROUTE_W0, ROUTE_W1 = range(6)


def _lane_pick(x, lane, k):
    return jnp.sum(jnp.where(lane == k, x, 0.0), axis=-1, keepdims=True)


def _router_kernel(lg_ref, tri_ref, route_ref, count_ref, base_ref):
    i = pl.program_id(0)

    @pl.when(i == 0)
    def _():
        base_ref[...] = jnp.zeros_like(base_ref)

    lg = lg_ref[...]
    lane = lax.broadcasted_iota(jnp.int32, lg.shape, 1)
    neg = jnp.float32(-jnp.inf)
    l1 = jnp.where(lane < N_EXPERTS, lg, neg)
    m1 = jnp.max(l1, axis=-1, keepdims=True)
    i1 = jnp.min(jnp.where(l1 == m1, lane, LANES), axis=-1, keepdims=True)
    l2 = jnp.where(lane == i1, neg, l1)
    m2 = jnp.max(l2, axis=-1, keepdims=True)
    i2 = jnp.min(jnp.where(l2 == m2, lane, LANES), axis=-1, keepdims=True)
    e = jnp.exp(m2 - m1)
    w1 = 1.0 / (1.0 + e)
    w2 = e / (1.0 + e)

    hot1 = lane == i1
    hot2 = lane == i2
    onehot = jnp.where(hot1 | hot2, 1.0, 0.0)
    base = base_ref[0:1, :]
    prefix = jnp.dot(tri_ref[...], onehot.astype(BF16), preferred_element_type=F32) + base
    r1 = jnp.sum(jnp.where(hot1, prefix, 0.0), axis=-1, keepdims=True)
    r2 = jnp.sum(jnp.where(hot2, prefix, 0.0), axis=-1, keepdims=True)
    total = base + jnp.sum(onehot, axis=0, keepdims=True)
    base_ref[...] = jnp.broadcast_to(total, base_ref.shape)
    count_ref[...] = jnp.broadcast_to(total, count_ref.shape)

    rec = jnp.where(lane == ROUTE_E0, i1.astype(F32), 0.0)
    rec = jnp.where(lane == ROUTE_E1, i2.astype(F32), rec)
    rec = jnp.where(lane == ROUTE_R0, r1, rec)
    rec = jnp.where(lane == ROUTE_R1, r2, rec)
    rec = jnp.where(lane == ROUTE_W0, w1, rec)
    rec = jnp.where(lane == ROUTE_W1, w2, rec)
    route_ref[...] = rec


def _row_copy(src, src_row, dst, dst_row, sem):
    return pltpu.make_async_copy(src.at[pl.ds(src_row, 1)], dst.at[pl.ds(dst_row, 1)], sem)


def _dispatch_kernel(pos_ref, h_ref, xs_in_ref, xs_ref, sem):
    del xs_in_ref
    tm = h_ref.shape[0]

    def issue(r, carry):
        _row_copy(h_ref, r, xs_ref, pos_ref[0, 0, r], sem).start()
        _row_copy(h_ref, r, xs_ref, pos_ref[0, 0, tm + r], sem).start()
        return carry

    def drain(r, carry):
        _row_copy(h_ref, 0, xs_ref, 0, sem).wait()
        _row_copy(h_ref, 0, xs_ref, 0, sem).wait()
        return carry

    lax.fori_loop(0, tm, issue, 0)
    lax.fori_loop(0, tm, drain, 0)


def _moe_ffn_kernel(te_ref, tv_ref, x_ref, wg_ref, wu_ref, wd_ref, y_ref, acc_ref):
    del te_ref
    g_idx = pl.program_id(0)
    j = pl.program_id(1)
    valid = tv_ref[g_idx] > 0

    @pl.when(j == 0)
    def _():
        acc_ref[...] = jnp.zeros_like(acc_ref)

    @pl.when(valid)
    def _():
        h = x_ref[...].astype(BF16)
        g = jnp.dot(h, wg_ref[0], preferred_element_type=F32)
        u = jnp.dot(h, wu_ref[0], preferred_element_type=F32)
        a = (g * jax.nn.sigmoid(g) * u).astype(BF16)
        acc_ref[...] += jnp.dot(a, wd_ref[0], preferred_element_type=F32)

    @pl.when(j == pl.num_programs(1) - 1)
    def _():
        y_ref[...] = acc_ref[...]


def _combine_kernel(pos_ref, x_ref, route_ref, fg_ref, y_hbm, o_ref, ybuf, sem):
    tc = x_ref.shape[0]

    def issue(r, carry):
        _row_copy(y_hbm, pos_ref[0, 0, r], ybuf.at[0], r, sem).start()
        _row_copy(y_hbm, pos_ref[0, 0, tc + r], ybuf.at[1], r, sem).start()
        return carry

    def drain(r, carry):
        _row_copy(y_hbm, 0, ybuf.at[0], 0, sem).wait()
        _row_copy(y_hbm, 0, ybuf.at[1], 0, sem).wait()
        return carry

    lax.fori_loop(0, tc, issue, 0)
    lax.fori_loop(0, tc, drain, 0)
    route = route_ref[...]
    lane = lax.broadcasted_iota(jnp.int32, route.shape, 1)
    w0 = _lane_pick(route, lane, ROUTE_W0)
    w1 = _lane_pick(route, lane, ROUTE_W1)
    out = x_ref[...] + (w0 * ybuf[0] + w1 * ybuf[1])
    o_ref[...] = _rms(out, fg_ref[...])


def _rope_tables(max_seq):
    t = jnp.arange(max_seq, dtype=jnp.int32)
    row = (t // GRID_W).astype(F32)
    col = (t % GRID_W).astype(F32)
    inv = 1.0 / (ROPE_THETA ** (jnp.arange(0, AXIS_DIM, 2, dtype=F32) / AXIS_DIM))
    ar = row[:, None] * inv[None, :]
    ac = col[:, None] * inv[None, :]
    cos64 = jnp.concatenate([jnp.cos(ar), jnp.cos(ar), jnp.cos(ac), jnp.cos(ac)], axis=-1)
    sin64 = jnp.concatenate([-jnp.sin(ar), jnp.sin(ar), -jnp.sin(ac), jnp.sin(ac)], axis=-1)
    return jnp.tile(cos64, (1, 2)), jnp.tile(sin64, (1, 2))


def _qkv_proj(x, gain, w, qg, kg, cos, sin, bd, pos_map):
    t = x.shape[0]
    row = lambda i: (i, 0)
    const = lambda i: (0, 0)
    return pl.pallas_call(
        _qkv_kernel,
        grid=(t // TM,),
        in_specs=[pl.BlockSpec((TM, D_MODEL), row),
                  pl.BlockSpec((1, D_MODEL), const),
                  pl.BlockSpec((D_MODEL, QKV_WIDTH), const),
                  pl.BlockSpec((1, LANES), const),
                  pl.BlockSpec((1, LANES), const),
                  pl.BlockSpec((TM, LANES), pos_map),
                  pl.BlockSpec((TM, LANES), pos_map),
                  pl.BlockSpec((LANES, LANES), const)],
        out_specs=[pl.BlockSpec((TM, Q_EXP_WIDTH), row),
                   pl.BlockSpec((KV_WIDTH, TM), lambda i: (0, i)),
                   pl.BlockSpec((TM, KV_WIDTH), row)],
        out_shape=[jax.ShapeDtypeStruct((t, Q_EXP_WIDTH), BF16),
                   jax.ShapeDtypeStruct((KV_WIDTH, t), BF16),
                   jax.ShapeDtypeStruct((t, KV_WIDTH), BF16)],
        compiler_params=_params(("parallel",)),
        name="qkv_proj",
    )(x, gain, w, qg, kg, cos, sin, bd)


def _rest_proj(x, gain, w):
    t = x.shape[0]
    row = lambda i: (i, 0)
    const = lambda i: (0, 0)
    out = jax.ShapeDtypeStruct((t, D_MODEL), BF16)
    return pl.pallas_call(
        _rest_kernel,
        grid=(t // TM,),
        in_specs=[pl.BlockSpec((TM, D_MODEL), row),
                  pl.BlockSpec((1, D_MODEL), const),
                  pl.BlockSpec((D_MODEL, REST_WIDTH), const)],
        out_specs=[pl.BlockSpec((TM, D_MODEL), row)] * 4,
        out_shape=[out] * 4,
        compiler_params=_params(("parallel",)),
        name="rest_proj",
    )(x, gain, w)


def _attention(q, kt, v, *, row0, n_seq, seq, tq):
    nq = seq // tq
    q0 = row0 // tq
    s0 = row0 // seq
    return pl.pallas_call(
        _attn_kernel,
        grid=(n_seq, nq),
        in_specs=[pl.BlockSpec((tq, Q_EXP_WIDTH), lambda b, i: (q0 + b * nq + i, 0)),
                  pl.BlockSpec((KV_WIDTH, seq), lambda b, i: (0, s0 + b)),
                  pl.BlockSpec((seq, KV_WIDTH), lambda b, i: (s0 + b, 0))],
        out_specs=pl.BlockSpec((tq, D_MODEL), lambda b, i: (b * nq + i, 0)),
        out_shape=jax.ShapeDtypeStruct((n_seq * seq, D_MODEL), BF16),
        compiler_params=_params(("parallel", "parallel")),
        name=f"attention_s{seq}",
    )(q, kt, v)


def _mix(attn, cb, u, ga, gb, x, cw, woa, wob, wo, gain, rw, *, n_prompt, seq_p, seq_s):
    t = x.shape[0]
    row = lambda i: (i, 0)
    const = lambda i: (0, 0)
    sub = TM // 8
    last = t // 8 - 1
    act = pl.BlockSpec((TM, D_MODEL), row)
    wspec = pl.BlockSpec((D_MODEL, D_MODEL), const)
    with_router = rw is not None
    kern = functools.partial(_mix_kernel, n_prompt_tiles=n_prompt // TM,
                             tiles4=seq_p // TM, tiles2=seq_s // TM, with_router=with_router)
    in_specs = [act, act, act,
                pl.BlockSpec((8, D_MODEL), lambda i: (jnp.maximum(i * sub - 1, 0), 0)),
                pl.BlockSpec((8, D_MODEL), lambda i: (jnp.minimum((i + 1) * sub, last), 0)),
                act, act, act,
                pl.BlockSpec((3, D_MODEL), const),
                wspec, wspec, wspec,
                pl.BlockSpec((1, D_MODEL), const)]
    args = [attn, cb, u, u, u, ga, gb, x, cw, woa, wob, wo, gain]
    out_specs = [act, act]
    out_shape = [jax.ShapeDtypeStruct((t, D_MODEL), F32),
                 jax.ShapeDtypeStruct((t, D_MODEL), F32 if with_router else BF16)]
    if with_router:
        in_specs.append(pl.BlockSpec((D_MODEL, LANES), const))
        args.append(rw)
        out_specs.append(pl.BlockSpec((TM, LANES), row))
        out_shape.append(jax.ShapeDtypeStruct((t, LANES), F32))
    return pl.pallas_call(
        kern,
        grid=(t // TM,),
        in_specs=in_specs,
        out_specs=out_specs,
        out_shape=out_shape,
        compiler_params=_params(("parallel",)),
        name="mix_proj_router" if with_router else "mix_proj",
    )(*args)


def _ffn(h, x, wg, wu, wd):
    t = x.shape[0]
    row = lambda i, j: (i, 0)
    return pl.pallas_call(
        _ffn_kernel,
        grid=(t // TM_FFN, D_FF // TF),
        in_specs=[pl.BlockSpec((TM_FFN, D_MODEL), row),
                  pl.BlockSpec((TM_FFN, D_MODEL), row),
                  pl.BlockSpec((D_MODEL, TF), lambda i, j: (0, j)),
                  pl.BlockSpec((D_MODEL, TF), lambda i, j: (0, j)),
                  pl.BlockSpec((TF, D_MODEL), lambda i, j: (j, 0))],
        out_specs=pl.BlockSpec((TM_FFN, D_MODEL), row),
        out_shape=jax.ShapeDtypeStruct((t, D_MODEL), F32),
        scratch_shapes=[pltpu.VMEM((TM_FFN, D_MODEL), F32)],
        compiler_params=_params(("parallel", "arbitrary")),
        name="ffn_dense",
    )(h, x, wg, wu, wd)


def _router(logits):
    t = logits.shape[0]
    row = lambda i: (i, 0)
    const = lambda i: (0, 0)
    r = jnp.arange(TR)
    tri = (r[None, :] < r[:, None]).astype(BF16)
    return pl.pallas_call(
        _router_kernel,
        grid=(t // TR,),
        in_specs=[pl.BlockSpec((TR, LANES), row),
                  pl.BlockSpec((TR, TR), const)],
        out_specs=[pl.BlockSpec((TR, LANES), row),
                   pl.BlockSpec((8, LANES), const)],
        out_shape=[jax.ShapeDtypeStruct((t, LANES), F32),
                   jax.ShapeDtypeStruct((8, LANES), F32)],
        scratch_shapes=[pltpu.VMEM((8, LANES), F32)],
        compiler_params=_params(("arbitrary",)),
        name="router",
    )(logits, tri)


def _route_plan(route, counts, t):
    e0 = route[:, ROUTE_E0].astype(jnp.int32)
    e1 = route[:, ROUTE_E1].astype(jnp.int32)
    r0 = route[:, ROUTE_R0].astype(jnp.int32)
    r1 = route[:, ROUTE_R1].astype(jnp.int32)
    cnt = counts[0, :N_EXPERTS].astype(jnp.int32)
    padded = ((cnt + TM_MOE - 1) // TM_MOE) * TM_MOE
    ends = jnp.cumsum(padded)
    starts = ends - padded
    experts = jnp.arange(N_EXPERTS, dtype=jnp.int32)
    pos0 = jnp.sum(jnp.where(e0[:, None] == experts[None, :], starts[None, :], 0), axis=1) + r0
    pos1 = jnp.sum(jnp.where(e1[:, None] == experts[None, :], starts[None, :], 0), axis=1) + r1
    n_tok_tiles = t // TM_MOE
    pos = jnp.concatenate([pos0.reshape(n_tok_tiles, 1, TM_MOE),
                           pos1.reshape(n_tok_tiles, 1, TM_MOE)], axis=2)
    n_tiles = 2 * t // TM_MOE + N_EXPERTS
    tile_start = jnp.arange(n_tiles, dtype=jnp.int32) * TM_MOE
    tile_valid = (tile_start < ends[-1]).astype(jnp.int32)
    tile_expert = jnp.sum((tile_start[:, None] >= ends[None, :]).astype(jnp.int32), axis=1)
    tile_expert = jnp.minimum(tile_expert, N_EXPERTS - 1)
    return pos, tile_expert, tile_valid, n_tiles


def _dispatch(pos, h, n_rows):
    t = h.shape[0]
    zeros = jnp.zeros((n_rows, D_MODEL), F32)
    return pl.pallas_call(
        _dispatch_kernel,
        grid=(t // TM_MOE,),
        in_specs=[pl.BlockSpec((1, 1, 2 * TM_MOE), lambda i: (i, 0, 0), memory_space=pltpu.SMEM),
                  pl.BlockSpec((TM_MOE, D_MODEL), lambda i: (i, 0)),
                  pl.BlockSpec(memory_space=pl.ANY)],
        out_specs=pl.BlockSpec(memory_space=pl.ANY),
        out_shape=jax.ShapeDtypeStruct((n_rows, D_MODEL), F32),
        scratch_shapes=[pltpu.SemaphoreType.DMA(())],
        input_output_aliases={2: 0},
        compiler_params=_params(("arbitrary",)),
        name="moe_dispatch",
    )(pos, h, zeros)


def _moe_ffn(tile_expert, tile_valid, xs, wg, wu, wd):
    n_rows = xs.shape[0]
    row = lambda g, j, te, tv: (g, 0)
    return pl.pallas_call(
        _moe_ffn_kernel,
        grid_spec=pltpu.PrefetchScalarGridSpec(
            num_scalar_prefetch=2,
            grid=(n_rows // TM_MOE, D_FF // TF),
            in_specs=[pl.BlockSpec((TM_MOE, D_MODEL), row),
                      pl.BlockSpec((1, D_MODEL, TF), lambda g, j, te, tv: (te[g], 0, j * tv[g])),
                      pl.BlockSpec((1, D_MODEL, TF), lambda g, j, te, tv: (te[g], 0, j * tv[g])),
                      pl.BlockSpec((1, TF, D_MODEL), lambda g, j, te, tv: (te[g], j * tv[g], 0))],
            out_specs=pl.BlockSpec((TM_MOE, D_MODEL), row),
            scratch_shapes=[pltpu.VMEM((TM_MOE, D_MODEL), F32)]),
        out_shape=jax.ShapeDtypeStruct((n_rows, D_MODEL), F32),
        compiler_params=_params(("parallel", "arbitrary")),
        name="moe_ffn",
    )(tile_expert, tile_valid, xs, wg, wu, wd)


def _combine(pos, x, route, fgain, ys):
    t = x.shape[0]
    row = lambda i: (i, 0)
    return pl.pallas_call(
        _combine_kernel,
        grid=(t // TM_MOE,),
        in_specs=[pl.BlockSpec((1, 1, 2 * TM_MOE), lambda i: (i, 0, 0), memory_space=pltpu.SMEM),
                  pl.BlockSpec((TM_MOE, D_MODEL), row),
                  pl.BlockSpec((TM_MOE, LANES), row),
                  pl.BlockSpec((1, D_MODEL), lambda i: (0, 0)),
                  pl.BlockSpec(memory_space=pl.ANY)],
        out_specs=pl.BlockSpec((TM_MOE, D_MODEL), row),
        out_shape=jax.ShapeDtypeStruct((t, D_MODEL), F32),
        scratch_shapes=[pltpu.VMEM((2, TM_MOE, D_MODEL), F32),
                        pltpu.SemaphoreType.DMA(())],
        compiler_params=_params(("arbitrary",)),
        name="moe_combine",
    )(pos, x, route, fgain, ys)


def kernel(x_prompt, x_sample, norm_mix, w_in, q_norm, k_norm, conv_w, w_oa, w_ob, w_o, norm_ffn,
           ffn_w_gate, ffn_w_up, ffn_w_down, router_w, moe_w_gate, moe_w_up, moe_w_down, final_norm):
    bp, sp, _ = x_prompt.shape
    bs, ss, _ = x_sample.shape
    n_prompt = bp * sp
    n_sample = bs * ss
    depth = norm_mix.shape[0]
    assert sp % TM == 0 and ss % TM == 0 and n_prompt % TM_FFN == 0 and n_sample % TM_FFN == 0
    assert depth == 2 and ffn_w_gate.shape[0] == 1 and moe_w_gate.shape[0] == 1

    x = jnp.concatenate([x_prompt.reshape(n_prompt, D_MODEL),
                         x_sample.reshape(n_sample, D_MODEL)], axis=0)

    cos, sin = _rope_tables(max(sp, ss))
    n_prompt_tiles, tiles4, tiles2 = n_prompt // TM, sp // TM, ss // TM
    pos_map = lambda i: (jnp.where(i < n_prompt_tiles, i % tiles4, i % tiles2), 0)
    idx = jnp.arange(LANES)
    bd = (idx[:, None] // HEAD_DIM == idx[None, :] // HEAD_DIM).astype(BF16)

    for l in range(depth):
        gain = norm_mix[l][None, :]
        w_l = w_in[l]
        w_qkv = w_l[:, :QKV_WIDTH].astype(BF16)
        w_rest = w_l[:, QKV_WIDTH:].astype(BF16)
        qg = jnp.tile(q_norm[l], 2)[None, :]
        kg = jnp.tile(k_norm[l], 2)[None, :]
        q, kt, v = _qkv_proj(x, gain, w_qkv, qg, kg, cos, sin, bd, pos_map)
        cb, u, ga, gb = _rest_proj(x, gain, w_rest)
        attn_p = _attention(q, kt, v, row0=0, n_seq=bp, seq=sp, tq=128)
        attn_s = _attention(q, kt, v, row0=n_prompt, n_seq=bs, seq=ss, tq=256)
        attn = jnp.concatenate([attn_p, attn_s], axis=0)
        j = l // 2
        is_moe = l % 2 == 1
        rw = jnp.pad(router_w[j], ((0, 0), (0, LANES - N_EXPERTS))) if is_moe else None
        outs = _mix(attn, cb, u, ga, gb, x, conv_w[l],
                    w_oa[l].astype(BF16), w_ob[l].astype(BF16), w_o[l].astype(BF16),
                    norm_ffn[l][None, :], rw, n_prompt=n_prompt, seq_p=sp, seq_s=ss)
        if not is_moe:
            x, h2 = outs
            x = _ffn(h2, x, ffn_w_gate[j].astype(BF16), ffn_w_up[j].astype(BF16),
                     ffn_w_down[j].astype(BF16))
        else:
            x, h2, logits = outs
            route, counts = _router(logits)
            pos, tile_expert, tile_valid, n_tiles = _route_plan(route, counts, x.shape[0])
            xs = _dispatch(pos, h2, n_tiles * TM_MOE)
            ys = _moe_ffn(tile_expert, tile_valid, xs, moe_w_gate[j].astype(BF16),
                          moe_w_up[j].astype(BF16), moe_w_down[j].astype(BF16))
            x = _combine(pos, x, route, final_norm[None, :], ys)

    y_prompt = x[:n_prompt].reshape(bp, sp, D_MODEL)
    y_sample = x[n_prompt:].reshape(bs, ss, D_MODEL)
    return (y_prompt, y_sample)
```

```python
import functools
import math

import jax
import jax.numpy as jnp
from jax import lax
from jax.experimental import pallas as pl
from jax.experimental.pallas import tpu as pltpu

F32 = jnp.float32
BF16 = jnp.bfloat16

D_MODEL = 1024
N_HEADS = 16
N_KV_HEADS = 4
HEAD_DIM = 64
GROUP = N_HEADS // N_KV_HEADS
KV_WIDTH = N_KV_HEADS * HEAD_DIM
AXIS_DIM = HEAD_DIM // 2
ROPE_THETA = 10000.0
GRID_W = 64
D_FF = 3584
N_EXPERTS = 8
EPS = 1e-6
LANES = 128
QKV_WIDTH = D_MODEL + 2 * KV_WIDTH
REST_WIDTH = 5 * D_MODEL
Q_EXP_WIDTH = N_HEADS * LANES
KV_EXP_WIDTH = N_KV_HEADS * LANES
Q_SCALE = math.log2(math.e) / math.sqrt(HEAD_DIM)
MAX_SHIFT_GAP = 100.0

TM = 512
TM_FFN = 1024
TF = 512
TR = 512
TM_MOE = 512
VMEM_LIMIT = 56 * 1024 * 1024


def _params(sem):
    return pltpu.CompilerParams(dimension_semantics=sem, vmem_limit_bytes=VMEM_LIMIT)


def _rms(x, gain):
    return x * lax.rsqrt(jnp.mean(x * x, axis=-1, keepdims=True) + EPS) * gain


def _qkv_kernel(x_ref, g_ref, w_ref, qg_ref, kg_ref, cos_ref, sin_ref, bd_ref,
                q_ref, kt_ref, v_ref):
    tm = x_ref.shape[0]
    h = _rms(x_ref[...], g_ref[...]).astype(BF16)
    p = jnp.dot(h, w_ref[...], preferred_element_type=F32)
    cos = cos_ref[...]
    sin = sin_ref[...]
    bd = bd_ref[...]
    lane = lax.broadcasted_iota(jnp.int32, (tm, LANES), 1)
    first16 = (lane & 31) < 16
    low_half = lane < HEAD_DIM

    def norm_rope(c, gain):
        sq = c * c
        hi = sq.astype(BF16)
        lo = (sq - hi.astype(F32)).astype(BF16)
        ss = (jnp.dot(hi, bd, preferred_element_type=F32)
              + jnp.dot(lo, bd, preferred_element_type=F32))
        n = c * lax.rsqrt(ss * (1.0 / HEAD_DIM) + EPS) * gain
        partner = jnp.where(first16, pltpu.roll(n, LANES - 16, 1), pltpu.roll(n, 16, 1))
        return n * cos + partner * sin

    is_aux = lane == HEAD_DIM
    zero = jnp.zeros((tm, LANES), F32)
    ones_aux = jnp.where(is_aux, 1.0, zero)

    qg = qg_ref[...]
    for c in range(N_HEADS // 2):
        r = norm_rope(p[:, c * LANES:(c + 1) * LANES], qg) * Q_SCALE
        rr = r * r
        norm = jnp.sqrt(jnp.dot(rr.astype(BF16), bd, preferred_element_type=F32))
        even = jnp.where(low_half, r, jnp.where(is_aux, pltpu.roll(norm, HEAD_DIM, 1), zero))
        odd = jnp.where(low_half, pltpu.roll(r, HEAD_DIM, 1), jnp.where(is_aux, norm, zero))
        q_ref[:, (2 * c) * LANES:(2 * c + 1) * LANES] = even.astype(BF16)
        q_ref[:, (2 * c + 1) * LANES:(2 * c + 2) * LANES] = odd.astype(BF16)

    kg = kg_ref[...]
    for c in range(KV_WIDTH // LANES):
        kn = norm_rope(p[:, D_MODEL + c * LANES:D_MODEL + (c + 1) * LANES], kg)
        even = jnp.where(low_half, kn, ones_aux)
        odd = jnp.where(low_half, pltpu.roll(kn, HEAD_DIM, 1), ones_aux)
        kt_ref[(2 * c) * LANES:(2 * c + 1) * LANES, :] = even.T.astype(BF16)
        kt_ref[(2 * c + 1) * LANES:(2 * c + 2) * LANES, :] = odd.T.astype(BF16)
        vv = p[:, D_MODEL + KV_WIDTH + c * LANES:D_MODEL + KV_WIDTH + (c + 1) * LANES]
        even = jnp.where(low_half, vv, ones_aux)
        odd = jnp.where(low_half, pltpu.roll(vv, HEAD_DIM, 1), ones_aux)
        v_ref[:, (2 * c) * LANES:(2 * c + 1) * LANES] = even.astype(BF16)
        v_ref[:, (2 * c + 1) * LANES:(2 * c + 2) * LANES] = odd.astype(BF16)


def _rest_kernel(x_ref, g_ref, w_ref, cb_ref, u_ref, ga_ref, gb_ref):
    h = _rms(x_ref[...], g_ref[...]).astype(BF16)
    p = jnp.dot(h, w_ref[...], preferred_element_type=F32)
    cb_ref[...] = p[:, :D_MODEL].astype(BF16)
    u_ref[...] = (p[:, D_MODEL:2 * D_MODEL] * p[:, 2 * D_MODEL:3 * D_MODEL]).astype(BF16)
    ga_ref[...] = p[:, 3 * D_MODEL:4 * D_MODEL].astype(BF16)
    gb_ref[...] = p[:, 4 * D_MODEL:].astype(BF16)


def _attn_kernel(kmax_ref, q_ref, kt_ref, v_ref, o_ref, *, seq0, exact_max):
    b = pl.program_id(0)
    tq = q_ref.shape[0]
    lane4 = lax.broadcasted_iota(jnp.int32, (GROUP * tq, LANES), 1)
    is_aux4 = lane4 == HEAD_DIM
    low_half = lax.broadcasted_iota(jnp.int32, (tq, LANES), 1) < HEAD_DIM
    for j in range(N_KV_HEADS):
        kc = kt_ref[j * LANES:(j + 1) * LANES, :]
        vc = v_ref[:, j * LANES:(j + 1) * LANES]
        q4 = jnp.concatenate(
            [q_ref[:, h * LANES:(h + 1) * LANES] for h in range(GROUP * j, GROUP * (j + 1))],
            axis=0)
        if exact_max:
            q4 = jnp.where(is_aux4, jnp.zeros_like(q4), q4)
            s = jnp.dot(q4, kc, preferred_element_type=F32)
            s = s - jnp.max(s, axis=-1, keepdims=True)
        else:
            neg_kmax = -kmax_ref[(seq0 + b) * N_KV_HEADS + j]
            q4 = jnp.where(is_aux4, (q4.astype(F32) * neg_kmax).astype(BF16), q4)
            s = jnp.dot(q4, kc, preferred_element_type=F32)
        p = jnp.exp2(s).astype(BF16)
        o = jnp.dot(p, vc, preferred_element_type=F32)
        o = o / jnp.sum(jnp.where(is_aux4, o, 0.0), axis=-1, keepdims=True)
        for a in range(2):
            oe = o[(2 * a) * tq:(2 * a + 1) * tq]
            oo = o[(2 * a + 1) * tq:(2 * a + 2) * tq]
            chunk = jnp.where(low_half, oe, pltpu.roll(oo, HEAD_DIM, 1))
            c = 2 * j + a
            o_ref[:, c * LANES:(c + 1) * LANES] = chunk.astype(BF16)


def _mix_kernel(attn_ref, cb_ref, u_ref, up_ref, un_ref, ga_ref, gb_ref, x_ref,
                cw_ref, woa_ref, wob_ref, wo_ref, g_ref, *rest,
                n_prompt_tiles, tiles4, tiles2, with_router):
    if with_router:
        rw_ref, xo_ref, h2_ref, lg_ref = rest
    else:
        xo_ref, h2_ref = rest
    i = pl.program_id(0)
    tm = x_ref.shape[0]
    is_prompt = i < n_prompt_tiles
    seq_start = jnp.where(is_prompt, i % tiles4 == 0, i % tiles2 == 0)
    seq_end = jnp.where(is_prompt, i % tiles4 == tiles4 - 1, i % tiles2 == tiles2 - 1)

    ya = jnp.dot(attn_ref[...], woa_ref[...], preferred_element_type=F32)

    u = u_ref[...].astype(F32)
    row = lax.broadcasted_iota(jnp.int32, u.shape, 0)
    prev_row = jnp.where(seq_start, 0.0, up_ref[7:8, :].astype(F32))
    next_row = jnp.where(seq_end, 0.0, un_ref[0:1, :].astype(F32))
    u_prev = jnp.where(row == 0, prev_row, pltpu.roll(u, 1, 0))
    u_next = jnp.where(row == tm - 1, next_row, pltpu.roll(u, tm - 1, 0))
    cw = cw_ref[...]
    conv = cw[0:1, :] * u_prev + cw[1:2, :] * u + cw[2:3, :] * u_next
    yb_in = (cb_ref[...].astype(F32) * conv).astype(BF16)
    yb = jnp.dot(yb_in, wob_ref[...], preferred_element_type=F32)

    m = (jax.nn.sigmoid(ga_ref[...].astype(F32)) * ya
         + jax.nn.sigmoid(gb_ref[...].astype(F32)) * yb)
    xn = x_ref[...] + jnp.dot(m.astype(BF16), wo_ref[...], preferred_element_type=F32)
    xo_ref[...] = xn
    h2 = _rms(xn, g_ref[...])
    h2_ref[...] = h2.astype(h2_ref.dtype)
    if not with_router:
        return
    hi = h2.astype(BF16)
    lo = (h2 - hi.astype(F32)).astype(BF16)
    rw = rw_ref[...]
    rhi = rw.astype(BF16)
    rlo = (rw - rhi.astype(F32)).astype(BF16)
    lg_ref[...] = (jnp.dot(hi, rhi, preferred_element_type=F32)
                   + jnp.dot(hi, rlo, preferred_element_type=F32)
                   + jnp.dot(lo, rhi, preferred_element_type=F32))


def _ffn_kernel(h_ref, x_ref, wg_ref, wu_ref, wd_ref, o_ref, acc_ref):
    j = pl.program_id(1)

    @pl.when(j == 0)
    def _():
        acc_ref[...] = jnp.zeros_like(acc_ref)

    h = h_ref[...]
    g = jnp.dot(h, wg_ref[...], preferred_element_type=F32)
    u = jnp.dot(h, wu_ref[...], preferred_element_type=F32)
    a = (g * jax.nn.sigmoid(g) * u).astype(BF16)
    acc_ref[...] += jnp.dot(a, wd_ref[...], preferred_element_type=F32)

    @pl.when(j == pl.num_programs(1) - 1)
    def _():
        o_ref[...] = x_ref[...] + acc_ref[...]


ROUTE_E0, ROUTE_E1, ROUTE_R0, ROUTE_R1, ROUTE_W0, ROUTE_W1 = range(6)


def _lane_pick(x, lane, k):
    return jnp.sum(jnp.where(lane == k, x, 0.0), axis=-1, keepdims=True)


def _router_kernel(lg_ref, tri_ref, route_ref, count_ref, base_ref):
    i = pl.program_id(0)

    @pl.when(i == 0)
    def _():
        base_ref[...] = jnp.zeros_like(base_ref)

    lg = lg_ref[...]
    lane = lax.broadcasted_iota(jnp.int32, lg.shape, 1)
    neg = jnp.float32(-jnp.inf)
    l1 = jnp.where(lane < N_EXPERTS, lg, neg)
    m1 = jnp.max(l1, axis=-1, keepdims=True)
    i1 = jnp.min(jnp.where(l1 == m1, lane, LANES), axis=-1, keepdims=True)
    l2 = jnp.where(lane == i1, neg, l1)
    m2 = jnp.max(l2, axis=-1, keepdims=True)
    i2 = jnp.min(jnp.where(l2 == m2, lane, LANES), axis=-1, keepdims=True)
    e = jnp.exp(m2 - m1)
    w1 = 1.0 / (1.0 + e)
    w2 = e / (1.0 + e)

    hot1 = lane == i1
    hot2 = lane == i2
    onehot = jnp.where(hot1 | hot2, 1.0, 0.0)
    base = base_ref[0:1, :]
    prefix = jnp.dot(tri_ref[...], onehot.astype(BF16), preferred_element_type=F32) + base
    r1 = jnp.sum(jnp.where(hot1, prefix, 0.0), axis=-1, keepdims=True)
    r2 = jnp.sum(jnp.where(hot2, prefix, 0.0), axis=-1, keepdims=True)
    total = base + jnp.sum(onehot, axis=0, keepdims=True)
    base_ref[...] = jnp.broadcast_to(total, base_ref.shape)
    count_ref[...] = jnp.broadcast_to(total, count_ref.shape)

    rec = jnp.where(lane == ROUTE_E0, i1.astype(F32), 0.0)
    rec = jnp.where(lane == ROUTE_E1, i2.astype(F32), rec)
    rec = jnp.where(lane == ROUTE_R0, r1, rec)
    rec = jnp.where(lane == ROUTE_R1, r2, rec)
    rec = jnp.where(lane == ROUTE_W0, w1, rec)
    rec = jnp.where(lane == ROUTE_W1, w2, rec)
    route_ref[...] = rec


def _row_copy(src, src_row, dst, dst_row, sem):
    return pltpu.make_async_copy(src.at[pl.ds(src_row, 1)], dst.at[pl.ds(dst_row, 1)], sem)


def _dispatch_kernel(pos_ref, h_ref, xs_in_ref, xs_ref, sem):
    del xs_in_ref
    tm = h_ref.shape[0]

    def issue(r, carry):
        _row_copy(h_ref, r, xs_ref, pos_ref[0, 0, r], sem).start()
        _row_copy(h_ref, r, xs_ref, pos_ref[0, 0, tm + r], sem).start()
        return carry

    def drain(r, carry):
        _row_copy(h_ref, 0, xs_ref, 0, sem).wait()
        _row_copy(h_ref, 0, xs_ref, 0, sem).wait()
        return carry

    lax.fori_loop(0, tm, issue, 0)
    lax.fori_loop(0, tm, drain, 0)


def _moe_ffn_kernel(te_ref, tv_ref, x_ref, wg_ref, wu_ref, wd_ref, y_ref, acc_ref):
    del te_ref
    g_idx = pl.program_id(0)
    j = pl.program_id(1)
    valid = tv_ref[g_idx] > 0

    @pl.when(j == 0)
    def _():
        acc_ref[...] = jnp.zeros_like(acc_ref)

    @pl.when(valid)
    def _():
        h = x_ref[...].astype(BF16)
        g = jnp.dot(h, wg_ref[0], preferred_element_type=F32)
        u = jnp.dot(h, wu_ref[0], preferred_element_type=F32)
        a = (g * jax.nn.sigmoid(g) * u).astype(BF16)
        acc_ref[...] += jnp.dot(a, wd_ref[0], preferred_element_type=F32)

    @pl.when(j == pl.num_programs(1) - 1)
    def _():
        y_ref[...] = acc_ref[...]


def _combine_kernel(pos_ref, x_ref, route_ref, fg_ref, y_hbm, o_ref, ybuf, sem):
    tc = x_ref.shape[0]

    def issue(r, carry):
        _row_copy(y_hbm, pos_ref[0, 0, r], ybuf.at[0], r, sem).start()
        _row_copy(y_hbm, pos_ref[0, 0, tc + r], ybuf.at[1], r, sem).start()
        return carry

    def drain(r, carry):
        _row_copy(y_hbm, 0, ybuf.at[0], 0, sem).wait()
        _row_copy(y_hbm, 0, ybuf.at[1], 0, sem).wait()
        return carry

    lax.fori_loop(0, tc, issue, 0)
    lax.fori_loop(0, tc, drain, 0)
    route = route_ref[...]
    lane = lax.broadcasted_iota(jnp.int32, route.shape, 1)
    w0 = _lane_pick(route, lane, ROUTE_W0)
    w1 = _lane_pick(route, lane, ROUTE_W1)
    out = x_ref[...] + (w0 * ybuf[0] + w1 * ybuf[1])
    o_ref[...] = _rms(out, fg_ref[...])


def _rope_tables(max_seq):
    t = jnp.arange(max_seq, dtype=jnp.int32)
    row = (t // GRID_W).astype(F32)
    col = (t % GRID_W).astype(F32)
    inv = 1.0 / (ROPE_THETA ** (jnp.arange(0, AXIS_DIM, 2, dtype=F32) / AXIS_DIM))
    ar = row[:, None] * inv[None, :]
    ac = col[:, None] * inv[None, :]
    cos64 = jnp.concatenate([jnp.cos(ar), jnp.cos(ar), jnp.cos(ac), jnp.cos(ac)], axis=-1)
    sin64 = jnp.concatenate([-jnp.sin(ar), jnp.sin(ar), -jnp.sin(ac), jnp.sin(ac)], axis=-1)
    return jnp.tile(cos64, (1, 2)), jnp.tile(sin64, (1, 2))


def _qkv_proj(x, gain, w, qg, kg, cos, sin, bd, pos_map):
    t = x.shape[0]
    row = lambda i: (i, 0)
    const = lambda i: (0, 0)
    return pl.pallas_call(
        _qkv_kernel,
        grid=(t // TM,),
        in_specs=[pl.BlockSpec((TM, D_MODEL), row),
                  pl.BlockSpec((1, D_MODEL), const),
                  pl.BlockSpec((D_MODEL, QKV_WIDTH), const),
                  pl.BlockSpec((1, LANES), const),
                  pl.BlockSpec((1, LANES), const),
                  pl.BlockSpec((TM, LANES), pos_map),
                  pl.BlockSpec((TM, LANES), pos_map),
                  pl.BlockSpec((LANES, LANES), const)],
        out_specs=[pl.BlockSpec((TM, Q_EXP_WIDTH), row),
                   pl.BlockSpec((KV_EXP_WIDTH, TM), lambda i: (0, i)),
                   pl.BlockSpec((TM, KV_EXP_WIDTH), row)],
        out_shape=[jax.ShapeDtypeStruct((t, Q_EXP_WIDTH), BF16),
                   jax.ShapeDtypeStruct((KV_EXP_WIDTH, t), BF16),
                   jax.ShapeDtypeStruct((t, KV_EXP_WIDTH), BF16)],
        compiler_params=_params(("parallel",)),
        name="qkv_proj",
    )(x, gain, w, qg, kg, cos, sin, bd)


def _rest_proj(x, gain, w):
    t = x.shape[0]
    row = lambda i: (i, 0)
    const = lambda i: (0, 0)
    out = jax.ShapeDtypeStruct((t, D_MODEL), BF16)
    return pl.pallas_call(
        _rest_kernel,
        grid=(t // TM,),
        in_specs=[pl.BlockSpec((TM, D_MODEL), row),
                  pl.BlockSpec((1, D_MODEL), const),
                  pl.BlockSpec((D_MODEL, REST_WIDTH), const)],
        out_specs=[pl.BlockSpec((TM, D_MODEL), row)] * 4,
        out_shape=[out] * 4,
        compiler_params=_params(("parallel",)),
        name="rest_proj",
    )(x, gain, w)


def _attention(kmax, q, kt, v, *, row0, seq0, n_seq, seq, tq, exact_max):
    nq = seq // tq
    q0 = row0 // tq
    s0 = row0 // seq
    kern = functools.partial(_attn_kernel, seq0=seq0, exact_max=exact_max)
    return pl.pallas_call(
        kern,
        grid_spec=pltpu.PrefetchScalarGridSpec(
            num_scalar_prefetch=1,
            grid=(n_seq, nq),
            in_specs=[pl.BlockSpec((tq, Q_EXP_WIDTH), lambda b, i, km: (q0 + b * nq + i, 0)),
                      pl.BlockSpec((KV_EXP_WIDTH, seq), lambda b, i, km: (0, s0 + b)),
                      pl.BlockSpec((seq, KV_EXP_WIDTH), lambda b, i, km: (s0 + b, 0))],
            out_specs=pl.BlockSpec((tq, D_MODEL), lambda b, i, km: (b * nq + i, 0))),
        out_shape=jax.ShapeDtypeStruct((n_seq * seq, D_MODEL), BF16),
        compiler_params=_params(("parallel", "parallel")),
        name=f"attention_s{seq}" + ("_exact" if exact_max else ""),
    )(kmax, q, kt, v)


def _key_norm_max(kt, n_prompt, bp, sp, bs, ss):
    k = kt.reshape(N_KV_HEADS, LANES, -1)[:, :HEAD_DIM, :].astype(F32)
    norm = jnp.sqrt(jnp.sum(k * k, axis=1))
    kp = jnp.max(norm[:, :n_prompt].reshape(N_KV_HEADS, bp, sp), axis=-1)
    ks = jnp.max(norm[:, n_prompt:].reshape(N_KV_HEADS, bs, ss), axis=-1)
    return jnp.concatenate([kp.T.reshape(-1), ks.T.reshape(-1)])


def _mix(attn, cb, u, ga, gb, x, cw, woa, wob, wo, gain, rw, *, n_prompt, seq_p, seq_s):
    t = x.shape[0]
    row = lambda i: (i, 0)
    const = lambda i: (0, 0)
    sub = TM // 8
    last = t // 8 - 1
    act = pl.BlockSpec((TM, D_MODEL), row)
    wspec = pl.BlockSpec((D_MODEL, D_MODEL), const)
    with_router = rw is not None
    kern = functools.partial(_mix_kernel, n_prompt_tiles=n_prompt // TM,
                             tiles4=seq_p // TM, tiles2=seq_s // TM, with_router=with_router)
    in_specs = [act, act, act,
                pl.BlockSpec((8, D_MODEL), lambda i: (jnp.maximum(i * sub - 1, 0), 0)),
                pl.BlockSpec((8, D_MODEL), lambda i: (jnp.minimum((i + 1) * sub, last), 0)),
                act, act, act,
                pl.BlockSpec((3, D_MODEL), const),
                wspec, wspec, wspec,
                pl.BlockSpec((1, D_MODEL), const)]
    args = [attn, cb, u, u, u, ga, gb, x, cw, woa, wob, wo, gain]
    out_specs = [act, act]
    out_shape = [jax.ShapeDtypeStruct((t, D_MODEL), F32),
                 jax.ShapeDtypeStruct((t, D_MODEL), F32 if with_router else BF16)]
    if with_router:
        in_specs.append(pl.BlockSpec((D_MODEL, LANES), const))
        args.append(rw)
        out_specs.append(pl.BlockSpec((TM, LANES), row))
        out_shape.append(jax.ShapeDtypeStruct((t, LANES), F32))
    return pl.pallas_call(
        kern,
        grid=(t // TM,),
        in_specs=in_specs,
        out_specs=out_specs,
        out_shape=out_shape,
        compiler_params=_params(("parallel",)),
        name="mix_proj_router" if with_router else "mix_proj",
    )(*args)


def _ffn(h, x, wg, wu, wd):
    t = x.shape[0]
    row = lambda i, j: (i, 0)
    return pl.pallas_call(
        _ffn_kernel,
        grid=(t // TM_FFN, D_FF // TF),
        in_specs=[pl.BlockSpec((TM_FFN, D_MODEL), row),
                  pl.BlockSpec((TM_FFN, D_MODEL), row),
                  pl.BlockSpec((D_MODEL, TF), lambda i, j: (0, j)),
                  pl.BlockSpec((D_MODEL, TF), lambda i, j: (0, j)),
                  pl.BlockSpec((TF, D_MODEL), lambda i, j: (j, 0))],
        out_specs=pl.BlockSpec((TM_FFN, D_MODEL), row),
        out_shape=jax.ShapeDtypeStruct((t, D_MODEL), F32),
        scratch_shapes=[pltpu.VMEM((TM_FFN, D_MODEL), F32)],
        compiler_params=_params(("parallel", "arbitrary")),
        name="ffn_dense",
    )(h, x, wg, wu, wd)


def _router(logits):
    t = logits.shape[0]
    row = lambda i: (i, 0)
    const = lambda i: (0, 0)
    r = jnp.arange(TR)
    tri = (r[None, :] < r[:, None]).astype(BF16)
    return pl.pallas_call(
        _router_kernel,
        grid=(t // TR,),
        in_specs=[pl.BlockSpec((TR, LANES), row),
                  pl.BlockSpec((TR, TR), const)],
        out_specs=[pl.BlockSpec((TR, LANES), row),
                   pl.BlockSpec((8, LANES), const)],
        out_shape=[jax.ShapeDtypeStruct((t, LANES), F32),
                   jax.ShapeDtypeStruct((8, LANES), F32)],
        scratch_shapes=[pltpu.VMEM((8, LANES), F32)],
        compiler_params=_params(("arbitrary",)),
        name="router",
    )(logits, tri)


def _route_plan(route, counts, t):
    e0 = route[:, ROUTE_E0].astype(jnp.int32)
    e1 = route[:, ROUTE_E1].astype(jnp.int32)
    r0 = route[:, ROUTE_R0].astype(jnp.int32)
    r1 = route[:, ROUTE_R1].astype(jnp.int32)
    cnt = counts[0, :N_EXPERTS].astype(jnp.int32)
    padded = ((cnt + TM_MOE - 1) // TM_MOE) * TM_MOE
    ends = jnp.cumsum(padded)
    starts = ends - padded
    experts = jnp.arange(N_EXPERTS, dtype=jnp.int32)
    pos0 = jnp.sum(jnp.where(e0[:, None] == experts[None, :], starts[None, :], 0), axis=1) + r0
    pos1 = jnp.sum(jnp.where(e1[:, None] == experts[None, :], starts[None, :], 0), axis=1) + r1
    n_tok_tiles = t // TM_MOE
    pos = jnp.concatenate([pos0.reshape(n_tok_tiles, 1, TM_MOE),
                           pos1.reshape(n_tok_tiles, 1, TM_MOE)], axis=2)
    n_tiles = 2 * t // TM_MOE + N_EXPERTS
    tile_start = jnp.arange(n_tiles, dtype=jnp.int32) * TM_MOE
    tile_valid = (tile_start < ends[-1]).astype(jnp.int32)
    tile_expert = jnp.sum((tile_start[:, None] >= ends[None, :]).astype(jnp.int32), axis=1)
    tile_expert = jnp.minimum(tile_expert, N_EXPERTS - 1)
    return pos, tile_expert, tile_valid, n_tiles


def _dispatch(pos, h, n_rows):
    t = h.shape[0]
    zeros = jnp.zeros((n_rows, D_MODEL), F32)
    return pl.pallas_call(
        _dispatch_kernel,
        grid=(t // TM_MOE,),
        in_specs=[pl.BlockSpec((1, 1, 2 * TM_MOE), lambda i: (i, 0, 0), memory_space=pltpu.SMEM),
                  pl.BlockSpec((TM_MOE, D_MODEL), lambda i: (i, 0)),
                  pl.BlockSpec(memory_space=pl.ANY)],
        out_specs=pl.BlockSpec(memory_space=pl.ANY),
        out_shape=jax.ShapeDtypeStruct((n_rows, D_MODEL), F32),
        scratch_shapes=[pltpu.SemaphoreType.DMA(())],
        input_output_aliases={2: 0},
        compiler_params=_params(("arbitrary",)),
        name="moe_dispatch",
    )(pos, h, zeros)


def _moe_ffn(tile_expert, tile_valid, xs, wg, wu, wd):
    n_rows = xs.shape[0]
    row = lambda g, j, te, tv: (g, 0)
    return pl.pallas_call(
        _moe_ffn_kernel,
        grid_spec=pltpu.PrefetchScalarGridSpec(
            num_scalar_prefetch=2,
            grid=(n_rows // TM_MOE, D_FF // TF),
            in_specs=[pl.BlockSpec((TM_MOE, D_MODEL), row),
                      pl.BlockSpec((1, D_MODEL, TF), lambda g, j, te, tv: (te[g], 0, j * tv[g])),
                      pl.BlockSpec((1, D_MODEL, TF), lambda g, j, te, tv: (te[g], 0, j * tv[g])),
                      pl.BlockSpec((1, TF, D_MODEL), lambda g, j, te, tv: (te[g], j * tv[g], 0))],
            out_specs=pl.BlockSpec((TM_MOE, D_MODEL), row),
            scratch_shapes=[pltpu.VMEM((TM_MOE, D_MODEL), F32)]),
        out_shape=jax.ShapeDtypeStruct((n_rows, D_MODEL), F32),
        compiler_params=_params(("parallel", "arbitrary")),
        name="moe_ffn",
    )(tile_expert, tile_valid, xs, wg, wu, wd)


def _combine(pos, x, route, fgain, ys):
    t = x.shape[0]
    row = lambda i: (i, 0)
    return pl.pallas_call(
        _combine_kernel,
        grid=(t // TM_MOE,),
        in_specs=[pl.BlockSpec((1, 1, 2 * TM_MOE), lambda i: (i, 0, 0), memory_space=pltpu.SMEM),
                  pl.BlockSpec((TM_MOE, D_MODEL), row),
                  pl.BlockSpec((TM_MOE, LANES), row),
                  pl.BlockSpec((1, D_MODEL), lambda i: (0, 0)),
                  pl.BlockSpec(memory_space=pl.ANY)],
        out_specs=pl.BlockSpec((TM_MOE, D_MODEL), row),
        out_shape=jax.ShapeDtypeStruct((t, D_MODEL), F32),
        scratch_shapes=[pltpu.VMEM((2, TM_MOE, D_MODEL), F32),
                        pltpu.SemaphoreType.DMA(())],
        compiler_params=_params(("arbitrary",)),
        name="moe_combine",
    )(pos, x, route, fgain, ys)


def kernel(x_prompt, x_sample, norm_mix, w_in, q_norm, k_norm, conv_w, w_oa, w_ob, w_o, norm_ffn,
           ffn_w_gate, ffn_w_up, ffn_w_down, router_w, moe_w_gate, moe_w_up, moe_w_down, final_norm):
    bp, sp, _ = x_prompt.shape
    bs, ss, _ = x_sample.shape
    n_prompt = bp * sp
    n_sample = bs * ss
    depth = norm_mix.shape[0]
    assert sp % TM == 0 and ss % TM == 0 and n_prompt % TM_FFN == 0 and n_sample % TM_FFN == 0
    assert depth == 2 and ffn_w_gate.shape[0] == 1 and moe_w_gate.shape[0] == 1

    x = jnp.concatenate([x_prompt.reshape(n_prompt, D_MODEL),
                         x_sample.reshape(n_sample, D_MODEL)], axis=0)

    cos, sin = _rope_tables(max(sp, ss))
    n_prompt_tiles, tiles4, tiles2 = n_prompt // TM, sp // TM, ss // TM
    pos_map = lambda i: (jnp.where(i < n_prompt_tiles, i % tiles4, i % tiles2), 0)
    idx = jnp.arange(LANES)
    bd = (idx[:, None] // HEAD_DIM == idx[None, :] // HEAD_DIM).astype(BF16)

    for l in range(depth):
        gain = norm_mix[l][None, :]
        w_l = w_in[l]
        w_qkv = w_l[:, :QKV_WIDTH].astype(BF16)
        w_rest = w_l[:, QKV_WIDTH:].astype(BF16)
        qg = jnp.tile(q_norm[l], 2)[None, :]
        kg = jnp.tile(k_norm[l], 2)[None, :]
        q, kt, v = _qkv_proj(x, gain, w_qkv, qg, kg, cos, sin, bd, pos_map)
        cb, u, ga, gb = _rest_proj(x, gain, w_rest)
        kmax = _key_norm_max(kt, n_prompt, bp, sp, bs, ss)
        q_bound = Q_SCALE * math.sqrt(HEAD_DIM) * jnp.max(jnp.abs(q_norm[l]))
        bound_ok = 2.0 * q_bound * jnp.max(kmax) < MAX_SHIFT_GAP

        def attend(exact_max, tq_p, tq_s):
            def run(kmax, q, kt, v):
                attn_p = _attention(kmax, q, kt, v, row0=0, seq0=0, n_seq=bp, seq=sp,
                                    tq=tq_p, exact_max=exact_max)
                attn_s = _attention(kmax, q, kt, v, row0=n_prompt, seq0=bp, n_seq=bs, seq=ss,
                                    tq=tq_s, exact_max=exact_max)
                return jnp.concatenate([attn_p, attn_s], axis=0)
            return run

        attn = lax.cond(bound_ok, attend(False, 256, 256), attend(True, 128, 256),
                        kmax, q, kt, v)
        j = l // 2
        is_moe = l % 2 == 1
        rw = jnp.pad(router_w[j], ((0, 0), (0, LANES - N_EXPERTS))) if is_moe else None
        outs = _mix(attn, cb, u, ga, gb, x, conv_w[l],
                    w_oa[l].astype(BF16), w_ob[l].astype(BF16), w_o[l].astype(BF16),
                    norm_ffn[l][None, :], rw, n_prompt=n_prompt, seq_p=sp, seq_s=ss)
        if not is_moe:
            x, h2 = outs
            x = _ffn(h2, x, ffn_w_gate[j].astype(BF16), ffn_w_up[j].astype(BF16),
                     ffn_w_down[j].astype(BF16))
        else:
            x, h2, logits = outs
            route, counts = _router(logits)
            pos, tile_expert, tile_valid, n_tiles = _route_plan(route, counts, x.shape[0])
            xs = _dispatch(pos, h2, n_tiles * TM_MOE)
            ys = _moe_ffn(tile_expert, tile_valid, xs, moe_w_gate[j].astype(BF16),
                          moe_w_up[j].astype(BF16), moe_w_down[j].astype(BF16))
            x = _combine(pos, x, route, final_norm[None, :], ys)

    y_prompt = x[:n_prompt].reshape(bp, sp, D_MODEL)
    y_sample = x[n_prompt:].reshape(bs, ss, D_MODEL)
    return (y_prompt, y_sample)
```

```python
import functools
import math

import jax
import jax.numpy as jnp
from jax import lax
from jax.experimental import pallas as pl
from jax.experimental.pallas import tpu as pltpu

F32 = jnp.float32
BF16 = jnp.bfloat16

D_MODEL = 1024
N_HEADS = 16
N_KV_HEADS = 4
HEAD_DIM = 64
GROUP = N_HEADS // N_KV_HEADS
KV_WIDTH = N_KV_HEADS * HEAD_DIM
AXIS_DIM = HEAD_DIM // 2
ROPE_THETA = 10000.0
GRID_W = 64
D_FF = 3584
N_EXPERTS = 8
EPS = 1e-6
LANES = 128
QKV_WIDTH = D_MODEL + 2 * KV_WIDTH
REST_WIDTH = 5 * D_MODEL
Q_EXP_WIDTH = N_HEADS * LANES
KV_EXP_WIDTH = N_KV_HEADS * LANES
Q_SCALE = math.log2(math.e) / math.sqrt(HEAD_DIM)
ROW_UNROLL = 8
MAX_SHIFT_GAP = 100.0

TM = 512
TM_FFN = 1024
TF = 512
TR = 512
TM_MOE = 512
VMEM_LIMIT = 56 * 1024 * 1024


def _params(sem):
    return pltpu.CompilerParams(dimension_semantics=sem, vmem_limit_bytes=VMEM_LIMIT)


def _rms(x, gain):
    return x * lax.rsqrt(jnp.mean(x * x, axis=-1, keepdims=True) + EPS) * gain


def _pick(n_a, a_ref, b_ref):
    return jnp.where(pl.program_id(0) < n_a, a_ref[...], b_ref[...])


def _qkv_kernel(xa_ref, xb_ref, g_ref, w_ref, qg_ref, kg_ref, cos_ref, sin_ref, bd_ref,
                q_ref, kt_ref, v_ref, *, n_a):
    tm = xa_ref.shape[0]
    h = _rms(_pick(n_a, xa_ref, xb_ref), g_ref[...]).astype(BF16)
    p = jnp.dot(h, w_ref[...], preferred_element_type=F32)
    cos = cos_ref[...]
    sin = sin_ref[...]
    bd = bd_ref[...]
    lane = lax.broadcasted_iota(jnp.int32, (tm, LANES), 1)
    first16 = (lane & 31) < 16
    low_half = lane < HEAD_DIM

    def norm_rope(c, gain):
        sq = c * c
        hi = sq.astype(BF16)
        lo = (sq - hi.astype(F32)).astype(BF16)
        ss = (jnp.dot(hi, bd, preferred_element_type=F32)
              + jnp.dot(lo, bd, preferred_element_type=F32))
        n = c * lax.rsqrt(ss * (1.0 / HEAD_DIM) + EPS) * gain
        partner = jnp.where(first16, pltpu.roll(n, LANES - 16, 1), pltpu.roll(n, 16, 1))
        return n * cos + partner * sin

    is_aux = lane == HEAD_DIM
    zero = jnp.zeros((tm, LANES), F32)
    ones_aux = jnp.where(is_aux, 1.0, zero)

    qg = qg_ref[...]
    for c in range(N_HEADS // 2):
        r = norm_rope(p[:, c * LANES:(c + 1) * LANES], qg) * Q_SCALE
        rr = r * r
        norm = jnp.sqrt(jnp.dot(rr.astype(BF16), bd, preferred_element_type=F32))
        even = jnp.where(low_half, r, jnp.where(is_aux, pltpu.roll(norm, HEAD_DIM, 1), zero))
        odd = jnp.where(low_half, pltpu.roll(r, HEAD_DIM, 1), jnp.where(is_aux, norm, zero))
        q_ref[:, (2 * c) * LANES:(2 * c + 1) * LANES] = even.astype(BF16)
        q_ref[:, (2 * c + 1) * LANES:(2 * c + 2) * LANES] = odd.astype(BF16)

    kg = kg_ref[...]
    for c in range(KV_WIDTH // LANES):
        kn = norm_rope(p[:, D_MODEL + c * LANES:D_MODEL + (c + 1) * LANES], kg)
        even = jnp.where(low_half, kn, ones_aux)
        odd = jnp.where(low_half, pltpu.roll(kn, HEAD_DIM, 1), ones_aux)
        kt_ref[(2 * c) * LANES:(2 * c + 1) * LANES, :] = even.T.astype(BF16)
        kt_ref[(2 * c + 1) * LANES:(2 * c + 2) * LANES, :] = odd.T.astype(BF16)
        vv = p[:, D_MODEL + KV_WIDTH + c * LANES:D_MODEL + KV_WIDTH + (c + 1) * LANES]
        even = jnp.where(low_half, vv, ones_aux)
        odd = jnp.where(low_half, pltpu.roll(vv, HEAD_DIM, 1), ones_aux)
        v_ref[:, (2 * c) * LANES:(2 * c + 1) * LANES] = even.astype(BF16)
        v_ref[:, (2 * c + 1) * LANES:(2 * c + 2) * LANES] = odd.astype(BF16)


def _rest_kernel(xa_ref, xb_ref, g_ref, w_ref, cb_ref, u_ref, ga_ref, gb_ref, *, n_a):
    h = _rms(_pick(n_a, xa_ref, xb_ref), g_ref[...]).astype(BF16)
    p = jnp.dot(h, w_ref[...], preferred_element_type=F32)
    cb_ref[...] = p[:, :D_MODEL].astype(BF16)
    u_ref[...] = (p[:, D_MODEL:2 * D_MODEL] * p[:, 2 * D_MODEL:3 * D_MODEL]).astype(BF16)
    ga_ref[...] = p[:, 3 * D_MODEL:4 * D_MODEL].astype(BF16)
    gb_ref[...] = p[:, 4 * D_MODEL:].astype(BF16)


def _attn_kernel(kmax_ref, q_ref, kt_ref, v_ref, o_ref, *, seq0, exact_max):
    b = pl.program_id(0)
    tq = q_ref.shape[0]
    lane4 = lax.broadcasted_iota(jnp.int32, (GROUP * tq, LANES), 1)
    is_aux4 = lane4 == HEAD_DIM
    low_half = lax.broadcasted_iota(jnp.int32, (tq, LANES), 1) < HEAD_DIM
    for j in range(N_KV_HEADS):
        kc = kt_ref[j * LANES:(j + 1) * LANES, :]
        vc = v_ref[:, j * LANES:(j + 1) * LANES]
        q4 = jnp.concatenate(
            [q_ref[:, h * LANES:(h + 1) * LANES] for h in range(GROUP * j, GROUP * (j + 1))],
            axis=0)
        if exact_max:
            q4 = jnp.where(is_aux4, jnp.zeros_like(q4), q4)
            s = jnp.dot(q4, kc, preferred_element_type=F32)
            s = s - jnp.max(s, axis=-1, keepdims=True)
        else:
            neg_kmax = -kmax_ref[(seq0 + b) * N_KV_HEADS + j]
            q4 = jnp.where(is_aux4, (q4.astype(F32) * neg_kmax).astype(BF16), q4)
            s = jnp.dot(q4, kc, preferred_element_type=F32)
        p = jnp.exp2(s).astype(BF16)
        o = jnp.dot(p, vc, preferred_element_type=F32)
        o = o / jnp.sum(jnp.where(is_aux4, o, 0.0), axis=-1, keepdims=True)
        for a in range(2):
            oe = o[(2 * a) * tq:(2 * a + 1) * tq]
            oo = o[(2 * a + 1) * tq:(2 * a + 2) * tq]
            chunk = jnp.where(low_half, oe, pltpu.roll(oo, HEAD_DIM, 1))
            c = 2 * j + a
            o_ref[:, c * LANES:(c + 1) * LANES] = chunk.astype(BF16)


def _mix_kernel(attn_a_ref, attn_b_ref, cb_ref, u_ref, up_ref, un_ref, ga_ref, gb_ref,
                xa_ref, xb_ref, cw_ref, woa_ref, wob_ref, wo_ref, g_ref, *rest,
                n_prompt_tiles, tiles4, tiles2, with_router):
    if with_router:
        rw_ref, xo_ref, h2_ref, lg_ref = rest
    else:
        xo_ref, h2_ref = rest
    i = pl.program_id(0)
    tm = xa_ref.shape[0]
    is_prompt = i < n_prompt_tiles
    seq_start = jnp.where(is_prompt, i % tiles4 == 0, i % tiles2 == 0)
    seq_end = jnp.where(is_prompt, i % tiles4 == tiles4 - 1, i % tiles2 == tiles2 - 1)

    attn = _pick(n_prompt_tiles, attn_a_ref, attn_b_ref)
    ya = jnp.dot(attn, woa_ref[...], preferred_element_type=F32)

    u = u_ref[...].astype(F32)
    row = lax.broadcasted_iota(jnp.int32, u.shape, 0)
    prev_row = jnp.where(seq_start, 0.0, up_ref[7:8, :].astype(F32))
    next_row = jnp.where(seq_end, 0.0, un_ref[0:1, :].astype(F32))
    u_prev = jnp.where(row == 0, prev_row, pltpu.roll(u, 1, 0))
    u_next = jnp.where(row == tm - 1, next_row, pltpu.roll(u, tm - 1, 0))
    cw = cw_ref[...]
    conv = cw[0:1, :] * u_prev + cw[1:2, :] * u + cw[2:3, :] * u_next
    yb_in = (cb_ref[...].astype(F32) * conv).astype(BF16)
    yb = jnp.dot(yb_in, wob_ref[...], preferred_element_type=F32)

    m = (jax.nn.sigmoid(ga_ref[...].astype(F32)) * ya
         + jax.nn.sigmoid(gb_ref[...].astype(F32)) * yb)
    xn = (_pick(n_prompt_tiles, xa_ref, xb_ref)
          + jnp.dot(m.astype(BF16), wo_ref[...], preferred_element_type=F32))
    xo_ref[...] = xn
    h2 = _rms(xn, g_ref[...])
    h2_ref[...] = h2.astype(h2_ref.dtype)
    if not with_router:
        return
    hi = h2.astype(BF16)
    lo = (h2 - hi.astype(F32)).astype(BF16)
    rw = rw_ref[...]
    rhi = rw.astype(BF16)
    rlo = (rw - rhi.astype(F32)).astype(BF16)
    lg_ref[...] = (jnp.dot(hi, rhi, preferred_element_type=F32)
                   + jnp.dot(hi, rlo, preferred_element_type=F32)
                   + jnp.dot(lo, rhi, preferred_element_type=F32))


def _ffn_kernel(h_ref, x_ref, wg_ref, wu_ref, wd_ref, o_ref, acc_ref):
    j = pl.program_id(1)

    @pl.when(j == 0)
    def _():
        acc_ref[...] = jnp.zeros_like(acc_ref)

    h = h_ref[...]
    g = jnp.dot(h, wg_ref[...], preferred_element_type=F32)
    u = jnp.dot(h, wu_ref[...], preferred_element_type=F32)
    a = (g * jax.nn.sigmoid(g) * u).astype(BF16)
    acc_ref[...] += jnp.dot(a, wd_ref[...], preferred_element_type=F32)

    @pl.when(j == pl.num_programs(1) - 1)
    def _():
        o_ref[...] = x_ref[...] + acc_ref[...]


ROUTE_E0, ROUTE_E1, ROUTE_R0, ROUTE_R1, ROUTE_W0, ROUTE_W1 = range(6)


def _lane_pick(x, lane, k):
    return jnp.sum(jnp.where(lane == k, x, 0.0), axis=-1, keepdims=True)


def _router_kernel(lg_ref, tri_ref, route_ref, count_ref, base_ref):
    i = pl.program_id(0)

    @pl.when(i == 0)
    def _():
        base_ref[...] = jnp.zeros_like(base_ref)

    lg = lg_ref[...]
    lane = lax.broadcasted_iota(jnp.int32, lg.shape, 1)
    neg = jnp.float32(-jnp.inf)
    l1 = jnp.where(lane < N_EXPERTS, lg, neg)
    m1 = jnp.max(l1, axis=-1, keepdims=True)
    i1 = jnp.min(jnp.where(l1 == m1, lane, LANES), axis=-1, keepdims=True)
    l2 = jnp.where(lane == i1, neg, l1)
    m2 = jnp.max(l2, axis=-1, keepdims=True)
    i2 = jnp.min(jnp.where(l2 == m2, lane, LANES), axis=-1, keepdims=True)
    e = jnp.exp(m2 - m1)
    w1 = 1.0 / (1.0 + e)
    w2 = e / (1.0 + e)

    hot1 = lane == i1
    hot2 = lane == i2
    onehot = jnp.where(hot1 | hot2, 1.0, 0.0)
    base = base_ref[0:1, :]
    prefix = jnp.dot(tri_ref[...], onehot.astype(BF16), preferred_element_type=F32) + base
    r1 = jnp.sum(jnp.where(hot1, prefix, 0.0), axis=-1, keepdims=True)
    r2 = jnp.sum(jnp.where(hot2, prefix, 0.0), axis=-1, keepdims=True)
    total = base + jnp.sum(onehot, axis=0, keepdims=True)
    base_ref[...] = jnp.broadcast_to(total, base_ref.shape)
    count_ref[...] = jnp.broadcast_to(total, count_ref.shape)

    rec = jnp.where(lane == ROUTE_E0, i1.astype(F32), 0.0)
    rec = jnp.where(lane == ROUTE_E1, i2.astype(F32), rec)
    rec = jnp.where(lane == ROUTE_R0, r1, rec)
    rec = jnp.where(lane == ROUTE_R1, r2, rec)
    rec = jnp.where(lane == ROUTE_W0, w1, rec)
    rec = jnp.where(lane == ROUTE_W1, w2, rec)
    route_ref[...] = rec


def _row_copy(src, src_row, dst, dst_row, sem):
    return pltpu.make_async_copy(src.at[pl.ds(src_row, 1)], dst.at[pl.ds(dst_row, 1)], sem)


def _dispatch_kernel(pos_ref, h_ref, xs_in_ref, xs_ref, sem):
    del xs_in_ref
    tm = h_ref.shape[0]

    def issue(i, carry):
        for k in range(ROW_UNROLL):
            r = i * ROW_UNROLL + k
            _row_copy(h_ref, r, xs_ref, pos_ref[0, 0, r], sem).start()
            _row_copy(h_ref, r, xs_ref, pos_ref[0, 0, tm + r], sem).start()
        return carry

    lax.fori_loop(0, tm // ROW_UNROLL, issue, 0)
    for _ in range(2):
        pltpu.make_async_copy(h_ref, xs_ref.at[pl.ds(0, tm)], sem).wait()


def _moe_ffn_kernel(te_ref, tv_ref, x_ref, wg_ref, wu_ref, wd_ref, y_ref, acc_ref):
    del te_ref
    g_idx = pl.program_id(0)
    j = pl.program_id(1)
    valid = tv_ref[g_idx] > 0

    @pl.when(j == 0)
    def _():
        acc_ref[...] = jnp.zeros_like(acc_ref)

    @pl.when(valid)
    def _():
        h = x_ref[...].astype(BF16)
        g = jnp.dot(h, wg_ref[0], preferred_element_type=F32)
        u = jnp.dot(h, wu_ref[0], preferred_element_type=F32)
        a = (g * jax.nn.sigmoid(g) * u).astype(BF16)
        acc_ref[...] += jnp.dot(a, wd_ref[0], preferred_element_type=F32)

    @pl.when(j == pl.num_programs(1) - 1)
    def _():
        y_ref[...] = acc_ref[...]


def _combine_kernel(pos_ref, x_ref, route_ref, fg_ref, y_hbm, oa_ref, ob_ref, ybuf, sem, *, n_a):
    tc = x_ref.shape[0]

    def issue(i, carry):
        for k in range(ROW_UNROLL):
            r = i * ROW_UNROLL + k
            _row_copy(y_hbm, pos_ref[0, 0, r], ybuf.at[0], r, sem).start()
            _row_copy(y_hbm, pos_ref[0, 0, tc + r], ybuf.at[1], r, sem).start()
        return carry

    lax.fori_loop(0, tc // ROW_UNROLL, issue, 0)
    for slot in range(2):
        pltpu.make_async_copy(y_hbm.at[pl.ds(0, tc)], ybuf.at[slot], sem).wait()
    route = route_ref[...]
    lane = lax.broadcasted_iota(jnp.int32, route.shape, 1)
    w0 = _lane_pick(route, lane, ROUTE_W0)
    w1 = _lane_pick(route, lane, ROUTE_W1)
    out = _rms(x_ref[...] + (w0 * ybuf[0] + w1 * ybuf[1]), fg_ref[...])
    in_a = pl.program_id(0) < n_a

    @pl.when(in_a)
    def _():
        oa_ref[...] = out

    @pl.when(jnp.logical_not(in_a))
    def _():
        ob_ref[...] = out


def _rope_tables(max_seq):
    t = jnp.arange(max_seq, dtype=jnp.int32)
    row = (t // GRID_W).astype(F32)
    col = (t % GRID_W).astype(F32)
    inv = 1.0 / (ROPE_THETA ** (jnp.arange(0, AXIS_DIM, 2, dtype=F32) / AXIS_DIM))
    ar = row[:, None] * inv[None, :]
    ac = col[:, None] * inv[None, :]
    cos64 = jnp.concatenate([jnp.cos(ar), jnp.cos(ar), jnp.cos(ac), jnp.cos(ac)], axis=-1)
    sin64 = jnp.concatenate([-jnp.sin(ar), jnp.sin(ar), -jnp.sin(ac), jnp.sin(ac)], axis=-1)
    return jnp.tile(cos64, (1, 2)), jnp.tile(sin64, (1, 2))


def _two_part_specs(block, n_a, parts):
    off = 0 if parts[0] is parts[1] else n_a
    return [pl.BlockSpec(block, lambda i: (jnp.minimum(i, n_a - 1), 0)),
            pl.BlockSpec(block, lambda i: (jnp.maximum(i, n_a) - off, 0))]


def _qkv_proj(xparts, t, n_a, gain, w, qg, kg, cos, sin, bd, pos_map):
    row = lambda i: (i, 0)
    const = lambda i: (0, 0)
    return pl.pallas_call(
        functools.partial(_qkv_kernel, n_a=n_a),
        grid=(t // TM,),
        in_specs=_two_part_specs((TM, D_MODEL), n_a, xparts) + [
                  pl.BlockSpec((1, D_MODEL), const),
                  pl.BlockSpec((D_MODEL, QKV_WIDTH), const),
                  pl.BlockSpec((1, LANES), const),
                  pl.BlockSpec((1, LANES), const),
                  pl.BlockSpec((TM, LANES), pos_map),
                  pl.BlockSpec((TM, LANES), pos_map),
                  pl.BlockSpec((LANES, LANES), const)],
        out_specs=[pl.BlockSpec((TM, Q_EXP_WIDTH), row),
                   pl.BlockSpec((KV_EXP_WIDTH, TM), lambda i: (0, i)),
                   pl.BlockSpec((TM, KV_EXP_WIDTH), row)],
        out_shape=[jax.ShapeDtypeStruct((t, Q_EXP_WIDTH), BF16),
                   jax.ShapeDtypeStruct((KV_EXP_WIDTH, t), BF16),
                   jax.ShapeDtypeStruct((t, KV_EXP_WIDTH), BF16)],
        compiler_params=_params(("parallel",)),
        name="qkv_proj",
    )(*xparts, gain, w, qg, kg, cos, sin, bd)


def _rest_proj(xparts, t, n_a, gain, w):
    row = lambda i: (i, 0)
    const = lambda i: (0, 0)
    out = jax.ShapeDtypeStruct((t, D_MODEL), BF16)
    return pl.pallas_call(
        functools.partial(_rest_kernel, n_a=n_a),
        grid=(t // TM,),
        in_specs=_two_part_specs((TM, D_MODEL), n_a, xparts) + [
                  pl.BlockSpec((1, D_MODEL), const),
                  pl.BlockSpec((D_MODEL, REST_WIDTH), const)],
        out_specs=[pl.BlockSpec((TM, D_MODEL), row)] * 4,
        out_shape=[out] * 4,
        compiler_params=_params(("parallel",)),
        name="rest_proj",
    )(*xparts, gain, w)


def _attention(kmax, q, kt, v, *, row0, seq0, n_seq, seq, tq, exact_max):
    nq = seq // tq
    q0 = row0 // tq
    s0 = row0 // seq
    kern = functools.partial(_attn_kernel, seq0=seq0, exact_max=exact_max)
    return pl.pallas_call(
        kern,
        grid_spec=pltpu.PrefetchScalarGridSpec(
            num_scalar_prefetch=1,
            grid=(n_seq, nq),
            in_specs=[pl.BlockSpec((tq, Q_EXP_WIDTH), lambda b, i, km: (q0 + b * nq + i, 0)),
                      pl.BlockSpec((KV_EXP_WIDTH, seq), lambda b, i, km: (0, s0 + b)),
                      pl.BlockSpec((seq, KV_EXP_WIDTH), lambda b, i, km: (s0 + b, 0))],
            out_specs=pl.BlockSpec((tq, D_MODEL), lambda b, i, km: (b * nq + i, 0))),
        out_shape=jax.ShapeDtypeStruct((n_seq * seq, D_MODEL), BF16),
        compiler_params=_params(("parallel", "parallel")),
        name=f"attention_s{seq}" + ("_exact" if exact_max else ""),
    )(kmax, q, kt, v)


def _key_norm_max(kt, n_prompt, bp, sp, bs, ss):
    k = kt.reshape(N_KV_HEADS, LANES, -1)[:, :HEAD_DIM, :].astype(F32)
    norm = jnp.sqrt(jnp.sum(k * k, axis=1))
    kp = jnp.max(norm[:, :n_prompt].reshape(N_KV_HEADS, bp, sp), axis=-1)
    ks = jnp.max(norm[:, n_prompt:].reshape(N_KV_HEADS, bs, ss), axis=-1)
    return jnp.concatenate([kp.T.reshape(-1), ks.T.reshape(-1)])


def _mix(attn_parts, cb, u, ga, gb, xparts, cw, woa, wob, wo, gain, rw, *, n_prompt, seq_p, seq_s):
    t = cb.shape[0]
    n_a = n_prompt // TM
    row = lambda i: (i, 0)
    const = lambda i: (0, 0)
    sub = TM // 8
    last = t // 8 - 1
    act = pl.BlockSpec((TM, D_MODEL), row)
    wspec = pl.BlockSpec((D_MODEL, D_MODEL), const)
    with_router = rw is not None
    kern = functools.partial(_mix_kernel, n_prompt_tiles=n_prompt // TM,
                             tiles4=seq_p // TM, tiles2=seq_s // TM, with_router=with_router)
    in_specs = (_two_part_specs((TM, D_MODEL), n_a, attn_parts) + [
                act, act,
                pl.BlockSpec((8, D_MODEL), lambda i: (jnp.maximum(i * sub - 1, 0), 0)),
                pl.BlockSpec((8, D_MODEL), lambda i: (jnp.minimum((i + 1) * sub, last), 0)),
                act, act]
                + _two_part_specs((TM, D_MODEL), n_a, xparts) + [
                pl.BlockSpec((3, D_MODEL), const),
                wspec, wspec, wspec,
                pl.BlockSpec((1, D_MODEL), const)])
    args = [*attn_parts, cb, u, u, u, ga, gb, *xparts, cw, woa, wob, wo, gain]
    out_specs = [act, act]
    out_shape = [jax.ShapeDtypeStruct((t, D_MODEL), F32),
                 jax.ShapeDtypeStruct((t, D_MODEL), F32 if with_router else BF16)]
    if with_router:
        in_specs.append(pl.BlockSpec((D_MODEL, LANES), const))
        args.append(rw)
        out_specs.append(pl.BlockSpec((TM, LANES), row))
        out_shape.append(jax.ShapeDtypeStruct((t, LANES), F32))
    return pl.pallas_call(
        kern,
        grid=(t // TM,),
        in_specs=in_specs,
        out_specs=out_specs,
        out_shape=out_shape,
        compiler_params=_params(("parallel",)),
        name="mix_proj_router" if with_router else "mix_proj",
    )(*args)


def _ffn(h, x, wg, wu, wd):
    t = x.shape[0]
    row = lambda i, j: (i, 0)
    return pl.pallas_call(
        _ffn_kernel,
        grid=(t // TM_FFN, D_FF // TF),
        in_specs=[pl.BlockSpec((TM_FFN, D_MODEL), row),
                  pl.BlockSpec((TM_FFN, D_MODEL), row),
                  pl.BlockSpec((D_MODEL, TF), lambda i, j: (0, j)),
                  pl.BlockSpec((D_MODEL, TF), lambda i, j: (0, j)),
                  pl.BlockSpec((TF, D_MODEL), lambda i, j: (j, 0))],
        out_specs=pl.BlockSpec((TM_FFN, D_MODEL), row),
        out_shape=jax.ShapeDtypeStruct((t, D_MODEL), F32),
        scratch_shapes=[pltpu.VMEM((TM_FFN, D_MODEL), F32)],
        compiler_params=_params(("parallel", "arbitrary")),
        name="ffn_dense",
    )(h, x, wg, wu, wd)


def _router(logits):
    t = logits.shape[0]
    row = lambda i: (i, 0)
    const = lambda i: (0, 0)
    r = jnp.arange(TR)
    tri = (r[None, :] < r[:, None]).astype(BF16)
    return pl.pallas_call(
        _router_kernel,
        grid=(t // TR,),
        in_specs=[pl.BlockSpec((TR, LANES), row),
                  pl.BlockSpec((TR, TR), const)],
        out_specs=[pl.BlockSpec((TR, LANES), row),
                   pl.BlockSpec((8, LANES), const)],
        out_shape=[jax.ShapeDtypeStruct((t, LANES), F32),
                   jax.ShapeDtypeStruct((8, LANES), F32)],
        scratch_shapes=[pltpu.VMEM((8, LANES), F32)],
        compiler_params=_params(("arbitrary",)),
        name="router",
    )(logits, tri)


def _route_plan(route, counts, t):
    e0 = route[:, ROUTE_E0].astype(jnp.int32)
    e1 = route[:, ROUTE_E1].astype(jnp.int32)
    r0 = route[:, ROUTE_R0].astype(jnp.int32)
    r1 = route[:, ROUTE_R1].astype(jnp.int32)
    cnt = counts[0, :N_EXPERTS].astype(jnp.int32)
    padded = ((cnt + TM_MOE - 1) // TM_MOE) * TM_MOE
    ends = jnp.cumsum(padded)
    starts = ends - padded
    experts = jnp.arange(N_EXPERTS, dtype=jnp.int32)
    pos0 = jnp.sum(jnp.where(e0[:, None] == experts[None, :], starts[None, :], 0), axis=1) + r0
    pos1 = jnp.sum(jnp.where(e1[:, None] == experts[None, :], starts[None, :], 0), axis=1) + r1
    n_tok_tiles = t // TM_MOE
    pos = jnp.concatenate([pos0.reshape(n_tok_tiles, 1, TM_MOE),
                           pos1.reshape(n_tok_tiles, 1, TM_MOE)], axis=2)
    n_tiles = 2 * t // TM_MOE + N_EXPERTS
    tile_start = jnp.arange(n_tiles, dtype=jnp.int32) * TM_MOE
    tile_valid = (tile_start < ends[-1]).astype(jnp.int32)
    tile_expert = jnp.sum((tile_start[:, None] >= ends[None, :]).astype(jnp.int32), axis=1)
    tile_expert = jnp.minimum(tile_expert, N_EXPERTS - 1)
    return pos, tile_expert, tile_valid, n_tiles


def _dispatch(pos, h, n_rows):
    t = h.shape[0]
    zeros = jnp.zeros((n_rows, D_MODEL), F32)
    return pl.pallas_call(
        _dispatch_kernel,
        grid=(t // TM_MOE,),
        in_specs=[pl.BlockSpec((1, 1, 2 * TM_MOE), lambda i: (i, 0, 0), memory_space=pltpu.SMEM),
                  pl.BlockSpec((TM_MOE, D_MODEL), lambda i: (i, 0)),
                  pl.BlockSpec(memory_space=pl.ANY)],
        out_specs=pl.BlockSpec(memory_space=pl.ANY),
        out_shape=jax.ShapeDtypeStruct((n_rows, D_MODEL), F32),
        scratch_shapes=[pltpu.SemaphoreType.DMA(())],
        input_output_aliases={2: 0},
        compiler_params=_params(("arbitrary",)),
        name="moe_dispatch",
    )(pos, h, zeros)


def _moe_ffn(tile_expert, tile_valid, xs, wg, wu, wd):
    n_rows = xs.shape[0]
    row = lambda g, j, te, tv: (g, 0)
    return pl.pallas_call(
        _moe_ffn_kernel,
        grid_spec=pltpu.PrefetchScalarGridSpec(
            num_scalar_prefetch=2,
            grid=(n_rows // TM_MOE, D_FF // TF),
            in_specs=[pl.BlockSpec((TM_MOE, D_MODEL), row),
                      pl.BlockSpec((1, D_MODEL, TF), lambda g, j, te, tv: (te[g], 0, j * tv[g])),
                      pl.BlockSpec((1, D_MODEL, TF), lambda g, j, te, tv: (te[g], 0, j * tv[g])),
                      pl.BlockSpec((1, TF, D_MODEL), lambda g, j, te, tv: (te[g], j * tv[g], 0))],
            out_specs=pl.BlockSpec((TM_MOE, D_MODEL), row),
            scratch_shapes=[pltpu.VMEM((TM_MOE, D_MODEL), F32)]),
        out_shape=jax.ShapeDtypeStruct((n_rows, D_MODEL), F32),
        compiler_params=_params(("parallel", "arbitrary")),
        name="moe_ffn",
    )(tile_expert, tile_valid, xs, wg, wu, wd)


def _combine(pos, x, route, fgain, ys, n_first):
    t = x.shape[0]
    n_a = n_first // TM_MOE
    row = lambda i: (i, 0)
    blk = (TM_MOE, D_MODEL)
    return pl.pallas_call(
        functools.partial(_combine_kernel, n_a=n_a),
        grid=(t // TM_MOE,),
        in_specs=[pl.BlockSpec((1, 1, 2 * TM_MOE), lambda i: (i, 0, 0), memory_space=pltpu.SMEM),
                  pl.BlockSpec(blk, row),
                  pl.BlockSpec((TM_MOE, LANES), row),
                  pl.BlockSpec((1, D_MODEL), lambda i: (0, 0)),
                  pl.BlockSpec(memory_space=pl.ANY)],
        out_specs=[pl.BlockSpec(blk, lambda i: (jnp.minimum(i, n_a - 1), 0)),
                   pl.BlockSpec(blk, lambda i: (jnp.maximum(i - n_a, 0), 0))],
        out_shape=[jax.ShapeDtypeStruct((n_first, D_MODEL), F32),
                   jax.ShapeDtypeStruct((t - n_first, D_MODEL), F32)],
        scratch_shapes=[pltpu.VMEM((2, TM_MOE, D_MODEL), F32),
                        pltpu.SemaphoreType.DMA(())],
        compiler_params=_params(("arbitrary",)),
        name="moe_combine",
    )(pos, x, route, fgain, ys)


def kernel(x_prompt, x_sample, norm_mix, w_in, q_norm, k_norm, conv_w, w_oa, w_ob, w_o, norm_ffn,
           ffn_w_gate, ffn_w_up, ffn_w_down, router_w, moe_w_gate, moe_w_up, moe_w_down, final_norm):
    bp, sp, _ = x_prompt.shape
    bs, ss, _ = x_sample.shape
    n_prompt = bp * sp
    n_sample = bs * ss
    depth = norm_mix.shape[0]
    assert sp % TM == 0 and ss % TM == 0 and n_prompt % TM_FFN == 0 and n_sample % TM_FFN == 0
    assert depth == 2 and ffn_w_gate.shape[0] == 1 and moe_w_gate.shape[0] == 1

    t_all = n_prompt + n_sample
    xparts = (x_prompt.reshape(n_prompt, D_MODEL), x_sample.reshape(n_sample, D_MODEL))

    cos, sin = _rope_tables(max(sp, ss))
    n_prompt_tiles, tiles4, tiles2 = n_prompt // TM, sp // TM, ss // TM
    pos_map = lambda i: (jnp.where(i < n_prompt_tiles, i % tiles4, i % tiles2), 0)
    idx = jnp.arange(LANES)
    bd = (idx[:, None] // HEAD_DIM == idx[None, :] // HEAD_DIM).astype(BF16)

    for l in range(depth):
        gain = norm_mix[l][None, :]
        w_l = w_in[l]
        w_qkv = w_l[:, :QKV_WIDTH].astype(BF16)
        w_rest = w_l[:, QKV_WIDTH:].astype(BF16)
        qg = jnp.tile(q_norm[l], 2)[None, :]
        kg = jnp.tile(k_norm[l], 2)[None, :]
        q, kt, v = _qkv_proj(xparts, t_all, n_prompt_tiles, gain, w_qkv, qg, kg, cos, sin, bd,
                             pos_map)
        cb, u, ga, gb = _rest_proj(xparts, t_all, n_prompt_tiles, gain, w_rest)
        kmax = _key_norm_max(kt, n_prompt, bp, sp, bs, ss)
        q_bound = Q_SCALE * math.sqrt(HEAD_DIM) * jnp.max(jnp.abs(q_norm[l]))
        bound_ok = 2.0 * q_bound * jnp.max(kmax) < MAX_SHIFT_GAP

        def attend(exact_max, tq_p, tq_s):
            def run(kmax, q, kt, v):
                attn_p = _attention(kmax, q, kt, v, row0=0, seq0=0, n_seq=bp, seq=sp,
                                    tq=tq_p, exact_max=exact_max)
                attn_s = _attention(kmax, q, kt, v, row0=n_prompt, seq0=bp, n_seq=bs, seq=ss,
                                    tq=tq_s, exact_max=exact_max)
                return attn_p, attn_s
            return run

        attn_parts = lax.cond(bound_ok, attend(False, 256, 256), attend(True, 128, 256),
                              kmax, q, kt, v)
        j = l // 2
        is_moe = l % 2 == 1
        rw = jnp.pad(router_w[j], ((0, 0), (0, LANES - N_EXPERTS))) if is_moe else None
        outs = _mix(attn_parts, cb, u, ga, gb, xparts, conv_w[l],
                    w_oa[l].astype(BF16), w_ob[l].astype(BF16), w_o[l].astype(BF16),
                    norm_ffn[l][None, :], rw, n_prompt=n_prompt, seq_p=sp, seq_s=ss)
        if not is_moe:
            x, h2 = outs
            x = _ffn(h2, x, ffn_w_gate[j].astype(BF16), ffn_w_up[j].astype(BF16),
                     ffn_w_down[j].astype(BF16))
            xparts = (x, x)
        else:
            x, h2, logits = outs
            route, counts = _router(logits)
            pos, tile_expert, tile_valid, n_tiles = _route_plan(route, counts, t_all)
            xs = _dispatch(pos, h2, n_tiles * TM_MOE)
            ys = _moe_ffn(tile_expert, tile_valid, xs, moe_w_gate[j].astype(BF16),
                          moe_w_up[j].astype(BF16), moe_w_down[j].astype(BF16))
            xparts = _combine(pos, x, route, final_norm[None, :], ys, n_prompt)

    return (xparts[0].reshape(bp, sp, D_MODEL), xparts[1].reshape(bs, ss, D_MODEL))
```

```python
import functools
import math

import jax
import jax.numpy as jnp
from jax import lax
from jax.experimental import pallas as pl
from jax.experimental.pallas import tpu as pltpu

F32 = jnp.float32
BF16 = jnp.bfloat16

D_MODEL = 1024
N_HEADS = 16
N_KV_HEADS = 4
HEAD_DIM = 64
GROUP = N_HEADS // N_KV_HEADS
KV_WIDTH = N_KV_HEADS * HEAD_DIM
AXIS_DIM = HEAD_DIM // 2
ROPE_THETA = 10000.0
GRID_W = 64
D_FF = 3584
N_EXPERTS = 8
EPS = 1e-6
LANES = 128
QKV_WIDTH = D_MODEL + 2 * KV_WIDTH
REST_WIDTH = 5 * D_MODEL
Q_EXP_WIDTH = N_HEADS * LANES
KV_EXP_WIDTH = N_KV_HEADS * LANES
Q_SCALE = math.log2(math.e) / math.sqrt(HEAD_DIM)
ROW_UNROLL = 8
MAX_SHIFT_GAP = 100.0

TM = 512
TM_FFN = 512
TF = 1792
TF_MOE = 1792
TR = 512
TM_MOE = 512
VMEM_LIMIT = 56 * 1024 * 1024


def _params(sem):
    return pltpu.CompilerParams(dimension_semantics=sem, vmem_limit_bytes=VMEM_LIMIT)


def _rms(x, gain):
    return x * lax.rsqrt(jnp.mean(x * x, axis=-1, keepdims=True) + EPS) * gain


def _pick(n_a, a_ref, b_ref):
    return jnp.where(pl.program_id(0) < n_a, a_ref[...], b_ref[...])


def _qkv_kernel(xa_ref, xb_ref, g_ref, w_ref, qg_ref, kg_ref, cos_ref, sin_ref, bd_ref,
                q_ref, kt_ref, v_ref, *, n_a):
    tm = xa_ref.shape[0]
    h = _rms(_pick(n_a, xa_ref, xb_ref), g_ref[...]).astype(BF16)
    p = jnp.dot(h, w_ref[...], preferred_element_type=F32)
    cos = cos_ref[...]
    sin = sin_ref[...]
    bd = bd_ref[...]
    lane = lax.broadcasted_iota(jnp.int32, (tm, LANES), 1)
    first16 = (lane & 31) < 16
    low_half = lane < HEAD_DIM

    def norm_rope(c, gain):
        sq = c * c
        hi = sq.astype(BF16)
        lo = (sq - hi.astype(F32)).astype(BF16)
        ss = (jnp.dot(hi, bd, preferred_element_type=F32)
              + jnp.dot(lo, bd, preferred_element_type=F32))
        n = c * lax.rsqrt(ss * (1.0 / HEAD_DIM) + EPS) * gain
        partner = jnp.where(first16, pltpu.roll(n, LANES - 16, 1), pltpu.roll(n, 16, 1))
        return n * cos + partner * sin

    is_aux = lane == HEAD_DIM
    zero = jnp.zeros((tm, LANES), F32)
    ones_aux = jnp.where(is_aux, 1.0, zero)

    qg = qg_ref[...]
    for c in range(N_HEADS // 2):
        r = norm_rope(p[:, c * LANES:(c + 1) * LANES], qg) * Q_SCALE
        rr = r * r
        norm = jnp.sqrt(jnp.dot(rr.astype(BF16), bd, preferred_element_type=F32))
        even = jnp.where(low_half, r, jnp.where(is_aux, pltpu.roll(norm, HEAD_DIM, 1), zero))
        odd = jnp.where(low_half, pltpu.roll(r, HEAD_DIM, 1), jnp.where(is_aux, norm, zero))
        q_ref[:, (2 * c) * LANES:(2 * c + 1) * LANES] = even.astype(BF16)
        q_ref[:, (2 * c + 1) * LANES:(2 * c + 2) * LANES] = odd.astype(BF16)

    kg = kg_ref[...]
    for c in range(KV_WIDTH // LANES):
        kn = norm_rope(p[:, D_MODEL + c * LANES:D_MODEL + (c + 1) * LANES], kg)
        even = jnp.where(low_half, kn, ones_aux)
        odd = jnp.where(low_half, pltpu.roll(kn, HEAD_DIM, 1), ones_aux)
        kt_ref[(2 * c) * LANES:(2 * c + 1) * LANES, :] = even.T.astype(BF16)
        kt_ref[(2 * c + 1) * LANES:(2 * c + 2) * LANES, :] = odd.T.astype(BF16)
        vv = p[:, D_MODEL + KV_WIDTH + c * LANES:D_MODEL + KV_WIDTH + (c + 1) * LANES]
        even = jnp.where(low_half, vv, ones_aux)
        odd = jnp.where(low_half, pltpu.roll(vv, HEAD_DIM, 1), ones_aux)
        v_ref[:, (2 * c) * LANES:(2 * c + 1) * LANES] = even.astype(BF16)
        v_ref[:, (2 * c + 1) * LANES:(2 * c + 2) * LANES] = odd.astype(BF16)


def _rest_kernel(xa_ref, xb_ref, g_ref, w_ref, cb_ref, u_ref, ga_ref, gb_ref, *, n_a):
    h = _rms(_pick(n_a, xa_ref, xb_ref), g_ref[...]).astype(BF16)
    p = jnp.dot(h, w_ref[...], preferred_element_type=F32)
    cb_ref[...] = p[:, :D_MODEL].astype(BF16)
    u_ref[...] = (p[:, D_MODEL:2 * D_MODEL] * p[:, 2 * D_MODEL:3 * D_MODEL]).astype(BF16)
    ga_ref[...] = p[:, 3 * D_MODEL:4 * D_MODEL].astype(BF16)
    gb_ref[...] = p[:, 4 * D_MODEL:].astype(BF16)


def _attn_kernel(kmax_ref, q_ref, kt_ref, v_ref, o_ref, *, seq0, exact_max):
    b = pl.program_id(0)
    tq = q_ref.shape[0]
    lane4 = lax.broadcasted_iota(jnp.int32, (GROUP * tq, LANES), 1)
    is_aux4 = lane4 == HEAD_DIM
    low_half = lax.broadcasted_iota(jnp.int32, (tq, LANES), 1) < HEAD_DIM
    for j in range(N_KV_HEADS):
        kc = kt_ref[j * LANES:(j + 1) * LANES, :]
        vc = v_ref[:, j * LANES:(j + 1) * LANES]
        q4 = jnp.concatenate(
            [q_ref[:, h * LANES:(h + 1) * LANES] for h in range(GROUP * j, GROUP * (j + 1))],
            axis=0)
        if exact_max:
            q4 = jnp.where(is_aux4, jnp.zeros_like(q4), q4)
            s = jnp.dot(q4, kc, preferred_element_type=F32)
            s = s - jnp.max(s, axis=-1, keepdims=True)
        else:
            neg_kmax = -kmax_ref[(seq0 + b) * N_KV_HEADS + j]
            q4 = jnp.where(is_aux4, (q4.astype(F32) * neg_kmax).astype(BF16), q4)
            s = jnp.dot(q4, kc, preferred_element_type=F32)
        p = jnp.exp2(s).astype(BF16)
        o = jnp.dot(p, vc, preferred_element_type=F32)
        o = o / jnp.sum(jnp.where(is_aux4, o, 0.0), axis=-1, keepdims=True)
        for a in range(2):
            oe = o[(2 * a) * tq:(2 * a + 1) * tq]
            oo = o[(2 * a + 1) * tq:(2 * a + 2) * tq]
            chunk = jnp.where(low_half, oe, pltpu.roll(oo, HEAD_DIM, 1))
            c = 2 * j + a
            o_ref[:, c * LANES:(c + 1) * LANES] = chunk.astype(BF16)


def _mix_kernel(attn_a_ref, attn_b_ref, cb_ref, u_ref, up_ref, un_ref, ga_ref, gb_ref,
                xa_ref, xb_ref, cw_ref, woa_ref, wob_ref, wo_ref, g_ref, *rest,
                n_prompt_tiles, tiles4, tiles2, with_router):
    if with_router:
        rw_ref, xo_ref, h2_ref, lg_ref = rest
    else:
        xo_ref, h2_ref = rest
    i = pl.program_id(0)
    tm = xa_ref.shape[0]
    is_prompt = i < n_prompt_tiles
    seq_start = jnp.where(is_prompt, i % tiles4 == 0, i % tiles2 == 0)
    seq_end = jnp.where(is_prompt, i % tiles4 == tiles4 - 1, i % tiles2 == tiles2 - 1)

    attn = _pick(n_prompt_tiles, attn_a_ref, attn_b_ref)
    ya = jnp.dot(attn, woa_ref[...], preferred_element_type=F32)

    u = u_ref[...].astype(F32)
    row = lax.broadcasted_iota(jnp.int32, u.shape, 0)
    prev_row = jnp.where(seq_start, 0.0, up_ref[7:8, :].astype(F32))
    next_row = jnp.where(seq_end, 0.0, un_ref[0:1, :].astype(F32))
    u_prev = jnp.where(row == 0, prev_row, pltpu.roll(u, 1, 0))
    u_next = jnp.where(row == tm - 1, next_row, pltpu.roll(u, tm - 1, 0))
    cw = cw_ref[...]
    conv = cw[0:1, :] * u_prev + cw[1:2, :] * u + cw[2:3, :] * u_next
    yb_in = (cb_ref[...].astype(F32) * conv).astype(BF16)
    yb = jnp.dot(yb_in, wob_ref[...], preferred_element_type=F32)

    m = (jax.nn.sigmoid(ga_ref[...].astype(F32)) * ya
         + jax.nn.sigmoid(gb_ref[...].astype(F32)) * yb)
    xn = (_pick(n_prompt_tiles, xa_ref, xb_ref)
          + jnp.dot(m.astype(BF16), wo_ref[...], preferred_element_type=F32))
    xo_ref[...] = xn
    h2 = _rms(xn, g_ref[...])
    h2_ref[...] = h2.astype(h2_ref.dtype)
    if not with_router:
        return
    hi = h2.astype(BF16)
    lo = (h2 - hi.astype(F32)).astype(BF16)
    rw = rw_ref[...]
    rhi = rw.astype(BF16)
    rlo = (rw - rhi.astype(F32)).astype(BF16)
    lg_ref[...] = (jnp.dot(hi, rhi, preferred_element_type=F32)
                   + jnp.dot(hi, rlo, preferred_element_type=F32)
                   + jnp.dot(lo, rhi, preferred_element_type=F32))


def _ffn_kernel(h_ref, x_ref, wg_ref, wu_ref, wd_ref, o_ref, acc_ref):
    j = pl.program_id(1)

    @pl.when(j == 0)
    def _():
        acc_ref[...] = jnp.zeros_like(acc_ref)

    h = h_ref[...]
    g = jnp.dot(h, wg_ref[...], preferred_element_type=F32)
    u = jnp.dot(h, wu_ref[...], preferred_element_type=F32)
    a = (g * jax.nn.sigmoid(g) * u).astype(BF16)
    acc_ref[...] += jnp.dot(a, wd_ref[...], preferred_element_type=F32)

    @pl.when(j == pl.num_programs(1) - 1)
    def _():
        o_ref[...] = x_ref[...] + acc_ref[...]


ROUTE_E0, ROUTE_E1, ROUTE_R0, ROUTE_R1, ROUTE_W0, ROUTE_W1 = range(6)


def _lane_pick(x, lane, k):
    return jnp.sum(jnp.where(lane == k, x, 0.0), axis=-1, keepdims=True)


def _router_kernel(lg_ref, tri_ref, route_ref, count_ref, base_ref):
    i = pl.program_id(0)

    @pl.when(i == 0)
    def _():
        base_ref[...] = jnp.zeros_like(base_ref)

    lg = lg_ref[...]
    lane = lax.broadcasted_iota(jnp.int32, lg.shape, 1)
    neg = jnp.float32(-jnp.inf)
    l1 = jnp.where(lane < N_EXPERTS, lg, neg)
    m1 = jnp.max(l1, axis=-1, keepdims=True)
    i1 = jnp.min(jnp.where(l1 == m1, lane, LANES), axis=-1, keepdims=True)
    l2 = jnp.where(lane == i1, neg, l1)
    m2 = jnp.max(l2, axis=-1, keepdims=True)
    i2 = jnp.min(jnp.where(l2 == m2, lane, LANES), axis=-1, keepdims=True)
    e = jnp.exp(m2 - m1)
    w1 = 1.0 / (1.0 + e)
    w2 = e / (1.0 + e)

    hot1 = lane == i1
    hot2 = lane == i2
    onehot = jnp.where(hot1 | hot2, 1.0, 0.0)
    base = base_ref[0:1, :]
    prefix = jnp.dot(tri_ref[...], onehot.astype(BF16), preferred_element_type=F32) + base
    r1 = jnp.sum(jnp.where(hot1, prefix, 0.0), axis=-1, keepdims=True)
    r2 = jnp.sum(jnp.where(hot2, prefix, 0.0), axis=-1, keepdims=True)
    total = base + jnp.sum(onehot, axis=0, keepdims=True)
    base_ref[...] = jnp.broadcast_to(total, base_ref.shape)
    count_ref[...] = jnp.broadcast_to(total, count_ref.shape)

    rec = jnp.where(lane == ROUTE_E0, i1.astype(F32), 0.0)
    rec = jnp.where(lane == ROUTE_E1, i2.astype(F32), rec)
    rec = jnp.where(lane == ROUTE_R0, r1, rec)
    rec = jnp.where(lane == ROUTE_R1, r2, rec)
    rec = jnp.where(lane == ROUTE_W0, w1, rec)
    rec = jnp.where(lane == ROUTE_W1, w2, rec)
    route_ref[...] = rec


def _row_copy(src, src_row, dst, dst_row, sem):
    return pltpu.make_async_copy(src.at[pl.ds(src_row, 1)], dst.at[pl.ds(dst_row, 1)], sem)


def _dispatch_kernel(pos_ref, h_ref, xs_in_ref, xs_ref, sem):
    del xs_in_ref
    tm = h_ref.shape[0]

    def issue(i, carry):
        for k in range(ROW_UNROLL):
            r = i * ROW_UNROLL + k
            _row_copy(h_ref, r, xs_ref, pos_ref[0, 0, r], sem).start()
            _row_copy(h_ref, r, xs_ref, pos_ref[0, 0, tm + r], sem).start()
        return carry

    lax.fori_loop(0, tm // ROW_UNROLL, issue, 0)
    for _ in range(2):
        pltpu.make_async_copy(h_ref, xs_ref.at[pl.ds(0, tm)], sem).wait()


def _moe_ffn_kernel(te_ref, tv_ref, x_ref, wg_ref, wu_ref, wd_ref, y_ref, acc_ref):
    del te_ref
    g_idx = pl.program_id(0)
    j = pl.program_id(1)
    valid = tv_ref[g_idx] > 0

    @pl.when(j == 0)
    def _():
        acc_ref[...] = jnp.zeros_like(acc_ref)

    @pl.when(valid)
    def _():
        h = x_ref[...].astype(BF16)
        g = jnp.dot(h, wg_ref[0], preferred_element_type=F32)
        u = jnp.dot(h, wu_ref[0], preferred_element_type=F32)
        a = (g * jax.nn.sigmoid(g) * u).astype(BF16)
        acc_ref[...] += jnp.dot(a, wd_ref[0], preferred_element_type=F32)

    @pl.when(j == pl.num_programs(1) - 1)
    def _():
        y_ref[...] = acc_ref[...]


def _combine_kernel(pos_ref, x_ref, route_ref, fg_ref, y_hbm, oa_ref, ob_ref, ybuf, sem, *, n_a):
    tc = x_ref.shape[0]

    def issue(i, carry):
        for k in range(ROW_UNROLL):
            r = i * ROW_UNROLL + k
            _row_copy(y_hbm, pos_ref[0, 0, r], ybuf.at[0], r, sem).start()
            _row_copy(y_hbm, pos_ref[0, 0, tc + r], ybuf.at[1], r, sem).start()
        return carry

    lax.fori_loop(0, tc // ROW_UNROLL, issue, 0)
    for slot in range(2):
        pltpu.make_async_copy(y_hbm.at[pl.ds(0, tc)], ybuf.at[slot], sem).wait()
    route = route_ref[...]
    lane = lax.broadcasted_iota(jnp.int32, route.shape, 1)
    w0 = _lane_pick(route, lane, ROUTE_W0)
    w1 = _lane_pick(route, lane, ROUTE_W1)
    out = _rms(x_ref[...] + (w0 * ybuf[0] + w1 * ybuf[1]), fg_ref[...])
    in_a = pl.program_id(0) < n_a

    @pl.when(in_a)
    def _():
        oa_ref[...] = out

    @pl.when(jnp.logical_not(in_a))
    def _():
        ob_ref[...] = out


def _rope_tables(max_seq):
    t = jnp.arange(max_seq, dtype=jnp.int32)
    row = (t // GRID_W).astype(F32)
    col = (t % GRID_W).astype(F32)
    inv = 1.0 / (ROPE_THETA ** (jnp.arange(0, AXIS_DIM, 2, dtype=F32) / AXIS_DIM))
    ar = row[:, None] * inv[None, :]
    ac = col[:, None] * inv[None, :]
    cos64 = jnp.concatenate([jnp.cos(ar), jnp.cos(ar), jnp.cos(ac), jnp.cos(ac)], axis=-1)
    sin64 = jnp.concatenate([-jnp.sin(ar), jnp.sin(ar), -jnp.sin(ac), jnp.sin(ac)], axis=-1)
    return jnp.tile(cos64, (1, 2)), jnp.tile(sin64, (1, 2))


def _two_part_specs(block, n_a, parts):
    off = 0 if parts[0] is parts[1] else n_a
    return [pl.BlockSpec(block, lambda i: (jnp.minimum(i, n_a - 1), 0)),
            pl.BlockSpec(block, lambda i: (jnp.maximum(i, n_a) - off, 0))]


def _qkv_proj(xparts, t, n_a, gain, w, qg, kg, cos, sin, bd, pos_map):
    row = lambda i: (i, 0)
    const = lambda i: (0, 0)
    return pl.pallas_call(
        functools.partial(_qkv_kernel, n_a=n_a),
        grid=(t // TM,),
        in_specs=_two_part_specs((TM, D_MODEL), n_a, xparts) + [
                  pl.BlockSpec((1, D_MODEL), const),
                  pl.BlockSpec((D_MODEL, QKV_WIDTH), const),
                  pl.BlockSpec((1, LANES), const),
                  pl.BlockSpec((1, LANES), const),
                  pl.BlockSpec((TM, LANES), pos_map),
                  pl.BlockSpec((TM, LANES), pos_map),
                  pl.BlockSpec((LANES, LANES), const)],
        out_specs=[pl.BlockSpec((TM, Q_EXP_WIDTH), row),
                   pl.BlockSpec((KV_EXP_WIDTH, TM), lambda i: (0, i)),
                   pl.BlockSpec((TM, KV_EXP_WIDTH), row)],
        out_shape=[jax.ShapeDtypeStruct((t, Q_EXP_WIDTH), BF16),
                   jax.ShapeDtypeStruct((KV_EXP_WIDTH, t), BF16),
                   jax.ShapeDtypeStruct((t, KV_EXP_WIDTH), BF16)],
        compiler_params=_params(("parallel",)),
        name="qkv_proj",
    )(*xparts, gain, w, qg, kg, cos, sin, bd)


def _rest_proj(xparts, t, n_a, gain, w):
    row = lambda i: (i, 0)
    const = lambda i: (0, 0)
    out = jax.ShapeDtypeStruct((t, D_MODEL), BF16)
    return pl.pallas_call(
        functools.partial(_rest_kernel, n_a=n_a),
        grid=(t // TM,),
        in_specs=_two_part_specs((TM, D_MODEL), n_a, xparts) + [
                  pl.BlockSpec((1, D_MODEL), const),
                  pl.BlockSpec((D_MODEL, REST_WIDTH), const)],
        out_specs=[pl.BlockSpec((TM, D_MODEL), row)] * 4,
        out_shape=[out] * 4,
        compiler_params=_params(("parallel",)),
        name="rest_proj",
    )(*xparts, gain, w)


def _attention(kmax, q, kt, v, *, row0, seq0, n_seq, seq, tq, exact_max):
    nq = seq // tq
    q0 = row0 // tq
    s0 = row0 // seq
    kern = functools.partial(_attn_kernel, seq0=seq0, exact_max=exact_max)
    return pl.pallas_call(
        kern,
        grid_spec=pltpu.PrefetchScalarGridSpec(
            num_scalar_prefetch=1,
            grid=(n_seq, nq),
            in_specs=[pl.BlockSpec((tq, Q_EXP_WIDTH), lambda b, i, km: (q0 + b * nq + i, 0)),
                      pl.BlockSpec((KV_EXP_WIDTH, seq), lambda b, i, km: (0, s0 + b)),
                      pl.BlockSpec((seq, KV_EXP_WIDTH), lambda b, i, km: (s0 + b, 0))],
            out_specs=pl.BlockSpec((tq, D_MODEL), lambda b, i, km: (b * nq + i, 0))),
        out_shape=jax.ShapeDtypeStruct((n_seq * seq, D_MODEL), BF16),
        compiler_params=_params(("parallel", "parallel")),
        name=f"attention_s{seq}" + ("_exact" if exact_max else ""),
    )(kmax, q, kt, v)


def _key_norm_max(kt, n_prompt, bp, sp, bs, ss):
    k = kt.reshape(N_KV_HEADS, LANES, -1)[:, :HEAD_DIM, :].astype(F32)
    norm = jnp.sqrt(jnp.sum(k * k, axis=1))
    kp = jnp.max(norm[:, :n_prompt].reshape(N_KV_HEADS, bp, sp), axis=-1)
    ks = jnp.max(norm[:, n_prompt:].reshape(N_KV_HEADS, bs, ss), axis=-1)
    return jnp.concatenate([kp.T.reshape(-1), ks.T.reshape(-1)])


def _mix(attn_parts, cb, u, ga, gb, xparts, cw, woa, wob, wo, gain, rw, *, n_prompt, seq_p, seq_s):
    t = cb.shape[0]
    n_a = n_prompt // TM
    row = lambda i: (i, 0)
    const = lambda i: (0, 0)
    sub = TM // 8
    last = t // 8 - 1
    act = pl.BlockSpec((TM, D_MODEL), row)
    wspec = pl.BlockSpec((D_MODEL, D_MODEL), const)
    with_router = rw is not None
    kern = functools.partial(_mix_kernel, n_prompt_tiles=n_prompt // TM,
                             tiles4=seq_p // TM, tiles2=seq_s // TM, with_router=with_router)
    in_specs = (_two_part_specs((TM, D_MODEL), n_a, attn_parts) + [
                act, act,
                pl.BlockSpec((8, D_MODEL), lambda i: (jnp.maximum(i * sub - 1, 0), 0)),
                pl.BlockSpec((8, D_MODEL), lambda i: (jnp.minimum((i + 1) * sub, last), 0)),
                act, act]
                + _two_part_specs((TM, D_MODEL), n_a, xparts) + [
                pl.BlockSpec((3, D_MODEL), const),
                wspec, wspec, wspec,
                pl.BlockSpec((1, D_MODEL), const)])
    args = [*attn_parts, cb, u, u, u, ga, gb, *xparts, cw, woa, wob, wo, gain]
    out_specs = [act, act]
    out_shape = [jax.ShapeDtypeStruct((t, D_MODEL), F32),
                 jax.ShapeDtypeStruct((t, D_MODEL), F32 if with_router else BF16)]
    if with_router:
        in_specs.append(pl.BlockSpec((D_MODEL, LANES), const))
        args.append(rw)
        out_specs.append(pl.BlockSpec((TM, LANES), row))
        out_shape.append(jax.ShapeDtypeStruct((t, LANES), F32))
    return pl.pallas_call(
        kern,
        grid=(t // TM,),
        in_specs=in_specs,
        out_specs=out_specs,
        out_shape=out_shape,
        compiler_params=_params(("parallel",)),
        name="mix_proj_router" if with_router else "mix_proj",
    )(*args)


def _ffn(h, x, wg, wu, wd):
    t = x.shape[0]
    row = lambda i, j: (i, 0)
    return pl.pallas_call(
        _ffn_kernel,
        grid=(t // TM_FFN, D_FF // TF),
        in_specs=[pl.BlockSpec((TM_FFN, D_MODEL), row),
                  pl.BlockSpec((TM_FFN, D_MODEL), row),
                  pl.BlockSpec((D_MODEL, TF), lambda i, j: (0, j)),
                  pl.BlockSpec((D_MODEL, TF), lambda i, j: (0, j)),
                  pl.BlockSpec((TF, D_MODEL), lambda i, j: (j, 0))],
        out_specs=pl.BlockSpec((TM_FFN, D_MODEL), row),
        out_shape=jax.ShapeDtypeStruct((t, D_MODEL), F32),
        scratch_shapes=[pltpu.VMEM((TM_FFN, D_MODEL), F32)],
        compiler_params=_params(("parallel", "arbitrary")),
        name="ffn_dense",
    )(h, x, wg, wu, wd)


def _router(logits):
    t = logits.shape[0]
    row = lambda i: (i, 0)
    const = lambda i: (0, 0)
    r = jnp.arange(TR)
    tri = (r[None, :] < r[:, None]).astype(BF16)
    return pl.pallas_call(
        _router_kernel,
        grid=(t // TR,),
        in_specs=[pl.BlockSpec((TR, LANES), row),
                  pl.BlockSpec((TR, TR), const)],
        out_specs=[pl.BlockSpec((TR, LANES), row),
                   pl.BlockSpec((8, LANES), const)],
        out_shape=[jax.ShapeDtypeStruct((t, LANES), F32),
                   jax.ShapeDtypeStruct((8, LANES), F32)],
        scratch_shapes=[pltpu.VMEM((8, LANES), F32)],
        compiler_params=_params(("arbitrary",)),
        name="router",
    )(logits, tri)


def _route_plan(route, counts, t):
    e0 = route[:, ROUTE_E0].astype(jnp.int32)
    e1 = route[:, ROUTE_E1].astype(jnp.int32)
    r0 = route[:, ROUTE_R0].astype(jnp.int32)
    r1 = route[:, ROUTE_R1].astype(jnp.int32)
    cnt = counts[0, :N_EXPERTS].astype(jnp.int32)
    padded = ((cnt + TM_MOE - 1) // TM_MOE) * TM_MOE
    ends = jnp.cumsum(padded)
    starts = ends - padded
    experts = jnp.arange(N_EXPERTS, dtype=jnp.int32)
    pos0 = jnp.sum(jnp.where(e0[:, None] == experts[None, :], starts[None, :], 0), axis=1) + r0
    pos1 = jnp.sum(jnp.where(e1[:, None] == experts[None, :], starts[None, :], 0), axis=1) + r1
    n_tok_tiles = t // TM_MOE
    pos = jnp.concatenate([pos0.reshape(n_tok_tiles, 1, TM_MOE),
                           pos1.reshape(n_tok_tiles, 1, TM_MOE)], axis=2)
    n_tiles = 2 * t // TM_MOE + N_EXPERTS
    tile_start = jnp.arange(n_tiles, dtype=jnp.int32) * TM_MOE
    tile_valid = (tile_start < ends[-1]).astype(jnp.int32)
    tile_expert = jnp.sum((tile_start[:, None] >= ends[None, :]).astype(jnp.int32), axis=1)
    tile_expert = jnp.minimum(tile_expert, N_EXPERTS - 1)
    return pos, tile_expert, tile_valid, n_tiles


def _dispatch(pos, h, n_rows):
    t = h.shape[0]
    zeros = jnp.zeros((n_rows, D_MODEL), F32)
    return pl.pallas_call(
        _dispatch_kernel,
        grid=(t // TM_MOE,),
        in_specs=[pl.BlockSpec((1, 1, 2 * TM_MOE), lambda i: (i, 0, 0), memory_space=pltpu.SMEM),
                  pl.BlockSpec((TM_MOE, D_MODEL), lambda i: (i, 0)),
                  pl.BlockSpec(memory_space=pl.ANY)],
        out_specs=pl.BlockSpec(memory_space=pl.ANY),
        out_shape=jax.ShapeDtypeStruct((n_rows, D_MODEL), F32),
        scratch_shapes=[pltpu.SemaphoreType.DMA(())],
        input_output_aliases={2: 0},
        compiler_params=_params(("arbitrary",)),
        name="moe_dispatch",
    )(pos, h, zeros)


def _moe_ffn(tile_expert, tile_valid, xs, wg, wu, wd):
    n_rows = xs.shape[0]
    row = lambda g, j, te, tv: (g, 0)
    return pl.pallas_call(
        _moe_ffn_kernel,
        grid_spec=pltpu.PrefetchScalarGridSpec(
            num_scalar_prefetch=2,
            grid=(n_rows // TM_MOE, D_FF // TF_MOE),
            in_specs=[pl.BlockSpec((TM_MOE, D_MODEL), row),
                      pl.BlockSpec((1, D_MODEL, TF_MOE),
                                   lambda g, j, te, tv: (te[g], 0, j * tv[g])),
                      pl.BlockSpec((1, D_MODEL, TF_MOE),
                                   lambda g, j, te, tv: (te[g], 0, j * tv[g])),
                      pl.BlockSpec((1, TF_MOE, D_MODEL),
                                   lambda g, j, te, tv: (te[g], j * tv[g], 0))],
            out_specs=pl.BlockSpec((TM_MOE, D_MODEL), row),
            scratch_shapes=[pltpu.VMEM((TM_MOE, D_MODEL), F32)]),
        out_shape=jax.ShapeDtypeStruct((n_rows, D_MODEL), F32),
        compiler_params=_params(("parallel", "arbitrary")),
        name="moe_ffn",
    )(tile_expert, tile_valid, xs, wg, wu, wd)


def _combine(pos, x, route, fgain, ys, n_first):
    t = x.shape[0]
    n_a = n_first // TM_MOE
    row = lambda i: (i, 0)
    blk = (TM_MOE, D_MODEL)
    return pl.pallas_call(
        functools.partial(_combine_kernel, n_a=n_a),
        grid=(t // TM_MOE,),
        in_specs=[pl.BlockSpec((1, 1, 2 * TM_MOE), lambda i: (i, 0, 0), memory_space=pltpu.SMEM),
                  pl.BlockSpec(blk, row),
                  pl.BlockSpec((TM_MOE, LANES), row),
                  pl.BlockSpec((1, D_MODEL), lambda i: (0, 0)),
                  pl.BlockSpec(memory_space=pl.ANY)],
        out_specs=[pl.BlockSpec(blk, lambda i: (jnp.minimum(i, n_a - 1), 0)),
                   pl.BlockSpec(blk, lambda i: (jnp.maximum(i - n_a, 0), 0))],
        out_shape=[jax.ShapeDtypeStruct((n_first, D_MODEL), F32),
                   jax.ShapeDtypeStruct((t - n_first, D_MODEL), F32)],
        scratch_shapes=[pltpu.VMEM((2, TM_MOE, D_MODEL), F32),
                        pltpu.SemaphoreType.DMA(())],
        compiler_params=_params(("arbitrary",)),
        name="moe_combine",
    )(pos, x, route, fgain, ys)


def kernel(x_prompt, x_sample, norm_mix, w_in, q_norm, k_norm, conv_w, w_oa, w_ob, w_o, norm_ffn,
           ffn_w_gate, ffn_w_up, ffn_w_down, router_w, moe_w_gate, moe_w_up, moe_w_down, final_norm):
    bp, sp, _ = x_prompt.shape
    bs, ss, _ = x_sample.shape
    n_prompt = bp * sp
    n_sample = bs * ss
    depth = norm_mix.shape[0]
    assert sp % TM == 0 and ss % TM == 0 and n_prompt % TM_FFN == 0 and n_sample % TM_FFN == 0
    assert depth == 2 and ffn_w_gate.shape[0] == 1 and moe_w_gate.shape[0] == 1

    t_all = n_prompt + n_sample
    xparts = (x_prompt.reshape(n_prompt, D_MODEL), x_sample.reshape(n_sample, D_MODEL))

    cos, sin = _rope_tables(max(sp, ss))
    n_prompt_tiles, tiles4, tiles2 = n_prompt // TM, sp // TM, ss // TM
    pos_map = lambda i: (jnp.where(i < n_prompt_tiles, i % tiles4, i % tiles2), 0)
    idx = jnp.arange(LANES)
    bd = (idx[:, None] // HEAD_DIM == idx[None, :] // HEAD_DIM).astype(BF16)

    for l in range(depth):
        gain = norm_mix[l][None, :]
        w_l = w_in[l]
        w_qkv = w_l[:, :QKV_WIDTH].astype(BF16)
        w_rest = w_l[:, QKV_WIDTH:].astype(BF16)
        qg = jnp.tile(q_norm[l], 2)[None, :]
        kg = jnp.tile(k_norm[l], 2)[None, :]
        q, kt, v = _qkv_proj(xparts, t_all, n_prompt_tiles, gain, w_qkv, qg, kg, cos, sin, bd,
                             pos_map)
        cb, u, ga, gb = _rest_proj(xparts, t_all, n_prompt_tiles, gain, w_rest)
        kmax = _key_norm_max(kt, n_prompt, bp, sp, bs, ss)
        q_bound = Q_SCALE * math.sqrt(HEAD_DIM) * jnp.max(jnp.abs(q_norm[l]))
        bound_ok = 2.0 * q_bound * jnp.max(kmax) < MAX_SHIFT_GAP

        def attend(exact_max, tq_p, tq_s):
            def run(kmax, q, kt, v):
                attn_p = _attention(kmax, q, kt, v, row0=0, seq0=0, n_seq=bp, seq=sp,
                                    tq=tq_p, exact_max=exact_max)
                attn_s = _attention(kmax, q, kt, v, row0=n_prompt, seq0=bp, n_seq=bs, seq=ss,
                                    tq=tq_s, exact_max=exact_max)
                return attn_p, attn_s
            return run

        attn_parts = lax.cond(bound_ok, attend(False, 256, 256), attend(True, 128, 256),
                              kmax, q, kt, v)
        j = l // 2
        is_moe = l % 2 == 1
        rw = jnp.pad(router_w[j], ((0, 0), (0, LANES - N_EXPERTS))) if is_moe else None
        outs = _mix(attn_parts, cb, u, ga, gb, xparts, conv_w[l],
                    w_oa[l].astype(BF16), w_ob[l].astype(BF16), w_o[l].astype(BF16),
                    norm_ffn[l][None, :], rw, n_prompt=n_prompt, seq_p=sp, seq_s=ss)
        if not is_moe:
            x, h2 = outs
            x = _ffn(h2, x, ffn_w_gate[j].astype(BF16), ffn_w_up[j].astype(BF16),
                     ffn_w_down[j].astype(BF16))
            xparts = (x, x)
        else:
            x, h2, logits = outs
            route, counts = _router(logits)
            pos, tile_expert, tile_valid, n_tiles = _route_plan(route, counts, t_all)
            xs = _dispatch(pos, h2, n_tiles * TM_MOE)
            ys = _moe_ffn(tile_expert, tile_valid, xs, moe_w_gate[j].astype(BF16),
                          moe_w_up[j].astype(BF16), moe_w_down[j].astype(BF16))
            xparts = _combine(pos, x, route, final_norm[None, :], ys, n_prompt)

    return (xparts[0].reshape(bp, sp, D_MODEL), xparts[1].reshape(bs, ss, D_MODEL))
```

```python
import functools
import math

import jax
import jax.numpy as jnp
from jax import lax
from jax.experimental import pallas as pl
from jax.experimental.pallas import tpu as pltpu

F32 = jnp.float32
BF16 = jnp.bfloat16

D_MODEL = 1024
N_HEADS = 16
N_KV_HEADS = 4
HEAD_DIM = 64
GROUP = N_HEADS // N_KV_HEADS
KV_WIDTH = N_KV_HEADS * HEAD_DIM
AXIS_DIM = HEAD_DIM // 2
ROPE_THETA = 10000.0
GRID_W = 64
D_FF = 3584
N_EXPERTS = 8
EPS = 1e-6
LANES = 128
QKV_WIDTH = D_MODEL + 2 * KV_WIDTH
REST_WIDTH = 5 * D_MODEL
Q_EXP_WIDTH = N_HEADS * LANES
KV_EXP_WIDTH = N_KV_HEADS * LANES
Q_SCALE = math.log2(math.e) / math.sqrt(HEAD_DIM)
V_ROWS = 80
KEY_CHUNK = 512
ROW_UNROLL = 8
MAX_SHIFT_GAP = 100.0

TM = 512
TM_FFN = 512
TF = 1792
TF_MOE = 1792
TR = 512
TM_MOE = 512
VMEM_LIMIT = 56 * 1024 * 1024


def _params(sem):
    return pltpu.CompilerParams(dimension_semantics=sem, vmem_limit_bytes=VMEM_LIMIT)


def _rms(x, gain):
    return x * lax.rsqrt(jnp.mean(x * x, axis=-1, keepdims=True) + EPS) * gain


def _pick(n_a, a_ref, b_ref):
    return jnp.where(pl.program_id(0) < n_a, a_ref[...], b_ref[...])


def _qkv_kernel(xa_ref, xb_ref, g_ref, w_ref, qg_ref, kg_ref, cos_ref, sin_ref, bd_ref,
                q_ref, k_ref, vt_ref, *, n_a):
    tm = xa_ref.shape[0]
    h = _rms(_pick(n_a, xa_ref, xb_ref), g_ref[...]).astype(BF16)
    p = jnp.dot(h, w_ref[...], preferred_element_type=F32)
    cos = cos_ref[...]
    sin = sin_ref[...]
    bd = bd_ref[...]
    lane = lax.broadcasted_iota(jnp.int32, (tm, LANES), 1)
    first16 = (lane & 31) < 16
    low_half = lane < HEAD_DIM

    def norm_rope(c, gain):
        sq = c * c
        hi = sq.astype(BF16)
        lo = (sq - hi.astype(F32)).astype(BF16)
        ss = (jnp.dot(hi, bd, preferred_element_type=F32)
              + jnp.dot(lo, bd, preferred_element_type=F32))
        n = c * lax.rsqrt(ss * (1.0 / HEAD_DIM) + EPS) * gain
        partner = jnp.where(first16, pltpu.roll(n, LANES - 16, 1), pltpu.roll(n, 16, 1))
        return n * cos + partner * sin

    is_aux = lane == HEAD_DIM
    zero = jnp.zeros((tm, LANES), F32)
    ones_aux = jnp.where(is_aux, 1.0, zero)

    qg = qg_ref[...]
    for c in range(N_HEADS // 2):
        r = norm_rope(p[:, c * LANES:(c + 1) * LANES], qg) * Q_SCALE
        rr = r * r
        norm = jnp.sqrt(jnp.dot(rr.astype(BF16), bd, preferred_element_type=F32))
        even = jnp.where(low_half, r, jnp.where(is_aux, pltpu.roll(norm, HEAD_DIM, 1), zero))
        odd = jnp.where(low_half, pltpu.roll(r, HEAD_DIM, 1), jnp.where(is_aux, norm, zero))
        q_ref[:, (2 * c) * LANES:(2 * c + 1) * LANES] = even.astype(BF16)
        q_ref[:, (2 * c + 1) * LANES:(2 * c + 2) * LANES] = odd.astype(BF16)

    kg = kg_ref[...]
    for c in range(KV_WIDTH // LANES):
        kn = norm_rope(p[:, D_MODEL + c * LANES:D_MODEL + (c + 1) * LANES], kg)
        even = jnp.where(low_half, kn, ones_aux)
        odd = jnp.where(low_half, pltpu.roll(kn, HEAD_DIM, 1), ones_aux)
        k_ref[:, (2 * c) * LANES:(2 * c + 1) * LANES] = even.astype(BF16)
        k_ref[:, (2 * c + 1) * LANES:(2 * c + 2) * LANES] = odd.astype(BF16)
        vv = p[:, D_MODEL + KV_WIDTH + c * LANES:D_MODEL + KV_WIDTH + (c + 1) * LANES]
        even = jnp.where(low_half, vv, ones_aux)
        odd = jnp.where(low_half, pltpu.roll(vv, HEAD_DIM, 1), ones_aux)
        vt_ref[(2 * c) * LANES:(2 * c + 1) * LANES, :] = even.T.astype(BF16)
        vt_ref[(2 * c + 1) * LANES:(2 * c + 2) * LANES, :] = odd.T.astype(BF16)


def _rest_kernel(xa_ref, xb_ref, g_ref, w_ref, cb_ref, u_ref, ga_ref, gb_ref, *, n_a):
    h = _rms(_pick(n_a, xa_ref, xb_ref), g_ref[...]).astype(BF16)
    p = jnp.dot(h, w_ref[...], preferred_element_type=F32)
    cb_ref[...] = p[:, :D_MODEL].astype(BF16)
    u_ref[...] = (p[:, D_MODEL:2 * D_MODEL] * p[:, 2 * D_MODEL:3 * D_MODEL]).astype(BF16)
    ga_ref[...] = p[:, 3 * D_MODEL:4 * D_MODEL].astype(BF16)
    gb_ref[...] = p[:, 4 * D_MODEL:].astype(BF16)


def _attn_kernel(kmax_ref, q_ref, k_ref, vt_ref, o_ref, *, seq0, exact_max):
    b = pl.program_id(0)
    tq = q_ref.shape[0]
    is_aux_row = lax.broadcasted_iota(jnp.int32, (LANES, GROUP * tq), 0) == HEAD_DIM
    for j in range(N_KV_HEADS):
        qt = jnp.concatenate(
            [q_ref[:, h * LANES:(h + 1) * LANES].T for h in range(GROUP * j, GROUP * (j + 1))],
            axis=1)
        if exact_max:
            qt = jnp.where(is_aux_row, jnp.zeros_like(qt), qt)
            st = jnp.dot(k_ref[:, j * LANES:(j + 1) * LANES], qt,
                         preferred_element_type=F32)
            st = st - jnp.max(st, axis=0, keepdims=True)
            pt = jnp.exp2(st).astype(BF16)
            ot = jnp.dot(vt_ref[j * LANES:j * LANES + V_ROWS, :], pt,
                         preferred_element_type=F32)
        else:
            neg_kmax = -kmax_ref[(seq0 + b) * N_KV_HEADS + j]
            qt = jnp.where(is_aux_row, (qt.astype(F32) * neg_kmax).astype(BF16), qt)
            ot = jnp.zeros((V_ROWS, GROUP * tq), F32)
            for c in range(k_ref.shape[0] // KEY_CHUNK):
                rows = slice(c * KEY_CHUNK, (c + 1) * KEY_CHUNK)
                st = jnp.dot(k_ref[rows, j * LANES:(j + 1) * LANES], qt,
                             preferred_element_type=F32)
                pt = jnp.exp2(st).astype(BF16)
                ot = ot + jnp.dot(vt_ref[j * LANES:j * LANES + V_ROWS, rows], pt,
                                  preferred_element_type=F32)
        ot = ot[:HEAD_DIM] / ot[HEAD_DIM:HEAD_DIM + 1]
        for a in range(2):
            pair = jnp.concatenate([ot[:, (2 * a) * tq:(2 * a + 1) * tq],
                                    ot[:, (2 * a + 1) * tq:(2 * a + 2) * tq]], axis=0)
            c = 2 * j + a
            o_ref[:, c * LANES:(c + 1) * LANES] = pair.T.astype(BF16)


def _mix_kernel(attn_a_ref, attn_b_ref, cb_ref, u_ref, up_ref, un_ref, ga_ref, gb_ref,
                xa_ref, xb_ref, cw_ref, woa_ref, wob_ref, wo_ref, g_ref, *rest,
                n_prompt_tiles, tiles4, tiles2, with_router):
    if with_router:
        rw_ref, xo_ref, h2_ref, lg_ref = rest
    else:
        xo_ref, h2_ref = rest
    i = pl.program_id(0)
    tm = xa_ref.shape[0]
    is_prompt = i < n_prompt_tiles
    seq_start = jnp.where(is_prompt, i % tiles4 == 0, i % tiles2 == 0)
    seq_end = jnp.where(is_prompt, i % tiles4 == tiles4 - 1, i % tiles2 == tiles2 - 1)

    attn = _pick(n_prompt_tiles, attn_a_ref, attn_b_ref)
    ya = jnp.dot(attn, woa_ref[...], preferred_element_type=F32)

    u = u_ref[...].astype(F32)
    row = lax.broadcasted_iota(jnp.int32, u.shape, 0)
    prev_row = jnp.where(seq_start, 0.0, up_ref[7:8, :].astype(F32))
    next_row = jnp.where(seq_end, 0.0, un_ref[0:1, :].astype(F32))
    u_prev = jnp.where(row == 0, prev_row, pltpu.roll(u, 1, 0))
    u_next = jnp.where(row == tm - 1, next_row, pltpu.roll(u, tm - 1, 0))
    cw = cw_ref[...]
    conv = cw[0:1, :] * u_prev + cw[1:2, :] * u + cw[2:3, :] * u_next
    yb_in = (cb_ref[...].astype(F32) * conv).astype(BF16)
    yb = jnp.dot(yb_in, wob_ref[...], preferred_element_type=F32)

    m = (jax.nn.sigmoid(ga_ref[...].astype(F32)) * ya
         + jax.nn.sigmoid(gb_ref[...].astype(F32)) * yb)
    xn = (_pick(n_prompt_tiles, xa_ref, xb_ref)
          + jnp.dot(m.astype(BF16), wo_ref[...], preferred_element_type=F32))
    xo_ref[...] = xn
    h2 = _rms(xn, g_ref[...])
    h2_ref[...] = h2.astype(h2_ref.dtype)
    if not with_router:
        return
    hi = h2.astype(BF16)
    lo = (h2 - hi.astype(F32)).astype(BF16)
    rw = rw_ref[...]
    rhi = rw.astype(BF16)
    rlo = (rw - rhi.astype(F32)).astype(BF16)
    lg_ref[...] = (jnp.dot(hi, rhi, preferred_element_type=F32)
                   + jnp.dot(hi, rlo, preferred_element_type=F32)
                   + jnp.dot(lo, rhi, preferred_element_type=F32))


def _ffn_kernel(h_ref, x_ref, wg_ref, wu_ref, wd_ref, o_ref, acc_ref):
    j = pl.program_id(1)

    @pl.when(j == 0)
    def _():
        acc_ref[...] = jnp.zeros_like(acc_ref)

    h = h_ref[...]
    g = jnp.dot(h, wg_ref[...], preferred_element_type=F32)
    u = jnp.dot(h, wu_ref[...], preferred_element_type=F32)
    a = (g * jax.nn.sigmoid(g) * u).astype(BF16)
    acc_ref[...] += jnp.dot(a, wd_ref[...], preferred_element_type=F32)

    @pl.when(j == pl.num_programs(1) - 1)
    def _():
        o_ref[...] = x_ref[...] + acc_ref[...]


ROUTE_E0, ROUTE_E1, ROUTE_R0, ROUTE_R1, ROUTE_W0, ROUTE_W1 = range(6)


def _lane_pick(x, lane, k):
    return jnp.sum(jnp.where(lane == k, x, 0.0), axis=-1, keepdims=True)


def _router_kernel(lg_ref, tri_ref, route_ref, count_ref, base_ref):
    i = pl.program_id(0)

    @pl.when(i == 0)
    def _():
        base_ref[...] = jnp.zeros_like(base_ref)

    lg = lg_ref[...]
    lane = lax.broadcasted_iota(jnp.int32, lg.shape, 1)
    neg = jnp.float32(-jnp.inf)
    l1 = jnp.where(lane < N_EXPERTS, lg, neg)
    m1 = jnp.max(l1, axis=-1, keepdims=True)
    i1 = jnp.min(jnp.where(l1 == m1, lane, LANES), axis=-1, keepdims=True)
    l2 = jnp.where(lane == i1, neg, l1)
    m2 = jnp.max(l2, axis=-1, keepdims=True)
    i2 = jnp.min(jnp.where(l2 == m2, lane, LANES), axis=-1, keepdims=True)
    e = jnp.exp(m2 - m1)
    w1 = 1.0 / (1.0 + e)
    w2 = e / (1.0 + e)

    hot1 = lane == i1
    hot2 = lane == i2
    onehot = jnp.where(hot1 | hot2, 1.0, 0.0)
    base = base_ref[0:1, :]
    prefix = jnp.dot(tri_ref[...], onehot.astype(BF16), preferred_element_type=F32) + base
    r1 = jnp.sum(jnp.where(hot1, prefix, 0.0), axis=-1, keepdims=True)
    r2 = jnp.sum(jnp.where(hot2, prefix, 0.0), axis=-1, keepdims=True)
    total = base + jnp.sum(onehot, axis=0, keepdims=True)
    base_ref[...] = jnp.broadcast_to(total, base_ref.shape)
    count_ref[...] = jnp.broadcast_to(total, count_ref.shape)

    rec = jnp.where(lane == ROUTE_E0, i1.astype(F32), 0.0)
    rec = jnp.where(lane == ROUTE_E1, i2.astype(F32), rec)
    rec = jnp.where(lane == ROUTE_R0, r1, rec)
    rec = jnp.where(lane == ROUTE_R1, r2, rec)
    rec = jnp.where(lane == ROUTE_W0, w1, rec)
    rec = jnp.where(lane == ROUTE_W1, w2, rec)
    route_ref[...] = rec


def _row_copy(src, src_row, dst, dst_row, sem):
    return pltpu.make_async_copy(src.at[pl.ds(src_row, 1)], dst.at[pl.ds(dst_row, 1)], sem)


def _dispatch_kernel(pos_ref, h_ref, xs_in_ref, xs_ref, sem):
    del xs_in_ref
    tm = h_ref.shape[0]

    def issue(i, carry):
        for k in range(ROW_UNROLL):
            r = i * ROW_UNROLL + k
            _row_copy(h_ref, r, xs_ref, pos_ref[0, 0, r], sem).start()
            _row_copy(h_ref, r, xs_ref, pos_ref[0, 0, tm + r], sem).start()
        return carry

    lax.fori_loop(0, tm // ROW_UNROLL, issue, 0)
    for _ in range(2):
        pltpu.make_async_copy(h_ref, xs_ref.at[pl.ds(0, tm)], sem).wait()


def _moe_ffn_kernel(te_ref, tv_ref, x_ref, wg_ref, wu_ref, wd_ref, y_ref, acc_ref):
    del te_ref
    g_idx = pl.program_id(0)
    j = pl.program_id(1)
    valid = tv_ref[g_idx] > 0

    @pl.when(j == 0)
    def _():
        acc_ref[...] = jnp.zeros_like(acc_ref)

    @pl.when(valid)
    def _():
        h = x_ref[...].astype(BF16)
        g = jnp.dot(h, wg_ref[0], preferred_element_type=F32)
        u = jnp.dot(h, wu_ref[0], preferred_element_type=F32)
        a = (g * jax.nn.sigmoid(g) * u).astype(BF16)
        acc_ref[...] += jnp.dot(a, wd_ref[0], preferred_element_type=F32)

    @pl.when(j == pl.num_programs(1) - 1)
    def _():
        y_ref[...] = acc_ref[...]


def _combine_kernel(pos_ref, x_ref, route_ref, fg_ref, y_hbm, oa_ref, ob_ref, ybuf, sem, *, n_a):
    tc = x_ref.shape[0]

    def issue(i, carry):
        for k in range(ROW_UNROLL):
            r = i * ROW_UNROLL + k
            _row_copy(y_hbm, pos_ref[0, 0, r], ybuf.at[0], r, sem).start()
            _row_copy(y_hbm, pos_ref[0, 0, tc + r], ybuf.at[1], r, sem).start()
        return carry

    lax.fori_loop(0, tc // ROW_UNROLL, issue, 0)
    for slot in range(2):
        pltpu.make_async_copy(y_hbm.at[pl.ds(0, tc)], ybuf.at[slot], sem).wait()
    route = route_ref[...]
    lane = lax.broadcasted_iota(jnp.int32, route.shape, 1)
    w0 = _lane_pick(route, lane, ROUTE_W0)
    w1 = _lane_pick(route, lane, ROUTE_W1)
    out = _rms(x_ref[...] + (w0 * ybuf[0] + w1 * ybuf[1]), fg_ref[...])
    in_a = pl.program_id(0) < n_a

    @pl.when(in_a)
    def _():
        oa_ref[...] = out

    @pl.when(jnp.logical_not(in_a))
    def _():
        ob_ref[...] = out


def _rope_tables(max_seq):
    t = jnp.arange(max_seq, dtype=jnp.int32)
    row = (t // GRID_W).astype(F32)
    col = (t % GRID_W).astype(F32)
    inv = 1.0 / (ROPE_THETA ** (jnp.arange(0, AXIS_DIM, 2, dtype=F32) / AXIS_DIM))
    ar = row[:, None] * inv[None, :]
    ac = col[:, None] * inv[None, :]
    cos64 = jnp.concatenate([jnp.cos(ar), jnp.cos(ar), jnp.cos(ac), jnp.cos(ac)], axis=-1)
    sin64 = jnp.concatenate([-jnp.sin(ar), jnp.sin(ar), -jnp.sin(ac), jnp.sin(ac)], axis=-1)
    return jnp.tile(cos64, (1, 2)), jnp.tile(sin64, (1, 2))


def _two_part_specs(block, n_a, parts):
    off = 0 if parts[0] is parts[1] else n_a
    return [pl.BlockSpec(block, lambda i: (jnp.minimum(i, n_a - 1), 0)),
            pl.BlockSpec(block, lambda i: (jnp.maximum(i, n_a) - off, 0))]


def _qkv_proj(xparts, t, n_a, gain, w, qg, kg, cos, sin, bd, pos_map):
    row = lambda i: (i, 0)
    const = lambda i: (0, 0)
    return pl.pallas_call(
        functools.partial(_qkv_kernel, n_a=n_a),
        grid=(t // TM,),
        in_specs=_two_part_specs((TM, D_MODEL), n_a, xparts) + [
                  pl.BlockSpec((1, D_MODEL), const),
                  pl.BlockSpec((D_MODEL, QKV_WIDTH), const),
                  pl.BlockSpec((1, LANES), const),
                  pl.BlockSpec((1, LANES), const),
                  pl.BlockSpec((TM, LANES), pos_map),
                  pl.BlockSpec((TM, LANES), pos_map),
                  pl.BlockSpec((LANES, LANES), const)],
        out_specs=[pl.BlockSpec((TM, Q_EXP_WIDTH), row),
                   pl.BlockSpec((TM, KV_EXP_WIDTH), row),
                   pl.BlockSpec((KV_EXP_WIDTH, TM), lambda i: (0, i))],
        out_shape=[jax.ShapeDtypeStruct((t, Q_EXP_WIDTH), BF16),
                   jax.ShapeDtypeStruct((t, KV_EXP_WIDTH), BF16),
                   jax.ShapeDtypeStruct((KV_EXP_WIDTH, t), BF16)],
        compiler_params=_params(("parallel",)),
        name="qkv_proj",
    )(*xparts, gain, w, qg, kg, cos, sin, bd)


def _rest_proj(xparts, t, n_a, gain, w):
    row = lambda i: (i, 0)
    const = lambda i: (0, 0)
    out = jax.ShapeDtypeStruct((t, D_MODEL), BF16)
    return pl.pallas_call(
        functools.partial(_rest_kernel, n_a=n_a),
        grid=(t // TM,),
        in_specs=_two_part_specs((TM, D_MODEL), n_a, xparts) + [
                  pl.BlockSpec((1, D_MODEL), const),
                  pl.BlockSpec((D_MODEL, REST_WIDTH), const)],
        out_specs=[pl.BlockSpec((TM, D_MODEL), row)] * 4,
        out_shape=[out] * 4,
        compiler_params=_params(("parallel",)),
        name="rest_proj",
    )(*xparts, gain, w)


def _attention(kmax, q, k, vt, *, row0, seq0, n_seq, seq, tq, exact_max):
    nq = seq // tq
    q0 = row0 // tq
    s0 = row0 // seq
    kern = functools.partial(_attn_kernel, seq0=seq0, exact_max=exact_max)
    return pl.pallas_call(
        kern,
        grid_spec=pltpu.PrefetchScalarGridSpec(
            num_scalar_prefetch=1,
            grid=(n_seq, nq),
            in_specs=[pl.BlockSpec((tq, Q_EXP_WIDTH), lambda b, i, km: (q0 + b * nq + i, 0)),
                      pl.BlockSpec((seq, KV_EXP_WIDTH), lambda b, i, km: (s0 + b, 0)),
                      pl.BlockSpec((KV_EXP_WIDTH, seq), lambda b, i, km: (0, s0 + b))],
            out_specs=pl.BlockSpec((tq, D_MODEL), lambda b, i, km: (b * nq + i, 0))),
        out_shape=jax.ShapeDtypeStruct((n_seq * seq, D_MODEL), BF16),
        compiler_params=_params(("parallel", "parallel")),
        name=f"attention_s{seq}" + ("_exact" if exact_max else ""),
    )(kmax, q, k, vt)


def _key_norm_max(k, n_prompt, bp, sp, bs, ss):
    kh = k.reshape(-1, N_KV_HEADS, LANES)[:, :, :HEAD_DIM].astype(F32)
    norm = jnp.sqrt(jnp.sum(kh * kh, axis=-1))
    kp = jnp.max(norm[:n_prompt].reshape(bp, sp, N_KV_HEADS), axis=1)
    ks = jnp.max(norm[n_prompt:].reshape(bs, ss, N_KV_HEADS), axis=1)
    return jnp.concatenate([kp.reshape(-1), ks.reshape(-1)])


def _mix(attn_parts, cb, u, ga, gb, xparts, cw, woa, wob, wo, gain, rw, *, n_prompt, seq_p, seq_s):
    t = cb.shape[0]
    n_a = n_prompt // TM
    row = lambda i: (i, 0)
    const = lambda i: (0, 0)
    sub = TM // 8
    last = t // 8 - 1
    act = pl.BlockSpec((TM, D_MODEL), row)
    wspec = pl.BlockSpec((D_MODEL, D_MODEL), const)
    with_router = rw is not None
    kern = functools.partial(_mix_kernel, n_prompt_tiles=n_prompt // TM,
                             tiles4=seq_p // TM, tiles2=seq_s // TM, with_router=with_router)
    in_specs = (_two_part_specs((TM, D_MODEL), n_a, attn_parts) + [
                act, act,
                pl.BlockSpec((8, D_MODEL), lambda i: (jnp.maximum(i * sub - 1, 0), 0)),
                pl.BlockSpec((8, D_MODEL), lambda i: (jnp.minimum((i + 1) * sub, last), 0)),
                act, act]
                + _two_part_specs((TM, D_MODEL), n_a, xparts) + [
                pl.BlockSpec((3, D_MODEL), const),
                wspec, wspec, wspec,
                pl.BlockSpec((1, D_MODEL), const)])
    args = [*attn_parts, cb, u, u, u, ga, gb, *xparts, cw, woa, wob, wo, gain]
    out_specs = [act, act]
    out_shape = [jax.ShapeDtypeStruct((t, D_MODEL), F32),
                 jax.ShapeDtypeStruct((t, D_MODEL), F32 if with_router else BF16)]
    if with_router:
        in_specs.append(pl.BlockSpec((D_MODEL, LANES), const))
        args.append(rw)
        out_specs.append(pl.BlockSpec((TM, LANES), row))
        out_shape.append(jax.ShapeDtypeStruct((t, LANES), F32))
    return pl.pallas_call(
        kern,
        grid=(t // TM,),
        in_specs=in_specs,
        out_specs=out_specs,
        out_shape=out_shape,
        compiler_params=_params(("parallel",)),
        name="mix_proj_router" if with_router else "mix_proj",
    )(*args)


def _ffn(h, x, wg, wu, wd):
    t = x.shape[0]
    row = lambda i, j: (i, 0)
    return pl.pallas_call(
        _ffn_kernel,
        grid=(t // TM_FFN, D_FF // TF),
        in_specs=[pl.BlockSpec((TM_FFN, D_MODEL), row),
                  pl.BlockSpec((TM_FFN, D_MODEL), row),
                  pl.BlockSpec((D_MODEL, TF), lambda i, j: (0, j)),
                  pl.BlockSpec((D_MODEL, TF), lambda i, j: (0, j)),
                  pl.BlockSpec((TF, D_MODEL), lambda i, j: (j, 0))],
        out_specs=pl.BlockSpec((TM_FFN, D_MODEL), row),
        out_shape=jax.ShapeDtypeStruct((t, D_MODEL), F32),
        scratch_shapes=[pltpu.VMEM((TM_FFN, D_MODEL), F32)],
        compiler_params=_params(("parallel", "arbitrary")),
        name="ffn_dense",
    )(h, x, wg, wu, wd)


def _router(logits):
    t = logits.shape[0]
    row = lambda i: (i, 0)
    const = lambda i: (0, 0)
    r = jnp.arange(TR)
    tri = (r[None, :] < r[:, None]).astype(BF16)
    return pl.pallas_call(
        _router_kernel,
        grid=(t // TR,),
        in_specs=[pl.BlockSpec((TR, LANES), row),
                  pl.BlockSpec((TR, TR), const)],
        out_specs=[pl.BlockSpec((TR, LANES), row),
                   pl.BlockSpec((8, LANES), const)],
        out_shape=[jax.ShapeDtypeStruct((t, LANES), F32),
                   jax.ShapeDtypeStruct((8, LANES), F32)],
        scratch_shapes=[pltpu.VMEM((8, LANES), F32)],
        compiler_params=_params(("arbitrary",)),
        name="router",
    )(logits, tri)


def _route_plan(route, counts, t):
    e0 = route[:, ROUTE_E0].astype(jnp.int32)
    e1 = route[:, ROUTE_E1].astype(jnp.int32)
    r0 = route[:, ROUTE_R0].astype(jnp.int32)
    r1 = route[:, ROUTE_R1].astype(jnp.int32)
    cnt = counts[0, :N_EXPERTS].astype(jnp.int32)
    padded = ((cnt + TM_MOE - 1) // TM_MOE) * TM_MOE
    ends = jnp.cumsum(padded)
    starts = ends - padded
    experts = jnp.arange(N_EXPERTS, dtype=jnp.int32)
    pos0 = jnp.sum(jnp.where(e0[:, None] == experts[None, :], starts[None, :], 0), axis=1) + r0
    pos1 = jnp.sum(jnp.where(e1[:, None] == experts[None, :], starts[None, :], 0), axis=1) + r1
    n_tok_tiles = t // TM_MOE
    pos = jnp.concatenate([pos0.reshape(n_tok_tiles, 1, TM_MOE),
                           pos1.reshape(n_tok_tiles, 1, TM_MOE)], axis=2)
    n_tiles = 2 * t // TM_MOE + N_EXPERTS
    tile_start = jnp.arange(n_tiles, dtype=jnp.int32) * TM_MOE
    tile_valid = (tile_start < ends[-1]).astype(jnp.int32)
    tile_expert = jnp.sum((tile_start[:, None] >= ends[None, :]).astype(jnp.int32), axis=1)
    tile_expert = jnp.minimum(tile_expert, N_EXPERTS - 1)
    return pos, tile_expert, tile_valid, n_tiles


def _dispatch(pos, h, n_rows):
    t = h.shape[0]
    zeros = jnp.zeros((n_rows, D_MODEL), F32)
    return pl.pallas_call(
        _dispatch_kernel,
        grid=(t // TM_MOE,),
        in_specs=[pl.BlockSpec((1, 1, 2 * TM_MOE), lambda i: (i, 0, 0), memory_space=pltpu.SMEM),
                  pl.BlockSpec((TM_MOE, D_MODEL), lambda i: (i, 0)),
                  pl.BlockSpec(memory_space=pl.ANY)],
        out_specs=pl.BlockSpec(memory_space=pl.ANY),
        out_shape=jax.ShapeDtypeStruct((n_rows, D_MODEL), F32),
        scratch_shapes=[pltpu.SemaphoreType.DMA(())],
        input_output_aliases={2: 0},
        compiler_params=_params(("arbitrary",)),
        name="moe_dispatch",
    )(pos, h, zeros)


def _moe_ffn(tile_expert, tile_valid, xs, wg, wu, wd):
    n_rows = xs.shape[0]
    row = lambda g, j, te, tv: (g, 0)
    return pl.pallas_call(
        _moe_ffn_kernel,
        grid_spec=pltpu.PrefetchScalarGridSpec(
            num_scalar_prefetch=2,
            grid=(n_rows // TM_MOE, D_FF // TF_MOE),
            in_specs=[pl.BlockSpec((TM_MOE, D_MODEL), row),
                      pl.BlockSpec((1, D_MODEL, TF_MOE),
                                   lambda g, j, te, tv: (te[g], 0, j * tv[g])),
                      pl.BlockSpec((1, D_MODEL, TF_MOE),
                                   lambda g, j, te, tv: (te[g], 0, j * tv[g])),
                      pl.BlockSpec((1, TF_MOE, D_MODEL),
                                   lambda g, j, te, tv: (te[g], j * tv[g], 0))],
            out_specs=pl.BlockSpec((TM_MOE, D_MODEL), row),
            scratch_shapes=[pltpu.VMEM((TM_MOE, D_MODEL), F32)]),
        out_shape=jax.ShapeDtypeStruct((n_rows, D_MODEL), F32),
        compiler_params=_params(("parallel", "arbitrary")),
        name="moe_ffn",
    )(tile_expert, tile_valid, xs, wg, wu, wd)


def _combine(pos, x, route, fgain, ys, n_first):
    t = x.shape[0]
    n_a = n_first // TM_MOE
    row = lambda i: (i, 0)
    blk = (TM_MOE, D_MODEL)
    return pl.pallas_call(
        functools.partial(_combine_kernel, n_a=n_a),
        grid=(t // TM_MOE,),
        in_specs=[pl.BlockSpec((1, 1, 2 * TM_MOE), lambda i: (i, 0, 0), memory_space=pltpu.SMEM),
                  pl.BlockSpec(blk, row),
                  pl.BlockSpec((TM_MOE, LANES), row),
                  pl.BlockSpec((1, D_MODEL), lambda i: (0, 0)),
                  pl.BlockSpec(memory_space=pl.ANY)],
        out_specs=[pl.BlockSpec(blk, lambda i: (jnp.minimum(i, n_a - 1), 0)),
                   pl.BlockSpec(blk, lambda i: (jnp.maximum(i - n_a, 0), 0))],
        out_shape=[jax.ShapeDtypeStruct((n_first, D_MODEL), F32),
                   jax.ShapeDtypeStruct((t - n_first, D_MODEL), F32)],
        scratch_shapes=[pltpu.VMEM((2, TM_MOE, D_MODEL), F32),
                        pltpu.SemaphoreType.DMA(())],
        compiler_params=_params(("arbitrary",)),
        name="moe_combine",
    )(pos, x, route, fgain, ys)


def kernel(x_prompt, x_sample, norm_mix, w_in, q_norm, k_norm, conv_w, w_oa, w_ob, w_o, norm_ffn,
           ffn_w_gate, ffn_w_up, ffn_w_down, router_w, moe_w_gate, moe_w_up, moe_w_down, final_norm):
    bp, sp, _ = x_prompt.shape
    bs, ss, _ = x_sample.shape
    n_prompt = bp * sp
    n_sample = bs * ss
    depth = norm_mix.shape[0]
    assert sp % TM == 0 and ss % TM == 0 and n_prompt % TM_FFN == 0 and n_sample % TM_FFN == 0
    assert depth == 2 and ffn_w_gate.shape[0] == 1 and moe_w_gate.shape[0] == 1

    t_all = n_prompt + n_sample
    xparts = (x_prompt.reshape(n_prompt, D_MODEL), x_sample.reshape(n_sample, D_MODEL))

    cos, sin = _rope_tables(max(sp, ss))
    n_prompt_tiles, tiles4, tiles2 = n_prompt // TM, sp // TM, ss // TM
    pos_map = lambda i: (jnp.where(i < n_prompt_tiles, i % tiles4, i % tiles2), 0)
    idx = jnp.arange(LANES)
    bd = (idx[:, None] // HEAD_DIM == idx[None, :] // HEAD_DIM).astype(BF16)

    for l in range(depth):
        gain = norm_mix[l][None, :]
        w_l = w_in[l]
        w_qkv = w_l[:, :QKV_WIDTH].astype(BF16)
        w_rest = w_l[:, QKV_WIDTH:].astype(BF16)
        qg = jnp.tile(q_norm[l], 2)[None, :]
        kg = jnp.tile(k_norm[l], 2)[None, :]
        q, k, vt = _qkv_proj(xparts, t_all, n_prompt_tiles, gain, w_qkv, qg, kg, cos, sin, bd,
                             pos_map)
        cb, u, ga, gb = _rest_proj(xparts, t_all, n_prompt_tiles, gain, w_rest)
        kmax = _key_norm_max(k, n_prompt, bp, sp, bs, ss)
        q_bound = Q_SCALE * math.sqrt(HEAD_DIM) * jnp.max(jnp.abs(q_norm[l]))
        bound_ok = 2.0 * q_bound * jnp.max(kmax) < MAX_SHIFT_GAP

        def attend(exact_max, tq_p, tq_s):
            def run(kmax, q, k, vt):
                attn_p = _attention(kmax, q, k, vt, row0=0, seq0=0, n_seq=bp, seq=sp,
                                    tq=tq_p, exact_max=exact_max)
                attn_s = _attention(kmax, q, k, vt, row0=n_prompt, seq0=bp, n_seq=bs, seq=ss,
                                    tq=tq_s, exact_max=exact_max)
                return attn_p, attn_s
            return run

        attn_parts = lax.cond(bound_ok, attend(False, 256, 256), attend(True, 128, 256),
                              kmax, q, k, vt)
        j = l // 2
        is_moe = l % 2 == 1
        rw = jnp.pad(router_w[j], ((0, 0), (0, LANES - N_EXPERTS))) if is_moe else None
        outs = _mix(attn_parts, cb, u, ga, gb, xparts, conv_w[l],
                    w_oa[l].astype(BF16), w_ob[l].astype(BF16), w_o[l].astype(BF16),
                    norm_ffn[l][None, :], rw, n_prompt=n_prompt, seq_p=sp, seq_s=ss)
        if not is_moe:
            x, h2 = outs
            x = _ffn(h2, x, ffn_w_gate[j].astype(BF16), ffn_w_up[j].astype(BF16),
                     ffn_w_down[j].astype(BF16))
            xparts = (x, x)
        else:
            x, h2, logits = outs
            route, counts = _router(logits)
            pos, tile_expert, tile_valid, n_tiles = _route_plan(route, counts, t_all)
            xs = _dispatch(pos, h2, n_tiles * TM_MOE)
            ys = _moe_ffn(tile_expert, tile_valid, xs, moe_w_gate[j].astype(BF16),
                          moe_w_up[j].astype(BF16), moe_w_down[j].astype(BF16))
            xparts = _combine(pos, x, route, final_norm[None, :], ys, n_prompt)

    return (xparts[0].reshape(bp, sp, D_MODEL), xparts[1].reshape(bs, ss, D_MODEL))
```

```python
import functools
import math

import jax
import jax.numpy as jnp
from jax import lax
from jax.experimental import pallas as pl
from jax.experimental.pallas import tpu as pltpu

F32 = jnp.float32
BF16 = jnp.bfloat16

D_MODEL = 1024
N_HEADS = 16
N_KV_HEADS = 4
HEAD_DIM = 64
GROUP = N_HEADS // N_KV_HEADS
KV_WIDTH = N_KV_HEADS * HEAD_DIM
AXIS_DIM = HEAD_DIM // 2
ROPE_THETA = 10000.0
GRID_W = 64
D_FF = 3584
N_EXPERTS = 8
EPS = 1e-6
LANES = 128
QKV_WIDTH = D_MODEL + 2 * KV_WIDTH
REST_WIDTH = 5 * D_MODEL
Q_EXP_WIDTH = N_HEADS * LANES
KV_EXP_WIDTH = N_KV_HEADS * LANES
Q_SCALE = math.log2(math.e) / math.sqrt(HEAD_DIM)
V_ROWS = 80
KEY_CHUNK = 512
ROW_UNROLL = 8
MAX_SHIFT_GAP = 100.0

TM = 512
TM_FFN = 512
TF = 1792
TF_MOE = 1792
TR = 512
TM_MOE = 512
VMEM_LIMIT = 56 * 1024 * 1024


def _params(sem):
    return pltpu.CompilerParams(dimension_semantics=sem, vmem_limit_bytes=VMEM_LIMIT)


def _rms(x, gain):
    return x * lax.rsqrt(jnp.mean(x * x, axis=-1, keepdims=True) + EPS) * gain


def _pick(n_a, a_ref, b_ref):
    return jnp.where(pl.program_id(0) < n_a, a_ref[...], b_ref[...])


def _qkv_kernel(xa_ref, xb_ref, g_ref, w_ref, qg_ref, kg_ref, cos_ref, sin_ref, bd_ref,
                q_ref, k_ref, vt_ref, *, n_a):
    tm = xa_ref.shape[0]
    h = _rms(_pick(n_a, xa_ref, xb_ref), g_ref[...]).astype(BF16)
    p = jnp.dot(h, w_ref[...], preferred_element_type=F32)
    cos = cos_ref[...]
    sin = sin_ref[...]
    bd = bd_ref[...]
    lane = lax.broadcasted_iota(jnp.int32, (tm, LANES), 1)
    first16 = (lane & 31) < 16
    low_half = lane < HEAD_DIM

    def norm_rope(c, gain):
        sq = c * c
        hi = sq.astype(BF16)
        lo = (sq - hi.astype(F32)).astype(BF16)
        ss = (jnp.dot(hi, bd, preferred_element_type=F32)
              + jnp.dot(lo, bd, preferred_element_type=F32))
        n = c * lax.rsqrt(ss * (1.0 / HEAD_DIM) + EPS) * gain
        partner = jnp.where(first16, pltpu.roll(n, LANES - 16, 1), pltpu.roll(n, 16, 1))
        return n * cos + partner * sin

    is_aux = lane == HEAD_DIM
    zero = jnp.zeros((tm, LANES), F32)
    ones_aux = jnp.where(is_aux, 1.0, zero)

    qg = qg_ref[...]
    for c in range(N_HEADS // 2):
        r = norm_rope(p[:, c * LANES:(c + 1) * LANES], qg)
        even = jnp.where(low_half, r, zero)
        odd = jnp.where(low_half, pltpu.roll(r, HEAD_DIM, 1), zero)
        q_ref[:, (2 * c) * LANES:(2 * c + 1) * LANES] = even.astype(BF16)
        q_ref[:, (2 * c + 1) * LANES:(2 * c + 2) * LANES] = odd.astype(BF16)

    kg = kg_ref[...]
    for c in range(KV_WIDTH // LANES):
        kn = norm_rope(p[:, D_MODEL + c * LANES:D_MODEL + (c + 1) * LANES], kg)
        even = jnp.where(low_half, kn, ones_aux)
        odd = jnp.where(low_half, pltpu.roll(kn, HEAD_DIM, 1), ones_aux)
        k_ref[:, (2 * c) * LANES:(2 * c + 1) * LANES] = even.astype(BF16)
        k_ref[:, (2 * c + 1) * LANES:(2 * c + 2) * LANES] = odd.astype(BF16)
        vv = p[:, D_MODEL + KV_WIDTH + c * LANES:D_MODEL + KV_WIDTH + (c + 1) * LANES]
        even = jnp.where(low_half, vv, ones_aux)
        odd = jnp.where(low_half, pltpu.roll(vv, HEAD_DIM, 1), ones_aux)
        vt_ref[(2 * c) * LANES:(2 * c + 1) * LANES, :] = even.T.astype(BF16)
        vt_ref[(2 * c + 1) * LANES:(2 * c + 2) * LANES, :] = odd.T.astype(BF16)


def _rest_kernel(xa_ref, xb_ref, g_ref, w_ref, cb_ref, u_ref, ga_ref, gb_ref, *, n_a):
    h = _rms(_pick(n_a, xa_ref, xb_ref), g_ref[...]).astype(BF16)
    p = jnp.dot(h, w_ref[...], preferred_element_type=F32)
    cb_ref[...] = p[:, :D_MODEL].astype(BF16)
    u_ref[...] = (p[:, D_MODEL:2 * D_MODEL] * p[:, 2 * D_MODEL:3 * D_MODEL]).astype(BF16)
    ga_ref[...] = p[:, 3 * D_MODEL:4 * D_MODEL].astype(BF16)
    gb_ref[...] = p[:, 4 * D_MODEL:].astype(BF16)


def _attn_kernel(shift_ref, q_ref, k_ref, vt_ref, o_ref, *, exact_max):
    tq = q_ref.shape[0]
    is_aux_row = lax.broadcasted_iota(jnp.int32, (LANES, GROUP * tq), 0) == HEAD_DIM
    for j in range(N_KV_HEADS):
        qt = jnp.concatenate(
            [q_ref[:, h * LANES:(h + 1) * LANES].T for h in range(GROUP * j, GROUP * (j + 1))],
            axis=1)
        if exact_max:
            st = jnp.dot(k_ref[:, j * LANES:(j + 1) * LANES], qt,
                         preferred_element_type=F32)
            st = st - jnp.max(st, axis=0, keepdims=True)
            pt = jnp.exp2(st).astype(BF16)
            ot = jnp.dot(vt_ref[j * LANES:j * LANES + V_ROWS, :], pt,
                         preferred_element_type=F32)
        else:
            neg_shift = jnp.full(qt.shape, -shift_ref[0], F32).astype(BF16)
            qt = jnp.where(is_aux_row, neg_shift, qt)
            ot = jnp.zeros((V_ROWS, GROUP * tq), F32)
            for c in range(k_ref.shape[0] // KEY_CHUNK):
                rows = slice(c * KEY_CHUNK, (c + 1) * KEY_CHUNK)
                st = jnp.dot(k_ref[rows, j * LANES:(j + 1) * LANES], qt,
                             preferred_element_type=F32)
                pt = jnp.exp2(st).astype(BF16)
                ot = ot + jnp.dot(vt_ref[j * LANES:j * LANES + V_ROWS, rows], pt,
                                  preferred_element_type=F32)
        ot = ot[:HEAD_DIM] / ot[HEAD_DIM:HEAD_DIM + 1]
        for a in range(2):
            pair = jnp.concatenate([ot[:, (2 * a) * tq:(2 * a + 1) * tq],
                                    ot[:, (2 * a + 1) * tq:(2 * a + 2) * tq]], axis=0)
            c = 2 * j + a
            o_ref[:, c * LANES:(c + 1) * LANES] = pair.T.astype(BF16)


def _mix_kernel(attn_a_ref, attn_b_ref, cb_ref, u_ref, up_ref, un_ref, ga_ref, gb_ref,
                xa_ref, xb_ref, cw_ref, woa_ref, wob_ref, wo_ref, g_ref, *rest,
                n_prompt_tiles, tiles4, tiles2, with_router):
    if with_router:
        rw_ref, xo_ref, h2_ref, lg_ref = rest
    else:
        xo_ref, h2_ref = rest
    i = pl.program_id(0)
    tm = xa_ref.shape[0]
    is_prompt = i < n_prompt_tiles
    seq_start = jnp.where(is_prompt, i % tiles4 == 0, i % tiles2 == 0)
    seq_end = jnp.where(is_prompt, i % tiles4 == tiles4 - 1, i % tiles2 == tiles2 - 1)

    attn = _pick(n_prompt_tiles, attn_a_ref, attn_b_ref)
    ya = jnp.dot(attn, woa_ref[...], preferred_element_type=F32)

    u = u_ref[...].astype(F32)
    row = lax.broadcasted_iota(jnp.int32, u.shape, 0)
    prev_row = jnp.where(seq_start, 0.0, up_ref[7:8, :].astype(F32))
    next_row = jnp.where(seq_end, 0.0, un_ref[0:1, :].astype(F32))
    u_prev = jnp.where(row == 0, prev_row, pltpu.roll(u, 1, 0))
    u_next = jnp.where(row == tm - 1, next_row, pltpu.roll(u, tm - 1, 0))
    cw = cw_ref[...]
    conv = cw[0:1, :] * u_prev + cw[1:2, :] * u + cw[2:3, :] * u_next
    yb_in = (cb_ref[...].astype(F32) * conv).astype(BF16)
    yb = jnp.dot(yb_in, wob_ref[...], preferred_element_type=F32)

    m = (jax.nn.sigmoid(ga_ref[...].astype(F32)) * ya
         + jax.nn.sigmoid(gb_ref[...].astype(F32)) * yb)
    xn = (_pick(n_prompt_tiles, xa_ref, xb_ref)
          + jnp.dot(m.astype(BF16), wo_ref[...], preferred_element_type=F32))
    xo_ref[...] = xn
    h2 = _rms(xn, g_ref[...])
    h2_ref[...] = h2.astype(h2_ref.dtype)
    if not with_router:
        return
    hi = h2.astype(BF16)
    lo = (h2 - hi.astype(F32)).astype(BF16)
    rw = rw_ref[...]
    rhi = rw.astype(BF16)
    rlo = (rw - rhi.astype(F32)).astype(BF16)
    lane = lax.broadcasted_iota(jnp.int32, rw.shape, 1)
    hi_terms = jnp.dot(hi, jnp.where(lane < N_EXPERTS, rhi, rlo), preferred_element_type=F32)
    lg_ref[...] = (hi_terms + pltpu.roll(hi_terms, LANES - N_EXPERTS, 1)
                   + jnp.dot(lo, rhi, preferred_element_type=F32))


def _ffn_kernel(h_ref, x_ref, wg_ref, wu_ref, wd_ref, o_ref, acc_ref):
    j = pl.program_id(1)

    @pl.when(j == 0)
    def _():
        acc_ref[...] = jnp.zeros_like(acc_ref)

    h = h_ref[...]
    g = jnp.dot(h, wg_ref[...], preferred_element_type=F32)
    u = jnp.dot(h, wu_ref[...], preferred_element_type=F32)
    a = (g * jax.nn.sigmoid(g) * u).astype(BF16)
    acc_ref[...] += jnp.dot(a, wd_ref[...], preferred_element_type=F32)

    @pl.when(j == pl.num_programs(1) - 1)
    def _():
        o_ref[...] = x_ref[...] + acc_ref[...]


ROUTE_E0, ROUTE_E1, ROUTE_R0, ROUTE_R1, ROUTE_W0, ROUTE_W1 = range(6)


def _lane_pick(x, lane, k):
    return jnp.sum(jnp.where(lane == k, x, 0.0), axis=-1, keepdims=True)


def _router_kernel(lg_ref, tri_ref, route_ref, count_ref, base_ref):
    i = pl.program_id(0)

    @pl.when(i == 0)
    def _():
        base_ref[...] = jnp.zeros_like(base_ref)

    lg = lg_ref[...]
    lane = lax.broadcasted_iota(jnp.int32, lg.shape, 1)
    neg = jnp.float32(-jnp.inf)
    l1 = jnp.where(lane < N_EXPERTS, lg, neg)
    m1 = jnp.max(l1, axis=-1, keepdims=True)
    i1 = jnp.min(jnp.where(l1 == m1, lane, LANES), axis=-1, keepdims=True)
    l2 = jnp.where(lane == i1, neg, l1)
    m2 = jnp.max(l2, axis=-1, keepdims=True)
    i2 = jnp.min(jnp.where(l2 == m2, lane, LANES), axis=-1, keepdims=True)
    e = jnp.exp(m2 - m1)
    w1 = 1.0 / (1.0 + e)
    w2 = e / (1.0 + e)

    hot1 = lane == i1
    hot2 = lane == i2
    onehot = jnp.where(hot1 | hot2, 1.0, 0.0)
    base = base_ref[0:1, :]
    prefix = jnp.dot(tri_ref[...], onehot.astype(BF16), preferred_element_type=F32) + base
    r1 = jnp.sum(jnp.where(hot1, prefix, 0.0), axis=-1, keepdims=True)
    r2 = jnp.sum(jnp.where(hot2, prefix, 0.0), axis=-1, keepdims=True)
    total = base + jnp.sum(onehot, axis=0, keepdims=True)
    base_ref[...] = jnp.broadcast_to(total, base_ref.shape)
    count_ref[...] = jnp.broadcast_to(total, count_ref.shape)

    rec = jnp.where(lane == ROUTE_E0, i1.astype(F32), 0.0)
    rec = jnp.where(lane == ROUTE_E1, i2.astype(F32), rec)
    rec = jnp.where(lane == ROUTE_R0, r1, rec)
    rec = jnp.where(lane == ROUTE_R1, r2, rec)
    rec = jnp.where(lane == ROUTE_W0, w1, rec)
    rec = jnp.where(lane == ROUTE_W1, w2, rec)
    route_ref[...] = rec


def _row_copy(src, src_row, dst, dst_row, sem):
    return pltpu.make_async_copy(src.at[pl.ds(src_row, 1)], dst.at[pl.ds(dst_row, 1)], sem)


def _dispatch_kernel(pos_ref, h_ref, xs_in_ref, xs_ref, sem):
    del xs_in_ref
    tm = h_ref.shape[0]

    def issue(i, carry):
        for k in range(ROW_UNROLL):
            r = i * ROW_UNROLL + k
            _row_copy(h_ref, r, xs_ref, pos_ref[0, 0, r], sem).start()
            _row_copy(h_ref, r, xs_ref, pos_ref[0, 0, tm + r], sem).start()
        return carry

    lax.fori_loop(0, tm // ROW_UNROLL, issue, 0)
    for _ in range(2):
        pltpu.make_async_copy(h_ref, xs_ref.at[pl.ds(0, tm)], sem).wait()


def _moe_ffn_kernel(te_ref, tv_ref, x_ref, wg_ref, wu_ref, wd_ref, y_ref, acc_ref):
    del te_ref
    g_idx = pl.program_id(0)
    j = pl.program_id(1)
    valid = tv_ref[g_idx] > 0

    @pl.when(j == 0)
    def _():
        acc_ref[...] = jnp.zeros_like(acc_ref)

    @pl.when(valid)
    def _():
        h = x_ref[...].astype(BF16)
        g = jnp.dot(h, wg_ref[0], preferred_element_type=F32)
        u = jnp.dot(h, wu_ref[0], preferred_element_type=F32)
        a = (g * jax.nn.sigmoid(g) * u).astype(BF16)
        acc_ref[...] += jnp.dot(a, wd_ref[0], preferred_element_type=F32)

    @pl.when(j == pl.num_programs(1) - 1)
    def _():
        y_ref[...] = acc_ref[...]


def _combine_kernel(pos_ref, x_ref, route_ref, fg_ref, y_hbm, oa_ref, ob_ref, ybuf, sem, *, n_a):
    tc = x_ref.shape[0]

    def issue(i, carry):
        for k in range(ROW_UNROLL):
            r = i * ROW_UNROLL + k
            _row_copy(y_hbm, pos_ref[0, 0, r], ybuf.at[0], r, sem).start()
            _row_copy(y_hbm, pos_ref[0, 0, tc + r], ybuf.at[1], r, sem).start()
        return carry

    lax.fori_loop(0, tc // ROW_UNROLL, issue, 0)
    for slot in range(2):
        pltpu.make_async_copy(y_hbm.at[pl.ds(0, tc)], ybuf.at[slot], sem).wait()
    route = route_ref[...]
    lane = lax.broadcasted_iota(jnp.int32, route.shape, 1)
    w0 = _lane_pick(route, lane, ROUTE_W0)
    w1 = _lane_pick(route, lane, ROUTE_W1)
    out = _rms(x_ref[...] + (w0 * ybuf[0] + w1 * ybuf[1]), fg_ref[...])
    in_a = pl.program_id(0) < n_a

    @pl.when(in_a)
    def _():
        oa_ref[...] = out

    @pl.when(jnp.logical_not(in_a))
    def _():
        ob_ref[...] = out


def _rope_tables(max_seq):
    t = jnp.arange(max_seq, dtype=jnp.int32)
    row = (t // GRID_W).astype(F32)
    col = (t % GRID_W).astype(F32)
    inv = 1.0 / (ROPE_THETA ** (jnp.arange(0, AXIS_DIM, 2, dtype=F32) / AXIS_DIM))
    ar = row[:, None] * inv[None, :]
    ac = col[:, None] * inv[None, :]
    cos64 = jnp.concatenate([jnp.cos(ar), jnp.cos(ar), jnp.cos(ac), jnp.cos(ac)], axis=-1)
    sin64 = jnp.concatenate([-jnp.sin(ar), jnp.sin(ar), -jnp.sin(ac), jnp.sin(ac)], axis=-1)
    return jnp.tile(cos64, (1, 2)), jnp.tile(sin64, (1, 2))


def _two_part_specs(block, n_a, parts):
    off = 0 if parts[0] is parts[1] else n_a
    return [pl.BlockSpec(block, lambda i: (jnp.minimum(i, n_a - 1), 0)),
            pl.BlockSpec(block, lambda i: (jnp.maximum(i, n_a) - off, 0))]


def _qkv_proj(xparts, t, n_a, gain, w, qg, kg, cos, sin, bd, pos_map):
    row = lambda i: (i, 0)
    const = lambda i: (0, 0)
    return pl.pallas_call(
        functools.partial(_qkv_kernel, n_a=n_a),
        grid=(t // TM,),
        in_specs=_two_part_specs((TM, D_MODEL), n_a, xparts) + [
                  pl.BlockSpec((1, D_MODEL), const),
                  pl.BlockSpec((D_MODEL, QKV_WIDTH), const),
                  pl.BlockSpec((1, LANES), const),
                  pl.BlockSpec((1, LANES), const),
                  pl.BlockSpec((TM, LANES), pos_map),
                  pl.BlockSpec((TM, LANES), pos_map),
                  pl.BlockSpec((LANES, LANES), const)],
        out_specs=[pl.BlockSpec((TM, Q_EXP_WIDTH), row),
                   pl.BlockSpec((TM, KV_EXP_WIDTH), row),
                   pl.BlockSpec((KV_EXP_WIDTH, TM), lambda i: (0, i))],
        out_shape=[jax.ShapeDtypeStruct((t, Q_EXP_WIDTH), BF16),
                   jax.ShapeDtypeStruct((t, KV_EXP_WIDTH), BF16),
                   jax.ShapeDtypeStruct((KV_EXP_WIDTH, t), BF16)],
        compiler_params=_params(("parallel",)),
        name="qkv_proj",
    )(*xparts, gain, w, qg, kg, cos, sin, bd)


def _rest_proj(xparts, t, n_a, gain, w):
    row = lambda i: (i, 0)
    const = lambda i: (0, 0)
    out = jax.ShapeDtypeStruct((t, D_MODEL), BF16)
    return pl.pallas_call(
        functools.partial(_rest_kernel, n_a=n_a),
        grid=(t // TM,),
        in_specs=_two_part_specs((TM, D_MODEL), n_a, xparts) + [
                  pl.BlockSpec((1, D_MODEL), const),
                  pl.BlockSpec((D_MODEL, REST_WIDTH), const)],
        out_specs=[pl.BlockSpec((TM, D_MODEL), row)] * 4,
        out_shape=[out] * 4,
        compiler_params=_params(("parallel",)),
        name="rest_proj",
    )(*xparts, gain, w)


def _attention(shift, q, k, vt, *, row0, n_seq, seq, tq, exact_max):
    nq = seq // tq
    q0 = row0 // tq
    s0 = row0 // seq
    return pl.pallas_call(
        functools.partial(_attn_kernel, exact_max=exact_max),
        grid_spec=pltpu.PrefetchScalarGridSpec(
            num_scalar_prefetch=1,
            grid=(n_seq, nq),
            in_specs=[pl.BlockSpec((tq, Q_EXP_WIDTH), lambda b, i, km: (q0 + b * nq + i, 0)),
                      pl.BlockSpec((seq, KV_EXP_WIDTH), lambda b, i, km: (s0 + b, 0)),
                      pl.BlockSpec((KV_EXP_WIDTH, seq), lambda b, i, km: (0, s0 + b))],
            out_specs=pl.BlockSpec((tq, D_MODEL), lambda b, i, km: (b * nq + i, 0))),
        out_shape=jax.ShapeDtypeStruct((n_seq * seq, D_MODEL), BF16),
        compiler_params=_params(("parallel", "parallel")),
        name=f"attention_s{seq}" + ("_exact" if exact_max else ""),
    )(shift, q, k, vt)


def _mix(attn_parts, cb, u, ga, gb, xparts, cw, woa, wob, wo, gain, rw, *, n_prompt, seq_p, seq_s):
    t = cb.shape[0]
    n_a = n_prompt // TM
    row = lambda i: (i, 0)
    const = lambda i: (0, 0)
    sub = TM // 8
    last = t // 8 - 1
    act = pl.BlockSpec((TM, D_MODEL), row)
    wspec = pl.BlockSpec((D_MODEL, D_MODEL), const)
    with_router = rw is not None
    kern = functools.partial(_mix_kernel, n_prompt_tiles=n_prompt // TM,
                             tiles4=seq_p // TM, tiles2=seq_s // TM, with_router=with_router)
    in_specs = (_two_part_specs((TM, D_MODEL), n_a, attn_parts) + [
                act, act,
                pl.BlockSpec((8, D_MODEL), lambda i: (jnp.maximum(i * sub - 1, 0), 0)),
                pl.BlockSpec((8, D_MODEL), lambda i: (jnp.minimum((i + 1) * sub, last), 0)),
                act, act]
                + _two_part_specs((TM, D_MODEL), n_a, xparts) + [
                pl.BlockSpec((3, D_MODEL), const),
                wspec, wspec, wspec,
                pl.BlockSpec((1, D_MODEL), const)])
    args = [*attn_parts, cb, u, u, u, ga, gb, *xparts, cw, woa, wob, wo, gain]
    out_specs = [act, act]
    out_shape = [jax.ShapeDtypeStruct((t, D_MODEL), F32),
                 jax.ShapeDtypeStruct((t, D_MODEL), F32 if with_router else BF16)]
    if with_router:
        in_specs.append(pl.BlockSpec((D_MODEL, LANES), const))
        args.append(rw)
        out_specs.append(pl.BlockSpec((TM, LANES), row))
        out_shape.append(jax.ShapeDtypeStruct((t, LANES), F32))
    return pl.pallas_call(
        kern,
        grid=(t // TM,),
        in_specs=in_specs,
        out_specs=out_specs,
        out_shape=out_shape,
        compiler_params=_params(("parallel",)),
        name="mix_proj_router" if with_router else "mix_proj",
    )(*args)


def _ffn(h, x, wg, wu, wd):
    t = x.shape[0]
    row = lambda i, j: (i, 0)
    return pl.pallas_call(
        _ffn_kernel,
        grid=(t // TM_FFN, D_FF // TF),
        in_specs=[pl.BlockSpec((TM_FFN, D_MODEL), row),
                  pl.BlockSpec((TM_FFN, D_MODEL), row),
                  pl.BlockSpec((D_MODEL, TF), lambda i, j: (0, j)),
                  pl.BlockSpec((D_MODEL, TF), lambda i, j: (0, j)),
                  pl.BlockSpec((TF, D_MODEL), lambda i, j: (j, 0))],
        out_specs=pl.BlockSpec((TM_FFN, D_MODEL), row),
        out_shape=jax.ShapeDtypeStruct((t, D_MODEL), F32),
        scratch_shapes=[pltpu.VMEM((TM_FFN, D_MODEL), F32)],
        compiler_params=_params(("parallel", "arbitrary")),
        name="ffn_dense",
    )(h, x, wg, wu, wd)


def _router(logits):
    t = logits.shape[0]
    row = lambda i: (i, 0)
    const = lambda i: (0, 0)
    r = jnp.arange(TR)
    tri = (r[None, :] < r[:, None]).astype(BF16)
    return pl.pallas_call(
        _router_kernel,
        grid=(t // TR,),
        in_specs=[pl.BlockSpec((TR, LANES), row),
                  pl.BlockSpec((TR, TR), const)],
        out_specs=[pl.BlockSpec((TR, LANES), row),
                   pl.BlockSpec((8, LANES), const)],
        out_shape=[jax.ShapeDtypeStruct((t, LANES), F32),
                   jax.ShapeDtypeStruct((8, LANES), F32)],
        scratch_shapes=[pltpu.VMEM((8, LANES), F32)],
        compiler_params=_params(("arbitrary",)),
        name="router",
    )(logits, tri)


def _route_plan(route, counts, t):
    e0 = route[:, ROUTE_E0].astype(jnp.int32)
    e1 = route[:, ROUTE_E1].astype(jnp.int32)
    r0 = route[:, ROUTE_R0].astype(jnp.int32)
    r1 = route[:, ROUTE_R1].astype(jnp.int32)
    cnt = counts[0, :N_EXPERTS].astype(jnp.int32)
    padded = ((cnt + TM_MOE - 1) // TM_MOE) * TM_MOE
    ends = jnp.cumsum(padded)
    starts = ends - padded
    experts = jnp.arange(N_EXPERTS, dtype=jnp.int32)
    pos0 = jnp.sum(jnp.where(e0[:, None] == experts[None, :], starts[None, :], 0), axis=1) + r0
    pos1 = jnp.sum(jnp.where(e1[:, None] == experts[None, :], starts[None, :], 0), axis=1) + r1
    n_tok_tiles = t // TM_MOE
    pos = jnp.concatenate([pos0.reshape(n_tok_tiles, 1, TM_MOE),
                           pos1.reshape(n_tok_tiles, 1, TM_MOE)], axis=2)
    n_tiles = 2 * t // TM_MOE + N_EXPERTS
    tile_start = jnp.arange(n_tiles, dtype=jnp.int32) * TM_MOE
    tile_valid = (tile_start < ends[-1]).astype(jnp.int32)
    tile_expert = jnp.sum((tile_start[:, None] >= ends[None, :]).astype(jnp.int32), axis=1)
    tile_expert = jnp.minimum(tile_expert, N_EXPERTS - 1)
    return pos, tile_expert, tile_valid, n_tiles


def _dispatch(pos, h, n_rows):
    t = h.shape[0]
    zeros = jnp.zeros((n_rows, D_MODEL), F32)
    return pl.pallas_call(
        _dispatch_kernel,
        grid=(t // TM_MOE,),
        in_specs=[pl.BlockSpec((1, 1, 2 * TM_MOE), lambda i: (i, 0, 0), memory_space=pltpu.SMEM),
                  pl.BlockSpec((TM_MOE, D_MODEL), lambda i: (i, 0)),
                  pl.BlockSpec(memory_space=pl.ANY)],
        out_specs=pl.BlockSpec(memory_space=pl.ANY),
        out_shape=jax.ShapeDtypeStruct((n_rows, D_MODEL), F32),
        scratch_shapes=[pltpu.SemaphoreType.DMA(())],
        input_output_aliases={2: 0},
        compiler_params=_params(("arbitrary",)),
        name="moe_dispatch",
    )(pos, h, zeros)


def _moe_ffn(tile_expert, tile_valid, xs, wg, wu, wd):
    n_rows = xs.shape[0]
    row = lambda g, j, te, tv: (g, 0)
    return pl.pallas_call(
        _moe_ffn_kernel,
        grid_spec=pltpu.PrefetchScalarGridSpec(
            num_scalar_prefetch=2,
            grid=(n_rows // TM_MOE, D_FF // TF_MOE),
            in_specs=[pl.BlockSpec((TM_MOE, D_MODEL), row),
                      pl.BlockSpec((1, D_MODEL, TF_MOE),
                                   lambda g, j, te, tv: (te[g], 0, j * tv[g])),
                      pl.BlockSpec((1, D_MODEL, TF_MOE),
                                   lambda g, j, te, tv: (te[g], 0, j * tv[g])),
                      pl.BlockSpec((1, TF_MOE, D_MODEL),
                                   lambda g, j, te, tv: (te[g], j * tv[g], 0))],
            out_specs=pl.BlockSpec((TM_MOE, D_MODEL), row),
            scratch_shapes=[pltpu.VMEM((TM_MOE, D_MODEL), F32)]),
        out_shape=jax.ShapeDtypeStruct((n_rows, D_MODEL), F32),
        compiler_params=_params(("parallel", "arbitrary")),
        name="moe_ffn",
    )(tile_expert, tile_valid, xs, wg, wu, wd)


def _combine(pos, x, route, fgain, ys, n_first):
    t = x.shape[0]
    n_a = n_first // TM_MOE
    row = lambda i: (i, 0)
    blk = (TM_MOE, D_MODEL)
    return pl.pallas_call(
        functools.partial(_combine_kernel, n_a=n_a),
        grid=(t // TM_MOE,),
        in_specs=[pl.BlockSpec((1, 1, 2 * TM_MOE), lambda i: (i, 0, 0), memory_space=pltpu.SMEM),
                  pl.BlockSpec(blk, row),
                  pl.BlockSpec((TM_MOE, LANES), row),
                  pl.BlockSpec((1, D_MODEL), lambda i: (0, 0)),
                  pl.BlockSpec(memory_space=pl.ANY)],
        out_specs=[pl.BlockSpec(blk, lambda i: (jnp.minimum(i, n_a - 1), 0)),
                   pl.BlockSpec(blk, lambda i: (jnp.maximum(i - n_a, 0), 0))],
        out_shape=[jax.ShapeDtypeStruct((n_first, D_MODEL), F32),
                   jax.ShapeDtypeStruct((t - n_first, D_MODEL), F32)],
        scratch_shapes=[pltpu.VMEM((2, TM_MOE, D_MODEL), F32),
                        pltpu.SemaphoreType.DMA(())],
        compiler_params=_params(("arbitrary",)),
        name="moe_combine",
    )(pos, x, route, fgain, ys)


def kernel(x_prompt, x_sample, norm_mix, w_in, q_norm, k_norm, conv_w, w_oa, w_ob, w_o, norm_ffn,
           ffn_w_gate, ffn_w_up, ffn_w_down, router_w, moe_w_gate, moe_w_up, moe_w_down, final_norm):
    bp, sp, _ = x_prompt.shape
    bs, ss, _ = x_sample.shape
    n_prompt = bp * sp
    n_sample = bs * ss
    depth = norm_mix.shape[0]
    assert sp % TM == 0 and ss % TM == 0 and n_prompt % TM_FFN == 0 and n_sample % TM_FFN == 0
    assert depth == 2 and ffn_w_gate.shape[0] == 1 and moe_w_gate.shape[0] == 1

    t_all = n_prompt + n_sample
    xparts = (x_prompt.reshape(n_prompt, D_MODEL), x_sample.reshape(n_sample, D_MODEL))

    cos, sin = _rope_tables(max(sp, ss))
    n_prompt_tiles, tiles4, tiles2 = n_prompt // TM, sp // TM, ss // TM
    pos_map = lambda i: (jnp.where(i < n_prompt_tiles, i % tiles4, i % tiles2), 0)
    idx = jnp.arange(LANES)
    bd = (idx[:, None] // HEAD_DIM == idx[None, :] // HEAD_DIM).astype(BF16)

    for l in range(depth):
        gain = norm_mix[l][None, :]
        w_l = w_in[l]
        w_qkv = w_l[:, :QKV_WIDTH].astype(BF16)
        w_rest = w_l[:, QKV_WIDTH:].astype(BF16)
        qg = jnp.tile(q_norm[l] * Q_SCALE, 2)[None, :]
        kg = jnp.tile(k_norm[l], 2)[None, :]
        q, k, vt = _qkv_proj(xparts, t_all, n_prompt_tiles, gain, w_qkv, qg, kg, cos, sin, bd,
                             pos_map)
        cb, u, ga, gb = _rest_proj(xparts, t_all, n_prompt_tiles, gain, w_rest)
        shift = (Q_SCALE * HEAD_DIM * jnp.max(jnp.abs(q_norm[l]))
                 * jnp.max(jnp.abs(k_norm[l]))).reshape(1)
        bound_ok = 2.0 * shift[0] < MAX_SHIFT_GAP

        def attend(exact_max, tq_p, tq_s):
            def run(shift, q, k, vt):
                attn_p = _attention(shift, q, k, vt, row0=0, n_seq=bp, seq=sp,
                                    tq=tq_p, exact_max=exact_max)
                attn_s = _attention(shift, q, k, vt, row0=n_prompt, n_seq=bs, seq=ss,
                                    tq=tq_s, exact_max=exact_max)
                return attn_p, attn_s
            return run

        attn_parts = lax.cond(bound_ok, attend(False, 256, 256), attend(True, 128, 256),
                              shift, q, k, vt)
        j = l // 2
        is_moe = l % 2 == 1
        rw = (jnp.pad(jnp.tile(router_w[j], (1, 2)), ((0, 0), (0, LANES - 2 * N_EXPERTS)))
              if is_moe else None)
        outs = _mix(attn_parts, cb, u, ga, gb, xparts, conv_w[l],
                    w_oa[l].astype(BF16), w_ob[l].astype(BF16), w_o[l].astype(BF16),
                    norm_ffn[l][None, :], rw, n_prompt=n_prompt, seq_p=sp, seq_s=ss)
        if not is_moe:
            x, h2 = outs
            x = _ffn(h2, x, ffn_w_gate[j].astype(BF16), ffn_w_up[j].astype(BF16),
                     ffn_w_down[j].astype(BF16))
            xparts = (x, x)
        else:
            x, h2, logits = outs
            route, counts = _router(logits)
            pos, tile_expert, tile_valid, n_tiles = _route_plan(route, counts, t_all)
            xs = _dispatch(pos, h2, n_tiles * TM_MOE)
            ys = _moe_ffn(tile_expert, tile_valid, xs, moe_w_gate[j].astype(BF16),
                          moe_w_up[j].astype(BF16), moe_w_down[j].astype(BF16))
            xparts = _combine(pos, x, route, final_norm[None, :], ys, n_prompt)

    return (xparts[0].reshape(bp, sp, D_MODEL), xparts[1].reshape(bs, ss, D_MODEL))
```

```python
import functools
import math

import jax
import jax.numpy as jnp
from jax import lax
from jax.experimental import pallas as pl
from jax.experimental.pallas import tpu as pltpu

F32 = jnp.float32
BF16 = jnp.bfloat16

D_MODEL = 1024
N_HEADS = 16
N_KV_HEADS = 4
HEAD_DIM = 64
GROUP = N_HEADS // N_KV_HEADS
KV_WIDTH = N_KV_HEADS * HEAD_DIM
AXIS_DIM = HEAD_DIM // 2
ROPE_THETA = 10000.0
GRID_W = 64
D_FF = 3584
N_EXPERTS = 8
EPS = 1e-6
LANES = 128
QKV_WIDTH = D_MODEL + 2 * KV_WIDTH
REST_WIDTH = 5 * D_MODEL
Q_EXP_WIDTH = N_HEADS * LANES
KV_EXP_WIDTH = N_KV_HEADS * LANES
Q_SCALE = math.log2(math.e) / math.sqrt(HEAD_DIM)
V_ROWS = 80
KEY_CHUNK = 512
ROW_UNROLL = 8
MAX_SHIFT_GAP = 100.0

TM = 512
TM_FFN = 512
TF = 1792
TF_MOE = 1792
TR = 512
TM_MOE = 512
VMEM_LIMIT = 56 * 1024 * 1024


def _params(sem):
    return pltpu.CompilerParams(dimension_semantics=sem, vmem_limit_bytes=VMEM_LIMIT)


def _rms(x, gain):
    return x * lax.rsqrt(jnp.mean(x * x, axis=-1, keepdims=True) + EPS) * gain


def _pick(n_a, a_ref, b_ref):
    return jnp.where(pl.program_id(0) < n_a, a_ref[...], b_ref[...])


def _qkv_kernel(xa_ref, xb_ref, g_ref, w_ref, qg_ref, kg_ref, cos_ref, sin_ref, bd_ref,
                q_ref, k_ref, vt_ref, *, n_a):
    tm = xa_ref.shape[0]
    h = _rms(_pick(n_a, xa_ref, xb_ref), g_ref[...]).astype(BF16)
    p = jnp.dot(h, w_ref[...], preferred_element_type=F32)
    cos = cos_ref[...]
    sin = sin_ref[...]
    bd = bd_ref[...]
    lane = lax.broadcasted_iota(jnp.int32, (tm, LANES), 1)
    first16 = (lane & 31) < 16
    low_half = lane < HEAD_DIM

    def norm_rope(c, gain):
        sq = c * c
        hi = sq.astype(BF16)
        lo = (sq - hi.astype(F32)).astype(BF16)
        ss = (jnp.dot(hi, bd, preferred_element_type=F32)
              + jnp.dot(lo, bd, preferred_element_type=F32))
        n = c * lax.rsqrt(ss * (1.0 / HEAD_DIM) + EPS) * gain
        partner = jnp.where(first16, pltpu.roll(n, LANES - 16, 1), pltpu.roll(n, 16, 1))
        return n * cos + partner * sin

    is_aux = lane == HEAD_DIM
    zero = jnp.zeros((tm, LANES), F32)
    ones_aux = jnp.where(is_aux, 1.0, zero)

    qg = qg_ref[...]
    for c in range(N_HEADS // 2):
        r = norm_rope(p[:, c * LANES:(c + 1) * LANES], qg)
        even = jnp.where(low_half, r, zero)
        odd = jnp.where(low_half, pltpu.roll(r, HEAD_DIM, 1), zero)
        q_ref[:, (2 * c) * LANES:(2 * c + 1) * LANES] = even.astype(BF16)
        q_ref[:, (2 * c + 1) * LANES:(2 * c + 2) * LANES] = odd.astype(BF16)

    kg = kg_ref[...]
    for c in range(KV_WIDTH // LANES):
        kn = norm_rope(p[:, D_MODEL + c * LANES:D_MODEL + (c + 1) * LANES], kg)
        even = jnp.where(low_half, kn, ones_aux)
        odd = jnp.where(low_half, pltpu.roll(kn, HEAD_DIM, 1), ones_aux)
        k_ref[:, (2 * c) * LANES:(2 * c + 1) * LANES] = even.astype(BF16)
        k_ref[:, (2 * c + 1) * LANES:(2 * c + 2) * LANES] = odd.astype(BF16)
        vv = p[:, D_MODEL + KV_WIDTH + c * LANES:D_MODEL + KV_WIDTH + (c + 1) * LANES]
        even = jnp.where(low_half, vv, ones_aux)
        odd = jnp.where(low_half, pltpu.roll(vv, HEAD_DIM, 1), ones_aux)
        vt_ref[(2 * c) * LANES:(2 * c + 1) * LANES, :] = even.T.astype(BF16)
        vt_ref[(2 * c + 1) * LANES:(2 * c + 2) * LANES, :] = odd.T.astype(BF16)


def _rest_kernel(xa_ref, xb_ref, g_ref, w_ref, cb_ref, u_ref, ga_ref, gb_ref, *, n_a):
    h = _rms(_pick(n_a, xa_ref, xb_ref), g_ref[...]).astype(BF16)
    p = jnp.dot(h, w_ref[...], preferred_element_type=F32)
    cb_ref[...] = p[:, :D_MODEL].astype(BF16)
    u_ref[...] = (p[:, D_MODEL:2 * D_MODEL] * p[:, 2 * D_MODEL:3 * D_MODEL]).astype(BF16)
    ga_ref[...] = p[:, 3 * D_MODEL:4 * D_MODEL].astype(BF16)
    gb_ref[...] = p[:, 4 * D_MODEL:].astype(BF16)


def _attn_kernel(shift_ref, q_ref, k_ref, vt_ref, o_ref, *, exact_max):
    tq = q_ref.shape[0]
    is_aux_row = lax.broadcasted_iota(jnp.int32, (LANES, GROUP * tq), 0) == HEAD_DIM
    for j in range(N_KV_HEADS):
        qt = jnp.concatenate(
            [q_ref[:, h * LANES:(h + 1) * LANES].T for h in range(GROUP * j, GROUP * (j + 1))],
            axis=1)
        if exact_max:
            st = jnp.dot(k_ref[:, j * LANES:(j + 1) * LANES], qt,
                         preferred_element_type=F32)
            st = st - jnp.max(st, axis=0, keepdims=True)
            pt = jnp.exp2(st).astype(BF16)
            ot = jnp.dot(vt_ref[j * LANES:j * LANES + V_ROWS, :], pt,
                         preferred_element_type=F32)
        else:
            neg_shift = jnp.full(qt.shape, -shift_ref[0], F32).astype(BF16)
            qt = jnp.where(is_aux_row, neg_shift, qt)
            ot = jnp.zeros((V_ROWS, GROUP * tq), F32)
            for c in range(k_ref.shape[0] // KEY_CHUNK):
                rows = slice(c * KEY_CHUNK, (c + 1) * KEY_CHUNK)
                st = jnp.dot(k_ref[rows, j * LANES:(j + 1) * LANES], qt,
                             preferred_element_type=F32)
                pt = jnp.exp2(st).astype(BF16)
                ot = ot + jnp.dot(vt_ref[j * LANES:j * LANES + V_ROWS, rows], pt,
                                  preferred_element_type=F32)
        ot = ot[:HEAD_DIM] / ot[HEAD_DIM:HEAD_DIM + 1]
        for a in range(2):
            pair = jnp.concatenate([ot[:, (2 * a) * tq:(2 * a + 1) * tq],
                                    ot[:, (2 * a + 1) * tq:(2 * a + 2) * tq]], axis=0)
            c = 2 * j + a
            o_ref[:, c * LANES:(c + 1) * LANES] = pair.T.astype(BF16)


def _mix_kernel(attn_a_ref, attn_b_ref, cb_ref, u_ref, up_ref, un_ref, ga_ref, gb_ref,
                xa_ref, xb_ref, cw_ref, woa_ref, wob_ref, wo_ref, g_ref, *rest,
                n_prompt_tiles, tiles4, tiles2, with_router):
    if with_router:
        rw_ref, xo_ref, h2_ref, lg_ref = rest
    else:
        xo_ref, h2_ref = rest
    i = pl.program_id(0)
    tm = xa_ref.shape[0]
    is_prompt = i < n_prompt_tiles
    seq_start = jnp.where(is_prompt, i % tiles4 == 0, i % tiles2 == 0)
    seq_end = jnp.where(is_prompt, i % tiles4 == tiles4 - 1, i % tiles2 == tiles2 - 1)

    attn = _pick(n_prompt_tiles, attn_a_ref, attn_b_ref)
    ya = jnp.dot(attn, woa_ref[...], preferred_element_type=F32)

    u = u_ref[...].astype(F32)
    row = lax.broadcasted_iota(jnp.int32, u.shape, 0)
    prev_row = jnp.where(seq_start, 0.0, up_ref[7:8, :].astype(F32))
    next_row = jnp.where(seq_end, 0.0, un_ref[0:1, :].astype(F32))
    u_prev = jnp.where(row == 0, prev_row, pltpu.roll(u, 1, 0))
    u_next = jnp.where(row == tm - 1, next_row, pltpu.roll(u, tm - 1, 0))
    cw = cw_ref[...]
    conv = cw[0:1, :] * u_prev + cw[1:2, :] * u + cw[2:3, :] * u_next
    yb_in = (cb_ref[...].astype(F32) * conv).astype(BF16)
    yb = jnp.dot(yb_in, wob_ref[...], preferred_element_type=F32)

    m = (jax.nn.sigmoid(ga_ref[...].astype(F32)) * ya
         + jax.nn.sigmoid(gb_ref[...].astype(F32)) * yb)
    xn = (_pick(n_prompt_tiles, xa_ref, xb_ref)
          + jnp.dot(m.astype(BF16), wo_ref[...], preferred_element_type=F32))
    xo_ref[...] = xn
    h2 = _rms(xn, g_ref[...])
    h2_ref[...] = h2.astype(h2_ref.dtype)
    if not with_router:
        return
    hi = h2.astype(BF16)
    lo = (h2 - hi.astype(F32)).astype(BF16)
    rw = rw_ref[...]
    rhi = rw.astype(BF16)
    rlo = (rw - rhi.astype(F32)).astype(BF16)
    lane = lax.broadcasted_iota(jnp.int32, rw.shape, 1)
    hi_terms = jnp.dot(hi, jnp.where(lane < N_EXPERTS, rhi, rlo), preferred_element_type=F32)
    lg_ref[...] = (hi_terms + pltpu.roll(hi_terms, LANES - N_EXPERTS, 1)
                   + jnp.dot(lo, rhi, preferred_element_type=F32))


def _ffn_kernel(h_ref, x_ref, wg_ref, wu_ref, wd_ref, o_ref, acc_ref):
    j = pl.program_id(1)

    @pl.when(j == 0)
    def _():
        acc_ref[...] = jnp.zeros_like(acc_ref)

    h = h_ref[...]
    g = jnp.dot(h, wg_ref[...], preferred_element_type=F32)
    u = jnp.dot(h, wu_ref[...], preferred_element_type=F32)
    a = (g * jax.nn.sigmoid(g) * u).astype(BF16)
    acc_ref[...] += jnp.dot(a, wd_ref[...], preferred_element_type=F32)

    @pl.when(j == pl.num_programs(1) - 1)
    def _():
        o_ref[...] = x_ref[...] + acc_ref[...]


ROUTE_E0, ROUTE_E1, ROUTE_R0, ROUTE_R1, ROUTE_W0, ROUTE_W1 = range(6)


def _lane_pick(x, lane, k):
    return jnp.sum(jnp.where(lane == k, x, 0.0), axis=-1, keepdims=True)


def _router_kernel(lg_ref, tri_ref, route_ref, count_ref, base_ref):
    i = pl.program_id(0)

    @pl.when(i == 0)
    def _():
        base_ref[...] = jnp.zeros_like(base_ref)

    lg = lg_ref[...]
    lane = lax.broadcasted_iota(jnp.int32, lg.shape, 1)
    neg = jnp.float32(-jnp.inf)
    l1 = jnp.where(lane < N_EXPERTS, lg, neg)
    m1 = jnp.max(l1, axis=-1, keepdims=True)
    i1 = jnp.min(jnp.where(l1 == m1, lane, LANES), axis=-1, keepdims=True)
    l2 = jnp.where(lane == i1, neg, l1)
    m2 = jnp.max(l2, axis=-1, keepdims=True)
    i2 = jnp.min(jnp.where(l2 == m2, lane, LANES), axis=-1, keepdims=True)
    e = jnp.exp(m2 - m1)
    w1 = 1.0 / (1.0 + e)
    w2 = e / (1.0 + e)

    hot1 = lane == i1
    hot2 = lane == i2
    onehot = jnp.where(hot1 | hot2, 1.0, 0.0)
    base = base_ref[0:1, :]
    prefix = jnp.dot(tri_ref[...], onehot.astype(BF16), preferred_element_type=F32) + base
    r1 = jnp.sum(jnp.where(hot1, prefix, 0.0), axis=-1, keepdims=True)
    r2 = jnp.sum(jnp.where(hot2, prefix, 0.0), axis=-1, keepdims=True)
    total = base + jnp.sum(onehot, axis=0, keepdims=True)
    base_ref[...] = jnp.broadcast_to(total, base_ref.shape)
    count_ref[...] = jnp.broadcast_to(total, count_ref.shape)

    rec = jnp.where(lane == ROUTE_E0, i1.astype(F32), 0.0)
    rec = jnp.where(lane == ROUTE_E1, i2.astype(F32), rec)
    rec = jnp.where(lane == ROUTE_R0, r1, rec)
    rec = jnp.where(lane == ROUTE_R1, r2, rec)
    rec = jnp.where(lane == ROUTE_W0, w1, rec)
    rec = jnp.where(lane == ROUTE_W1, w2, rec)
    route_ref[...] = rec


def _row_copy(src, src_row, dst, dst_row, sem):
    return pltpu.make_async_copy(src.at[pl.ds(src_row, 1)], dst.at[pl.ds(dst_row, 1)], sem)


def _dispatch_kernel(pad_start_ref, pad_count_ref, tile_valid_ref, pos_ref, h_ref, xs_ref,
                     zero_ref, sem, zsem):
    tm = h_ref.shape[0]
    n_tiles = xs_ref.shape[0] // tm

    @pl.when(pl.program_id(0) == 0)
    def _():
        zero_ref[...] = jnp.zeros_like(zero_ref)
        for e in range(N_EXPERTS):
            start = pad_start_ref[e]
            count = pad_count_ref[e]

            def zero_issue(r, carry):
                _row_copy(zero_ref, 0, xs_ref, start + r, zsem).start()
                return carry

            def zero_wait(r, carry):
                _row_copy(zero_ref, 0, xs_ref, 0, zsem).wait()
                return carry

            lax.fori_loop(0, count, zero_issue, 0)
            lax.fori_loop(0, count, zero_wait, 0)
        for g in range(n_tiles - N_EXPERTS, n_tiles):
            @pl.when(tile_valid_ref[g] == 0)
            def _():
                fill = pltpu.make_async_copy(zero_ref, xs_ref.at[pl.ds(g * tm, tm)], zsem)
                fill.start()
                fill.wait()

    def issue(i, carry):
        for k in range(ROW_UNROLL):
            r = i * ROW_UNROLL + k
            _row_copy(h_ref, r, xs_ref, pos_ref[0, 0, r], sem).start()
            _row_copy(h_ref, r, xs_ref, pos_ref[0, 0, tm + r], sem).start()
        return carry

    lax.fori_loop(0, tm // ROW_UNROLL, issue, 0)
    for _ in range(2):
        pltpu.make_async_copy(h_ref, xs_ref.at[pl.ds(0, tm)], sem).wait()


def _moe_ffn_kernel(te_ref, tv_ref, x_ref, wg_ref, wu_ref, wd_ref, y_ref, acc_ref):
    del te_ref
    g_idx = pl.program_id(0)
    j = pl.program_id(1)
    valid = tv_ref[g_idx] > 0

    @pl.when(j == 0)
    def _():
        acc_ref[...] = jnp.zeros_like(acc_ref)

    @pl.when(valid)
    def _():
        h = x_ref[...].astype(BF16)
        g = jnp.dot(h, wg_ref[0], preferred_element_type=F32)
        u = jnp.dot(h, wu_ref[0], preferred_element_type=F32)
        a = (g * jax.nn.sigmoid(g) * u).astype(BF16)
        acc_ref[...] += jnp.dot(a, wd_ref[0], preferred_element_type=F32)

    @pl.when(j == pl.num_programs(1) - 1)
    def _():
        y_ref[...] = acc_ref[...]


def _combine_kernel(pos_ref, nxt_ref, x_ref, route_ref, fg_ref, y_hbm, oa_ref, ob_ref,
                    ybuf, sem, *, n_a):
    i = pl.program_id(0)
    tc = x_ref.shape[0]
    cur = i % 2

    def issue_tile(p_ref, par):
        def issue(blk, carry):
            for k in range(ROW_UNROLL):
                r = blk * ROW_UNROLL + k
                _row_copy(y_hbm, p_ref[0, 0, r], ybuf.at[par, 0], r, sem.at[par]).start()
                _row_copy(y_hbm, p_ref[0, 0, tc + r], ybuf.at[par, 1], r, sem.at[par]).start()
            return carry
        lax.fori_loop(0, tc // ROW_UNROLL, issue, 0)

    @pl.when(i == 0)
    def _():
        issue_tile(pos_ref, 0)

    @pl.when(i + 1 < pl.num_programs(0))
    def _():
        issue_tile(nxt_ref, 1 - cur)

    for slot in range(2):
        pltpu.make_async_copy(y_hbm.at[pl.ds(0, tc)], ybuf.at[cur, slot], sem.at[cur]).wait()
    route = route_ref[...]
    lane = lax.broadcasted_iota(jnp.int32, route.shape, 1)
    w0 = _lane_pick(route, lane, ROUTE_W0)
    w1 = _lane_pick(route, lane, ROUTE_W1)
    out = _rms(x_ref[...] + (w0 * ybuf[cur, 0] + w1 * ybuf[cur, 1]), fg_ref[...])
    in_a = i < n_a

    @pl.when(in_a)
    def _():
        oa_ref[...] = out

    @pl.when(jnp.logical_not(in_a))
    def _():
        ob_ref[...] = out


def _rope_tables(max_seq):
    t = jnp.arange(max_seq, dtype=jnp.int32)
    row = (t // GRID_W).astype(F32)
    col = (t % GRID_W).astype(F32)
    inv = 1.0 / (ROPE_THETA ** (jnp.arange(0, AXIS_DIM, 2, dtype=F32) / AXIS_DIM))
    ar = row[:, None] * inv[None, :]
    ac = col[:, None] * inv[None, :]
    cos64 = jnp.concatenate([jnp.cos(ar), jnp.cos(ar), jnp.cos(ac), jnp.cos(ac)], axis=-1)
    sin64 = jnp.concatenate([-jnp.sin(ar), jnp.sin(ar), -jnp.sin(ac), jnp.sin(ac)], axis=-1)
    return jnp.tile(cos64, (1, 2)), jnp.tile(sin64, (1, 2))


def _two_part_specs(block, n_a, parts):
    off = 0 if parts[0] is parts[1] else n_a
    return [pl.BlockSpec(block, lambda i: (jnp.minimum(i, n_a - 1), 0)),
            pl.BlockSpec(block, lambda i: (jnp.maximum(i, n_a) - off, 0))]


def _qkv_proj(xparts, t, n_a, gain, w, qg, kg, cos, sin, bd, pos_map):
    row = lambda i: (i, 0)
    const = lambda i: (0, 0)
    return pl.pallas_call(
        functools.partial(_qkv_kernel, n_a=n_a),
        grid=(t // TM,),
        in_specs=_two_part_specs((TM, D_MODEL), n_a, xparts) + [
                  pl.BlockSpec((1, D_MODEL), const),
                  pl.BlockSpec((D_MODEL, QKV_WIDTH), const),
                  pl.BlockSpec((1, LANES), const),
                  pl.BlockSpec((1, LANES), const),
                  pl.BlockSpec((TM, LANES), pos_map),
                  pl.BlockSpec((TM, LANES), pos_map),
                  pl.BlockSpec((LANES, LANES), const)],
        out_specs=[pl.BlockSpec((TM, Q_EXP_WIDTH), row),
                   pl.BlockSpec((TM, KV_EXP_WIDTH), row),
                   pl.BlockSpec((KV_EXP_WIDTH, TM), lambda i: (0, i))],
        out_shape=[jax.ShapeDtypeStruct((t, Q_EXP_WIDTH), BF16),
                   jax.ShapeDtypeStruct((t, KV_EXP_WIDTH), BF16),
                   jax.ShapeDtypeStruct((KV_EXP_WIDTH, t), BF16)],
        compiler_params=_params(("parallel",)),
        name="qkv_proj",
    )(*xparts, gain, w, qg, kg, cos, sin, bd)


def _rest_proj(xparts, t, n_a, gain, w):
    row = lambda i: (i, 0)
    const = lambda i: (0, 0)
    out = jax.ShapeDtypeStruct((t, D_MODEL), BF16)
    return pl.pallas_call(
        functools.partial(_rest_kernel, n_a=n_a),
        grid=(t // TM,),
        in_specs=_two_part_specs((TM, D_MODEL), n_a, xparts) + [
                  pl.BlockSpec((1, D_MODEL), const),
                  pl.BlockSpec((D_MODEL, REST_WIDTH), const)],
        out_specs=[pl.BlockSpec((TM, D_MODEL), row)] * 4,
        out_shape=[out] * 4,
        compiler_params=_params(("parallel",)),
        name="rest_proj",
    )(*xparts, gain, w)


def _attention(shift, q, k, vt, *, row0, n_seq, seq, tq, exact_max):
    nq = seq // tq
    q0 = row0 // tq
    s0 = row0 // seq
    return pl.pallas_call(
        functools.partial(_attn_kernel, exact_max=exact_max),
        grid_spec=pltpu.PrefetchScalarGridSpec(
            num_scalar_prefetch=1,
            grid=(n_seq, nq),
            in_specs=[pl.BlockSpec((tq, Q_EXP_WIDTH), lambda b, i, km: (q0 + b * nq + i, 0)),
                      pl.BlockSpec((seq, KV_EXP_WIDTH), lambda b, i, km: (s0 + b, 0)),
                      pl.BlockSpec((KV_EXP_WIDTH, seq), lambda b, i, km: (0, s0 + b))],
            out_specs=pl.BlockSpec((tq, D_MODEL), lambda b, i, km: (b * nq + i, 0))),
        out_shape=jax.ShapeDtypeStruct((n_seq * seq, D_MODEL), BF16),
        compiler_params=_params(("parallel", "parallel")),
        name=f"attention_s{seq}" + ("_exact" if exact_max else ""),
    )(shift, q, k, vt)


def _mix(attn_parts, cb, u, ga, gb, xparts, cw, woa, wob, wo, gain, rw, *, n_prompt, seq_p, seq_s):
    t = cb.shape[0]
    n_a = n_prompt // TM
    row = lambda i: (i, 0)
    const = lambda i: (0, 0)
    sub = TM // 8
    last = t // 8 - 1
    act = pl.BlockSpec((TM, D_MODEL), row)
    wspec = pl.BlockSpec((D_MODEL, D_MODEL), const)
    with_router = rw is not None
    kern = functools.partial(_mix_kernel, n_prompt_tiles=n_prompt // TM,
                             tiles4=seq_p // TM, tiles2=seq_s // TM, with_router=with_router)
    in_specs = (_two_part_specs((TM, D_MODEL), n_a, attn_parts) + [
                act, act,
                pl.BlockSpec((8, D_MODEL), lambda i: (jnp.maximum(i * sub - 1, 0), 0)),
                pl.BlockSpec((8, D_MODEL), lambda i: (jnp.minimum((i + 1) * sub, last), 0)),
                act, act]
                + _two_part_specs((TM, D_MODEL), n_a, xparts) + [
                pl.BlockSpec((3, D_MODEL), const),
                wspec, wspec, wspec,
                pl.BlockSpec((1, D_MODEL), const)])
    args = [*attn_parts, cb, u, u, u, ga, gb, *xparts, cw, woa, wob, wo, gain]
    out_specs = [act, act]
    out_shape = [jax.ShapeDtypeStruct((t, D_MODEL), F32),
                 jax.ShapeDtypeStruct((t, D_MODEL), F32 if with_router else BF16)]
    if with_router:
        in_specs.append(pl.BlockSpec((D_MODEL, LANES), const))
        args.append(rw)
        out_specs.append(pl.BlockSpec((TM, LANES), row))
        out_shape.append(jax.ShapeDtypeStruct((t, LANES), F32))
    return pl.pallas_call(
        kern,
        grid=(t // TM,),
        in_specs=in_specs,
        out_specs=out_specs,
        out_shape=out_shape,
        compiler_params=_params(("parallel",)),
        name="mix_proj_router" if with_router else "mix_proj",
    )(*args)


def _ffn(h, x, wg, wu, wd):
    t = x.shape[0]
    row = lambda i, j: (i, 0)
    return pl.pallas_call(
        _ffn_kernel,
        grid=(t // TM_FFN, D_FF // TF),
        in_specs=[pl.BlockSpec((TM_FFN, D_MODEL), row),
                  pl.BlockSpec((TM_FFN, D_MODEL), row),
                  pl.BlockSpec((D_MODEL, TF), lambda i, j: (0, j)),
                  pl.BlockSpec((D_MODEL, TF), lambda i, j: (0, j)),
                  pl.BlockSpec((TF, D_MODEL), lambda i, j: (j, 0))],
        out_specs=pl.BlockSpec((TM_FFN, D_MODEL), row),
        out_shape=jax.ShapeDtypeStruct((t, D_MODEL), F32),
        scratch_shapes=[pltpu.VMEM((TM_FFN, D_MODEL), F32)],
        compiler_params=_params(("parallel", "arbitrary")),
        name="ffn_dense",
    )(h, x, wg, wu, wd)


def _router(logits):
    t = logits.shape[0]
    row = lambda i: (i, 0)
    const = lambda i: (0, 0)
    r = jnp.arange(TR)
    tri = (r[None, :] < r[:, None]).astype(BF16)
    return pl.pallas_call(
        _router_kernel,
        grid=(t // TR,),
        in_specs=[pl.BlockSpec((TR, LANES), row),
                  pl.BlockSpec((TR, TR), const)],
        out_specs=[pl.BlockSpec((TR, LANES), row),
                   pl.BlockSpec((8, LANES), const)],
        out_shape=[jax.ShapeDtypeStruct((t, LANES), F32),
                   jax.ShapeDtypeStruct((8, LANES), F32)],
        scratch_shapes=[pltpu.VMEM((8, LANES), F32)],
        compiler_params=_params(("arbitrary",)),
        name="router",
    )(logits, tri)


def _route_plan(route, counts, t):
    e0 = route[:, ROUTE_E0].astype(jnp.int32)
    e1 = route[:, ROUTE_E1].astype(jnp.int32)
    r0 = route[:, ROUTE_R0].astype(jnp.int32)
    r1 = route[:, ROUTE_R1].astype(jnp.int32)
    cnt = counts[0, :N_EXPERTS].astype(jnp.int32)
    padded = ((cnt + TM_MOE - 1) // TM_MOE) * TM_MOE
    ends = jnp.cumsum(padded)
    starts = ends - padded
    experts = jnp.arange(N_EXPERTS, dtype=jnp.int32)
    pos0 = jnp.sum(jnp.where(e0[:, None] == experts[None, :], starts[None, :], 0), axis=1) + r0
    pos1 = jnp.sum(jnp.where(e1[:, None] == experts[None, :], starts[None, :], 0), axis=1) + r1
    n_tok_tiles = t // TM_MOE
    pos = jnp.concatenate([pos0.reshape(n_tok_tiles, 1, TM_MOE),
                           pos1.reshape(n_tok_tiles, 1, TM_MOE)], axis=2)
    n_tiles = 2 * t // TM_MOE + N_EXPERTS
    tile_start = jnp.arange(n_tiles, dtype=jnp.int32) * TM_MOE
    tile_valid = (tile_start < ends[-1]).astype(jnp.int32)
    tile_expert = jnp.sum((tile_start[:, None] >= ends[None, :]).astype(jnp.int32), axis=1)
    tile_expert = jnp.minimum(tile_expert, N_EXPERTS - 1)
    pad = (starts + cnt, padded - cnt)
    return pos, pad, tile_expert, tile_valid, n_tiles


def _dispatch(pad_start, pad_count, tile_valid, pos, h):
    t = h.shape[0]
    n_rows = tile_valid.shape[0] * TM_MOE
    return pl.pallas_call(
        _dispatch_kernel,
        grid_spec=pltpu.PrefetchScalarGridSpec(
            num_scalar_prefetch=3,
            grid=(t // TM_MOE,),
            in_specs=[pl.BlockSpec((1, 1, 2 * TM_MOE), lambda i, ps, pc, tv: (i, 0, 0),
                                   memory_space=pltpu.SMEM),
                      pl.BlockSpec((TM_MOE, D_MODEL), lambda i, ps, pc, tv: (i, 0))],
            out_specs=pl.BlockSpec(memory_space=pl.ANY),
            scratch_shapes=[pltpu.VMEM((TM_MOE, D_MODEL), F32),
                            pltpu.SemaphoreType.DMA(()),
                            pltpu.SemaphoreType.DMA(())]),
        out_shape=jax.ShapeDtypeStruct((n_rows, D_MODEL), F32),
        compiler_params=_params(("arbitrary",)),
        name="moe_dispatch",
    )(pad_start, pad_count, tile_valid, pos, h)


def _moe_ffn(tile_expert, tile_valid, xs, wg, wu, wd):
    n_rows = xs.shape[0]
    row = lambda g, j, te, tv: (g, 0)
    return pl.pallas_call(
        _moe_ffn_kernel,
        grid_spec=pltpu.PrefetchScalarGridSpec(
            num_scalar_prefetch=2,
            grid=(n_rows // TM_MOE, D_FF // TF_MOE),
            in_specs=[pl.BlockSpec((TM_MOE, D_MODEL), row),
                      pl.BlockSpec((1, D_MODEL, TF_MOE),
                                   lambda g, j, te, tv: (te[g], 0, j * tv[g])),
                      pl.BlockSpec((1, D_MODEL, TF_MOE),
                                   lambda g, j, te, tv: (te[g], 0, j * tv[g])),
                      pl.BlockSpec((1, TF_MOE, D_MODEL),
                                   lambda g, j, te, tv: (te[g], j * tv[g], 0))],
            out_specs=pl.BlockSpec((TM_MOE, D_MODEL), row),
            scratch_shapes=[pltpu.VMEM((TM_MOE, D_MODEL), F32)]),
        out_shape=jax.ShapeDtypeStruct((n_rows, D_MODEL), F32),
        compiler_params=_params(("parallel", "arbitrary")),
        name="moe_ffn",
    )(tile_expert, tile_valid, xs, wg, wu, wd)


def _combine(pos, x, route, fgain, ys, n_first):
    t = x.shape[0]
    n_a = n_first // TM_MOE
    row = lambda i: (i, 0)
    blk = (TM_MOE, D_MODEL)
    last = t // TM_MOE - 1
    pos_blk = (1, 1, 2 * TM_MOE)
    return pl.pallas_call(
        functools.partial(_combine_kernel, n_a=n_a),
        grid=(t // TM_MOE,),
        in_specs=[pl.BlockSpec(pos_blk, lambda i: (i, 0, 0), memory_space=pltpu.SMEM),
                  pl.BlockSpec(pos_blk, lambda i: (jnp.minimum(i + 1, last), 0, 0),
                               memory_space=pltpu.SMEM),
                  pl.BlockSpec(blk, row),
                  pl.BlockSpec((TM_MOE, LANES), row),
                  pl.BlockSpec((1, D_MODEL), lambda i: (0, 0)),
                  pl.BlockSpec(memory_space=pl.ANY)],
        out_specs=[pl.BlockSpec(blk, lambda i: (jnp.minimum(i, n_a - 1), 0)),
                   pl.BlockSpec(blk, lambda i: (jnp.maximum(i - n_a, 0), 0))],
        out_shape=[jax.ShapeDtypeStruct((n_first, D_MODEL), F32),
                   jax.ShapeDtypeStruct((t - n_first, D_MODEL), F32)],
        scratch_shapes=[pltpu.VMEM((2, 2, TM_MOE, D_MODEL), F32),
                        pltpu.SemaphoreType.DMA((2,))],
        compiler_params=_params(("arbitrary",)),
        name="moe_combine",
    )(pos, pos, x, route, fgain, ys)


def kernel(x_prompt, x_sample, norm_mix, w_in, q_norm, k_norm, conv_w, w_oa, w_ob, w_o, norm_ffn,
           ffn_w_gate, ffn_w_up, ffn_w_down, router_w, moe_w_gate, moe_w_up, moe_w_down, final_norm):
    bp, sp, _ = x_prompt.shape
    bs, ss, _ = x_sample.shape
    n_prompt = bp * sp
    n_sample = bs * ss
    depth = norm_mix.shape[0]
    assert sp % TM == 0 and ss % TM == 0 and n_prompt % TM_FFN == 0 and n_sample % TM_FFN == 0
    assert depth == 2 and ffn_w_gate.shape[0] == 1 and moe_w_gate.shape[0] == 1

    t_all = n_prompt + n_sample
    xparts = (x_prompt.reshape(n_prompt, D_MODEL), x_sample.reshape(n_sample, D_MODEL))

    cos, sin = _rope_tables(max(sp, ss))
    n_prompt_tiles, tiles4, tiles2 = n_prompt // TM, sp // TM, ss // TM
    pos_map = lambda i: (jnp.where(i < n_prompt_tiles, i % tiles4, i % tiles2), 0)
    idx = jnp.arange(LANES)
    bd = (idx[:, None] // HEAD_DIM == idx[None, :] // HEAD_DIM).astype(BF16)

    for l in range(depth):
        gain = norm_mix[l][None, :]
        w_l = w_in[l]
        w_qkv = w_l[:, :QKV_WIDTH].astype(BF16)
        w_rest = w_l[:, QKV_WIDTH:].astype(BF16)
        qg = jnp.tile(q_norm[l] * Q_SCALE, 2)[None, :]
        kg = jnp.tile(k_norm[l], 2)[None, :]
        q, k, vt = _qkv_proj(xparts, t_all, n_prompt_tiles, gain, w_qkv, qg, kg, cos, sin, bd,
                             pos_map)
        cb, u, ga, gb = _rest_proj(xparts, t_all, n_prompt_tiles, gain, w_rest)
        shift = (Q_SCALE * HEAD_DIM * jnp.max(jnp.abs(q_norm[l]))
                 * jnp.max(jnp.abs(k_norm[l]))).reshape(1)
        bound_ok = 2.0 * shift[0] < MAX_SHIFT_GAP

        def attend(exact_max, tq_p, tq_s):
            def run(shift, q, k, vt):
                attn_p = _attention(shift, q, k, vt, row0=0, n_seq=bp, seq=sp,
                                    tq=tq_p, exact_max=exact_max)
                attn_s = _attention(shift, q, k, vt, row0=n_prompt, n_seq=bs, seq=ss,
                                    tq=tq_s, exact_max=exact_max)
                return attn_p, attn_s
            return run

        attn_parts = lax.cond(bound_ok, attend(False, 256, 256), attend(True, 128, 256),
                              shift, q, k, vt)
        j = l // 2
        is_moe = l % 2 == 1
        rw = (jnp.pad(jnp.tile(router_w[j], (1, 2)), ((0, 0), (0, LANES - 2 * N_EXPERTS)))
              if is_moe else None)
        outs = _mix(attn_parts, cb, u, ga, gb, xparts, conv_w[l],
                    w_oa[l].astype(BF16), w_ob[l].astype(BF16), w_o[l].astype(BF16),
                    norm_ffn[l][None, :], rw, n_prompt=n_prompt, seq_p=sp, seq_s=ss)
        if not is_moe:
            x, h2 = outs
            x = _ffn(h2, x, ffn_w_gate[j].astype(BF16), ffn_w_up[j].astype(BF16),
                     ffn_w_down[j].astype(BF16))
            xparts = (x, x)
        else:
            x, h2, logits = outs
            route, counts = _router(logits)
            pos, pad, tile_expert, tile_valid, n_tiles = _route_plan(route, counts, t_all)
            xs = _dispatch(*pad, tile_valid, pos, h2)
            ys = _moe_ffn(tile_expert, tile_valid, xs, moe_w_gate[j].astype(BF16),
                          moe_w_up[j].astype(BF16), moe_w_down[j].astype(BF16))
            xparts = _combine(pos, x, route, final_norm[None, :], ys, n_prompt)

    return (xparts[0].reshape(bp, sp, D_MODEL), xparts[1].reshape(bs, ss, D_MODEL))
```

```python
import functools
import math

import jax
import jax.numpy as jnp
from jax import lax
from jax.experimental import pallas as pl
from jax.experimental.pallas import tpu as pltpu

F32 = jnp.float32
BF16 = jnp.bfloat16

D_MODEL = 1024
N_HEADS = 16
N_KV_HEADS = 4
HEAD_DIM = 64
GROUP = N_HEADS // N_KV_HEADS
KV_WIDTH = N_KV_HEADS * HEAD_DIM
AXIS_DIM = HEAD_DIM // 2
ROPE_THETA = 10000.0
GRID_W = 64
D_FF = 3584
N_EXPERTS = 8
EPS = 1e-6
LANES = 128
QKV_WIDTH = D_MODEL + 2 * KV_WIDTH
REST_WIDTH = 5 * D_MODEL
Q_EXP_WIDTH = N_HEADS * LANES
KV_EXP_WIDTH = N_KV_HEADS * LANES
Q_SCALE = math.log2(math.e) / math.sqrt(HEAD_DIM)
V_ROWS = 80
KEY_CHUNK = 512
ROW_UNROLL = 8
MAX_SHIFT_GAP = 100.0

TM = 512
TM_FFN = 512
TF = 1792
TF_MOE = 1792
TR = 512
TM_MOE = 512
VMEM_LIMIT = 56 * 1024 * 1024


def _params(sem):
    return pltpu.CompilerParams(dimension_semantics=sem, vmem_limit_bytes=VMEM_LIMIT)


def _rms(x, gain):
    return x * lax.rsqrt(jnp.mean(x * x, axis=-1, keepdims=True) + EPS) * gain


def _pick(n_a, a_ref, b_ref):
    return jnp.where(pl.program_id(0) < n_a, a_ref[...], b_ref[...])


def _qkv_kernel(xa_ref, xb_ref, g_ref, w_ref, qg_ref, kg_ref, cos_ref, sin_ref, bd_ref,
                q_ref, k_ref, vt_ref, *, n_a):
    tm = xa_ref.shape[0]
    h = _rms(_pick(n_a, xa_ref, xb_ref), g_ref[...]).astype(BF16)
    p = jnp.dot(h, w_ref[...], preferred_element_type=F32)
    cos = cos_ref[...]
    sin = sin_ref[...]
    bd = bd_ref[...]
    lane = lax.broadcasted_iota(jnp.int32, (tm, LANES), 1)
    first16 = (lane & 31) < 16
    low_half = lane < HEAD_DIM

    def norm_rope(c, gain):
        sq = c * c
        hi = sq.astype(BF16)
        lo = (sq - hi.astype(F32)).astype(BF16)
        ss = (jnp.dot(hi, bd, preferred_element_type=F32)
              + jnp.dot(lo, bd, preferred_element_type=F32))
        n = c * lax.rsqrt(ss * (1.0 / HEAD_DIM) + EPS) * gain
        partner = jnp.where(first16, pltpu.roll(n, LANES - 16, 1), pltpu.roll(n, 16, 1))
        return n * cos + partner * sin

    is_aux = lane == HEAD_DIM
    zero = jnp.zeros((tm, LANES), F32)
    ones_aux = jnp.where(is_aux, 1.0, zero)

    qg = qg_ref[...]
    for c in range(N_HEADS // 2):
        r = norm_rope(p[:, c * LANES:(c + 1) * LANES], qg)
        even = jnp.where(low_half, r, zero)
        odd = jnp.where(low_half, pltpu.roll(r, HEAD_DIM, 1), zero)
        q_ref[:, (2 * c) * LANES:(2 * c + 1) * LANES] = even.astype(BF16)
        q_ref[:, (2 * c + 1) * LANES:(2 * c + 2) * LANES] = odd.astype(BF16)

    kg = kg_ref[...]
    for c in range(KV_WIDTH // LANES):
        kn = norm_rope(p[:, D_MODEL + c * LANES:D_MODEL + (c + 1) * LANES], kg)
        even = jnp.where(low_half, kn, ones_aux)
        odd = jnp.where(low_half, pltpu.roll(kn, HEAD_DIM, 1), ones_aux)
        k_ref[:, (2 * c) * LANES:(2 * c + 1) * LANES] = even.astype(BF16)
        k_ref[:, (2 * c + 1) * LANES:(2 * c + 2) * LANES] = odd.astype(BF16)
        vv = p[:, D_MODEL + KV_WIDTH + c * LANES:D_MODEL + KV_WIDTH + (c + 1) * LANES]
        even = jnp.where(low_half, vv, ones_aux)
        odd = jnp.where(low_half, pltpu.roll(vv, HEAD_DIM, 1), ones_aux)
        vt_ref[(2 * c) * LANES:(2 * c + 1) * LANES, :] = even.T.astype(BF16)
        vt_ref[(2 * c + 1) * LANES:(2 * c + 2) * LANES, :] = odd.T.astype(BF16)


def _rest_kernel(xa_ref, xb_ref, g_ref, w_ref, cb_ref, u_ref, ga_ref, gb_ref, *, n_a):
    h = _rms(_pick(n_a, xa_ref, xb_ref), g_ref[...]).astype(BF16)
    p = jnp.dot(h, w_ref[...], preferred_element_type=F32)
    cb_ref[...] = p[:, :D_MODEL].astype(BF16)
    u_ref[...] = (p[:, D_MODEL:2 * D_MODEL] * p[:, 2 * D_MODEL:3 * D_MODEL]).astype(BF16)
    ga_ref[...] = p[:, 3 * D_MODEL:4 * D_MODEL].astype(BF16)
    gb_ref[...] = p[:, 4 * D_MODEL:].astype(BF16)


def _attn_kernel(shift_ref, q_ref, k_ref, vt_ref, *refs, exact_max):
    n_cast = (len(refs) - 1) // 2
    o_ref = refs[n_cast]
    for src, dst in zip(refs[:n_cast], refs[n_cast + 1:]):
        dst[...] = src[...].astype(BF16)
    tq = q_ref.shape[0]
    is_aux_row = lax.broadcasted_iota(jnp.int32, (LANES, GROUP * tq), 0) == HEAD_DIM
    for j in range(N_KV_HEADS):
        qt = jnp.concatenate(
            [q_ref[:, h * LANES:(h + 1) * LANES].T for h in range(GROUP * j, GROUP * (j + 1))],
            axis=1)
        if exact_max:
            st = jnp.dot(k_ref[:, j * LANES:(j + 1) * LANES], qt,
                         preferred_element_type=F32)
            st = st - jnp.max(st, axis=0, keepdims=True)
            pt = jnp.exp2(st).astype(BF16)
            ot = jnp.dot(vt_ref[j * LANES:j * LANES + V_ROWS, :], pt,
                         preferred_element_type=F32)
        else:
            neg_shift = jnp.full(qt.shape, -shift_ref[0], F32).astype(BF16)
            qt = jnp.where(is_aux_row, neg_shift, qt)
            ot = jnp.zeros((V_ROWS, GROUP * tq), F32)
            for c in range(k_ref.shape[0] // KEY_CHUNK):
                rows = slice(c * KEY_CHUNK, (c + 1) * KEY_CHUNK)
                st = jnp.dot(k_ref[rows, j * LANES:(j + 1) * LANES], qt,
                             preferred_element_type=F32)
                pt = jnp.exp2(st).astype(BF16)
                ot = ot + jnp.dot(vt_ref[j * LANES:j * LANES + V_ROWS, rows], pt,
                                  preferred_element_type=F32)
        ot = ot[:HEAD_DIM] / ot[HEAD_DIM:HEAD_DIM + 1]
        for a in range(2):
            pair = jnp.concatenate([ot[:, (2 * a) * tq:(2 * a + 1) * tq],
                                    ot[:, (2 * a + 1) * tq:(2 * a + 2) * tq]], axis=0)
            c = 2 * j + a
            o_ref[:, c * LANES:(c + 1) * LANES] = pair.T.astype(BF16)


def _mix_kernel(attn_a_ref, attn_b_ref, cb_ref, u_ref, up_ref, un_ref, ga_ref, gb_ref,
                xa_ref, xb_ref, cw_ref, woa_ref, wob_ref, wo_ref, g_ref, *rest,
                n_prompt_tiles, tiles4, tiles2, with_router):
    if with_router:
        rw_ref, xo_ref, h2_ref, lg_ref = rest
    else:
        xo_ref, h2_ref = rest
    i = pl.program_id(0)
    tm = xa_ref.shape[0]
    is_prompt = i < n_prompt_tiles
    seq_start = jnp.where(is_prompt, i % tiles4 == 0, i % tiles2 == 0)
    seq_end = jnp.where(is_prompt, i % tiles4 == tiles4 - 1, i % tiles2 == tiles2 - 1)

    attn = _pick(n_prompt_tiles, attn_a_ref, attn_b_ref)
    ya = jnp.dot(attn, woa_ref[...], preferred_element_type=F32)

    u = u_ref[...].astype(F32)
    row = lax.broadcasted_iota(jnp.int32, u.shape, 0)
    prev_row = jnp.where(seq_start, 0.0, up_ref[7:8, :].astype(F32))
    next_row = jnp.where(seq_end, 0.0, un_ref[0:1, :].astype(F32))
    u_prev = jnp.where(row == 0, prev_row, pltpu.roll(u, 1, 0))
    u_next = jnp.where(row == tm - 1, next_row, pltpu.roll(u, tm - 1, 0))
    cw = cw_ref[...]
    conv = cw[0:1, :] * u_prev + cw[1:2, :] * u + cw[2:3, :] * u_next
    yb_in = (cb_ref[...].astype(F32) * conv).astype(BF16)
    yb = jnp.dot(yb_in, wob_ref[...], preferred_element_type=F32)

    m = (jax.nn.sigmoid(ga_ref[...].astype(F32)) * ya
         + jax.nn.sigmoid(gb_ref[...].astype(F32)) * yb)
    xn = (_pick(n_prompt_tiles, xa_ref, xb_ref)
          + jnp.dot(m.astype(BF16), wo_ref[...], preferred_element_type=F32))
    xo_ref[...] = xn
    h2 = _rms(xn, g_ref[...])
    h2_ref[...] = h2.astype(h2_ref.dtype)
    if not with_router:
        return
    hi = h2.astype(BF16)
    lo = (h2 - hi.astype(F32)).astype(BF16)
    rw = rw_ref[...]
    rhi = rw.astype(BF16)
    rlo = (rw - rhi.astype(F32)).astype(BF16)
    lane = lax.broadcasted_iota(jnp.int32, rw.shape, 1)
    hi_terms = jnp.dot(hi, jnp.where(lane < N_EXPERTS, rhi, rlo), preferred_element_type=F32)
    lg_ref[...] = (hi_terms + pltpu.roll(hi_terms, LANES - N_EXPERTS, 1)
                   + jnp.dot(lo, rhi, preferred_element_type=F32))


def _ffn_kernel(h_ref, x_ref, wg_ref, wu_ref, wd_ref, o_ref, acc_ref):
    j = pl.program_id(1)

    @pl.when(j == 0)
    def _():
        acc_ref[...] = jnp.zeros_like(acc_ref)

    h = h_ref[...]
    g = jnp.dot(h, wg_ref[...], preferred_element_type=F32)
    u = jnp.dot(h, wu_ref[...], preferred_element_type=F32)
    a = (g * jax.nn.sigmoid(g) * u).astype(BF16)
    acc_ref[...] += jnp.dot(a, wd_ref[...], preferred_element_type=F32)

    @pl.when(j == pl.num_programs(1) - 1)
    def _():
        o_ref[...] = x_ref[...] + acc_ref[...]


ROUTE_E0, ROUTE_E1, ROUTE_R0, ROUTE_R1, ROUTE_W0, ROUTE_W1 = range(6)


def _lane_pick(x, lane, k):
    return jnp.sum(jnp.where(lane == k, x, 0.0), axis=-1, keepdims=True)


def _router_kernel(lg_ref, tri_ref, route_ref, count_ref, base_ref):
    i = pl.program_id(0)

    @pl.when(i == 0)
    def _():
        base_ref[...] = jnp.zeros_like(base_ref)

    lg = lg_ref[...]
    lane = lax.broadcasted_iota(jnp.int32, lg.shape, 1)
    neg = jnp.float32(-jnp.inf)
    l1 = jnp.where(lane < N_EXPERTS, lg, neg)
    m1 = jnp.max(l1, axis=-1, keepdims=True)
    i1 = jnp.min(jnp.where(l1 == m1, lane, LANES), axis=-1, keepdims=True)
    l2 = jnp.where(lane == i1, neg, l1)
    m2 = jnp.max(l2, axis=-1, keepdims=True)
    i2 = jnp.min(jnp.where(l2 == m2, lane, LANES), axis=-1, keepdims=True)
    e = jnp.exp(m2 - m1)
    w1 = 1.0 / (1.0 + e)
    w2 = e / (1.0 + e)

    hot1 = lane == i1
    hot2 = lane == i2
    onehot = jnp.where(hot1 | hot2, 1.0, 0.0)
    base = base_ref[0:1, :]
    prefix = jnp.dot(tri_ref[...], onehot.astype(BF16), preferred_element_type=F32) + base
    r1 = jnp.sum(jnp.where(hot1, prefix, 0.0), axis=-1, keepdims=True)
    r2 = jnp.sum(jnp.where(hot2, prefix, 0.0), axis=-1, keepdims=True)
    total = base + jnp.sum(onehot, axis=0, keepdims=True)
    base_ref[...] = jnp.broadcast_to(total, base_ref.shape)
    count_ref[...] = jnp.broadcast_to(total, count_ref.shape)

    rec = jnp.where(lane == ROUTE_E0, i1.astype(F32), 0.0)
    rec = jnp.where(lane == ROUTE_E1, i2.astype(F32), rec)
    rec = jnp.where(lane == ROUTE_R0, r1, rec)
    rec = jnp.where(lane == ROUTE_R1, r2, rec)
    rec = jnp.where(lane == ROUTE_W0, w1, rec)
    rec = jnp.where(lane == ROUTE_W1, w2, rec)
    route_ref[...] = rec


def _row_copy(src, src_row, dst, dst_row, sem):
    return pltpu.make_async_copy(src.at[pl.ds(src_row, 1)], dst.at[pl.ds(dst_row, 1)], sem)


def _dispatch_kernel(pad_start_ref, pad_count_ref, tile_valid_ref, pos_ref, h_ref, xs_ref,
                     zero_ref, sem, zsem):
    tm = h_ref.shape[0]
    n_tiles = xs_ref.shape[0] // tm

    @pl.when(pl.program_id(0) == 0)
    def _():
        zero_ref[...] = jnp.zeros_like(zero_ref)
        for e in range(N_EXPERTS):
            start = pad_start_ref[e]
            count = pad_count_ref[e]

            def zero_issue(r, carry):
                _row_copy(zero_ref, 0, xs_ref, start + r, zsem).start()
                return carry

            def zero_wait(r, carry):
                _row_copy(zero_ref, 0, xs_ref, 0, zsem).wait()
                return carry

            lax.fori_loop(0, count, zero_issue, 0)
            lax.fori_loop(0, count, zero_wait, 0)
        for g in range(n_tiles - N_EXPERTS, n_tiles):
            @pl.when(tile_valid_ref[g] == 0)
            def _():
                fill = pltpu.make_async_copy(zero_ref, xs_ref.at[pl.ds(g * tm, tm)], zsem)
                fill.start()
                fill.wait()

    def issue(i, carry):
        for k in range(ROW_UNROLL):
            r = i * ROW_UNROLL + k
            _row_copy(h_ref, r, xs_ref, pos_ref[0, 0, r], sem).start()
            _row_copy(h_ref, r, xs_ref, pos_ref[0, 0, tm + r], sem).start()
        return carry

    lax.fori_loop(0, tm // ROW_UNROLL, issue, 0)
    for _ in range(2):
        pltpu.make_async_copy(h_ref, xs_ref.at[pl.ds(0, tm)], sem).wait()


def _moe_ffn_kernel(te_ref, tv_ref, x_ref, wg_ref, wu_ref, wd_ref, y_ref, acc_ref):
    del te_ref
    g_idx = pl.program_id(0)
    j = pl.program_id(1)
    valid = tv_ref[g_idx] > 0

    @pl.when(j == 0)
    def _():
        acc_ref[...] = jnp.zeros_like(acc_ref)

    @pl.when(valid)
    def _():
        h = x_ref[...].astype(BF16)
        g = jnp.dot(h, wg_ref[0], preferred_element_type=F32)
        u = jnp.dot(h, wu_ref[0], preferred_element_type=F32)
        a = (g * jax.nn.sigmoid(g) * u).astype(BF16)
        acc_ref[...] += jnp.dot(a, wd_ref[0], preferred_element_type=F32)

    @pl.when(j == pl.num_programs(1) - 1)
    def _():
        y_ref[...] = acc_ref[...]


def _combine_kernel(pos_ref, nxt_ref, x_ref, route_ref, fg_ref, y_hbm, oa_ref, ob_ref,
                    ybuf, sem, *, n_a):
    i = pl.program_id(0)
    tc = x_ref.shape[0]
    cur = i % 2

    def issue_tile(p_ref, par):
        def issue(blk, carry):
            for k in range(ROW_UNROLL):
                r = blk * ROW_UNROLL + k
                _row_copy(y_hbm, p_ref[0, 0, r], ybuf.at[par, 0], r, sem.at[par]).start()
                _row_copy(y_hbm, p_ref[0, 0, tc + r], ybuf.at[par, 1], r, sem.at[par]).start()
            return carry
        lax.fori_loop(0, tc // ROW_UNROLL, issue, 0)

    @pl.when(i == 0)
    def _():
        issue_tile(pos_ref, 0)

    @pl.when(i + 1 < pl.num_programs(0))
    def _():
        issue_tile(nxt_ref, 1 - cur)

    for slot in range(2):
        pltpu.make_async_copy(y_hbm.at[pl.ds(0, tc)], ybuf.at[cur, slot], sem.at[cur]).wait()
    route = route_ref[...]
    lane = lax.broadcasted_iota(jnp.int32, route.shape, 1)
    w0 = _lane_pick(route, lane, ROUTE_W0)
    w1 = _lane_pick(route, lane, ROUTE_W1)
    out = _rms(x_ref[...] + (w0 * ybuf[cur, 0] + w1 * ybuf[cur, 1]), fg_ref[...])
    in_a = i < n_a

    @pl.when(in_a)
    def _():
        oa_ref[...] = out

    @pl.when(jnp.logical_not(in_a))
    def _():
        ob_ref[...] = out


def _rope_tables(max_seq):
    t = jnp.arange(max_seq, dtype=jnp.int32)
    row = (t // GRID_W).astype(F32)
    col = (t % GRID_W).astype(F32)
    inv = 1.0 / (ROPE_THETA ** (jnp.arange(0, AXIS_DIM, 2, dtype=F32) / AXIS_DIM))
    ar = row[:, None] * inv[None, :]
    ac = col[:, None] * inv[None, :]
    cos64 = jnp.concatenate([jnp.cos(ar), jnp.cos(ar), jnp.cos(ac), jnp.cos(ac)], axis=-1)
    sin64 = jnp.concatenate([-jnp.sin(ar), jnp.sin(ar), -jnp.sin(ac), jnp.sin(ac)], axis=-1)
    return jnp.tile(cos64, (1, 2)), jnp.tile(sin64, (1, 2))


def _two_part_specs(block, n_a, parts):
    off = 0 if parts[0] is parts[1] else n_a
    return [pl.BlockSpec(block, lambda i: (jnp.minimum(i, n_a - 1), 0)),
            pl.BlockSpec(block, lambda i: (jnp.maximum(i, n_a) - off, 0))]


def _qkv_proj(xparts, t, n_a, gain, w, qg, kg, cos, sin, bd, pos_map):
    row = lambda i: (i, 0)
    const = lambda i: (0, 0)
    return pl.pallas_call(
        functools.partial(_qkv_kernel, n_a=n_a),
        grid=(t // TM,),
        in_specs=_two_part_specs((TM, D_MODEL), n_a, xparts) + [
                  pl.BlockSpec((1, D_MODEL), const),
                  pl.BlockSpec((D_MODEL, QKV_WIDTH), const),
                  pl.BlockSpec((1, LANES), const),
                  pl.BlockSpec((1, LANES), const),
                  pl.BlockSpec((TM, LANES), pos_map),
                  pl.BlockSpec((TM, LANES), pos_map),
                  pl.BlockSpec((LANES, LANES), const)],
        out_specs=[pl.BlockSpec((TM, Q_EXP_WIDTH), row),
                   pl.BlockSpec((TM, KV_EXP_WIDTH), row),
                   pl.BlockSpec((KV_EXP_WIDTH, TM), lambda i: (0, i))],
        out_shape=[jax.ShapeDtypeStruct((t, Q_EXP_WIDTH), BF16),
                   jax.ShapeDtypeStruct((t, KV_EXP_WIDTH), BF16),
                   jax.ShapeDtypeStruct((KV_EXP_WIDTH, t), BF16)],
        compiler_params=_params(("parallel",)),
        name="qkv_proj",
    )(*xparts, gain, w, qg, kg, cos, sin, bd)


def _rest_proj(xparts, t, n_a, gain, w):
    row = lambda i: (i, 0)
    const = lambda i: (0, 0)
    out = jax.ShapeDtypeStruct((t, D_MODEL), BF16)
    return pl.pallas_call(
        functools.partial(_rest_kernel, n_a=n_a),
        grid=(t // TM,),
        in_specs=_two_part_specs((TM, D_MODEL), n_a, xparts) + [
                  pl.BlockSpec((1, D_MODEL), const),
                  pl.BlockSpec((D_MODEL, REST_WIDTH), const)],
        out_specs=[pl.BlockSpec((TM, D_MODEL), row)] * 4,
        out_shape=[out] * 4,
        compiler_params=_params(("parallel",)),
        name="rest_proj",
    )(*xparts, gain, w)


def _attention(shift, q, k, vt, casts, *, row0, n_seq, seq, tq, exact_max):
    nq = seq // tq
    q0 = row0 // tq
    s0 = row0 // seq
    n_steps = n_seq * nq
    step_blk = lambda b, i, km: (b * nq + i, 0, 0)
    views = [w.reshape(n_steps, w.shape[0] * w.shape[1] // n_steps, w.shape[2]) for w in casts]
    cast_specs = [pl.BlockSpec((1,) + v.shape[1:], step_blk) for v in views]
    outs = pl.pallas_call(
        functools.partial(_attn_kernel, exact_max=exact_max),
        grid_spec=pltpu.PrefetchScalarGridSpec(
            num_scalar_prefetch=1,
            grid=(n_seq, nq),
            in_specs=[pl.BlockSpec((tq, Q_EXP_WIDTH), lambda b, i, km: (q0 + b * nq + i, 0)),
                      pl.BlockSpec((seq, KV_EXP_WIDTH), lambda b, i, km: (s0 + b, 0)),
                      pl.BlockSpec((KV_EXP_WIDTH, seq), lambda b, i, km: (0, s0 + b))]
                     + cast_specs,
            out_specs=[pl.BlockSpec((tq, D_MODEL), lambda b, i, km: (b * nq + i, 0))]
                      + cast_specs),
        out_shape=[jax.ShapeDtypeStruct((n_seq * seq, D_MODEL), BF16)]
                  + [jax.ShapeDtypeStruct(v.shape, BF16) for v in views],
        compiler_params=_params(("parallel", "parallel")),
        name=f"attention_s{seq}" + ("_exact" if exact_max else ""),
    )(shift, q, k, vt, *views)
    return outs[0], [o.reshape(w.shape) for o, w in zip(outs[1:], casts)]


def _mix(attn_parts, cb, u, ga, gb, xparts, cw, woa, wob, wo, gain, rw, *, n_prompt, seq_p, seq_s):
    t = cb.shape[0]
    n_a = n_prompt // TM
    row = lambda i: (i, 0)
    const = lambda i: (0, 0)
    sub = TM // 8
    last = t // 8 - 1
    act = pl.BlockSpec((TM, D_MODEL), row)
    wspec = pl.BlockSpec((D_MODEL, D_MODEL), const)
    with_router = rw is not None
    kern = functools.partial(_mix_kernel, n_prompt_tiles=n_prompt // TM,
                             tiles4=seq_p // TM, tiles2=seq_s // TM, with_router=with_router)
    in_specs = (_two_part_specs((TM, D_MODEL), n_a, attn_parts) + [
                act, act,
                pl.BlockSpec((8, D_MODEL), lambda i: (jnp.maximum(i * sub - 1, 0), 0)),
                pl.BlockSpec((8, D_MODEL), lambda i: (jnp.minimum((i + 1) * sub, last), 0)),
                act, act]
                + _two_part_specs((TM, D_MODEL), n_a, xparts) + [
                pl.BlockSpec((3, D_MODEL), const),
                wspec, wspec, wspec,
                pl.BlockSpec((1, D_MODEL), const)])
    args = [*attn_parts, cb, u, u, u, ga, gb, *xparts, cw, woa, wob, wo, gain]
    out_specs = [act, act]
    out_shape = [jax.ShapeDtypeStruct((t, D_MODEL), F32),
                 jax.ShapeDtypeStruct((t, D_MODEL), F32 if with_router else BF16)]
    if with_router:
        in_specs.append(pl.BlockSpec((D_MODEL, LANES), const))
        args.append(rw)
        out_specs.append(pl.BlockSpec((TM, LANES), row))
        out_shape.append(jax.ShapeDtypeStruct((t, LANES), F32))
    return pl.pallas_call(
        kern,
        grid=(t // TM,),
        in_specs=in_specs,
        out_specs=out_specs,
        out_shape=out_shape,
        compiler_params=_params(("parallel",)),
        name="mix_proj_router" if with_router else "mix_proj",
    )(*args)


def _ffn(h, x, wg, wu, wd):
    t = x.shape[0]
    row = lambda i, j: (i, 0)
    return pl.pallas_call(
        _ffn_kernel,
        grid=(t // TM_FFN, D_FF // TF),
        in_specs=[pl.BlockSpec((TM_FFN, D_MODEL), row),
                  pl.BlockSpec((TM_FFN, D_MODEL), row),
                  pl.BlockSpec((D_MODEL, TF), lambda i, j: (0, j)),
                  pl.BlockSpec((D_MODEL, TF), lambda i, j: (0, j)),
                  pl.BlockSpec((TF, D_MODEL), lambda i, j: (j, 0))],
        out_specs=pl.BlockSpec((TM_FFN, D_MODEL), row),
        out_shape=jax.ShapeDtypeStruct((t, D_MODEL), F32),
        scratch_shapes=[pltpu.VMEM((TM_FFN, D_MODEL), F32)],
        compiler_params=_params(("parallel", "arbitrary")),
        name="ffn_dense",
    )(h, x, wg, wu, wd)


def _router(logits):
    t = logits.shape[0]
    row = lambda i: (i, 0)
    const = lambda i: (0, 0)
    r = jnp.arange(TR)
    tri = (r[None, :] < r[:, None]).astype(BF16)
    return pl.pallas_call(
        _router_kernel,
        grid=(t // TR,),
        in_specs=[pl.BlockSpec((TR, LANES), row),
                  pl.BlockSpec((TR, TR), const)],
        out_specs=[pl.BlockSpec((TR, LANES), row),
                   pl.BlockSpec((8, LANES), const)],
        out_shape=[jax.ShapeDtypeStruct((t, LANES), F32),
                   jax.ShapeDtypeStruct((8, LANES), F32)],
        scratch_shapes=[pltpu.VMEM((8, LANES), F32)],
        compiler_params=_params(("arbitrary",)),
        name="router",
    )(logits, tri)


def _route_plan(route, counts, t):
    e0 = route[:, ROUTE_E0].astype(jnp.int32)
    e1 = route[:, ROUTE_E1].astype(jnp.int32)
    r0 = route[:, ROUTE_R0].astype(jnp.int32)
    r1 = route[:, ROUTE_R1].astype(jnp.int32)
    cnt = counts[0, :N_EXPERTS].astype(jnp.int32)
    padded = ((cnt + TM_MOE - 1) // TM_MOE) * TM_MOE
    ends = jnp.cumsum(padded)
    starts = ends - padded
    experts = jnp.arange(N_EXPERTS, dtype=jnp.int32)
    pos0 = jnp.sum(jnp.where(e0[:, None] == experts[None, :], starts[None, :], 0), axis=1) + r0
    pos1 = jnp.sum(jnp.where(e1[:, None] == experts[None, :], starts[None, :], 0), axis=1) + r1
    n_tok_tiles = t // TM_MOE
    pos = jnp.concatenate([pos0.reshape(n_tok_tiles, 1, TM_MOE),
                           pos1.reshape(n_tok_tiles, 1, TM_MOE)], axis=2)
    n_tiles = 2 * t // TM_MOE + N_EXPERTS
    tile_start = jnp.arange(n_tiles, dtype=jnp.int32) * TM_MOE
    tile_valid = (tile_start < ends[-1]).astype(jnp.int32)
    tile_expert = jnp.sum((tile_start[:, None] >= ends[None, :]).astype(jnp.int32), axis=1)
    tile_expert = jnp.minimum(tile_expert, N_EXPERTS - 1)
    pad = (starts + cnt, padded - cnt)
    return pos, pad, tile_expert, tile_valid, n_tiles


def _dispatch(pad_start, pad_count, tile_valid, pos, h):
    t = h.shape[0]
    n_rows = tile_valid.shape[0] * TM_MOE
    return pl.pallas_call(
        _dispatch_kernel,
        grid_spec=pltpu.PrefetchScalarGridSpec(
            num_scalar_prefetch=3,
            grid=(t // TM_MOE,),
            in_specs=[pl.BlockSpec((1, 1, 2 * TM_MOE), lambda i, ps, pc, tv: (i, 0, 0),
                                   memory_space=pltpu.SMEM),
                      pl.BlockSpec((TM_MOE, D_MODEL), lambda i, ps, pc, tv: (i, 0))],
            out_specs=pl.BlockSpec(memory_space=pl.ANY),
            scratch_shapes=[pltpu.VMEM((TM_MOE, D_MODEL), F32),
                            pltpu.SemaphoreType.DMA(()),
                            pltpu.SemaphoreType.DMA(())]),
        out_shape=jax.ShapeDtypeStruct((n_rows, D_MODEL), F32),
        compiler_params=_params(("arbitrary",)),
        name="moe_dispatch",
    )(pad_start, pad_count, tile_valid, pos, h)


def _moe_ffn(tile_expert, tile_valid, xs, wg, wu, wd):
    n_rows = xs.shape[0]
    row = lambda g, j, te, tv: (g, 0)
    return pl.pallas_call(
        _moe_ffn_kernel,
        grid_spec=pltpu.PrefetchScalarGridSpec(
            num_scalar_prefetch=2,
            grid=(n_rows // TM_MOE, D_FF // TF_MOE),
            in_specs=[pl.BlockSpec((TM_MOE, D_MODEL), row),
                      pl.BlockSpec((1, D_MODEL, TF_MOE),
                                   lambda g, j, te, tv: (te[g], 0, j * tv[g])),
                      pl.BlockSpec((1, D_MODEL, TF_MOE),
                                   lambda g, j, te, tv: (te[g], 0, j * tv[g])),
                      pl.BlockSpec((1, TF_MOE, D_MODEL),
                                   lambda g, j, te, tv: (te[g], j * tv[g], 0))],
            out_specs=pl.BlockSpec((TM_MOE, D_MODEL), row),
            scratch_shapes=[pltpu.VMEM((TM_MOE, D_MODEL), F32)]),
        out_shape=jax.ShapeDtypeStruct((n_rows, D_MODEL), F32),
        compiler_params=_params(("parallel", "arbitrary")),
        name="moe_ffn",
    )(tile_expert, tile_valid, xs, wg, wu, wd)


def _combine(pos, x, route, fgain, ys, n_first):
    t = x.shape[0]
    n_a = n_first // TM_MOE
    row = lambda i: (i, 0)
    blk = (TM_MOE, D_MODEL)
    last = t // TM_MOE - 1
    pos_blk = (1, 1, 2 * TM_MOE)
    return pl.pallas_call(
        functools.partial(_combine_kernel, n_a=n_a),
        grid=(t // TM_MOE,),
        in_specs=[pl.BlockSpec(pos_blk, lambda i: (i, 0, 0), memory_space=pltpu.SMEM),
                  pl.BlockSpec(pos_blk, lambda i: (jnp.minimum(i + 1, last), 0, 0),
                               memory_space=pltpu.SMEM),
                  pl.BlockSpec(blk, row),
                  pl.BlockSpec((TM_MOE, LANES), row),
                  pl.BlockSpec((1, D_MODEL), lambda i: (0, 0)),
                  pl.BlockSpec(memory_space=pl.ANY)],
        out_specs=[pl.BlockSpec(blk, lambda i: (jnp.minimum(i, n_a - 1), 0)),
                   pl.BlockSpec(blk, lambda i: (jnp.maximum(i - n_a, 0), 0))],
        out_shape=[jax.ShapeDtypeStruct((n_first, D_MODEL), F32),
                   jax.ShapeDtypeStruct((t - n_first, D_MODEL), F32)],
        scratch_shapes=[pltpu.VMEM((2, 2, TM_MOE, D_MODEL), F32),
                        pltpu.SemaphoreType.DMA((2,))],
        compiler_params=_params(("arbitrary",)),
        name="moe_combine",
    )(pos, pos, x, route, fgain, ys)


def kernel(x_prompt, x_sample, norm_mix, w_in, q_norm, k_norm, conv_w, w_oa, w_ob, w_o, norm_ffn,
           ffn_w_gate, ffn_w_up, ffn_w_down, router_w, moe_w_gate, moe_w_up, moe_w_down, final_norm):
    bp, sp, _ = x_prompt.shape
    bs, ss, _ = x_sample.shape
    n_prompt = bp * sp
    n_sample = bs * ss
    depth = norm_mix.shape[0]
    assert sp % TM == 0 and ss % TM == 0 and n_prompt % TM_FFN == 0 and n_sample % TM_FFN == 0
    assert depth == 2 and ffn_w_gate.shape[0] == 1 and moe_w_gate.shape[0] == 1

    t_all = n_prompt + n_sample
    xparts = (x_prompt.reshape(n_prompt, D_MODEL), x_sample.reshape(n_sample, D_MODEL))

    cos, sin = _rope_tables(max(sp, ss))
    n_prompt_tiles, tiles4, tiles2 = n_prompt // TM, sp // TM, ss // TM
    pos_map = lambda i: (jnp.where(i < n_prompt_tiles, i % tiles4, i % tiles2), 0)
    idx = jnp.arange(LANES)
    bd = (idx[:, None] // HEAD_DIM == idx[None, :] // HEAD_DIM).astype(BF16)

    cast_queue = [((name, j), w[j]) for j in range(moe_w_gate.shape[0])
                  for name, w in (("gate", moe_w_gate), ("up", moe_w_up), ("down", moe_w_down))]
    cast_done = {}

    for l in range(depth):
        gain = norm_mix[l][None, :]
        w_l = w_in[l]
        w_qkv = w_l[:, :QKV_WIDTH].astype(BF16)
        w_rest = w_l[:, QKV_WIDTH:].astype(BF16)
        qg = jnp.tile(q_norm[l] * Q_SCALE, 2)[None, :]
        kg = jnp.tile(k_norm[l], 2)[None, :]
        q, k, vt = _qkv_proj(xparts, t_all, n_prompt_tiles, gain, w_qkv, qg, kg, cos, sin, bd,
                             pos_map)
        cb, u, ga, gb = _rest_proj(xparts, t_all, n_prompt_tiles, gain, w_rest)
        shift = (Q_SCALE * HEAD_DIM * jnp.max(jnp.abs(q_norm[l]))
                 * jnp.max(jnp.abs(k_norm[l]))).reshape(1)
        bound_ok = 2.0 * shift[0] < MAX_SHIFT_GAP

        cast_queue = [job for job in cast_queue if 2 * job[0][1] + 1 >= l]
        jobs_p, jobs_s = cast_queue[:1], cast_queue[1:2]
        cast_queue = cast_queue[2:]

        def attend(exact_max, tq_p, tq_s):
            def run(shift, q, k, vt, w_p, w_s):
                attn_p, done_p = _attention(shift, q, k, vt, w_p, row0=0, n_seq=bp, seq=sp,
                                            tq=tq_p, exact_max=exact_max)
                attn_s, done_s = _attention(shift, q, k, vt, w_s, row0=n_prompt, n_seq=bs,
                                            seq=ss, tq=tq_s, exact_max=exact_max)
                return (attn_p, attn_s), done_p + done_s
            return run

        attn_parts, done = lax.cond(
            bound_ok, attend(False, 256, 256), attend(True, 128, 256), shift, q, k, vt,
            [w for _, w in jobs_p], [w for _, w in jobs_s])
        cast_done.update(zip([key for key, _ in jobs_p + jobs_s], done))
        j = l // 2
        is_moe = l % 2 == 1
        rw = (jnp.pad(jnp.tile(router_w[j], (1, 2)), ((0, 0), (0, LANES - 2 * N_EXPERTS)))
              if is_moe else None)
        outs = _mix(attn_parts, cb, u, ga, gb, xparts, conv_w[l],
                    w_oa[l].astype(BF16), w_ob[l].astype(BF16), w_o[l].astype(BF16),
                    norm_ffn[l][None, :], rw, n_prompt=n_prompt, seq_p=sp, seq_s=ss)
        if not is_moe:
            x, h2 = outs
            x = _ffn(h2, x, ffn_w_gate[j].astype(BF16), ffn_w_up[j].astype(BF16),
                     ffn_w_down[j].astype(BF16))
            xparts = (x, x)
        else:
            x, h2, logits = outs
            route, counts = _router(logits)
            pos, pad, tile_expert, tile_valid, n_tiles = _route_plan(route, counts, t_all)
            xs = _dispatch(*pad, tile_valid, pos, h2)
            wg, wu, wd = [cast_done[(name, j)] if (name, j) in cast_done else w[j].astype(BF16)
                          for name, w in (("gate", moe_w_gate), ("up", moe_w_up),
                                          ("down", moe_w_down))]
            ys = _moe_ffn(tile_expert, tile_valid, xs, wg, wu, wd)
            xparts = _combine(pos, x, route, final_norm[None, :], ys, n_prompt)

    return (xparts[0].reshape(bp, sp, D_MODEL), xparts[1].reshape(bs, ss, D_MODEL))
```

```python
import functools
import math

import jax
import jax.numpy as jnp
from jax import lax
from jax.experimental import pallas as pl
from jax.experimental.pallas import tpu as pltpu

F32 = jnp.float32
BF16 = jnp.bfloat16

D_MODEL = 1024
N_HEADS = 16
N_KV_HEADS = 4
HEAD_DIM = 64
GROUP = N_HEADS // N_KV_HEADS
KV_WIDTH = N_KV_HEADS * HEAD_DIM
AXIS_DIM = HEAD_DIM // 2
ROPE_THETA = 10000.0
GRID_W = 64
D_FF = 3584
N_EXPERTS = 8
EPS = 1e-6
LANES = 128
QKV_WIDTH = D_MODEL + 2 * KV_WIDTH
REST_WIDTH = 5 * D_MODEL
Q_EXP_WIDTH = N_HEADS * LANES
KV_EXP_WIDTH = N_KV_HEADS * LANES
Q_SCALE = math.log2(math.e) / math.sqrt(HEAD_DIM)
V_ROWS = 80
KEY_CHUNK = 512
ROW_UNROLL = 8
MAX_SHIFT_GAP = 100.0

TM = 512
TM_FFN = 512
TF = 1792
TF_MOE = 1792
TR = 512
TM_MOE = 512
VMEM_LIMIT = 56 * 1024 * 1024


def _params(sem):
    return pltpu.CompilerParams(dimension_semantics=sem, vmem_limit_bytes=VMEM_LIMIT)


def _rms(x, gain):
    return x * lax.rsqrt(jnp.mean(x * x, axis=-1, keepdims=True) + EPS) * gain


def _pick(n_a, a_ref, b_ref):
    return jnp.where(pl.program_id(0) < n_a, a_ref[...], b_ref[...])


def _qkv_kernel(xa_ref, xb_ref, g_ref, w_ref, q_own_ref, q_other_ref, k_own_ref, k_other_ref,
                bd_ref, q_ref, k_ref, vt_ref, *, n_a):
    tm = xa_ref.shape[0]
    h = _rms(_pick(n_a, xa_ref, xb_ref), g_ref[...]).astype(BF16)
    p = jnp.dot(h, w_ref[...], preferred_element_type=F32)
    bd = bd_ref[...]
    lane = lax.broadcasted_iota(jnp.int32, (tm, LANES), 1)
    first16 = (lane & 31) < 16
    low_half = lane < HEAD_DIM

    def norm_rope(c, own, other):
        ms = jnp.dot((c * c).astype(BF16), bd, preferred_element_type=F32)
        partner = jnp.where(first16, pltpu.roll(c, LANES - 16, 1), pltpu.roll(c, 16, 1))
        return (c * own + partner * other) * lax.rsqrt(ms + EPS)

    is_aux = lane == HEAD_DIM
    zero = jnp.zeros((tm, LANES), F32)
    ones_aux = jnp.where(is_aux, 1.0, zero)

    q_own = q_own_ref[...]
    q_other = q_other_ref[...]
    for c in range(N_HEADS // 2):
        r = norm_rope(p[:, c * LANES:(c + 1) * LANES], q_own, q_other)
        even = jnp.where(low_half, r, zero)
        odd = jnp.where(low_half, pltpu.roll(r, HEAD_DIM, 1), zero)
        q_ref[:, (2 * c) * LANES:(2 * c + 1) * LANES] = even.astype(BF16)
        q_ref[:, (2 * c + 1) * LANES:(2 * c + 2) * LANES] = odd.astype(BF16)

    k_own = k_own_ref[...]
    k_other = k_other_ref[...]
    for c in range(KV_WIDTH // LANES):
        kn = norm_rope(p[:, D_MODEL + c * LANES:D_MODEL + (c + 1) * LANES], k_own, k_other)
        even = jnp.where(low_half, kn, ones_aux)
        odd = jnp.where(low_half, pltpu.roll(kn, HEAD_DIM, 1), ones_aux)
        k_ref[:, (2 * c) * LANES:(2 * c + 1) * LANES] = even.astype(BF16)
        k_ref[:, (2 * c + 1) * LANES:(2 * c + 2) * LANES] = odd.astype(BF16)
        vv = p[:, D_MODEL + KV_WIDTH + c * LANES:D_MODEL + KV_WIDTH + (c + 1) * LANES]
        even = jnp.where(low_half, vv, ones_aux)
        odd = jnp.where(low_half, pltpu.roll(vv, HEAD_DIM, 1), ones_aux)
        vt_ref[(2 * c) * LANES:(2 * c + 1) * LANES, :] = even.T.astype(BF16)
        vt_ref[(2 * c + 1) * LANES:(2 * c + 2) * LANES, :] = odd.T.astype(BF16)


def _rest_kernel(xa_ref, xb_ref, g_ref, w_ref, cb_ref, u_ref, ga_ref, gb_ref, *, n_a):
    h = _rms(_pick(n_a, xa_ref, xb_ref), g_ref[...]).astype(BF16)
    p = jnp.dot(h, w_ref[...], preferred_element_type=F32)
    cb_ref[...] = p[:, :D_MODEL].astype(BF16)
    u_ref[...] = (p[:, D_MODEL:2 * D_MODEL] * p[:, 2 * D_MODEL:3 * D_MODEL]).astype(BF16)
    ga_ref[...] = p[:, 3 * D_MODEL:4 * D_MODEL].astype(BF16)
    gb_ref[...] = p[:, 4 * D_MODEL:].astype(BF16)


def _attn_kernel(shift_ref, q_ref, k_ref, vt_ref, *refs, exact_max):
    n_cast = (len(refs) - 1) // 2
    o_ref = refs[n_cast]
    for src, dst in zip(refs[:n_cast], refs[n_cast + 1:]):
        dst[...] = src[...].astype(BF16)
    tq = q_ref.shape[0]
    is_aux_row = lax.broadcasted_iota(jnp.int32, (LANES, GROUP * tq), 0) == HEAD_DIM
    for j in range(N_KV_HEADS):
        qt = jnp.concatenate(
            [q_ref[:, h * LANES:(h + 1) * LANES].T for h in range(GROUP * j, GROUP * (j + 1))],
            axis=1)
        if exact_max:
            st = jnp.dot(k_ref[:, j * LANES:(j + 1) * LANES], qt,
                         preferred_element_type=F32)
            st = st - jnp.max(st, axis=0, keepdims=True)
            pt = jnp.exp2(st).astype(BF16)
            ot = jnp.dot(vt_ref[j * LANES:j * LANES + V_ROWS, :], pt,
                         preferred_element_type=F32)
        else:
            neg_shift = jnp.full(qt.shape, -shift_ref[0], F32).astype(BF16)
            qt = jnp.where(is_aux_row, neg_shift, qt)
            ot = jnp.zeros((V_ROWS, GROUP * tq), F32)
            for c in range(k_ref.shape[0] // KEY_CHUNK):
                rows = slice(c * KEY_CHUNK, (c + 1) * KEY_CHUNK)
                st = jnp.dot(k_ref[rows, j * LANES:(j + 1) * LANES], qt,
                             preferred_element_type=F32)
                pt = jnp.exp2(st).astype(BF16)
                ot = ot + jnp.dot(vt_ref[j * LANES:j * LANES + V_ROWS, rows], pt,
                                  preferred_element_type=F32)
        ot = ot[:HEAD_DIM] / ot[HEAD_DIM:HEAD_DIM + 1]
        for a in range(2):
            pair = jnp.concatenate([ot[:, (2 * a) * tq:(2 * a + 1) * tq],
                                    ot[:, (2 * a + 1) * tq:(2 * a + 2) * tq]], axis=0)
            c = 2 * j + a
            o_ref[:, c * LANES:(c + 1) * LANES] = pair.T.astype(BF16)


def _mix_kernel(attn_a_ref, attn_b_ref, cb_ref, u_ref, up_ref, un_ref, ga_ref, gb_ref,
                xa_ref, xb_ref, cw_ref, woa_ref, wob_ref, wo_ref, g_ref, *rest,
                n_prompt_tiles, tiles4, tiles2, with_router):
    if with_router:
        rw_ref, xo_ref, h2_ref, lg_ref = rest
    else:
        xo_ref, h2_ref = rest
    i = pl.program_id(0)
    tm = xa_ref.shape[0]
    is_prompt = i < n_prompt_tiles
    seq_start = jnp.where(is_prompt, i % tiles4 == 0, i % tiles2 == 0)
    seq_end = jnp.where(is_prompt, i % tiles4 == tiles4 - 1, i % tiles2 == tiles2 - 1)

    attn = _pick(n_prompt_tiles, attn_a_ref, attn_b_ref)
    ya = jnp.dot(attn, woa_ref[...], preferred_element_type=F32)

    u = u_ref[...].astype(F32)
    row = lax.broadcasted_iota(jnp.int32, u.shape, 0)
    prev_row = jnp.where(seq_start, 0.0, up_ref[7:8, :].astype(F32))
    next_row = jnp.where(seq_end, 0.0, un_ref[0:1, :].astype(F32))
    u_prev = jnp.where(row == 0, prev_row, pltpu.roll(u, 1, 0))
    u_next = jnp.where(row == tm - 1, next_row, pltpu.roll(u, tm - 1, 0))
    cw = cw_ref[...]
    conv = cw[0:1, :] * u_prev + cw[1:2, :] * u + cw[2:3, :] * u_next
    yb_in = (cb_ref[...].astype(F32) * conv).astype(BF16)
    yb = jnp.dot(yb_in, wob_ref[...], preferred_element_type=F32)

    m = (jax.nn.sigmoid(ga_ref[...].astype(F32)) * ya
         + jax.nn.sigmoid(gb_ref[...].astype(F32)) * yb)
    xn = (_pick(n_prompt_tiles, xa_ref, xb_ref)
          + jnp.dot(m.astype(BF16), wo_ref[...], preferred_element_type=F32))
    xo_ref[...] = xn
    h2 = _rms(xn, g_ref[...])
    h2_ref[...] = h2.astype(h2_ref.dtype)
    if not with_router:
        return
    hi = h2.astype(BF16)
    lo = (h2 - hi.astype(F32)).astype(BF16)
    rw = rw_ref[...]
    rhi = rw.astype(BF16)
    rlo = (rw - rhi.astype(F32)).astype(BF16)
    lane = lax.broadcasted_iota(jnp.int32, rw.shape, 1)
    hi_terms = jnp.dot(hi, jnp.where(lane < N_EXPERTS, rhi, rlo), preferred_element_type=F32)
    lg_ref[...] = (hi_terms + pltpu.roll(hi_terms, LANES - N_EXPERTS, 1)
                   + jnp.dot(lo, rhi, preferred_element_type=F32))


def _ffn_kernel(h_ref, x_ref, wg_ref, wu_ref, wd_ref, o_ref, acc_ref):
    j = pl.program_id(1)

    @pl.when(j == 0)
    def _():
        acc_ref[...] = jnp.zeros_like(acc_ref)

    h = h_ref[...]
    g = jnp.dot(h, wg_ref[...], preferred_element_type=F32)
    u = jnp.dot(h, wu_ref[...], preferred_element_type=F32)
    a = (g * jax.nn.sigmoid(g) * u).astype(BF16)
    acc_ref[...] += jnp.dot(a, wd_ref[...], preferred_element_type=F32)

    @pl.when(j == pl.num_programs(1) - 1)
    def _():
        o_ref[...] = x_ref[...] + acc_ref[...]


ROUTE_E0, ROUTE_E1, ROUTE_R0, ROUTE_R1, ROUTE_W0, ROUTE_W1 = range(6)


def _lane_pick(x, lane, k):
    return jnp.sum(jnp.where(lane == k, x, 0.0), axis=-1, keepdims=True)


def _router_kernel(lg_ref, tri_ref, route_ref, count_ref, base_ref):
    i = pl.program_id(0)

    @pl.when(i == 0)
    def _():
        base_ref[...] = jnp.zeros_like(base_ref)

    lg = lg_ref[...]
    lane = lax.broadcasted_iota(jnp.int32, lg.shape, 1)
    neg = jnp.float32(-jnp.inf)
    l1 = jnp.where(lane < N_EXPERTS, lg, neg)
    m1 = jnp.max(l1, axis=-1, keepdims=True)
    i1 = jnp.min(jnp.where(l1 == m1, lane, LANES), axis=-1, keepdims=True)
    l2 = jnp.where(lane == i1, neg, l1)
    m2 = jnp.max(l2, axis=-1, keepdims=True)
    i2 = jnp.min(jnp.where(l2 == m2, lane, LANES), axis=-1, keepdims=True)
    e = jnp.exp(m2 - m1)
    w1 = 1.0 / (1.0 + e)
    w2 = e / (1.0 + e)

    hot1 = lane == i1
    hot2 = lane == i2
    onehot = jnp.where(hot1 | hot2, 1.0, 0.0)
    base = base_ref[0:1, :]
    prefix = jnp.dot(tri_ref[...], onehot.astype(BF16), preferred_element_type=F32) + base
    r1 = jnp.sum(jnp.where(hot1, prefix, 0.0), axis=-1, keepdims=True)
    r2 = jnp.sum(jnp.where(hot2, prefix, 0.0), axis=-1, keepdims=True)
    total = base + jnp.sum(onehot, axis=0, keepdims=True)
    base_ref[...] = jnp.broadcast_to(total, base_ref.shape)
    count_ref[...] = jnp.broadcast_to(total, count_ref.shape)

    rec = jnp.where(lane == ROUTE_E0, i1.astype(F32), 0.0)
    rec = jnp.where(lane == ROUTE_E1, i2.astype(F32), rec)
    rec = jnp.where(lane == ROUTE_R0, r1, rec)
    rec = jnp.where(lane == ROUTE_R1, r2, rec)
    rec = jnp.where(lane == ROUTE_W0, w1, rec)
    rec = jnp.where(lane == ROUTE_W1, w2, rec)
    route_ref[...] = rec


def _row_copy(src, src_row, dst, dst_row, sem):
    return pltpu.make_async_copy(src.at[pl.ds(src_row, 1)], dst.at[pl.ds(dst_row, 1)], sem)


def _dispatch_kernel(pad_start_ref, pad_count_ref, tile_valid_ref, pos_ref, h_ref, xs_ref,
                     zero_ref, sem, zsem):
    tm = h_ref.shape[0]
    n_tiles = xs_ref.shape[0] // tm

    @pl.when(pl.program_id(0) == 0)
    def _():
        zero_ref[...] = jnp.zeros_like(zero_ref)
        for e in range(N_EXPERTS):
            start = pad_start_ref[e]
            count = pad_count_ref[e]

            def zero_issue(r, carry):
                _row_copy(zero_ref, 0, xs_ref, start + r, zsem).start()
                return carry

            def zero_wait(r, carry):
                _row_copy(zero_ref, 0, xs_ref, 0, zsem).wait()
                return carry

            lax.fori_loop(0, count, zero_issue, 0)
            lax.fori_loop(0, count, zero_wait, 0)
        for g in range(n_tiles - N_EXPERTS, n_tiles):
            @pl.when(tile_valid_ref[g] == 0)
            def _():
                fill = pltpu.make_async_copy(zero_ref, xs_ref.at[pl.ds(g * tm, tm)], zsem)
                fill.start()
                fill.wait()

    def issue(i, carry):
        for k in range(ROW_UNROLL):
            r = i * ROW_UNROLL + k
            _row_copy(h_ref, r, xs_ref, pos_ref[0, 0, r], sem).start()
            _row_copy(h_ref, r, xs_ref, pos_ref[0, 0, tm + r], sem).start()
        return carry

    lax.fori_loop(0, tm // ROW_UNROLL, issue, 0)
    for _ in range(2):
        pltpu.make_async_copy(h_ref, xs_ref.at[pl.ds(0, tm)], sem).wait()


def _moe_ffn_kernel(te_ref, tv_ref, x_ref, wg_ref, wu_ref, wd_ref, y_ref, acc_ref):
    del te_ref
    g_idx = pl.program_id(0)
    j = pl.program_id(1)
    valid = tv_ref[g_idx] > 0

    @pl.when(j == 0)
    def _():
        acc_ref[...] = jnp.zeros_like(acc_ref)

    @pl.when(valid)
    def _():
        h = x_ref[...].astype(BF16)
        g = jnp.dot(h, wg_ref[0], preferred_element_type=F32)
        u = jnp.dot(h, wu_ref[0], preferred_element_type=F32)
        a = (g * jax.nn.sigmoid(g) * u).astype(BF16)
        acc_ref[...] += jnp.dot(a, wd_ref[0], preferred_element_type=F32)

    @pl.when(j == pl.num_programs(1) - 1)
    def _():
        y_ref[...] = acc_ref[...]


def _combine_kernel(pos_ref, nxt_ref, x_ref, route_ref, fg_ref, y_hbm, oa_ref, ob_ref,
                    ybuf, sem, *, n_a):
    i = pl.program_id(0)
    tc = x_ref.shape[0]
    cur = i % 2

    def issue_tile(p_ref, par):
        def issue(blk, carry):
            for k in range(ROW_UNROLL):
                r = blk * ROW_UNROLL + k
                _row_copy(y_hbm, p_ref[0, 0, r], ybuf.at[par, 0], r, sem.at[par]).start()
                _row_copy(y_hbm, p_ref[0, 0, tc + r], ybuf.at[par, 1], r, sem.at[par]).start()
            return carry
        lax.fori_loop(0, tc // ROW_UNROLL, issue, 0)

    @pl.when(i == 0)
    def _():
        issue_tile(pos_ref, 0)

    @pl.when(i + 1 < pl.num_programs(0))
    def _():
        issue_tile(nxt_ref, 1 - cur)

    for slot in range(2):
        pltpu.make_async_copy(y_hbm.at[pl.ds(0, tc)], ybuf.at[cur, slot], sem.at[cur]).wait()
    route = route_ref[...]
    lane = lax.broadcasted_iota(jnp.int32, route.shape, 1)
    w0 = _lane_pick(route, lane, ROUTE_W0)
    w1 = _lane_pick(route, lane, ROUTE_W1)
    out = _rms(x_ref[...] + (w0 * ybuf[cur, 0] + w1 * ybuf[cur, 1]), fg_ref[...])
    in_a = i < n_a

    @pl.when(in_a)
    def _():
        oa_ref[...] = out

    @pl.when(jnp.logical_not(in_a))
    def _():
        ob_ref[...] = out


def _rope_tables(max_seq):
    t = jnp.arange(max_seq, dtype=jnp.int32)
    row = (t // GRID_W).astype(F32)
    col = (t % GRID_W).astype(F32)
    inv = 1.0 / (ROPE_THETA ** (jnp.arange(0, AXIS_DIM, 2, dtype=F32) / AXIS_DIM))
    ar = row[:, None] * inv[None, :]
    ac = col[:, None] * inv[None, :]
    cos64 = jnp.concatenate([jnp.cos(ar), jnp.cos(ar), jnp.cos(ac), jnp.cos(ac)], axis=-1)
    sin64 = jnp.concatenate([-jnp.sin(ar), jnp.sin(ar), -jnp.sin(ac), jnp.sin(ac)], axis=-1)
    return cos64, sin64


def _gain_rope_tables(cos64, sin64, gain):
    half = AXIS_DIM // 2
    partner_gain = gain.reshape(-1, 2, half)[:, ::-1, :].reshape(-1)
    return (jnp.tile(cos64 * gain[None, :], (1, 2)),
            jnp.tile(sin64 * partner_gain[None, :], (1, 2)))


def _two_part_specs(block, n_a, parts):
    off = 0 if parts[0] is parts[1] else n_a
    return [pl.BlockSpec(block, lambda i: (jnp.minimum(i, n_a - 1), 0)),
            pl.BlockSpec(block, lambda i: (jnp.maximum(i, n_a) - off, 0))]


def _qkv_proj(xparts, t, n_a, gain, w, q_tables, k_tables, bd, pos_map):
    row = lambda i: (i, 0)
    const = lambda i: (0, 0)
    table = pl.BlockSpec((TM, LANES), pos_map)
    return pl.pallas_call(
        functools.partial(_qkv_kernel, n_a=n_a),
        grid=(t // TM,),
        in_specs=_two_part_specs((TM, D_MODEL), n_a, xparts) + [
                  pl.BlockSpec((1, D_MODEL), const),
                  pl.BlockSpec((D_MODEL, QKV_WIDTH), const),
                  table, table, table, table,
                  pl.BlockSpec((LANES, LANES), const)],
        out_specs=[pl.BlockSpec((TM, Q_EXP_WIDTH), row),
                   pl.BlockSpec((TM, KV_EXP_WIDTH), row),
                   pl.BlockSpec((KV_EXP_WIDTH, TM), lambda i: (0, i))],
        out_shape=[jax.ShapeDtypeStruct((t, Q_EXP_WIDTH), BF16),
                   jax.ShapeDtypeStruct((t, KV_EXP_WIDTH), BF16),
                   jax.ShapeDtypeStruct((KV_EXP_WIDTH, t), BF16)],
        compiler_params=_params(("parallel",)),
        name="qkv_proj",
    )(*xparts, gain, w, *q_tables, *k_tables, bd)


def _rest_proj(xparts, t, n_a, gain, w):
    row = lambda i: (i, 0)
    const = lambda i: (0, 0)
    out = jax.ShapeDtypeStruct((t, D_MODEL), BF16)
    return pl.pallas_call(
        functools.partial(_rest_kernel, n_a=n_a),
        grid=(t // TM,),
        in_specs=_two_part_specs((TM, D_MODEL), n_a, xparts) + [
                  pl.BlockSpec((1, D_MODEL), const),
                  pl.BlockSpec((D_MODEL, REST_WIDTH), const)],
        out_specs=[pl.BlockSpec((TM, D_MODEL), row)] * 4,
        out_shape=[out] * 4,
        compiler_params=_params(("parallel",)),
        name="rest_proj",
    )(*xparts, gain, w)


def _attention(shift, q, k, vt, casts, *, row0, n_seq, seq, tq, exact_max):
    nq = seq // tq
    q0 = row0 // tq
    s0 = row0 // seq
    n_steps = n_seq * nq
    step_blk = lambda b, i, km: (b * nq + i, 0, 0)
    views = [w.reshape(n_steps, w.shape[0] * w.shape[1] // n_steps, w.shape[2]) for w in casts]
    cast_specs = [pl.BlockSpec((1,) + v.shape[1:], step_blk) for v in views]
    outs = pl.pallas_call(
        functools.partial(_attn_kernel, exact_max=exact_max),
        grid_spec=pltpu.PrefetchScalarGridSpec(
            num_scalar_prefetch=1,
            grid=(n_seq, nq),
            in_specs=[pl.BlockSpec((tq, Q_EXP_WIDTH), lambda b, i, km: (q0 + b * nq + i, 0)),
                      pl.BlockSpec((seq, KV_EXP_WIDTH), lambda b, i, km: (s0 + b, 0)),
                      pl.BlockSpec((KV_EXP_WIDTH, seq), lambda b, i, km: (0, s0 + b))]
                     + cast_specs,
            out_specs=[pl.BlockSpec((tq, D_MODEL), lambda b, i, km: (b * nq + i, 0))]
                      + cast_specs),
        out_shape=[jax.ShapeDtypeStruct((n_seq * seq, D_MODEL), BF16)]
                  + [jax.ShapeDtypeStruct(v.shape, BF16) for v in views],
        compiler_params=_params(("parallel", "parallel")),
        name=f"attention_s{seq}" + ("_exact" if exact_max else ""),
    )(shift, q, k, vt, *views)
    return outs[0], [o.reshape(w.shape) for o, w in zip(outs[1:], casts)]


def _mix(attn_parts, cb, u, ga, gb, xparts, cw, woa, wob, wo, gain, rw, *, n_prompt, seq_p, seq_s):
    t = cb.shape[0]
    n_a = n_prompt // TM
    row = lambda i: (i, 0)
    const = lambda i: (0, 0)
    sub = TM // 8
    last = t // 8 - 1
    act = pl.BlockSpec((TM, D_MODEL), row)
    wspec = pl.BlockSpec((D_MODEL, D_MODEL), const)
    with_router = rw is not None
    kern = functools.partial(_mix_kernel, n_prompt_tiles=n_prompt // TM,
                             tiles4=seq_p // TM, tiles2=seq_s // TM, with_router=with_router)
    in_specs = (_two_part_specs((TM, D_MODEL), n_a, attn_parts) + [
                act, act,
                pl.BlockSpec((8, D_MODEL), lambda i: (jnp.maximum(i * sub - 1, 0), 0)),
                pl.BlockSpec((8, D_MODEL), lambda i: (jnp.minimum((i + 1) * sub, last), 0)),
                act, act]
                + _two_part_specs((TM, D_MODEL), n_a, xparts) + [
                pl.BlockSpec((3, D_MODEL), const),
                wspec, wspec, wspec,
                pl.BlockSpec((1, D_MODEL), const)])
    args = [*attn_parts, cb, u, u, u, ga, gb, *xparts, cw, woa, wob, wo, gain]
    out_specs = [act, act]
    out_shape = [jax.ShapeDtypeStruct((t, D_MODEL), F32),
                 jax.ShapeDtypeStruct((t, D_MODEL), F32 if with_router else BF16)]
    if with_router:
        in_specs.append(pl.BlockSpec((D_MODEL, LANES), const))
        args.append(rw)
        out_specs.append(pl.BlockSpec((TM, LANES), row))
        out_shape.append(jax.ShapeDtypeStruct((t, LANES), F32))
    return pl.pallas_call(
        kern,
        grid=(t // TM,),
        in_specs=in_specs,
        out_specs=out_specs,
        out_shape=out_shape,
        compiler_params=_params(("parallel",)),
        name="mix_proj_router" if with_router else "mix_proj",
    )(*args)


def _ffn(h, x, wg, wu, wd):
    t = x.shape[0]
    row = lambda i, j: (i, 0)
    return pl.pallas_call(
        _ffn_kernel,
        grid=(t // TM_FFN, D_FF // TF),
        in_specs=[pl.BlockSpec((TM_FFN, D_MODEL), row),
                  pl.BlockSpec((TM_FFN, D_MODEL), row),
                  pl.BlockSpec((D_MODEL, TF), lambda i, j: (0, j)),
                  pl.BlockSpec((D_MODEL, TF), lambda i, j: (0, j)),
                  pl.BlockSpec((TF, D_MODEL), lambda i, j: (j, 0))],
        out_specs=pl.BlockSpec((TM_FFN, D_MODEL), row),
        out_shape=jax.ShapeDtypeStruct((t, D_MODEL), F32),
        scratch_shapes=[pltpu.VMEM((TM_FFN, D_MODEL), F32)],
        compiler_params=_params(("parallel", "arbitrary")),
        name="ffn_dense",
    )(h, x, wg, wu, wd)


def _router(logits):
    t = logits.shape[0]
    row = lambda i: (i, 0)
    const = lambda i: (0, 0)
    r = jnp.arange(TR)
    tri = (r[None, :] < r[:, None]).astype(BF16)
    return pl.pallas_call(
        _router_kernel,
        grid=(t // TR,),
        in_specs=[pl.BlockSpec((TR, LANES), row),
                  pl.BlockSpec((TR, TR), const)],
        out_specs=[pl.BlockSpec((TR, LANES), row),
                   pl.BlockSpec((8, LANES), const)],
        out_shape=[jax.ShapeDtypeStruct((t, LANES), F32),
                   jax.ShapeDtypeStruct((8, LANES), F32)],
        scratch_shapes=[pltpu.VMEM((8, LANES), F32)],
        compiler_params=_params(("arbitrary",)),
        name="router",
    )(logits, tri)


def _route_plan(route, counts, t):
    e0 = route[:, ROUTE_E0].astype(jnp.int32)
    e1 = route[:, ROUTE_E1].astype(jnp.int32)
    r0 = route[:, ROUTE_R0].astype(jnp.int32)
    r1 = route[:, ROUTE_R1].astype(jnp.int32)
    cnt = counts[0, :N_EXPERTS].astype(jnp.int32)
    padded = ((cnt + TM_MOE - 1) // TM_MOE) * TM_MOE
    ends = jnp.cumsum(padded)
    starts = ends - padded
    experts = jnp.arange(N_EXPERTS, dtype=jnp.int32)
    pos0 = jnp.sum(jnp.where(e0[:, None] == experts[None, :], starts[None, :], 0), axis=1) + r0
    pos1 = jnp.sum(jnp.where(e1[:, None] == experts[None, :], starts[None, :], 0), axis=1) + r1
    n_tok_tiles = t // TM_MOE
    pos = jnp.concatenate([pos0.reshape(n_tok_tiles, 1, TM_MOE),
                           pos1.reshape(n_tok_tiles, 1, TM_MOE)], axis=2)
    n_tiles = 2 * t // TM_MOE + N_EXPERTS
    tile_start = jnp.arange(n_tiles, dtype=jnp.int32) * TM_MOE
    tile_valid = (tile_start < ends[-1]).astype(jnp.int32)
    tile_expert = jnp.sum((tile_start[:, None] >= ends[None, :]).astype(jnp.int32), axis=1)
    tile_expert = jnp.minimum(tile_expert, N_EXPERTS - 1)
    pad = (starts + cnt, padded - cnt)
    return pos, pad, tile_expert, tile_valid, n_tiles


def _dispatch(pad_start, pad_count, tile_valid, pos, h):
    t = h.shape[0]
    n_rows = tile_valid.shape[0] * TM_MOE
    return pl.pallas_call(
        _dispatch_kernel,
        grid_spec=pltpu.PrefetchScalarGridSpec(
            num_scalar_prefetch=3,
            grid=(t // TM_MOE,),
            in_specs=[pl.BlockSpec((1, 1, 2 * TM_MOE), lambda i, ps, pc, tv: (i, 0, 0),
                                   memory_space=pltpu.SMEM),
                      pl.BlockSpec((TM_MOE, D_MODEL), lambda i, ps, pc, tv: (i, 0))],
            out_specs=pl.BlockSpec(memory_space=pl.ANY),
            scratch_shapes=[pltpu.VMEM((TM_MOE, D_MODEL), F32),
                            pltpu.SemaphoreType.DMA(()),
                            pltpu.SemaphoreType.DMA(())]),
        out_shape=jax.ShapeDtypeStruct((n_rows, D_MODEL), F32),
        compiler_params=_params(("arbitrary",)),
        name="moe_dispatch",
    )(pad_start, pad_count, tile_valid, pos, h)


def _moe_ffn(tile_expert, tile_valid, xs, wg, wu, wd):
    n_rows = xs.shape[0]
    row = lambda g, j, te, tv: (g, 0)
    return pl.pallas_call(
        _moe_ffn_kernel,
        grid_spec=pltpu.PrefetchScalarGridSpec(
            num_scalar_prefetch=2,
            grid=(n_rows // TM_MOE, D_FF // TF_MOE),
            in_specs=[pl.BlockSpec((TM_MOE, D_MODEL), row),
                      pl.BlockSpec((1, D_MODEL, TF_MOE),
                                   lambda g, j, te, tv: (te[g], 0, j * tv[g])),
                      pl.BlockSpec((1, D_MODEL, TF_MOE),
                                   lambda g, j, te, tv: (te[g], 0, j * tv[g])),
                      pl.BlockSpec((1, TF_MOE, D_MODEL),
                                   lambda g, j, te, tv: (te[g], j * tv[g], 0))],
            out_specs=pl.BlockSpec((TM_MOE, D_MODEL), row),
            scratch_shapes=[pltpu.VMEM((TM_MOE, D_MODEL), F32)]),
        out_shape=jax.ShapeDtypeStruct((n_rows, D_MODEL), F32),
        compiler_params=_params(("parallel", "arbitrary")),
        name="moe_ffn",
    )(tile_expert, tile_valid, xs, wg, wu, wd)


def _combine(pos, x, route, fgain, ys, n_first):
    t = x.shape[0]
    n_a = n_first // TM_MOE
    row = lambda i: (i, 0)
    blk = (TM_MOE, D_MODEL)
    last = t // TM_MOE - 1
    pos_blk = (1, 1, 2 * TM_MOE)
    return pl.pallas_call(
        functools.partial(_combine_kernel, n_a=n_a),
        grid=(t // TM_MOE,),
        in_specs=[pl.BlockSpec(pos_blk, lambda i: (i, 0, 0), memory_space=pltpu.SMEM),
                  pl.BlockSpec(pos_blk, lambda i: (jnp.minimum(i + 1, last), 0, 0),
                               memory_space=pltpu.SMEM),
                  pl.BlockSpec(blk, row),
                  pl.BlockSpec((TM_MOE, LANES), row),
                  pl.BlockSpec((1, D_MODEL), lambda i: (0, 0)),
                  pl.BlockSpec(memory_space=pl.ANY)],
        out_specs=[pl.BlockSpec(blk, lambda i: (jnp.minimum(i, n_a - 1), 0)),
                   pl.BlockSpec(blk, lambda i: (jnp.maximum(i - n_a, 0), 0))],
        out_shape=[jax.ShapeDtypeStruct((n_first, D_MODEL), F32),
                   jax.ShapeDtypeStruct((t - n_first, D_MODEL), F32)],
        scratch_shapes=[pltpu.VMEM((2, 2, TM_MOE, D_MODEL), F32),
                        pltpu.SemaphoreType.DMA((2,))],
        compiler_params=_params(("arbitrary",)),
        name="moe_combine",
    )(pos, pos, x, route, fgain, ys)


def kernel(x_prompt, x_sample, norm_mix, w_in, q_norm, k_norm, conv_w, w_oa, w_ob, w_o, norm_ffn,
           ffn_w_gate, ffn_w_up, ffn_w_down, router_w, moe_w_gate, moe_w_up, moe_w_down, final_norm):
    bp, sp, _ = x_prompt.shape
    bs, ss, _ = x_sample.shape
    n_prompt = bp * sp
    n_sample = bs * ss
    depth = norm_mix.shape[0]
    assert sp % TM == 0 and ss % TM == 0 and n_prompt % TM_FFN == 0 and n_sample % TM_FFN == 0
    assert depth == 2 and ffn_w_gate.shape[0] == 1 and moe_w_gate.shape[0] == 1

    t_all = n_prompt + n_sample
    xparts = (x_prompt.reshape(n_prompt, D_MODEL), x_sample.reshape(n_sample, D_MODEL))

    cos, sin = _rope_tables(max(sp, ss))
    n_prompt_tiles, tiles4, tiles2 = n_prompt // TM, sp // TM, ss // TM
    pos_map = lambda i: (jnp.where(i < n_prompt_tiles, i % tiles4, i % tiles2), 0)
    idx = jnp.arange(LANES)
    bd = jnp.where(idx[:, None] // HEAD_DIM == idx[None, :] // HEAD_DIM,
                   1.0 / HEAD_DIM, 0.0).astype(BF16)

    cast_queue = [((name, j), w[j]) for j in range(moe_w_gate.shape[0])
                  for name, w in (("gate", moe_w_gate), ("up", moe_w_up), ("down", moe_w_down))]
    cast_done = {}

    for l in range(depth):
        gain = norm_mix[l][None, :]
        w_l = w_in[l]
        w_qkv = w_l[:, :QKV_WIDTH].astype(BF16)
        w_rest = w_l[:, QKV_WIDTH:].astype(BF16)
        q, k, vt = _qkv_proj(xparts, t_all, n_prompt_tiles, gain, w_qkv,
                             _gain_rope_tables(cos, sin, q_norm[l] * Q_SCALE),
                             _gain_rope_tables(cos, sin, k_norm[l]), bd, pos_map)
        cb, u, ga, gb = _rest_proj(xparts, t_all, n_prompt_tiles, gain, w_rest)
        shift = (Q_SCALE * HEAD_DIM * jnp.max(jnp.abs(q_norm[l]))
                 * jnp.max(jnp.abs(k_norm[l]))).reshape(1)
        bound_ok = 2.0 * shift[0] < MAX_SHIFT_GAP

        cast_queue = [job for job in cast_queue if 2 * job[0][1] + 1 >= l]
        jobs_p, jobs_s = cast_queue[:1], cast_queue[1:2]
        cast_queue = cast_queue[2:]

        def attend(exact_max, tq_p, tq_s):
            def run(shift, q, k, vt, w_p, w_s):
                attn_p, done_p = _attention(shift, q, k, vt, w_p, row0=0, n_seq=bp, seq=sp,
                                            tq=tq_p, exact_max=exact_max)
                attn_s, done_s = _attention(shift, q, k, vt, w_s, row0=n_prompt, n_seq=bs,
                                            seq=ss, tq=tq_s, exact_max=exact_max)
                return (attn_p, attn_s), done_p + done_s
            return run

        attn_parts, done = lax.cond(
            bound_ok, attend(False, 256, 256), attend(True, 128, 256), shift, q, k, vt,
            [w for _, w in jobs_p], [w for _, w in jobs_s])
        cast_done.update(zip([key for key, _ in jobs_p + jobs_s], done))
        j = l // 2
        is_moe = l % 2 == 1
        rw = (jnp.pad(jnp.tile(router_w[j], (1, 2)), ((0, 0), (0, LANES - 2 * N_EXPERTS)))
              if is_moe else None)
        outs = _mix(attn_parts, cb, u, ga, gb, xparts, conv_w[l],
                    w_oa[l].astype(BF16), w_ob[l].astype(BF16), w_o[l].astype(BF16),
                    norm_ffn[l][None, :], rw, n_prompt=n_prompt, seq_p=sp, seq_s=ss)
        if not is_moe:
            x, h2 = outs
            x = _ffn(h2, x, ffn_w_gate[j].astype(BF16), ffn_w_up[j].astype(BF16),
                     ffn_w_down[j].astype(BF16))
            xparts = (x, x)
        else:
            x, h2, logits = outs
            route, counts = _router(logits)
            pos, pad, tile_expert, tile_valid, n_tiles = _route_plan(route, counts, t_all)
            xs = _dispatch(*pad, tile_valid, pos, h2)
            wg, wu, wd = [cast_done[(name, j)] if (name, j) in cast_done else w[j].astype(BF16)
                          for name, w in (("gate", moe_w_gate), ("up", moe_w_up),
                                          ("down", moe_w_down))]
            ys = _moe_ffn(tile_expert, tile_valid, xs, wg, wu, wd)
            xparts = _combine(pos, x, route, final_norm[None, :], ys, n_prompt)

    return (xparts[0].reshape(bp, sp, D_MODEL), xparts[1].reshape(bs, ss, D_MODEL))
```

```python
import functools
import math

import jax
import jax.numpy as jnp
from jax import lax
from jax.experimental import pallas as pl
from jax.experimental.pallas import tpu as pltpu

F32 = jnp.float32
BF16 = jnp.bfloat16

D_MODEL = 1024
N_HEADS = 16
N_KV_HEADS = 4
HEAD_DIM = 64
GROUP = N_HEADS // N_KV_HEADS
KV_WIDTH = N_KV_HEADS * HEAD_DIM
AXIS_DIM = HEAD_DIM // 2
ROPE_THETA = 10000.0
GRID_W = 64
D_FF = 3584
N_EXPERTS = 8
EPS = 1e-6
LANES = 128
QKV_WIDTH = D_MODEL + 2 * KV_WIDTH
REST_WIDTH = 5 * D_MODEL
Q_EXP_WIDTH = N_HEADS * LANES
KV_EXP_WIDTH = N_KV_HEADS * LANES
Q_SCALE = math.log2(math.e) / math.sqrt(HEAD_DIM)
V_ROWS = 80
KEY_CHUNK = 512
ROW_UNROLL = 8
MAX_SHIFT_GAP = 100.0

TM = 512
TM_FFN = 512
TF = 1792
TF_MOE = 1792
TR = 512
TM_MOE = 512
VMEM_LIMIT = 56 * 1024 * 1024


def _params(sem):
    return pltpu.CompilerParams(dimension_semantics=sem, vmem_limit_bytes=VMEM_LIMIT)


def _rms(x, gain):
    return x * lax.rsqrt(jnp.mean(x * x, axis=-1, keepdims=True) + EPS) * gain


def _pick(n_a, a_ref, b_ref):
    return jnp.where(pl.program_id(0) < n_a, a_ref[...], b_ref[...])


def _inproj_kernel(xa_ref, xb_ref, g_ref, w_ref, q_own_ref, q_other_ref, k_own_ref, k_other_ref,
                   bd_ref, q_ref, k_ref, vt_ref, cb_ref, u_ref, ga_ref, gb_ref, *, n_a):
    tm = xa_ref.shape[0]
    h = _rms(_pick(n_a, xa_ref, xb_ref), g_ref[...]).astype(BF16)
    p = jnp.dot(h, w_ref[:, :QKV_WIDTH], preferred_element_type=F32)
    rest = jnp.dot(h, w_ref[:, QKV_WIDTH:], preferred_element_type=F32)
    cb_ref[...] = rest[:, :D_MODEL].astype(BF16)
    u_ref[...] = (rest[:, D_MODEL:2 * D_MODEL] * rest[:, 2 * D_MODEL:3 * D_MODEL]).astype(BF16)
    ga_ref[...] = rest[:, 3 * D_MODEL:4 * D_MODEL].astype(BF16)
    gb_ref[...] = rest[:, 4 * D_MODEL:].astype(BF16)
    bd = bd_ref[...]
    lane = lax.broadcasted_iota(jnp.int32, (tm, LANES), 1)
    first16 = (lane & 31) < 16
    low_half = lane < HEAD_DIM

    def norm_rope(c, own, other):
        ms = jnp.dot((c * c).astype(BF16), bd, preferred_element_type=F32)
        partner = jnp.where(first16, pltpu.roll(c, LANES - 16, 1), pltpu.roll(c, 16, 1))
        return (c * own + partner * other) * lax.rsqrt(ms + EPS)

    is_aux = lane == HEAD_DIM
    zero = jnp.zeros((tm, LANES), F32)
    ones_aux = jnp.where(is_aux, 1.0, zero)

    q_own = q_own_ref[...]
    q_other = q_other_ref[...]
    for c in range(N_HEADS // 2):
        r = norm_rope(p[:, c * LANES:(c + 1) * LANES], q_own, q_other)
        even = jnp.where(low_half, r, zero)
        odd = jnp.where(low_half, pltpu.roll(r, HEAD_DIM, 1), zero)
        q_ref[:, (2 * c) * LANES:(2 * c + 1) * LANES] = even.astype(BF16)
        q_ref[:, (2 * c + 1) * LANES:(2 * c + 2) * LANES] = odd.astype(BF16)

    k_own = k_own_ref[...]
    k_other = k_other_ref[...]
    for c in range(KV_WIDTH // LANES):
        kn = norm_rope(p[:, D_MODEL + c * LANES:D_MODEL + (c + 1) * LANES], k_own, k_other)
        even = jnp.where(low_half, kn, ones_aux)
        odd = jnp.where(low_half, pltpu.roll(kn, HEAD_DIM, 1), ones_aux)
        k_ref[:, (2 * c) * LANES:(2 * c + 1) * LANES] = even.astype(BF16)
        k_ref[:, (2 * c + 1) * LANES:(2 * c + 2) * LANES] = odd.astype(BF16)
        vv = p[:, D_MODEL + KV_WIDTH + c * LANES:D_MODEL + KV_WIDTH + (c + 1) * LANES]
        even = jnp.where(low_half, vv, ones_aux)
        odd = jnp.where(low_half, pltpu.roll(vv, HEAD_DIM, 1), ones_aux)
        vt_ref[(2 * c) * LANES:(2 * c + 1) * LANES, :] = even.T.astype(BF16)
        vt_ref[(2 * c + 1) * LANES:(2 * c + 2) * LANES, :] = odd.T.astype(BF16)


def _attn_kernel(shift_ref, q_ref, k_ref, vt_ref, *refs, exact_max):
    n_cast = (len(refs) - 1) // 2
    o_ref = refs[n_cast]
    for src, dst in zip(refs[:n_cast], refs[n_cast + 1:]):
        dst[...] = src[...].astype(BF16)
    tq = q_ref.shape[0]
    is_aux_row = lax.broadcasted_iota(jnp.int32, (LANES, GROUP * tq), 0) == HEAD_DIM
    for j in range(N_KV_HEADS):
        qt = jnp.concatenate(
            [q_ref[:, h * LANES:(h + 1) * LANES].T for h in range(GROUP * j, GROUP * (j + 1))],
            axis=1)
        if exact_max:
            st = jnp.dot(k_ref[:, j * LANES:(j + 1) * LANES], qt,
                         preferred_element_type=F32)
            st = st - jnp.max(st, axis=0, keepdims=True)
            pt = jnp.exp2(st).astype(BF16)
            ot = jnp.dot(vt_ref[j * LANES:j * LANES + V_ROWS, :], pt,
                         preferred_element_type=F32)
        else:
            neg_shift = jnp.full(qt.shape, -shift_ref[0], F32).astype(BF16)
            qt = jnp.where(is_aux_row, neg_shift, qt)
            ot = jnp.zeros((V_ROWS, GROUP * tq), F32)
            for c in range(k_ref.shape[0] // KEY_CHUNK):
                rows = slice(c * KEY_CHUNK, (c + 1) * KEY_CHUNK)
                st = jnp.dot(k_ref[rows, j * LANES:(j + 1) * LANES], qt,
                             preferred_element_type=F32)
                pt = jnp.exp2(st).astype(BF16)
                ot = ot + jnp.dot(vt_ref[j * LANES:j * LANES + V_ROWS, rows], pt,
                                  preferred_element_type=F32)
        ot = ot[:HEAD_DIM] / ot[HEAD_DIM:HEAD_DIM + 1]
        for a in range(2):
            pair = jnp.concatenate([ot[:, (2 * a) * tq:(2 * a + 1) * tq],
                                    ot[:, (2 * a + 1) * tq:(2 * a + 2) * tq]], axis=0)
            c = 2 * j + a
            o_ref[:, c * LANES:(c + 1) * LANES] = pair.T.astype(BF16)


def _mix_kernel(attn_a_ref, attn_b_ref, cb_ref, u_ref, up_ref, un_ref, ga_ref, gb_ref,
                xa_ref, xb_ref, cw_ref, woa_ref, wob_ref, wo_ref, g_ref, *rest,
                n_prompt_tiles, tiles4, tiles2, with_router):
    if with_router:
        rw_ref, xo_ref, h2_ref, lg_ref = rest
    else:
        xo_ref, h2_ref = rest
    i = pl.program_id(0)
    tm = xa_ref.shape[0]
    is_prompt = i < n_prompt_tiles
    seq_start = jnp.where(is_prompt, i % tiles4 == 0, i % tiles2 == 0)
    seq_end = jnp.where(is_prompt, i % tiles4 == tiles4 - 1, i % tiles2 == tiles2 - 1)

    attn = _pick(n_prompt_tiles, attn_a_ref, attn_b_ref)
    ya = jnp.dot(attn, woa_ref[...], preferred_element_type=F32)

    u = u_ref[...].astype(F32)
    row = lax.broadcasted_iota(jnp.int32, u.shape, 0)
    prev_row = jnp.where(seq_start, 0.0, up_ref[7:8, :].astype(F32))
    next_row = jnp.where(seq_end, 0.0, un_ref[0:1, :].astype(F32))
    u_prev = jnp.where(row == 0, prev_row, pltpu.roll(u, 1, 0))
    u_next = jnp.where(row == tm - 1, next_row, pltpu.roll(u, tm - 1, 0))
    cw = cw_ref[...]
    conv = cw[0:1, :] * u_prev + cw[1:2, :] * u + cw[2:3, :] * u_next
    yb_in = (cb_ref[...].astype(F32) * conv).astype(BF16)
    yb = jnp.dot(yb_in, wob_ref[...], preferred_element_type=F32)

    m = (jax.nn.sigmoid(ga_ref[...].astype(F32)) * ya
         + jax.nn.sigmoid(gb_ref[...].astype(F32)) * yb)
    xn = (_pick(n_prompt_tiles, xa_ref, xb_ref)
          + jnp.dot(m.astype(BF16), wo_ref[...], preferred_element_type=F32))
    xo_ref[...] = xn
    h2 = _rms(xn, g_ref[...])
    h2_ref[...] = h2.astype(h2_ref.dtype)
    if not with_router:
        return
    hi = h2.astype(BF16)
    lo = (h2 - hi.astype(F32)).astype(BF16)
    rw = rw_ref[...]
    rhi = rw.astype(BF16)
    rlo = (rw - rhi.astype(F32)).astype(BF16)
    lane = lax.broadcasted_iota(jnp.int32, rw.shape, 1)
    hi_terms = jnp.dot(hi, jnp.where(lane < N_EXPERTS, rhi, rlo), preferred_element_type=F32)
    lg_ref[...] = (hi_terms + pltpu.roll(hi_terms, LANES - N_EXPERTS, 1)
                   + jnp.dot(lo, rhi, preferred_element_type=F32))


def _ffn_kernel(h_ref, x_ref, wg_ref, wu_ref, wd_ref, o_ref, acc_ref):
    j = pl.program_id(1)

    @pl.when(j == 0)
    def _():
        acc_ref[...] = jnp.zeros_like(acc_ref)

    h = h_ref[...]
    g = jnp.dot(h, wg_ref[...], preferred_element_type=F32)
    u = jnp.dot(h, wu_ref[...], preferred_element_type=F32)
    a = (g * jax.nn.sigmoid(g) * u).astype(BF16)
    acc_ref[...] += jnp.dot(a, wd_ref[...], preferred_element_type=F32)

    @pl.when(j == pl.num_programs(1) - 1)
    def _():
        o_ref[...] = x_ref[...] + acc_ref[...]


ROUTE_E0, ROUTE_E1, ROUTE_R0, ROUTE_R1, ROUTE_W0, ROUTE_W1 = range(6)


def _lane_pick(x, lane, k):
    return jnp.sum(jnp.where(lane == k, x, 0.0), axis=-1, keepdims=True)


def _router_kernel(lg_ref, tri_ref, route_ref, count_ref, base_ref):
    i = pl.program_id(0)

    @pl.when(i == 0)
    def _():
        base_ref[...] = jnp.zeros_like(base_ref)

    lg = lg_ref[...]
    lane = lax.broadcasted_iota(jnp.int32, lg.shape, 1)
    neg = jnp.float32(-jnp.inf)
    l1 = jnp.where(lane < N_EXPERTS, lg, neg)
    m1 = jnp.max(l1, axis=-1, keepdims=True)
    i1 = jnp.min(jnp.where(l1 == m1, lane, LANES), axis=-1, keepdims=True)
    l2 = jnp.where(lane == i1, neg, l1)
    m2 = jnp.max(l2, axis=-1, keepdims=True)
    i2 = jnp.min(jnp.where(l2 == m2, lane, LANES), axis=-1, keepdims=True)
    e = jnp.exp(m2 - m1)
    w1 = 1.0 / (1.0 + e)
    w2 = e / (1.0 + e)

    hot1 = lane == i1
    hot2 = lane == i2
    onehot = jnp.where(hot1 | hot2, 1.0, 0.0)
    base = base_ref[0:1, :]
    prefix = jnp.dot(tri_ref[...], onehot.astype(BF16), preferred_element_type=F32) + base
    r1 = jnp.sum(jnp.where(hot1, prefix, 0.0), axis=-1, keepdims=True)
    r2 = jnp.sum(jnp.where(hot2, prefix, 0.0), axis=-1, keepdims=True)
    total = base + jnp.sum(onehot, axis=0, keepdims=True)
    base_ref[...] = jnp.broadcast_to(total, base_ref.shape)
    count_ref[...] = jnp.broadcast_to(total, count_ref.shape)

    rec = jnp.where(lane == ROUTE_E0, i1.astype(F32), 0.0)
    rec = jnp.where(lane == ROUTE_E1, i2.astype(F32), rec)
    rec = jnp.where(lane == ROUTE_R0, r1, rec)
    rec = jnp.where(lane == ROUTE_R1, r2, rec)
    rec = jnp.where(lane == ROUTE_W0, w1, rec)
    rec = jnp.where(lane == ROUTE_W1, w2, rec)
    route_ref[...] = rec


def _row_copy(src, src_row, dst, dst_row, sem):
    return pltpu.make_async_copy(src.at[pl.ds(src_row, 1)], dst.at[pl.ds(dst_row, 1)], sem)


def _dispatch_kernel(pad_start_ref, pad_count_ref, tile_valid_ref, pos_ref, h_ref, xs_ref,
                     zero_ref, sem, zsem):
    tm = h_ref.shape[0]
    n_tiles = xs_ref.shape[0] // tm

    @pl.when(pl.program_id(0) == 0)
    def _():
        zero_ref[...] = jnp.zeros_like(zero_ref)
        for e in range(N_EXPERTS):
            start = pad_start_ref[e]
            count = pad_count_ref[e]

            def zero_issue(r, carry):
                _row_copy(zero_ref, 0, xs_ref, start + r, zsem).start()
                return carry

            def zero_wait(r, carry):
                _row_copy(zero_ref, 0, xs_ref, 0, zsem).wait()
                return carry

            lax.fori_loop(0, count, zero_issue, 0)
            lax.fori_loop(0, count, zero_wait, 0)
        for g in range(n_tiles - N_EXPERTS, n_tiles):
            @pl.when(tile_valid_ref[g] == 0)
            def _():
                fill = pltpu.make_async_copy(zero_ref, xs_ref.at[pl.ds(g * tm, tm)], zsem)
                fill.start()
                fill.wait()

    def issue(i, carry):
        for k in range(ROW_UNROLL):
            r = i * ROW_UNROLL + k
            _row_copy(h_ref, r, xs_ref, pos_ref[0, 0, r], sem).start()
            _row_copy(h_ref, r, xs_ref, pos_ref[0, 0, tm + r], sem).start()
        return carry

    lax.fori_loop(0, tm // ROW_UNROLL, issue, 0)
    for _ in range(2):
        pltpu.make_async_copy(h_ref, xs_ref.at[pl.ds(0, tm)], sem).wait()


def _moe_ffn_kernel(te_ref, tv_ref, x_ref, wg_ref, wu_ref, wd_ref, y_ref, acc_ref):
    del te_ref
    g_idx = pl.program_id(0)
    j = pl.program_id(1)
    valid = tv_ref[g_idx] > 0

    @pl.when(j == 0)
    def _():
        acc_ref[...] = jnp.zeros_like(acc_ref)

    @pl.when(valid)
    def _():
        h = x_ref[...].astype(BF16)
        g = jnp.dot(h, wg_ref[0], preferred_element_type=F32)
        u = jnp.dot(h, wu_ref[0], preferred_element_type=F32)
        a = (g * jax.nn.sigmoid(g) * u).astype(BF16)
        acc_ref[...] += jnp.dot(a, wd_ref[0], preferred_element_type=F32)

    @pl.when(j == pl.num_programs(1) - 1)
    def _():
        y_ref[...] = acc_ref[...]


def _combine_kernel(pos_ref, nxt_ref, x_ref, route_ref, fg_ref, y_hbm, oa_ref, ob_ref,
                    ybuf, sem, *, n_a):
    i = pl.program_id(0)
    tc = x_ref.shape[0]
    cur = i % 2

    def issue_tile(p_ref, par):
        def issue(blk, carry):
            for k in range(ROW_UNROLL):
                r = blk * ROW_UNROLL + k
                _row_copy(y_hbm, p_ref[0, 0, r], ybuf.at[par, 0], r, sem.at[par]).start()
                _row_copy(y_hbm, p_ref[0, 0, tc + r], ybuf.at[par, 1], r, sem.at[par]).start()
            return carry
        lax.fori_loop(0, tc // ROW_UNROLL, issue, 0)

    @pl.when(i == 0)
    def _():
        issue_tile(pos_ref, 0)

    @pl.when(i + 1 < pl.num_programs(0))
    def _():
        issue_tile(nxt_ref, 1 - cur)

    for slot in range(2):
        pltpu.make_async_copy(y_hbm.at[pl.ds(0, tc)], ybuf.at[cur, slot], sem.at[cur]).wait()
    route = route_ref[...]
    lane = lax.broadcasted_iota(jnp.int32, route.shape, 1)
    w0 = _lane_pick(route, lane, ROUTE_W0)
    w1 = _lane_pick(route, lane, ROUTE_W1)
    out = _rms(x_ref[...] + (w0 * ybuf[cur, 0] + w1 * ybuf[cur, 1]), fg_ref[...])
    in_a = i < n_a

    @pl.when(in_a)
    def _():
        oa_ref[...] = out

    @pl.when(jnp.logical_not(in_a))
    def _():
        ob_ref[...] = out


def _rope_tables(max_seq):
    t = jnp.arange(max_seq, dtype=jnp.int32)
    row = (t // GRID_W).astype(F32)
    col = (t % GRID_W).astype(F32)
    inv = 1.0 / (ROPE_THETA ** (jnp.arange(0, AXIS_DIM, 2, dtype=F32) / AXIS_DIM))
    ar = row[:, None] * inv[None, :]
    ac = col[:, None] * inv[None, :]
    cos64 = jnp.concatenate([jnp.cos(ar), jnp.cos(ar), jnp.cos(ac), jnp.cos(ac)], axis=-1)
    sin64 = jnp.concatenate([-jnp.sin(ar), jnp.sin(ar), -jnp.sin(ac), jnp.sin(ac)], axis=-1)
    return cos64, sin64


def _gain_rope_tables(cos64, sin64, gain):
    half = AXIS_DIM // 2
    partner_gain = gain.reshape(-1, 2, half)[:, ::-1, :].reshape(-1)
    return (jnp.tile(cos64 * gain[None, :], (1, 2)),
            jnp.tile(sin64 * partner_gain[None, :], (1, 2)))


def _two_part_specs(block, n_a, parts):
    off = 0 if parts[0] is parts[1] else n_a
    return [pl.BlockSpec(block, lambda i: (jnp.minimum(i, n_a - 1), 0)),
            pl.BlockSpec(block, lambda i: (jnp.maximum(i, n_a) - off, 0))]


def _in_proj(xparts, t, n_a, gain, w, q_tables, k_tables, bd, pos_map):
    row = lambda i: (i, 0)
    const = lambda i: (0, 0)
    table = pl.BlockSpec((TM, LANES), pos_map)
    act = pl.BlockSpec((TM, D_MODEL), row)
    act_shape = jax.ShapeDtypeStruct((t, D_MODEL), BF16)
    return pl.pallas_call(
        functools.partial(_inproj_kernel, n_a=n_a),
        grid=(t // TM,),
        in_specs=_two_part_specs((TM, D_MODEL), n_a, xparts) + [
                  pl.BlockSpec((1, D_MODEL), const),
                  pl.BlockSpec((D_MODEL, QKV_WIDTH + REST_WIDTH), const,
                               pipeline_mode=pl.Buffered(1)),
                  table, table, table, table,
                  pl.BlockSpec((LANES, LANES), const)],
        out_specs=[pl.BlockSpec((TM, Q_EXP_WIDTH), row),
                   pl.BlockSpec((TM, KV_EXP_WIDTH), row),
                   pl.BlockSpec((KV_EXP_WIDTH, TM), lambda i: (0, i)),
                   act, act, act, act],
        out_shape=[jax.ShapeDtypeStruct((t, Q_EXP_WIDTH), BF16),
                   jax.ShapeDtypeStruct((t, KV_EXP_WIDTH), BF16),
                   jax.ShapeDtypeStruct((KV_EXP_WIDTH, t), BF16),
                   act_shape, act_shape, act_shape, act_shape],
        compiler_params=_params(("parallel",)),
        name="in_proj",
    )(*xparts, gain, w, *q_tables, *k_tables, bd)


def _attention(shift, q, k, vt, casts, *, row0, n_seq, seq, tq, exact_max):
    nq = seq // tq
    q0 = row0 // tq
    s0 = row0 // seq
    n_steps = n_seq * nq
    step_blk = lambda b, i, km: (b * nq + i, 0, 0)
    views = [w.reshape(n_steps, w.shape[0] * w.shape[1] // n_steps, w.shape[2]) for w in casts]
    cast_specs = [pl.BlockSpec((1,) + v.shape[1:], step_blk) for v in views]
    outs = pl.pallas_call(
        functools.partial(_attn_kernel, exact_max=exact_max),
        grid_spec=pltpu.PrefetchScalarGridSpec(
            num_scalar_prefetch=1,
            grid=(n_seq, nq),
            in_specs=[pl.BlockSpec((tq, Q_EXP_WIDTH), lambda b, i, km: (q0 + b * nq + i, 0)),
                      pl.BlockSpec((seq, KV_EXP_WIDTH), lambda b, i, km: (s0 + b, 0)),
                      pl.BlockSpec((KV_EXP_WIDTH, seq), lambda b, i, km: (0, s0 + b))]
                     + cast_specs,
            out_specs=[pl.BlockSpec((tq, D_MODEL), lambda b, i, km: (b * nq + i, 0))]
                      + cast_specs),
        out_shape=[jax.ShapeDtypeStruct((n_seq * seq, D_MODEL), BF16)]
                  + [jax.ShapeDtypeStruct(v.shape, BF16) for v in views],
        compiler_params=_params(("parallel", "parallel")),
        name=f"attention_s{seq}" + ("_exact" if exact_max else ""),
    )(shift, q, k, vt, *views)
    return outs[0], [o.reshape(w.shape) for o, w in zip(outs[1:], casts)]


def _mix(attn_parts, cb, u, ga, gb, xparts, cw, woa, wob, wo, gain, rw, *, n_prompt, seq_p, seq_s):
    t = cb.shape[0]
    n_a = n_prompt // TM
    row = lambda i: (i, 0)
    const = lambda i: (0, 0)
    sub = TM // 8
    last = t // 8 - 1
    act = pl.BlockSpec((TM, D_MODEL), row)
    wspec = pl.BlockSpec((D_MODEL, D_MODEL), const)
    with_router = rw is not None
    kern = functools.partial(_mix_kernel, n_prompt_tiles=n_prompt // TM,
                             tiles4=seq_p // TM, tiles2=seq_s // TM, with_router=with_router)
    in_specs = (_two_part_specs((TM, D_MODEL), n_a, attn_parts) + [
                act, act,
                pl.BlockSpec((8, D_MODEL), lambda i: (jnp.maximum(i * sub - 1, 0), 0)),
                pl.BlockSpec((8, D_MODEL), lambda i: (jnp.minimum((i + 1) * sub, last), 0)),
                act, act]
                + _two_part_specs((TM, D_MODEL), n_a, xparts) + [
                pl.BlockSpec((3, D_MODEL), const),
                wspec, wspec, wspec,
                pl.BlockSpec((1, D_MODEL), const)])
    args = [*attn_parts, cb, u, u, u, ga, gb, *xparts, cw, woa, wob, wo, gain]
    out_specs = [act, act]
    out_shape = [jax.ShapeDtypeStruct((t, D_MODEL), F32),
                 jax.ShapeDtypeStruct((t, D_MODEL), F32 if with_router else BF16)]
    if with_router:
        in_specs.append(pl.BlockSpec((D_MODEL, LANES), const))
        args.append(rw)
        out_specs.append(pl.BlockSpec((TM, LANES), row))
        out_shape.append(jax.ShapeDtypeStruct((t, LANES), F32))
    return pl.pallas_call(
        kern,
        grid=(t // TM,),
        in_specs=in_specs,
        out_specs=out_specs,
        out_shape=out_shape,
        compiler_params=_params(("parallel",)),
        name="mix_proj_router" if with_router else "mix_proj",
    )(*args)


def _ffn(h, x, wg, wu, wd):
    t = x.shape[0]
    row = lambda i, j: (i, 0)
    return pl.pallas_call(
        _ffn_kernel,
        grid=(t // TM_FFN, D_FF // TF),
        in_specs=[pl.BlockSpec((TM_FFN, D_MODEL), row),
                  pl.BlockSpec((TM_FFN, D_MODEL), row),
                  pl.BlockSpec((D_MODEL, TF), lambda i, j: (0, j)),
                  pl.BlockSpec((D_MODEL, TF), lambda i, j: (0, j)),
                  pl.BlockSpec((TF, D_MODEL), lambda i, j: (j, 0))],
        out_specs=pl.BlockSpec((TM_FFN, D_MODEL), row),
        out_shape=jax.ShapeDtypeStruct((t, D_MODEL), F32),
        scratch_shapes=[pltpu.VMEM((TM_FFN, D_MODEL), F32)],
        compiler_params=_params(("parallel", "arbitrary")),
        name="ffn_dense",
    )(h, x, wg, wu, wd)


def _router(logits):
    t = logits.shape[0]
    row = lambda i: (i, 0)
    const = lambda i: (0, 0)
    r = jnp.arange(TR)
    tri = (r[None, :] < r[:, None]).astype(BF16)
    return pl.pallas_call(
        _router_kernel,
        grid=(t // TR,),
        in_specs=[pl.BlockSpec((TR, LANES), row),
                  pl.BlockSpec((TR, TR), const)],
        out_specs=[pl.BlockSpec((TR, LANES), row),
                   pl.BlockSpec((8, LANES), const)],
        out_shape=[jax.ShapeDtypeStruct((t, LANES), F32),
                   jax.ShapeDtypeStruct((8, LANES), F32)],
        scratch_shapes=[pltpu.VMEM((8, LANES), F32)],
        compiler_params=_params(("arbitrary",)),
        name="router",
    )(logits, tri)


def _route_plan(route, counts, t):
    e0 = route[:, ROUTE_E0].astype(jnp.int32)
    e1 = route[:, ROUTE_E1].astype(jnp.int32)
    r0 = route[:, ROUTE_R0].astype(jnp.int32)
    r1 = route[:, ROUTE_R1].astype(jnp.int32)
    cnt = counts[0, :N_EXPERTS].astype(jnp.int32)
    padded = ((cnt + TM_MOE - 1) // TM_MOE) * TM_MOE
    ends = jnp.cumsum(padded)
    starts = ends - padded
    experts = jnp.arange(N_EXPERTS, dtype=jnp.int32)
    pos0 = jnp.sum(jnp.where(e0[:, None] == experts[None, :], starts[None, :], 0), axis=1) + r0
    pos1 = jnp.sum(jnp.where(e1[:, None] == experts[None, :], starts[None, :], 0), axis=1) + r1
    n_tok_tiles = t // TM_MOE
    pos = jnp.concatenate([pos0.reshape(n_tok_tiles, 1, TM_MOE),
                           pos1.reshape(n_tok_tiles, 1, TM_MOE)], axis=2)
    n_tiles = 2 * t // TM_MOE + N_EXPERTS
    tile_start = jnp.arange(n_tiles, dtype=jnp.int32) * TM_MOE
    tile_valid = (tile_start < ends[-1]).astype(jnp.int32)
    tile_expert = jnp.sum((tile_start[:, None] >= ends[None, :]).astype(jnp.int32), axis=1)
    tile_expert = jnp.minimum(tile_expert, N_EXPERTS - 1)
    pad = (starts + cnt, padded - cnt)
    return pos, pad, tile_expert, tile_valid, n_tiles


def _dispatch(pad_start, pad_count, tile_valid, pos, h):
    t = h.shape[0]
    n_rows = tile_valid.shape[0] * TM_MOE
    return pl.pallas_call(
        _dispatch_kernel,
        grid_spec=pltpu.PrefetchScalarGridSpec(
            num_scalar_prefetch=3,
            grid=(t // TM_MOE,),
            in_specs=[pl.BlockSpec((1, 1, 2 * TM_MOE), lambda i, ps, pc, tv: (i, 0, 0),
                                   memory_space=pltpu.SMEM),
                      pl.BlockSpec((TM_MOE, D_MODEL), lambda i, ps, pc, tv: (i, 0))],
            out_specs=pl.BlockSpec(memory_space=pl.ANY),
            scratch_shapes=[pltpu.VMEM((TM_MOE, D_MODEL), F32),
                            pltpu.SemaphoreType.DMA(()),
                            pltpu.SemaphoreType.DMA(())]),
        out_shape=jax.ShapeDtypeStruct((n_rows, D_MODEL), F32),
        compiler_params=_params(("arbitrary",)),
        name="moe_dispatch",
    )(pad_start, pad_count, tile_valid, pos, h)


def _moe_ffn(tile_expert, tile_valid, xs, wg, wu, wd):
    n_rows = xs.shape[0]
    row = lambda g, j, te, tv: (g, 0)
    return pl.pallas_call(
        _moe_ffn_kernel,
        grid_spec=pltpu.PrefetchScalarGridSpec(
            num_scalar_prefetch=2,
            grid=(n_rows // TM_MOE, D_FF // TF_MOE),
            in_specs=[pl.BlockSpec((TM_MOE, D_MODEL), row),
                      pl.BlockSpec((1, D_MODEL, TF_MOE),
                                   lambda g, j, te, tv: (te[g], 0, j * tv[g])),
                      pl.BlockSpec((1, D_MODEL, TF_MOE),
                                   lambda g, j, te, tv: (te[g], 0, j * tv[g])),
                      pl.BlockSpec((1, TF_MOE, D_MODEL),
                                   lambda g, j, te, tv: (te[g], j * tv[g], 0))],
            out_specs=pl.BlockSpec((TM_MOE, D_MODEL), row),
            scratch_shapes=[pltpu.VMEM((TM_MOE, D_MODEL), F32)]),
        out_shape=jax.ShapeDtypeStruct((n_rows, D_MODEL), F32),
        compiler_params=_params(("parallel", "arbitrary")),
        name="moe_ffn",
    )(tile_expert, tile_valid, xs, wg, wu, wd)


def _combine(pos, x, route, fgain, ys, n_first):
    t = x.shape[0]
    n_a = n_first // TM_MOE
    row = lambda i: (i, 0)
    blk = (TM_MOE, D_MODEL)
    last = t // TM_MOE - 1
    pos_blk = (1, 1, 2 * TM_MOE)
    return pl.pallas_call(
        functools.partial(_combine_kernel, n_a=n_a),
        grid=(t // TM_MOE,),
        in_specs=[pl.BlockSpec(pos_blk, lambda i: (i, 0, 0), memory_space=pltpu.SMEM),
                  pl.BlockSpec(pos_blk, lambda i: (jnp.minimum(i + 1, last), 0, 0),
                               memory_space=pltpu.SMEM),
                  pl.BlockSpec(blk, row),
                  pl.BlockSpec((TM_MOE, LANES), row),
                  pl.BlockSpec((1, D_MODEL), lambda i: (0, 0)),
                  pl.BlockSpec(memory_space=pl.ANY)],
        out_specs=[pl.BlockSpec(blk, lambda i: (jnp.minimum(i, n_a - 1), 0)),
                   pl.BlockSpec(blk, lambda i: (jnp.maximum(i - n_a, 0), 0))],
        out_shape=[jax.ShapeDtypeStruct((n_first, D_MODEL), F32),
                   jax.ShapeDtypeStruct((t - n_first, D_MODEL), F32)],
        scratch_shapes=[pltpu.VMEM((2, 2, TM_MOE, D_MODEL), F32),
                        pltpu.SemaphoreType.DMA((2,))],
        compiler_params=_params(("arbitrary",)),
        name="moe_combine",
    )(pos, pos, x, route, fgain, ys)


def kernel(x_prompt, x_sample, norm_mix, w_in, q_norm, k_norm, conv_w, w_oa, w_ob, w_o, norm_ffn,
           ffn_w_gate, ffn_w_up, ffn_w_down, router_w, moe_w_gate, moe_w_up, moe_w_down, final_norm):
    bp, sp, _ = x_prompt.shape
    bs, ss, _ = x_sample.shape
    n_prompt = bp * sp
    n_sample = bs * ss
    depth = norm_mix.shape[0]
    assert sp % TM == 0 and ss % TM == 0 and n_prompt % TM_FFN == 0 and n_sample % TM_FFN == 0
    assert depth == 2 and ffn_w_gate.shape[0] == 1 and moe_w_gate.shape[0] == 1

    t_all = n_prompt + n_sample
    xparts = (x_prompt.reshape(n_prompt, D_MODEL), x_sample.reshape(n_sample, D_MODEL))

    cos, sin = _rope_tables(max(sp, ss))
    n_prompt_tiles, tiles4, tiles2 = n_prompt // TM, sp // TM, ss // TM
    pos_map = lambda i: (jnp.where(i < n_prompt_tiles, i % tiles4, i % tiles2), 0)
    idx = jnp.arange(LANES)
    bd = jnp.where(idx[:, None] // HEAD_DIM == idx[None, :] // HEAD_DIM,
                   1.0 / HEAD_DIM, 0.0).astype(BF16)

    cast_queue = [((name, j), w[j]) for j in range(moe_w_gate.shape[0])
                  for name, w in (("gate", moe_w_gate), ("up", moe_w_up), ("down", moe_w_down))]
    cast_done = {}

    for l in range(depth):
        gain = norm_mix[l][None, :]
        q, k, vt, cb, u, ga, gb = _in_proj(
            xparts, t_all, n_prompt_tiles, gain, w_in[l].astype(BF16),
            _gain_rope_tables(cos, sin, q_norm[l] * Q_SCALE),
            _gain_rope_tables(cos, sin, k_norm[l]), bd, pos_map)
        shift = (Q_SCALE * HEAD_DIM * jnp.max(jnp.abs(q_norm[l]))
                 * jnp.max(jnp.abs(k_norm[l]))).reshape(1)
        bound_ok = 2.0 * shift[0] < MAX_SHIFT_GAP

        cast_queue = [job for job in cast_queue if 2 * job[0][1] + 1 >= l]
        jobs_p, jobs_s = cast_queue[:1], cast_queue[1:2]
        cast_queue = cast_queue[2:]

        def attend(exact_max, tq_p, tq_s):
            def run(shift, q, k, vt, w_p, w_s):
                attn_p, done_p = _attention(shift, q, k, vt, w_p, row0=0, n_seq=bp, seq=sp,
                                            tq=tq_p, exact_max=exact_max)
                attn_s, done_s = _attention(shift, q, k, vt, w_s, row0=n_prompt, n_seq=bs,
                                            seq=ss, tq=tq_s, exact_max=exact_max)
                return (attn_p, attn_s), done_p + done_s
            return run

        attn_parts, done = lax.cond(
            bound_ok, attend(False, 256, 256), attend(True, 128, 256), shift, q, k, vt,
            [w for _, w in jobs_p], [w for _, w in jobs_s])
        cast_done.update(zip([key for key, _ in jobs_p + jobs_s], done))
        j = l // 2
        is_moe = l % 2 == 1
        rw = (jnp.pad(jnp.tile(router_w[j], (1, 2)), ((0, 0), (0, LANES - 2 * N_EXPERTS)))
              if is_moe else None)
        outs = _mix(attn_parts, cb, u, ga, gb, xparts, conv_w[l],
                    w_oa[l].astype(BF16), w_ob[l].astype(BF16), w_o[l].astype(BF16),
                    norm_ffn[l][None, :], rw, n_prompt=n_prompt, seq_p=sp, seq_s=ss)
        if not is_moe:
            x, h2 = outs
            x = _ffn(h2, x, ffn_w_gate[j].astype(BF16), ffn_w_up[j].astype(BF16),
                     ffn_w_down[j].astype(BF16))
            xparts = (x, x)
        else:
            x, h2, logits = outs
            route, counts = _router(logits)
            pos, pad, tile_expert, tile_valid, n_tiles = _route_plan(route, counts, t_all)
            xs = _dispatch(*pad, tile_valid, pos, h2)
            wg, wu, wd = [cast_done[(name, j)] if (name, j) in cast_done else w[j].astype(BF16)
                          for name, w in (("gate", moe_w_gate), ("up", moe_w_up),
                                          ("down", moe_w_down))]
            ys = _moe_ffn(tile_expert, tile_valid, xs, wg, wu, wd)
            xparts = _combine(pos, x, route, final_norm[None, :], ys, n_prompt)

    return (xparts[0].reshape(bp, sp, D_MODEL), xparts[1].reshape(bs, ss, D_MODEL))
```

```python
import functools
import math

import jax
import jax.numpy as jnp
from jax import lax
from jax.experimental import pallas as pl
from jax.experimental.pallas import tpu as pltpu

F32 = jnp.float32
BF16 = jnp.bfloat16

D_MODEL = 1024
N_HEADS = 16
N_KV_HEADS = 4
HEAD_DIM = 64
GROUP = N_HEADS // N_KV_HEADS
KV_WIDTH = N_KV_HEADS * HEAD_DIM
AXIS_DIM = HEAD_DIM // 2
ROPE_THETA = 10000.0
GRID_W = 64
D_FF = 3584
N_EXPERTS = 8
EPS = 1e-6
LANES = 128
QKV_WIDTH = D_MODEL + 2 * KV_WIDTH
REST_WIDTH = 5 * D_MODEL
Q_EXP_WIDTH = N_HEADS * LANES
KV_EXP_WIDTH = N_KV_HEADS * LANES
Q_SCALE = math.log2(math.e) / math.sqrt(HEAD_DIM)
V_ROWS = 80
BF16_SUBLANES = 16
KEY_CHUNK = 512
ROW_UNROLL = 8
MAX_SHIFT_GAP = 100.0

TM = 512
TM_FFN = 512
TF = 1792
TF_MOE = 1792
TR = 512
TM_MOE = 512
VMEM_LIMIT = 56 * 1024 * 1024


def _params(sem):
    return pltpu.CompilerParams(dimension_semantics=sem, vmem_limit_bytes=VMEM_LIMIT)


def _rms(x, gain):
    return x * lax.rsqrt(jnp.mean(x * x, axis=-1, keepdims=True) + EPS) * gain


def _pick(n_a, a_ref, b_ref):
    return jnp.where(pl.program_id(0) < n_a, a_ref[...], b_ref[...])


def _inproj_kernel(xa_ref, xb_ref, g_ref, w_ref, q_own_ref, q_other_ref, k_own_ref, k_other_ref,
                   bd_ref, q_ref, k_ref, vt_ref, cb_ref, u_ref, ga_ref, gb_ref, *, n_a):
    tm = xa_ref.shape[0]
    h = _rms(_pick(n_a, xa_ref, xb_ref), g_ref[...]).astype(BF16)
    p = jnp.dot(h, w_ref[:, :QKV_WIDTH], preferred_element_type=F32)
    rest = jnp.dot(h, w_ref[:, QKV_WIDTH:], preferred_element_type=F32)
    cb_ref[...] = rest[:, :D_MODEL].astype(BF16)
    u_ref[...] = (rest[:, D_MODEL:2 * D_MODEL] * rest[:, 2 * D_MODEL:3 * D_MODEL]).astype(BF16)
    ga_ref[...] = rest[:, 3 * D_MODEL:4 * D_MODEL].astype(BF16)
    gb_ref[...] = rest[:, 4 * D_MODEL:].astype(BF16)
    bd = bd_ref[...]
    lane = lax.broadcasted_iota(jnp.int32, (tm, LANES), 1)
    first16 = (lane & 31) < 16
    low_half = lane < HEAD_DIM

    def norm_rope(c, own, other):
        ms = jnp.dot((c * c).astype(BF16), bd, preferred_element_type=F32)
        partner = jnp.where(first16, pltpu.roll(c, LANES - 16, 1), pltpu.roll(c, 16, 1))
        return (c * own + partner * other) * lax.rsqrt(ms + EPS)

    is_aux = lane == HEAD_DIM
    zero = jnp.zeros((tm, LANES), F32)
    ones_aux = jnp.where(is_aux, 1.0, zero)

    q_own = q_own_ref[...]
    q_other = q_other_ref[...]
    for c in range(N_HEADS // 2):
        r = norm_rope(p[:, c * LANES:(c + 1) * LANES], q_own, q_other)
        even = jnp.where(low_half, r, zero)
        odd = jnp.where(low_half, pltpu.roll(r, HEAD_DIM, 1), zero)
        q_ref[:, (2 * c) * LANES:(2 * c + 1) * LANES] = even.astype(BF16)
        q_ref[:, (2 * c + 1) * LANES:(2 * c + 2) * LANES] = odd.astype(BF16)

    k_own = k_own_ref[...]
    k_other = k_other_ref[...]
    for c in range(KV_WIDTH // LANES):
        kn = norm_rope(p[:, D_MODEL + c * LANES:D_MODEL + (c + 1) * LANES], k_own, k_other)
        even = jnp.where(low_half, kn, ones_aux)
        odd = jnp.where(low_half, pltpu.roll(kn, HEAD_DIM, 1), ones_aux)
        k_ref[:, (2 * c) * LANES:(2 * c + 1) * LANES] = even.astype(BF16)
        k_ref[:, (2 * c + 1) * LANES:(2 * c + 2) * LANES] = odd.astype(BF16)
        vv = p[:, D_MODEL + KV_WIDTH + c * LANES:D_MODEL + KV_WIDTH + (c + 1) * LANES]
        even = jnp.where(low_half, vv, ones_aux)
        odd = jnp.where(low_half, pltpu.roll(vv, HEAD_DIM, 1), ones_aux)
        vt_ref[(2 * c) * LANES:(2 * c + 1) * LANES, :] = even.T.astype(BF16)
        vt_ref[(2 * c + 1) * LANES:(2 * c + 2) * LANES, :] = odd.T.astype(BF16)


def _attn_kernel(shift_ref, q_ref, k_ref, vt_ref, *refs, exact_max):
    n_cast = (len(refs) - 1) // 2
    o_ref = refs[n_cast]
    for src, dst in zip(refs[:n_cast], refs[n_cast + 1:]):
        dst[...] = src[...].astype(BF16)
    tq = q_ref.shape[0]
    is_aux_row = lax.broadcasted_iota(jnp.int32, (LANES, GROUP * tq), 0) == HEAD_DIM
    for j in range(N_KV_HEADS):
        qt = jnp.concatenate(
            [q_ref[:, h * LANES:(h + 1) * LANES].T for h in range(GROUP * j, GROUP * (j + 1))],
            axis=1)
        if exact_max:
            st = jnp.dot(k_ref[:, j * LANES:(j + 1) * LANES], qt,
                         preferred_element_type=F32)
            st = st - jnp.max(st, axis=0, keepdims=True)
            pt = jnp.exp2(st).astype(BF16)
            ot = jnp.dot(vt_ref[j * LANES:j * LANES + V_ROWS, :], pt,
                         preferred_element_type=F32)
        else:
            neg_shift = jnp.full(qt.shape, -shift_ref[0], F32).astype(BF16)
            qt = jnp.where(is_aux_row, neg_shift, qt)
            ot = jnp.zeros((V_ROWS, GROUP * tq), F32)
            for c in range(k_ref.shape[0] // KEY_CHUNK):
                rows = slice(c * KEY_CHUNK, (c + 1) * KEY_CHUNK)
                st = jnp.dot(k_ref[rows, j * LANES:(j + 1) * LANES], qt,
                             preferred_element_type=F32)
                pt = jnp.exp2(st).astype(BF16)
                ot = ot + jnp.dot(vt_ref[j * LANES:j * LANES + V_ROWS, rows], pt,
                                  preferred_element_type=F32)
        ot = ot[:HEAD_DIM] / ot[HEAD_DIM:HEAD_DIM + 1]
        for a in range(2):
            pair = jnp.concatenate([ot[:, (2 * a) * tq:(2 * a + 1) * tq],
                                    ot[:, (2 * a + 1) * tq:(2 * a + 2) * tq]], axis=0)
            c = 2 * j + a
            o_ref[:, c * LANES:(c + 1) * LANES] = pair.T.astype(BF16)


def _mix_kernel(attn_a_ref, attn_b_ref, cb_ref, u_ref, up_ref, un_ref, ga_ref, gb_ref,
                xa_ref, xb_ref, cw_ref, woa_ref, wob_ref, wo_ref, g_ref, *rest,
                n_prompt_tiles, tiles4, tiles2, with_router):
    if with_router:
        rw_ref, xo_ref, h2_ref, lg_ref = rest
    else:
        xo_ref, h2_ref = rest
    i = pl.program_id(0)
    tm = xa_ref.shape[0]
    is_prompt = i < n_prompt_tiles
    seq_start = jnp.where(is_prompt, i % tiles4 == 0, i % tiles2 == 0)
    seq_end = jnp.where(is_prompt, i % tiles4 == tiles4 - 1, i % tiles2 == tiles2 - 1)

    attn = _pick(n_prompt_tiles, attn_a_ref, attn_b_ref)
    ya = jnp.dot(attn, woa_ref[...], preferred_element_type=F32)

    u = u_ref[...].astype(F32)
    row = lax.broadcasted_iota(jnp.int32, u.shape, 0)
    prev_row = jnp.where(seq_start, 0.0, up_ref[7:8, :].astype(F32))
    next_row = jnp.where(seq_end, 0.0, un_ref[0:1, :].astype(F32))
    u_prev = jnp.where(row == 0, prev_row, pltpu.roll(u, 1, 0))
    u_next = jnp.where(row == tm - 1, next_row, pltpu.roll(u, tm - 1, 0))
    cw = cw_ref[...]
    conv = cw[0:1, :] * u_prev + cw[1:2, :] * u + cw[2:3, :] * u_next
    yb_in = (cb_ref[...].astype(F32) * conv).astype(BF16)
    yb = jnp.dot(yb_in, wob_ref[...], preferred_element_type=F32)

    m = (jax.nn.sigmoid(ga_ref[...].astype(F32)) * ya
         + jax.nn.sigmoid(gb_ref[...].astype(F32)) * yb)
    xn = (_pick(n_prompt_tiles, xa_ref, xb_ref)
          + jnp.dot(m.astype(BF16), wo_ref[...], preferred_element_type=F32))
    xo_ref[...] = xn
    h2 = _rms(xn, g_ref[...])
    h2_ref[...] = h2.astype(h2_ref.dtype)
    if not with_router:
        return
    hi = h2.astype(BF16)
    lo = (h2 - hi.astype(F32)).astype(BF16)
    rw = rw_ref[...]
    rhi = rw.astype(BF16)
    rlo = (rw - rhi.astype(F32)).astype(BF16)
    lane = lax.broadcasted_iota(jnp.int32, rw.shape, 1)
    hi_terms = jnp.dot(hi, jnp.where(lane < N_EXPERTS, rhi, rlo), preferred_element_type=F32)
    lg_ref[...] = (hi_terms + pltpu.roll(hi_terms, LANES - N_EXPERTS, 1)
                   + jnp.dot(lo, rhi, preferred_element_type=F32))


def _ffn_kernel(h_ref, x_ref, wg_ref, wu_ref, wd_ref, o_ref, acc_ref):
    j = pl.program_id(1)

    @pl.when(j == 0)
    def _():
        acc_ref[...] = jnp.zeros_like(acc_ref)

    h = h_ref[...]
    g = jnp.dot(h, wg_ref[...], preferred_element_type=F32)
    u = jnp.dot(h, wu_ref[...], preferred_element_type=F32)
    a = (g * jax.nn.sigmoid(g) * u).astype(BF16)
    acc_ref[...] += jnp.dot(a, wd_ref[...], preferred_element_type=F32)

    @pl.when(j == pl.num_programs(1) - 1)
    def _():
        o_ref[...] = x_ref[...] + acc_ref[...]


ROUTE_E0, ROUTE_E1, ROUTE_R0, ROUTE_R1, ROUTE_W0, ROUTE_W1 = range(6)


def _lane_pick(x, lane, k):
    return jnp.sum(jnp.where(lane == k, x, 0.0), axis=-1, keepdims=True)


def _router_kernel(lg_ref, tri_ref, route_ref, fields_ref, count_ref, base_ref):
    i = pl.program_id(0)

    @pl.when(i == 0)
    def _():
        base_ref[...] = jnp.zeros_like(base_ref)

    lg = lg_ref[...]
    lane = lax.broadcasted_iota(jnp.int32, lg.shape, 1)
    neg = jnp.float32(-jnp.inf)
    l1 = jnp.where(lane < N_EXPERTS, lg, neg)
    m1 = jnp.max(l1, axis=-1, keepdims=True)
    i1 = jnp.min(jnp.where(l1 == m1, lane, LANES), axis=-1, keepdims=True)
    l2 = jnp.where(lane == i1, neg, l1)
    m2 = jnp.max(l2, axis=-1, keepdims=True)
    i2 = jnp.min(jnp.where(l2 == m2, lane, LANES), axis=-1, keepdims=True)
    e = jnp.exp(m2 - m1)
    w1 = 1.0 / (1.0 + e)
    w2 = e / (1.0 + e)

    hot1 = lane == i1
    hot2 = lane == i2
    onehot = jnp.where(hot1 | hot2, 1.0, 0.0)
    base = base_ref[0:1, :]
    prefix = jnp.dot(tri_ref[...], onehot.astype(BF16), preferred_element_type=F32) + base
    r1 = jnp.sum(jnp.where(hot1, prefix, 0.0), axis=-1, keepdims=True)
    r2 = jnp.sum(jnp.where(hot2, prefix, 0.0), axis=-1, keepdims=True)
    total = base + jnp.sum(onehot, axis=0, keepdims=True)
    base_ref[...] = jnp.broadcast_to(total, base_ref.shape)
    count_ref[...] = jnp.broadcast_to(total, count_ref.shape)

    rec = jnp.where(lane == ROUTE_E0, i1.astype(F32), 0.0)
    rec = jnp.where(lane == ROUTE_E1, i2.astype(F32), rec)
    rec = jnp.where(lane == ROUTE_R0, r1, rec)
    rec = jnp.where(lane == ROUTE_R1, r2, rec)
    rec = jnp.where(lane == ROUTE_W0, w1, rec)
    rec = jnp.where(lane == ROUTE_W1, w2, rec)
    route_ref[...] = rec
    fields_ref[...] = rec.T[:fields_ref.shape[0], :]


def _row_copy(src, src_row, dst, dst_row, sem):
    return pltpu.make_async_copy(src.at[pl.ds(src_row, 1)], dst.at[pl.ds(dst_row, 1)], sem)


def _dispatch_kernel(pad_start_ref, pad_count_ref, tile_valid_ref, pos_ref, h_ref, xs_ref,
                     zero_ref, sem, zsem):
    tm = h_ref.shape[0]
    n_tiles = xs_ref.shape[0] // tm

    @pl.when(pl.program_id(0) == 0)
    def _():
        zero_ref[...] = jnp.zeros_like(zero_ref)
        for e in range(N_EXPERTS):
            start = pad_start_ref[e]
            count = pad_count_ref[e]

            def zero_issue(r, carry):
                _row_copy(zero_ref, 0, xs_ref, start + r, zsem).start()
                return carry

            def zero_wait(r, carry):
                _row_copy(zero_ref, 0, xs_ref, 0, zsem).wait()
                return carry

            lax.fori_loop(0, count, zero_issue, 0)
            lax.fori_loop(0, count, zero_wait, 0)
        for g in range(n_tiles - N_EXPERTS, n_tiles):
            @pl.when(tile_valid_ref[g] == 0)
            def _():
                fill = pltpu.make_async_copy(zero_ref, xs_ref.at[pl.ds(g * tm, tm)], zsem)
                fill.start()
                fill.wait()

    def issue(i, carry):
        for k in range(ROW_UNROLL):
            r = i * ROW_UNROLL + k
            _row_copy(h_ref, r, xs_ref, pos_ref[0, 0, r], sem).start()
            _row_copy(h_ref, r, xs_ref, pos_ref[0, 0, tm + r], sem).start()
        return carry

    lax.fori_loop(0, tm // ROW_UNROLL, issue, 0)
    for _ in range(2):
        pltpu.make_async_copy(h_ref, xs_ref.at[pl.ds(0, tm)], sem).wait()


def _moe_ffn_kernel(te_ref, tv_ref, x_ref, wg_ref, wu_ref, wd_ref, y_ref, acc_ref):
    del te_ref
    g_idx = pl.program_id(0)
    j = pl.program_id(1)
    valid = tv_ref[g_idx] > 0

    @pl.when(j == 0)
    def _():
        acc_ref[...] = jnp.zeros_like(acc_ref)

    @pl.when(valid)
    def _():
        h = x_ref[...].astype(BF16)
        g = jnp.dot(h, wg_ref[0], preferred_element_type=F32)
        u = jnp.dot(h, wu_ref[0], preferred_element_type=F32)
        a = (g * jax.nn.sigmoid(g) * u).astype(BF16)
        acc_ref[...] += jnp.dot(a, wd_ref[0], preferred_element_type=F32)

    @pl.when(j == pl.num_programs(1) - 1)
    def _():
        y_ref[...] = acc_ref[...]


def _combine_kernel(pos_ref, nxt_ref, x_ref, route_ref, fg_ref, y_hbm, oa_ref, ob_ref,
                    ybuf, sem, *, n_a):
    i = pl.program_id(0)
    tc = x_ref.shape[0]
    cur = i % 2

    def issue_tile(p_ref, par):
        def issue(blk, carry):
            for k in range(ROW_UNROLL):
                r = blk * ROW_UNROLL + k
                _row_copy(y_hbm, p_ref[0, 0, r], ybuf.at[par, 0], r, sem.at[par]).start()
                _row_copy(y_hbm, p_ref[0, 0, tc + r], ybuf.at[par, 1], r, sem.at[par]).start()
            return carry
        lax.fori_loop(0, tc // ROW_UNROLL, issue, 0)

    @pl.when(i == 0)
    def _():
        issue_tile(pos_ref, 0)

    @pl.when(i + 1 < pl.num_programs(0))
    def _():
        issue_tile(nxt_ref, 1 - cur)

    for slot in range(2):
        pltpu.make_async_copy(y_hbm.at[pl.ds(0, tc)], ybuf.at[cur, slot], sem.at[cur]).wait()
    route = route_ref[...]
    lane = lax.broadcasted_iota(jnp.int32, route.shape, 1)
    w0 = _lane_pick(route, lane, ROUTE_W0)
    w1 = _lane_pick(route, lane, ROUTE_W1)
    out = _rms(x_ref[...] + (w0 * ybuf[cur, 0] + w1 * ybuf[cur, 1]), fg_ref[...])
    in_a = i < n_a

    @pl.when(in_a)
    def _():
        oa_ref[...] = out

    @pl.when(jnp.logical_not(in_a))
    def _():
        ob_ref[...] = out


def _rope_tables(max_seq):
    t = jnp.arange(max_seq, dtype=jnp.int32)
    row = (t // GRID_W).astype(F32)
    col = (t % GRID_W).astype(F32)
    inv = 1.0 / (ROPE_THETA ** (jnp.arange(0, AXIS_DIM, 2, dtype=F32) / AXIS_DIM))
    ar = row[:, None] * inv[None, :]
    ac = col[:, None] * inv[None, :]
    cos64 = jnp.concatenate([jnp.cos(ar), jnp.cos(ar), jnp.cos(ac), jnp.cos(ac)], axis=-1)
    sin64 = jnp.concatenate([-jnp.sin(ar), jnp.sin(ar), -jnp.sin(ac), jnp.sin(ac)], axis=-1)
    return cos64, sin64


def _gain_rope_tables(cos64, sin64, gain):
    half = AXIS_DIM // 2
    partner_gain = gain.reshape(-1, 2, half)[:, ::-1, :].reshape(-1)
    return (jnp.tile(cos64 * gain[None, :], (1, 2)),
            jnp.tile(sin64 * partner_gain[None, :], (1, 2)))


def _two_part_specs(block, n_a, parts):
    off = 0 if parts[0] is parts[1] else n_a
    return [pl.BlockSpec(block, lambda i: (jnp.minimum(i, n_a - 1), 0)),
            pl.BlockSpec(block, lambda i: (jnp.maximum(i, n_a) - off, 0))]


def _in_proj(xparts, t, n_a, gain, w, q_tables, k_tables, bd, pos_map):
    row = lambda i: (i, 0)
    const = lambda i: (0, 0)
    table = pl.BlockSpec((TM, LANES), pos_map)
    act = pl.BlockSpec((TM, D_MODEL), row)
    act_shape = jax.ShapeDtypeStruct((t, D_MODEL), BF16)
    return pl.pallas_call(
        functools.partial(_inproj_kernel, n_a=n_a),
        grid=(t // TM,),
        in_specs=_two_part_specs((TM, D_MODEL), n_a, xparts) + [
                  pl.BlockSpec((1, D_MODEL), const),
                  pl.BlockSpec((D_MODEL, QKV_WIDTH + REST_WIDTH), const,
                               pipeline_mode=pl.Buffered(1)),
                  table, table, table, table,
                  pl.BlockSpec((LANES, LANES), const)],
        out_specs=[pl.BlockSpec((TM, Q_EXP_WIDTH), row),
                   pl.BlockSpec((TM, KV_EXP_WIDTH), row),
                   pl.BlockSpec((KV_EXP_WIDTH, TM), lambda i: (0, i)),
                   act, act, act, act],
        out_shape=[jax.ShapeDtypeStruct((t, Q_EXP_WIDTH), BF16),
                   jax.ShapeDtypeStruct((t, KV_EXP_WIDTH), BF16),
                   jax.ShapeDtypeStruct((KV_EXP_WIDTH, t), BF16),
                   act_shape, act_shape, act_shape, act_shape],
        compiler_params=_params(("parallel",)),
        name="in_proj",
    )(*xparts, gain, w, *q_tables, *k_tables, bd)


def _attention(shift, q, k, vt, casts, *, row0, n_seq, seq, tq, exact_max):
    nq = seq // tq
    q0 = row0 // tq
    s0 = row0 // seq
    n_steps = n_seq * nq
    step_blk = lambda b, i, km: (b * nq + i, 0, 0)
    rides = [(w.shape[0] * w.shape[1]) % (n_steps * BF16_SUBLANES) == 0 for w in casts]
    views = [w.reshape(n_steps, w.shape[0] * w.shape[1] // n_steps, w.shape[2])
             for w, ok in zip(casts, rides) if ok]
    cast_specs = [pl.BlockSpec((1,) + v.shape[1:], step_blk) for v in views]
    outs = pl.pallas_call(
        functools.partial(_attn_kernel, exact_max=exact_max),
        grid_spec=pltpu.PrefetchScalarGridSpec(
            num_scalar_prefetch=1,
            grid=(n_seq, nq),
            in_specs=[pl.BlockSpec((tq, Q_EXP_WIDTH), lambda b, i, km: (q0 + b * nq + i, 0)),
                      pl.BlockSpec((seq, KV_EXP_WIDTH), lambda b, i, km: (s0 + b, 0)),
                      pl.BlockSpec((KV_EXP_WIDTH, seq), lambda b, i, km: (0, s0 + b))]
                     + cast_specs,
            out_specs=[pl.BlockSpec((tq, D_MODEL), lambda b, i, km: (b * nq + i, 0))]
                      + cast_specs),
        out_shape=[jax.ShapeDtypeStruct((n_seq * seq, D_MODEL), BF16)]
                  + [jax.ShapeDtypeStruct(v.shape, BF16) for v in views],
        compiler_params=_params(("parallel", "parallel")),
        name=f"attention_s{seq}" + ("_exact" if exact_max else ""),
    )(shift, q, k, vt, *views)
    carried = iter(outs[1:])
    return outs[0], [next(carried).reshape(w.shape) if ok else w.astype(BF16)
                     for w, ok in zip(casts, rides)]


def _plan_casts(jobs, depth, call_weights=(2.0, 1.0)):
    plan = []
    pending = sorted(jobs, key=lambda job: job[2])
    for l in range(depth):
        budget = sum(job[1].size for job in pending) / (depth - l)
        take, used = [], 0
        for job in pending:
            if job[2] == l or used + job[1].size <= budget:
                take.append(job)
                used += job[1].size
        pending = [job for job in pending if all(job is not t for t in take)]
        calls = [[] for _ in call_weights]
        loads = [0.0] * len(call_weights)
        for job in sorted(take, key=lambda job: -job[1].size):
            c = min(range(len(calls)), key=lambda c: loads[c] / call_weights[c])
            calls[c].append(job)
            loads[c] += job[1].size
        plan.append(calls)
    return plan


def _mix(attn_parts, cb, u, ga, gb, xparts, cw, woa, wob, wo, gain, rw, *, n_prompt, seq_p, seq_s):
    t = cb.shape[0]
    n_a = n_prompt // TM
    row = lambda i: (i, 0)
    const = lambda i: (0, 0)
    sub = TM // 8
    last = t // 8 - 1
    act = pl.BlockSpec((TM, D_MODEL), row)
    wspec = pl.BlockSpec((D_MODEL, D_MODEL), const)
    with_router = rw is not None
    kern = functools.partial(_mix_kernel, n_prompt_tiles=n_prompt // TM,
                             tiles4=seq_p // TM, tiles2=seq_s // TM, with_router=with_router)
    in_specs = (_two_part_specs((TM, D_MODEL), n_a, attn_parts) + [
                act, act,
                pl.BlockSpec((8, D_MODEL), lambda i: (jnp.maximum(i * sub - 1, 0), 0)),
                pl.BlockSpec((8, D_MODEL), lambda i: (jnp.minimum((i + 1) * sub, last), 0)),
                act, act]
                + _two_part_specs((TM, D_MODEL), n_a, xparts) + [
                pl.BlockSpec((3, D_MODEL), const),
                wspec, wspec, wspec,
                pl.BlockSpec((1, D_MODEL), const)])
    args = [*attn_parts, cb, u, u, u, ga, gb, *xparts, cw, woa, wob, wo, gain]
    out_specs = [act, act]
    out_shape = [jax.ShapeDtypeStruct((t, D_MODEL), F32),
                 jax.ShapeDtypeStruct((t, D_MODEL), F32 if with_router else BF16)]
    if with_router:
        in_specs.append(pl.BlockSpec((D_MODEL, LANES), const))
        args.append(rw)
        out_specs.append(pl.BlockSpec((TM, LANES), row))
        out_shape.append(jax.ShapeDtypeStruct((t, LANES), F32))
    return pl.pallas_call(
        kern,
        grid=(t // TM,),
        in_specs=in_specs,
        out_specs=out_specs,
        out_shape=out_shape,
        compiler_params=_params(("parallel",)),
        name="mix_proj_router" if with_router else "mix_proj",
    )(*args)


def _ffn(h, x, wg, wu, wd):
    t = x.shape[0]
    row = lambda i, j: (i, 0)
    return pl.pallas_call(
        _ffn_kernel,
        grid=(t // TM_FFN, D_FF // TF),
        in_specs=[pl.BlockSpec((TM_FFN, D_MODEL), row),
                  pl.BlockSpec((TM_FFN, D_MODEL), row),
                  pl.BlockSpec((D_MODEL, TF), lambda i, j: (0, j)),
                  pl.BlockSpec((D_MODEL, TF), lambda i, j: (0, j)),
                  pl.BlockSpec((TF, D_MODEL), lambda i, j: (j, 0))],
        out_specs=pl.BlockSpec((TM_FFN, D_MODEL), row),
        out_shape=jax.ShapeDtypeStruct((t, D_MODEL), F32),
        scratch_shapes=[pltpu.VMEM((TM_FFN, D_MODEL), F32)],
        compiler_params=_params(("parallel", "arbitrary")),
        name="ffn_dense",
    )(h, x, wg, wu, wd)


def _router(logits):
    t = logits.shape[0]
    row = lambda i: (i, 0)
    const = lambda i: (0, 0)
    r = jnp.arange(TR)
    tri = (r[None, :] < r[:, None]).astype(BF16)
    return pl.pallas_call(
        _router_kernel,
        grid=(t // TR,),
        in_specs=[pl.BlockSpec((TR, LANES), row),
                  pl.BlockSpec((TR, TR), const)],
        out_specs=[pl.BlockSpec((TR, LANES), row),
                   pl.BlockSpec((8, TR), lambda i: (0, i)),
                   pl.BlockSpec((8, LANES), const)],
        out_shape=[jax.ShapeDtypeStruct((t, LANES), F32),
                   jax.ShapeDtypeStruct((8, t), F32),
                   jax.ShapeDtypeStruct((8, LANES), F32)],
        scratch_shapes=[pltpu.VMEM((8, LANES), F32)],
        compiler_params=_params(("arbitrary",)),
        name="router",
    )(logits, tri)


def _route_plan(fields, counts, t):
    e0 = fields[ROUTE_E0].astype(jnp.int32)
    e1 = fields[ROUTE_E1].astype(jnp.int32)
    r0 = fields[ROUTE_R0].astype(jnp.int32)
    r1 = fields[ROUTE_R1].astype(jnp.int32)
    cnt = counts[0, :N_EXPERTS].astype(jnp.int32)
    padded = ((cnt + TM_MOE - 1) // TM_MOE) * TM_MOE
    ends = jnp.cumsum(padded)
    starts = ends - padded
    experts = jnp.arange(N_EXPERTS, dtype=jnp.int32)
    pos0 = jnp.sum(jnp.where(e0[:, None] == experts[None, :], starts[None, :], 0), axis=1) + r0
    pos1 = jnp.sum(jnp.where(e1[:, None] == experts[None, :], starts[None, :], 0), axis=1) + r1
    n_tok_tiles = t // TM_MOE
    pos = jnp.concatenate([pos0.reshape(n_tok_tiles, 1, TM_MOE),
                           pos1.reshape(n_tok_tiles, 1, TM_MOE)], axis=2)
    n_tiles = 2 * t // TM_MOE + N_EXPERTS
    tile_start = jnp.arange(n_tiles, dtype=jnp.int32) * TM_MOE
    tile_valid = (tile_start < ends[-1]).astype(jnp.int32)
    tile_expert = jnp.sum((tile_start[:, None] >= ends[None, :]).astype(jnp.int32), axis=1)
    tile_expert = jnp.minimum(tile_expert, N_EXPERTS - 1)
    pad = (starts + cnt, padded - cnt)
    return pos, pad, tile_expert, tile_valid, n_tiles


def _dispatch(pad_start, pad_count, tile_valid, pos, h):
    t = h.shape[0]
    n_rows = tile_valid.shape[0] * TM_MOE
    return pl.pallas_call(
        _dispatch_kernel,
        grid_spec=pltpu.PrefetchScalarGridSpec(
            num_scalar_prefetch=3,
            grid=(t // TM_MOE,),
            in_specs=[pl.BlockSpec((1, 1, 2 * TM_MOE), lambda i, ps, pc, tv: (i, 0, 0),
                                   memory_space=pltpu.SMEM),
                      pl.BlockSpec((TM_MOE, D_MODEL), lambda i, ps, pc, tv: (i, 0))],
            out_specs=pl.BlockSpec(memory_space=pl.ANY),
            scratch_shapes=[pltpu.VMEM((TM_MOE, D_MODEL), F32),
                            pltpu.SemaphoreType.DMA(()),
                            pltpu.SemaphoreType.DMA(())]),
        out_shape=jax.ShapeDtypeStruct((n_rows, D_MODEL), F32),
        compiler_params=_params(("arbitrary",)),
        name="moe_dispatch",
    )(pad_start, pad_count, tile_valid, pos, h)


def _moe_ffn(tile_expert, tile_valid, xs, wg, wu, wd):
    n_rows = xs.shape[0]
    row = lambda g, j, te, tv: (g, 0)
    return pl.pallas_call(
        _moe_ffn_kernel,
        grid_spec=pltpu.PrefetchScalarGridSpec(
            num_scalar_prefetch=2,
            grid=(n_rows // TM_MOE, D_FF // TF_MOE),
            in_specs=[pl.BlockSpec((TM_MOE, D_MODEL), row),
                      pl.BlockSpec((1, D_MODEL, TF_MOE),
                                   lambda g, j, te, tv: (te[g], 0, j * tv[g])),
                      pl.BlockSpec((1, D_MODEL, TF_MOE),
                                   lambda g, j, te, tv: (te[g], 0, j * tv[g])),
                      pl.BlockSpec((1, TF_MOE, D_MODEL),
                                   lambda g, j, te, tv: (te[g], j * tv[g], 0))],
            out_specs=pl.BlockSpec((TM_MOE, D_MODEL), row),
            scratch_shapes=[pltpu.VMEM((TM_MOE, D_MODEL), F32)]),
        out_shape=jax.ShapeDtypeStruct((n_rows, D_MODEL), F32),
        compiler_params=_params(("parallel", "arbitrary")),
        name="moe_ffn",
    )(tile_expert, tile_valid, xs, wg, wu, wd)


def _combine(pos, x, route, fgain, ys, n_first):
    t = x.shape[0]
    n_a = n_first // TM_MOE
    row = lambda i: (i, 0)
    blk = (TM_MOE, D_MODEL)
    last = t // TM_MOE - 1
    pos_blk = (1, 1, 2 * TM_MOE)
    return pl.pallas_call(
        functools.partial(_combine_kernel, n_a=n_a),
        grid=(t // TM_MOE,),
        in_specs=[pl.BlockSpec(pos_blk, lambda i: (i, 0, 0), memory_space=pltpu.SMEM),
                  pl.BlockSpec(pos_blk, lambda i: (jnp.minimum(i + 1, last), 0, 0),
                               memory_space=pltpu.SMEM),
                  pl.BlockSpec(blk, row),
                  pl.BlockSpec((TM_MOE, LANES), row),
                  pl.BlockSpec((1, D_MODEL), lambda i: (0, 0)),
                  pl.BlockSpec(memory_space=pl.ANY)],
        out_specs=[pl.BlockSpec(blk, lambda i: (jnp.minimum(i, n_a - 1), 0)),
                   pl.BlockSpec(blk, lambda i: (jnp.maximum(i - n_a, 0), 0))],
        out_shape=[jax.ShapeDtypeStruct((n_first, D_MODEL), F32),
                   jax.ShapeDtypeStruct((t - n_first, D_MODEL), F32)],
        scratch_shapes=[pltpu.VMEM((2, 2, TM_MOE, D_MODEL), F32),
                        pltpu.SemaphoreType.DMA((2,))],
        compiler_params=_params(("arbitrary",)),
        name="moe_combine",
    )(pos, pos, x, route, fgain, ys)


def kernel(x_prompt, x_sample, norm_mix, w_in, q_norm, k_norm, conv_w, w_oa, w_ob, w_o, norm_ffn,
           ffn_w_gate, ffn_w_up, ffn_w_down, router_w, moe_w_gate, moe_w_up, moe_w_down, final_norm):
    bp, sp, _ = x_prompt.shape
    bs, ss, _ = x_sample.shape
    n_prompt = bp * sp
    n_sample = bs * ss
    depth = norm_mix.shape[0]
    assert sp % TM == 0 and ss % TM == 0 and n_prompt % TM_FFN == 0 and n_sample % TM_FFN == 0
    assert depth == 2 and ffn_w_gate.shape[0] == 1 and moe_w_gate.shape[0] == 1

    t_all = n_prompt + n_sample
    xparts = (x_prompt.reshape(n_prompt, D_MODEL), x_sample.reshape(n_sample, D_MODEL))

    cos, sin = _rope_tables(max(sp, ss))
    n_prompt_tiles, tiles4, tiles2 = n_prompt // TM, sp // TM, ss // TM
    pos_map = lambda i: (jnp.where(i < n_prompt_tiles, i % tiles4, i % tiles2), 0)
    idx = jnp.arange(LANES)
    bd = jnp.where(idx[:, None] // HEAD_DIM == idx[None, :] // HEAD_DIM,
                   1.0 / HEAD_DIM, 0.0).astype(BF16)

    jobs = []
    for l in range(depth):
        if l > 0:
            jobs.append((("w_in", l), w_in[l][None], l - 1))
        jobs += [((name, l), w[l][None], l)
                 for name, w in (("w_oa", w_oa), ("w_ob", w_ob), ("w_o", w_o))]
        named = ((("ffn_gate", ffn_w_gate), ("ffn_up", ffn_w_up), ("ffn_down", ffn_w_down))
                 if l % 2 == 0 else
                 (("moe_gate", moe_w_gate), ("moe_up", moe_w_up), ("moe_down", moe_w_down)))
        jobs += [((name, l), w[l // 2] if l % 2 else w[l // 2][None], l) for name, w in named]
    cast_plan = _plan_casts(jobs, depth)
    bf16_w = {("w_in", 0): w_in[0].astype(BF16)}

    for l in range(depth):
        gain = norm_mix[l][None, :]
        q, k, vt, cb, u, ga, gb = _in_proj(
            xparts, t_all, n_prompt_tiles, gain, bf16_w[("w_in", l)],
            _gain_rope_tables(cos, sin, q_norm[l] * Q_SCALE),
            _gain_rope_tables(cos, sin, k_norm[l]), bd, pos_map)
        shift = (Q_SCALE * HEAD_DIM * jnp.max(jnp.abs(q_norm[l]))
                 * jnp.max(jnp.abs(k_norm[l]))).reshape(1)
        bound_ok = 2.0 * shift[0] < MAX_SHIFT_GAP

        jobs_p, jobs_s = cast_plan[l]

        def attend(exact_max, tq_p, tq_s):
            def run(shift, q, k, vt, w_p, w_s):
                attn_p, done_p = _attention(shift, q, k, vt, w_p, row0=0, n_seq=bp, seq=sp,
                                            tq=tq_p, exact_max=exact_max)
                attn_s, done_s = _attention(shift, q, k, vt, w_s, row0=n_prompt, n_seq=bs,
                                            seq=ss, tq=tq_s, exact_max=exact_max)
                return (attn_p, attn_s), done_p + done_s
            return run

        attn_parts, done = lax.cond(
            bound_ok, attend(False, 256, 256), attend(True, 128, 256), shift, q, k, vt,
            [job[1] for job in jobs_p], [job[1] for job in jobs_s])
        for job, w16 in zip(jobs_p + jobs_s, done):
            bf16_w[job[0]] = w16 if job[0][0].startswith("moe") else w16[0]
        j = l // 2
        is_moe = l % 2 == 1
        rw = (jnp.pad(jnp.tile(router_w[j], (1, 2)), ((0, 0), (0, LANES - 2 * N_EXPERTS)))
              if is_moe else None)
        outs = _mix(attn_parts, cb, u, ga, gb, xparts, conv_w[l],
                    bf16_w[("w_oa", l)], bf16_w[("w_ob", l)], bf16_w[("w_o", l)],
                    norm_ffn[l][None, :], rw, n_prompt=n_prompt, seq_p=sp, seq_s=ss)
        if not is_moe:
            x, h2 = outs
            x = _ffn(h2, x, bf16_w[("ffn_gate", l)], bf16_w[("ffn_up", l)],
                     bf16_w[("ffn_down", l)])
            xparts = (x, x)
        else:
            x, h2, logits = outs
            route, fields, counts = _router(logits)
            pos, pad, tile_expert, tile_valid, n_tiles = _route_plan(fields, counts, t_all)
            xs = _dispatch(*pad, tile_valid, pos, h2)
            ys = _moe_ffn(tile_expert, tile_valid, xs, bf16_w[("moe_gate", l)],
                          bf16_w[("moe_up", l)], bf16_w[("moe_down", l)])
            xparts = _combine(pos, x, route, final_norm[None, :], ys, n_prompt)

    return (xparts[0].reshape(bp, sp, D_MODEL), xparts[1].reshape(bs, ss, D_MODEL))
```

```python
import functools
import math

import jax
import jax.numpy as jnp
from jax import lax
from jax.experimental import pallas as pl
from jax.experimental.pallas import tpu as pltpu

F32 = jnp.float32
BF16 = jnp.bfloat16

D_MODEL = 1024
N_HEADS = 16
N_KV_HEADS = 4
HEAD_DIM = 64
GROUP = N_HEADS // N_KV_HEADS
KV_WIDTH = N_KV_HEADS * HEAD_DIM
AXIS_DIM = HEAD_DIM // 2
ROPE_THETA = 10000.0
GRID_W = 64
D_FF = 3584
N_EXPERTS = 8
EPS = 1e-6
LANES = 128
QKV_WIDTH = D_MODEL + 2 * KV_WIDTH
REST_WIDTH = 5 * D_MODEL
Q_EXP_WIDTH = N_HEADS * LANES
KV_EXP_WIDTH = N_KV_HEADS * LANES
Q_SCALE = math.log2(math.e) / math.sqrt(HEAD_DIM)
SUBLANES = 8
BF16_SUBLANES = 16
ROT_HALF = AXIS_DIM // 2
V_ROWS = HEAD_DIM + BF16_SUBLANES
KEY_CHUNK = 512
ROW_UNROLL = SUBLANES
MAX_SHIFT_GAP = 100.0

TM = 512
TM_FFN = 512
TF = 3584
TF_MOE = 3584
TR = 512
TM_MOE = 512
VMEM_LIMIT = 56 * 1024 * 1024


def _params(sem):
    return pltpu.CompilerParams(dimension_semantics=sem, vmem_limit_bytes=VMEM_LIMIT)


def _rms(x, gain):
    return x * lax.rsqrt(jnp.mean(x * x, axis=-1, keepdims=True) + EPS) * gain


def _pick(n_a, a_ref, b_ref):
    return jnp.where(pl.program_id(0) < n_a, a_ref[...], b_ref[...])


def _inproj_kernel(xa_ref, xb_ref, g_ref, w_ref, q_own_ref, q_other_ref, k_own_ref, k_other_ref,
                   bd_ref, q_ref, k_ref, vt_ref, cb_ref, u_ref, ga_ref, gb_ref, *, n_a):
    tm = xa_ref.shape[0]
    h = _rms(_pick(n_a, xa_ref, xb_ref), g_ref[...]).astype(BF16)
    p = jnp.dot(h, w_ref[:, :QKV_WIDTH], preferred_element_type=F32)
    rest = jnp.dot(h, w_ref[:, QKV_WIDTH:], preferred_element_type=F32)
    cb_ref[...] = rest[:, :D_MODEL].astype(BF16)
    u_ref[...] = (rest[:, D_MODEL:2 * D_MODEL] * rest[:, 2 * D_MODEL:3 * D_MODEL]).astype(BF16)
    ga_ref[...] = rest[:, 3 * D_MODEL:4 * D_MODEL].astype(BF16)
    gb_ref[...] = rest[:, 4 * D_MODEL:].astype(BF16)
    bd = bd_ref[...]
    lane = lax.broadcasted_iota(jnp.int32, (tm, LANES), 1)
    first_half = (lane & (AXIS_DIM - 1)) < ROT_HALF
    low_half = lane < HEAD_DIM

    def norm_rope(c, own, other):
        ms = jnp.dot((c * c).astype(BF16), bd, preferred_element_type=F32)
        partner = jnp.where(first_half, pltpu.roll(c, LANES - ROT_HALF, 1),
                            pltpu.roll(c, ROT_HALF, 1))
        return (c * own + partner * other) * lax.rsqrt(ms + EPS)

    is_aux = lane == HEAD_DIM
    zero = jnp.zeros((tm, LANES), F32)
    ones_aux = jnp.where(is_aux, 1.0, zero)

    q_own = q_own_ref[...]
    q_other = q_other_ref[...]
    for c in range(N_HEADS // 2):
        r = norm_rope(p[:, c * LANES:(c + 1) * LANES], q_own, q_other)
        even = jnp.where(low_half, r, zero)
        odd = jnp.where(low_half, pltpu.roll(r, HEAD_DIM, 1), zero)
        q_ref[:, (2 * c) * LANES:(2 * c + 1) * LANES] = even.astype(BF16)
        q_ref[:, (2 * c + 1) * LANES:(2 * c + 2) * LANES] = odd.astype(BF16)

    k_own = k_own_ref[...]
    k_other = k_other_ref[...]
    for c in range(KV_WIDTH // LANES):
        kn = norm_rope(p[:, D_MODEL + c * LANES:D_MODEL + (c + 1) * LANES], k_own, k_other)
        even = jnp.where(low_half, kn, ones_aux)
        odd = jnp.where(low_half, pltpu.roll(kn, HEAD_DIM, 1), ones_aux)
        k_ref[:, (2 * c) * LANES:(2 * c + 1) * LANES] = even.astype(BF16)
        k_ref[:, (2 * c + 1) * LANES:(2 * c + 2) * LANES] = odd.astype(BF16)
        vv = p[:, D_MODEL + KV_WIDTH + c * LANES:D_MODEL + KV_WIDTH + (c + 1) * LANES]
        even = jnp.where(low_half, vv, ones_aux)
        odd = jnp.where(low_half, pltpu.roll(vv, HEAD_DIM, 1), ones_aux)
        vt_ref[(2 * c) * LANES:(2 * c + 1) * LANES, :] = even.T.astype(BF16)
        vt_ref[(2 * c + 1) * LANES:(2 * c + 2) * LANES, :] = odd.T.astype(BF16)


def _attn_kernel(shift_ref, q_ref, k_ref, vt_ref, *refs, exact_max):
    n_cast = (len(refs) - 1) // 2
    o_ref = refs[n_cast]
    for src, dst in zip(refs[:n_cast], refs[n_cast + 1:]):
        dst[...] = src[...].astype(BF16)
    tq = q_ref.shape[0]
    is_aux_row = lax.broadcasted_iota(jnp.int32, (LANES, GROUP * tq), 0) == HEAD_DIM
    for j in range(N_KV_HEADS):
        qt = jnp.concatenate(
            [q_ref[:, h * LANES:(h + 1) * LANES].T for h in range(GROUP * j, GROUP * (j + 1))],
            axis=1)
        if exact_max:
            st = jnp.dot(k_ref[:, j * LANES:(j + 1) * LANES], qt,
                         preferred_element_type=F32)
            st = st - jnp.max(st, axis=0, keepdims=True)
            pt = jnp.exp2(st).astype(BF16)
            ot = jnp.dot(vt_ref[j * LANES:j * LANES + V_ROWS, :], pt,
                         preferred_element_type=F32)
        else:
            neg_shift = jnp.full(qt.shape, -shift_ref[0], F32).astype(BF16)
            qt = jnp.where(is_aux_row, neg_shift, qt)
            ot = jnp.zeros((V_ROWS, GROUP * tq), F32)
            for c in range(k_ref.shape[0] // KEY_CHUNK):
                rows = slice(c * KEY_CHUNK, (c + 1) * KEY_CHUNK)
                st = jnp.dot(k_ref[rows, j * LANES:(j + 1) * LANES], qt,
                             preferred_element_type=F32)
                pt = jnp.exp2(st).astype(BF16)
                ot = ot + jnp.dot(vt_ref[j * LANES:j * LANES + V_ROWS, rows], pt,
                                  preferred_element_type=F32)
        ot = ot[:HEAD_DIM] / ot[HEAD_DIM:HEAD_DIM + 1]
        for a in range(2):
            pair = jnp.concatenate([ot[:, (2 * a) * tq:(2 * a + 1) * tq],
                                    ot[:, (2 * a + 1) * tq:(2 * a + 2) * tq]], axis=0)
            c = 2 * j + a
            o_ref[:, c * LANES:(c + 1) * LANES] = pair.T.astype(BF16)


def _mix_kernel(attn_a_ref, attn_b_ref, cb_ref, u_ref, up_ref, un_ref, ga_ref, gb_ref,
                xa_ref, xb_ref, cw_ref, woa_ref, wob_ref, wo_ref, g_ref, *rest,
                n_prompt_tiles, tiles4, tiles2, with_router):
    if with_router:
        rw_ref, xo_ref, h2_ref, lg_ref = rest
    else:
        xo_ref, h2_ref = rest
    i = pl.program_id(0)
    tm = xa_ref.shape[0]
    is_prompt = i < n_prompt_tiles
    seq_start = jnp.where(is_prompt, i % tiles4 == 0, i % tiles2 == 0)
    seq_end = jnp.where(is_prompt, i % tiles4 == tiles4 - 1, i % tiles2 == tiles2 - 1)

    attn = _pick(n_prompt_tiles, attn_a_ref, attn_b_ref)
    ya = jnp.dot(attn, woa_ref[...], preferred_element_type=F32)

    u = u_ref[...].astype(F32)
    row = lax.broadcasted_iota(jnp.int32, u.shape, 0)
    prev_row = jnp.where(seq_start, 0.0, up_ref[SUBLANES - 1:SUBLANES, :].astype(F32))
    next_row = jnp.where(seq_end, 0.0, un_ref[0:1, :].astype(F32))
    u_prev = jnp.where(row == 0, prev_row, pltpu.roll(u, 1, 0))
    u_next = jnp.where(row == tm - 1, next_row, pltpu.roll(u, tm - 1, 0))
    cw = cw_ref[...]
    conv = cw[0:1, :] * u_prev + cw[1:2, :] * u + cw[2:3, :] * u_next
    yb_in = (cb_ref[...].astype(F32) * conv).astype(BF16)
    yb = jnp.dot(yb_in, wob_ref[...], preferred_element_type=F32)

    m = (jax.nn.sigmoid(ga_ref[...].astype(F32)) * ya
         + jax.nn.sigmoid(gb_ref[...].astype(F32)) * yb)
    xn = (_pick(n_prompt_tiles, xa_ref, xb_ref)
          + jnp.dot(m.astype(BF16), wo_ref[...], preferred_element_type=F32))
    xo_ref[...] = xn
    h2 = _rms(xn, g_ref[...])
    h2_ref[...] = h2.astype(h2_ref.dtype)
    if not with_router:
        return
    hi = h2.astype(BF16)
    lo = (h2 - hi.astype(F32)).astype(BF16)
    rw = rw_ref[...]
    rhi = rw.astype(BF16)
    rlo = (rw - rhi.astype(F32)).astype(BF16)
    lane = lax.broadcasted_iota(jnp.int32, rw.shape, 1)
    hi_terms = jnp.dot(hi, jnp.where(lane < N_EXPERTS, rhi, rlo), preferred_element_type=F32)
    lg_ref[...] = (hi_terms + pltpu.roll(hi_terms, LANES - N_EXPERTS, 1)
                   + jnp.dot(lo, rhi, preferred_element_type=F32))


def _ffn_kernel(h_ref, x_ref, wg_ref, wu_ref, wd_ref, o_ref, acc_ref):
    j = pl.program_id(1)

    @pl.when(j == 0)
    def _():
        acc_ref[...] = jnp.zeros_like(acc_ref)

    h = h_ref[...]
    g = jnp.dot(h, wg_ref[...], preferred_element_type=F32)
    u = jnp.dot(h, wu_ref[...], preferred_element_type=F32)
    a = (g * jax.nn.sigmoid(g) * u).astype(BF16)
    acc_ref[...] += jnp.dot(a, wd_ref[...], preferred_element_type=F32)

    @pl.when(j == pl.num_programs(1) - 1)
    def _():
        o_ref[...] = x_ref[...] + acc_ref[...]


ROUTE_E0, ROUTE_E1, ROUTE_R0, ROUTE_R1, ROUTE_W0, ROUTE_W1 = range(6)


def _lane_pick(x, lane, k):
    return jnp.sum(jnp.where(lane == k, x, 0.0), axis=-1, keepdims=True)


def _router_kernel(lg_ref, tri_ref, route_ref, fields_ref, count_ref, base_ref):
    i = pl.program_id(0)

    @pl.when(i == 0)
    def _():
        base_ref[...] = jnp.zeros_like(base_ref)

    lg = lg_ref[...]
    lane = lax.broadcasted_iota(jnp.int32, lg.shape, 1)
    neg = jnp.float32(-jnp.inf)
    l1 = jnp.where(lane < N_EXPERTS, lg, neg)
    m1 = jnp.max(l1, axis=-1, keepdims=True)
    i1 = jnp.min(jnp.where(l1 == m1, lane, LANES), axis=-1, keepdims=True)
    l2 = jnp.where(lane == i1, neg, l1)
    m2 = jnp.max(l2, axis=-1, keepdims=True)
    i2 = jnp.min(jnp.where(l2 == m2, lane, LANES), axis=-1, keepdims=True)
    e = jnp.exp(m2 - m1)
    w1 = 1.0 / (1.0 + e)
    w2 = e / (1.0 + e)

    hot1 = lane == i1
    hot2 = lane == i2
    onehot = jnp.where(hot1 | hot2, 1.0, 0.0)
    base = base_ref[0:1, :]
    prefix = jnp.dot(tri_ref[...], onehot.astype(BF16), preferred_element_type=F32) + base
    r1 = jnp.sum(jnp.where(hot1, prefix, 0.0), axis=-1, keepdims=True)
    r2 = jnp.sum(jnp.where(hot2, prefix, 0.0), axis=-1, keepdims=True)
    total = base + jnp.sum(onehot, axis=0, keepdims=True)
    base_ref[...] = jnp.broadcast_to(total, base_ref.shape)
    count_ref[...] = jnp.broadcast_to(total, count_ref.shape)

    rec = jnp.where(lane == ROUTE_E0, i1.astype(F32), 0.0)
    rec = jnp.where(lane == ROUTE_E1, i2.astype(F32), rec)
    rec = jnp.where(lane == ROUTE_R0, r1, rec)
    rec = jnp.where(lane == ROUTE_R1, r2, rec)
    rec = jnp.where(lane == ROUTE_W0, w1, rec)
    rec = jnp.where(lane == ROUTE_W1, w2, rec)
    route_ref[...] = rec
    fields_ref[...] = rec.T[:fields_ref.shape[0], :]


def _row_copy(src, src_row, dst, dst_row, sem):
    return pltpu.make_async_copy(src.at[pl.ds(src_row, 1)], dst.at[pl.ds(dst_row, 1)], sem)


def _dispatch_kernel(pad_start_ref, pad_count_ref, tile_valid_ref, pos_ref, h_ref, xs_ref,
                     zero_ref, sem, zsem):
    tm = h_ref.shape[0]
    n_tiles = xs_ref.shape[0] // tm

    @pl.when(pl.program_id(0) == 0)
    def _():
        zero_ref[...] = jnp.zeros_like(zero_ref)
        for e in range(N_EXPERTS):
            start = pad_start_ref[e]
            count = pad_count_ref[e]

            def zero_issue(r, carry):
                _row_copy(zero_ref, 0, xs_ref, start + r, zsem).start()
                return carry

            def zero_wait(r, carry):
                _row_copy(zero_ref, 0, xs_ref, 0, zsem).wait()
                return carry

            lax.fori_loop(0, count, zero_issue, 0)
            lax.fori_loop(0, count, zero_wait, 0)
        for g in range(n_tiles - N_EXPERTS, n_tiles):
            @pl.when(tile_valid_ref[g] == 0)
            def _():
                fill = pltpu.make_async_copy(zero_ref, xs_ref.at[pl.ds(g * tm, tm)], zsem)
                fill.start()
                fill.wait()

    def issue(i, carry):
        for k in range(ROW_UNROLL):
            r = i * ROW_UNROLL + k
            _row_copy(h_ref, r, xs_ref, pos_ref[0, 0, r], sem).start()
            _row_copy(h_ref, r, xs_ref, pos_ref[0, 0, tm + r], sem).start()
        return carry

    lax.fori_loop(0, tm // ROW_UNROLL, issue, 0)
    for _ in range(2):
        pltpu.make_async_copy(h_ref, xs_ref.at[pl.ds(0, tm)], sem).wait()


def _moe_ffn_kernel(te_ref, tv_ref, x_ref, wg_ref, wu_ref, wd_ref, y_ref, acc_ref):
    del te_ref
    g_idx = pl.program_id(0)
    j = pl.program_id(1)
    valid = tv_ref[g_idx] > 0

    @pl.when(j == 0)
    def _():
        acc_ref[...] = jnp.zeros_like(acc_ref)

    @pl.when(valid)
    def _():
        h = x_ref[...].astype(BF16)
        g = jnp.dot(h, wg_ref[0], preferred_element_type=F32)
        u = jnp.dot(h, wu_ref[0], preferred_element_type=F32)
        a = (g * jax.nn.sigmoid(g) * u).astype(BF16)
        acc_ref[...] += jnp.dot(a, wd_ref[0], preferred_element_type=F32)

    @pl.when(j == pl.num_programs(1) - 1)
    def _():
        y_ref[...] = acc_ref[...]


def _combine_kernel(pos_ref, nxt_ref, x_ref, route_ref, fg_ref, y_hbm, oa_ref, ob_ref,
                    ybuf, sem, *, n_a):
    i = pl.program_id(0)
    tc = x_ref.shape[0]
    cur = i % 2

    def issue_tile(p_ref, par):
        def issue(blk, carry):
            for k in range(ROW_UNROLL):
                r = blk * ROW_UNROLL + k
                _row_copy(y_hbm, p_ref[0, 0, r], ybuf.at[par, 0], r, sem.at[par]).start()
                _row_copy(y_hbm, p_ref[0, 0, tc + r], ybuf.at[par, 1], r, sem.at[par]).start()
            return carry
        lax.fori_loop(0, tc // ROW_UNROLL, issue, 0)

    @pl.when(i == 0)
    def _():
        issue_tile(pos_ref, 0)

    @pl.when(i + 1 < pl.num_programs(0))
    def _():
        issue_tile(nxt_ref, 1 - cur)

    for slot in range(2):
        pltpu.make_async_copy(y_hbm.at[pl.ds(0, tc)], ybuf.at[cur, slot], sem.at[cur]).wait()
    route = route_ref[...]
    lane = lax.broadcasted_iota(jnp.int32, route.shape, 1)
    w0 = _lane_pick(route, lane, ROUTE_W0)
    w1 = _lane_pick(route, lane, ROUTE_W1)
    out = _rms(x_ref[...] + (w0 * ybuf[cur, 0] + w1 * ybuf[cur, 1]), fg_ref[...])
    in_a = i < n_a

    @pl.when(in_a)
    def _():
        oa_ref[...] = out

    @pl.when(jnp.logical_not(in_a))
    def _():
        ob_ref[...] = out


def _rope_tables(max_seq):
    t = jnp.arange(max_seq, dtype=jnp.int32)
    row = (t // GRID_W).astype(F32)
    col = (t % GRID_W).astype(F32)
    inv = 1.0 / (ROPE_THETA ** (jnp.arange(0, AXIS_DIM, 2, dtype=F32) / AXIS_DIM))
    ar = row[:, None] * inv[None, :]
    ac = col[:, None] * inv[None, :]
    cos64 = jnp.concatenate([jnp.cos(ar), jnp.cos(ar), jnp.cos(ac), jnp.cos(ac)], axis=-1)
    sin64 = jnp.concatenate([-jnp.sin(ar), jnp.sin(ar), -jnp.sin(ac), jnp.sin(ac)], axis=-1)
    return cos64, sin64


def _gain_rope_tables(cos64, sin64, gain):
    half = AXIS_DIM // 2
    partner_gain = gain.reshape(-1, 2, half)[:, ::-1, :].reshape(-1)
    return (jnp.tile(cos64 * gain[None, :], (1, 2)),
            jnp.tile(sin64 * partner_gain[None, :], (1, 2)))


def _two_part_specs(block, n_a, parts):
    off = 0 if parts[0] is parts[1] else n_a
    return [pl.BlockSpec(block, lambda i: (jnp.minimum(i, n_a - 1), 0)),
            pl.BlockSpec(block, lambda i: (jnp.maximum(i, n_a) - off, 0))]


def _in_proj(xparts, t, n_a, gain, w, q_tables, k_tables, bd, pos_map):
    row = lambda i: (i, 0)
    const = lambda i: (0, 0)
    table = pl.BlockSpec((TM, LANES), pos_map)
    act = pl.BlockSpec((TM, D_MODEL), row)
    act_shape = jax.ShapeDtypeStruct((t, D_MODEL), BF16)
    return pl.pallas_call(
        functools.partial(_inproj_kernel, n_a=n_a),
        grid=(t // TM,),
        in_specs=_two_part_specs((TM, D_MODEL), n_a, xparts) + [
                  pl.BlockSpec((1, D_MODEL), const),
                  pl.BlockSpec((D_MODEL, QKV_WIDTH + REST_WIDTH), const,
                               pipeline_mode=pl.Buffered(1)),
                  table, table, table, table,
                  pl.BlockSpec((LANES, LANES), const)],
        out_specs=[pl.BlockSpec((TM, Q_EXP_WIDTH), row),
                   pl.BlockSpec((TM, KV_EXP_WIDTH), row),
                   pl.BlockSpec((KV_EXP_WIDTH, TM), lambda i: (0, i)),
                   act, act, act, act],
        out_shape=[jax.ShapeDtypeStruct((t, Q_EXP_WIDTH), BF16),
                   jax.ShapeDtypeStruct((t, KV_EXP_WIDTH), BF16),
                   jax.ShapeDtypeStruct((KV_EXP_WIDTH, t), BF16),
                   act_shape, act_shape, act_shape, act_shape],
        compiler_params=_params(("parallel",)),
        name="in_proj",
    )(*xparts, gain, w, *q_tables, *k_tables, bd)


def _attention(shift, q, k, vt, casts, *, row0, n_seq, seq, tq, exact_max):
    nq = seq // tq
    q0 = row0 // tq
    s0 = row0 // seq
    n_steps = n_seq * nq
    step_blk = lambda b, i, km: (b * nq + i, 0, 0)
    rides = [(w.shape[0] * w.shape[1]) % (n_steps * BF16_SUBLANES) == 0 for w in casts]
    views = [w.reshape(n_steps, w.shape[0] * w.shape[1] // n_steps, w.shape[2])
             for w, ok in zip(casts, rides) if ok]
    cast_specs = [pl.BlockSpec((1,) + v.shape[1:], step_blk) for v in views]
    outs = pl.pallas_call(
        functools.partial(_attn_kernel, exact_max=exact_max),
        grid_spec=pltpu.PrefetchScalarGridSpec(
            num_scalar_prefetch=1,
            grid=(n_seq, nq),
            in_specs=[pl.BlockSpec((tq, Q_EXP_WIDTH), lambda b, i, km: (q0 + b * nq + i, 0)),
                      pl.BlockSpec((seq, KV_EXP_WIDTH), lambda b, i, km: (s0 + b, 0)),
                      pl.BlockSpec((KV_EXP_WIDTH, seq), lambda b, i, km: (0, s0 + b))]
                     + cast_specs,
            out_specs=[pl.BlockSpec((tq, D_MODEL), lambda b, i, km: (b * nq + i, 0))]
                      + cast_specs),
        out_shape=[jax.ShapeDtypeStruct((n_seq * seq, D_MODEL), BF16)]
                  + [jax.ShapeDtypeStruct(v.shape, BF16) for v in views],
        compiler_params=_params(("parallel", "parallel")),
        name=f"attention_s{seq}" + ("_exact" if exact_max else ""),
    )(shift, q, k, vt, *views)
    carried = iter(outs[1:])
    return outs[0], [next(carried).reshape(w.shape) if ok else w.astype(BF16)
                     for w, ok in zip(casts, rides)]


def _plan_casts(jobs, depth, call_weights=(2.0, 1.0)):
    plan = []
    pending = sorted(jobs, key=lambda job: job[2])
    for l in range(depth):
        budget = sum(job[1].size for job in pending) / (depth - l)
        take, used = [], 0
        for job in pending:
            if job[2] == l or used + job[1].size <= budget:
                take.append(job)
                used += job[1].size
        pending = [job for job in pending if all(job is not t for t in take)]
        calls = [[] for _ in call_weights]
        loads = [0.0] * len(call_weights)
        for job in sorted(take, key=lambda job: -job[1].size):
            c = min(range(len(calls)), key=lambda c: loads[c] / call_weights[c])
            calls[c].append(job)
            loads[c] += job[1].size
        plan.append(calls)
    return plan


def _mix(attn_parts, cb, u, ga, gb, xparts, cw, woa, wob, wo, gain, rw, *, n_prompt, seq_p, seq_s):
    t = cb.shape[0]
    n_a = n_prompt // TM
    row = lambda i: (i, 0)
    const = lambda i: (0, 0)
    sub = TM // SUBLANES
    last = t // SUBLANES - 1
    act = pl.BlockSpec((TM, D_MODEL), row)
    wspec = pl.BlockSpec((D_MODEL, D_MODEL), const)
    with_router = rw is not None
    kern = functools.partial(_mix_kernel, n_prompt_tiles=n_prompt // TM,
                             tiles4=seq_p // TM, tiles2=seq_s // TM, with_router=with_router)
    in_specs = (_two_part_specs((TM, D_MODEL), n_a, attn_parts) + [
                act, act,
                pl.BlockSpec((SUBLANES, D_MODEL), lambda i: (jnp.maximum(i * sub - 1, 0), 0)),
                pl.BlockSpec((SUBLANES, D_MODEL),
                             lambda i: (jnp.minimum((i + 1) * sub, last), 0)),
                act, act]
                + _two_part_specs((TM, D_MODEL), n_a, xparts) + [
                pl.BlockSpec((3, D_MODEL), const),
                wspec, wspec, wspec,
                pl.BlockSpec((1, D_MODEL), const)])
    args = [*attn_parts, cb, u, u, u, ga, gb, *xparts, cw, woa, wob, wo, gain]
    out_specs = [act, act]
    out_shape = [jax.ShapeDtypeStruct((t, D_MODEL), F32),
                 jax.ShapeDtypeStruct((t, D_MODEL), F32 if with_router else BF16)]
    if with_router:
        in_specs.append(pl.BlockSpec((D_MODEL, LANES), const))
        args.append(rw)
        out_specs.append(pl.BlockSpec((TM, LANES), row))
        out_shape.append(jax.ShapeDtypeStruct((t, LANES), F32))
    return pl.pallas_call(
        kern,
        grid=(t // TM,),
        in_specs=in_specs,
        out_specs=out_specs,
        out_shape=out_shape,
        compiler_params=_params(("parallel",)),
        name="mix_proj_router" if with_router else "mix_proj",
    )(*args)


def _ffn(h, x, wg, wu, wd):
    t = x.shape[0]
    row = lambda i, j: (i, 0)
    return pl.pallas_call(
        _ffn_kernel,
        grid=(t // TM_FFN, D_FF // TF),
        in_specs=[pl.BlockSpec((TM_FFN, D_MODEL), row),
                  pl.BlockSpec((TM_FFN, D_MODEL), row),
                  pl.BlockSpec((D_MODEL, TF), lambda i, j: (0, j), pipeline_mode=pl.Buffered(1)),
                  pl.BlockSpec((D_MODEL, TF), lambda i, j: (0, j), pipeline_mode=pl.Buffered(1)),
                  pl.BlockSpec((TF, D_MODEL), lambda i, j: (j, 0), pipeline_mode=pl.Buffered(1))],
        out_specs=pl.BlockSpec((TM_FFN, D_MODEL), row),
        out_shape=jax.ShapeDtypeStruct((t, D_MODEL), F32),
        scratch_shapes=[pltpu.VMEM((TM_FFN, D_MODEL), F32)],
        compiler_params=_params(("parallel", "arbitrary")),
        name="ffn_dense",
    )(h, x, wg, wu, wd)


def _router(logits):
    t = logits.shape[0]
    row = lambda i: (i, 0)
    const = lambda i: (0, 0)
    r = jnp.arange(TR)
    tri = (r[None, :] < r[:, None]).astype(BF16)
    return pl.pallas_call(
        _router_kernel,
        grid=(t // TR,),
        in_specs=[pl.BlockSpec((TR, LANES), row),
                  pl.BlockSpec((TR, TR), const)],
        out_specs=[pl.BlockSpec((TR, LANES), row),
                   pl.BlockSpec((SUBLANES, TR), lambda i: (0, i)),
                   pl.BlockSpec((SUBLANES, LANES), const)],
        out_shape=[jax.ShapeDtypeStruct((t, LANES), F32),
                   jax.ShapeDtypeStruct((SUBLANES, t), F32),
                   jax.ShapeDtypeStruct((SUBLANES, LANES), F32)],
        scratch_shapes=[pltpu.VMEM((SUBLANES, LANES), F32)],
        compiler_params=_params(("arbitrary",)),
        name="router",
    )(logits, tri)


def _route_plan(fields, counts, t):
    e0 = fields[ROUTE_E0].astype(jnp.int32)
    e1 = fields[ROUTE_E1].astype(jnp.int32)
    r0 = fields[ROUTE_R0].astype(jnp.int32)
    r1 = fields[ROUTE_R1].astype(jnp.int32)
    cnt = counts[0, :N_EXPERTS].astype(jnp.int32)
    padded = ((cnt + TM_MOE - 1) // TM_MOE) * TM_MOE
    ends = jnp.cumsum(padded)
    starts = ends - padded
    experts = jnp.arange(N_EXPERTS, dtype=jnp.int32)
    pos0 = jnp.sum(jnp.where(e0[:, None] == experts[None, :], starts[None, :], 0), axis=1) + r0
    pos1 = jnp.sum(jnp.where(e1[:, None] == experts[None, :], starts[None, :], 0), axis=1) + r1
    n_tok_tiles = t // TM_MOE
    pos = jnp.concatenate([pos0.reshape(n_tok_tiles, 1, TM_MOE),
                           pos1.reshape(n_tok_tiles, 1, TM_MOE)], axis=2)
    n_tiles = 2 * t // TM_MOE + N_EXPERTS
    tile_start = jnp.arange(n_tiles, dtype=jnp.int32) * TM_MOE
    tile_valid = (tile_start < ends[-1]).astype(jnp.int32)
    tile_expert = jnp.sum((tile_start[:, None] >= ends[None, :]).astype(jnp.int32), axis=1)
    tile_expert = jnp.minimum(tile_expert, N_EXPERTS - 1)
    pad = (starts + cnt, padded - cnt)
    return pos, pad, tile_expert, tile_valid


def _dispatch(pad_start, pad_count, tile_valid, pos, h):
    t = h.shape[0]
    n_rows = tile_valid.shape[0] * TM_MOE
    return pl.pallas_call(
        _dispatch_kernel,
        grid_spec=pltpu.PrefetchScalarGridSpec(
            num_scalar_prefetch=3,
            grid=(t // TM_MOE,),
            in_specs=[pl.BlockSpec((1, 1, 2 * TM_MOE), lambda i, ps, pc, tv: (i, 0, 0),
                                   memory_space=pltpu.SMEM),
                      pl.BlockSpec((TM_MOE, D_MODEL), lambda i, ps, pc, tv: (i, 0))],
            out_specs=pl.BlockSpec(memory_space=pl.ANY),
            scratch_shapes=[pltpu.VMEM((TM_MOE, D_MODEL), F32),
                            pltpu.SemaphoreType.DMA(()),
                            pltpu.SemaphoreType.DMA(())]),
        out_shape=jax.ShapeDtypeStruct((n_rows, D_MODEL), F32),
        compiler_params=_params(("arbitrary",)),
        name="moe_dispatch",
    )(pad_start, pad_count, tile_valid, pos, h)


def _moe_ffn(tile_expert, tile_valid, xs, wg, wu, wd):
    n_rows = xs.shape[0]
    row = lambda g, j, te, tv: (g, 0)
    return pl.pallas_call(
        _moe_ffn_kernel,
        grid_spec=pltpu.PrefetchScalarGridSpec(
            num_scalar_prefetch=2,
            grid=(n_rows // TM_MOE, D_FF // TF_MOE),
            in_specs=[pl.BlockSpec((TM_MOE, D_MODEL), row),
                      pl.BlockSpec((1, D_MODEL, TF_MOE),
                                   lambda g, j, te, tv: (te[g], 0, j * tv[g]),
                                   pipeline_mode=pl.Buffered(1)),
                      pl.BlockSpec((1, D_MODEL, TF_MOE),
                                   lambda g, j, te, tv: (te[g], 0, j * tv[g]),
                                   pipeline_mode=pl.Buffered(1)),
                      pl.BlockSpec((1, TF_MOE, D_MODEL),
                                   lambda g, j, te, tv: (te[g], j * tv[g], 0),
                                   pipeline_mode=pl.Buffered(1))],
            out_specs=pl.BlockSpec((TM_MOE, D_MODEL), row),
            scratch_shapes=[pltpu.VMEM((TM_MOE, D_MODEL), F32)]),
        out_shape=jax.ShapeDtypeStruct((n_rows, D_MODEL), F32),
        compiler_params=_params(("parallel", "arbitrary")),
        name="moe_ffn",
    )(tile_expert, tile_valid, xs, wg, wu, wd)


def _combine(pos, x, route, fgain, ys, n_first):
    t = x.shape[0]
    n_a = n_first // TM_MOE
    row = lambda i: (i, 0)
    blk = (TM_MOE, D_MODEL)
    last = t // TM_MOE - 1
    pos_blk = (1, 1, 2 * TM_MOE)
    return pl.pallas_call(
        functools.partial(_combine_kernel, n_a=n_a),
        grid=(t // TM_MOE,),
        in_specs=[pl.BlockSpec(pos_blk, lambda i: (i, 0, 0), memory_space=pltpu.SMEM),
                  pl.BlockSpec(pos_blk, lambda i: (jnp.minimum(i + 1, last), 0, 0),
                               memory_space=pltpu.SMEM),
                  pl.BlockSpec(blk, row),
                  pl.BlockSpec((TM_MOE, LANES), row),
                  pl.BlockSpec((1, D_MODEL), lambda i: (0, 0)),
                  pl.BlockSpec(memory_space=pl.ANY)],
        out_specs=[pl.BlockSpec(blk, lambda i: (jnp.minimum(i, n_a - 1), 0)),
                   pl.BlockSpec(blk, lambda i: (jnp.maximum(i - n_a, 0), 0))],
        out_shape=[jax.ShapeDtypeStruct((n_first, D_MODEL), F32),
                   jax.ShapeDtypeStruct((t - n_first, D_MODEL), F32)],
        scratch_shapes=[pltpu.VMEM((2, 2, TM_MOE, D_MODEL), F32),
                        pltpu.SemaphoreType.DMA((2,))],
        compiler_params=_params(("arbitrary",)),
        name="moe_combine",
    )(pos, pos, x, route, fgain, ys)


def kernel(x_prompt, x_sample, norm_mix, w_in, q_norm, k_norm, conv_w, w_oa, w_ob, w_o, norm_ffn,
           ffn_w_gate, ffn_w_up, ffn_w_down, router_w, moe_w_gate, moe_w_up, moe_w_down, final_norm):
    bp, sp, _ = x_prompt.shape
    bs, ss, _ = x_sample.shape
    n_prompt = bp * sp
    n_sample = bs * ss
    depth = norm_mix.shape[0]
    assert sp % TM == 0 and ss % TM == 0 and TM == TM_FFN == TM_MOE
    assert depth == 2 and ffn_w_gate.shape[0] == moe_w_gate.shape[0] == 1

    t_all = n_prompt + n_sample
    xparts = (x_prompt.reshape(n_prompt, D_MODEL), x_sample.reshape(n_sample, D_MODEL))

    cos, sin = _rope_tables(max(sp, ss))
    n_prompt_tiles, tiles4, tiles2 = n_prompt // TM, sp // TM, ss // TM
    pos_map = lambda i: (jnp.where(i < n_prompt_tiles, i % tiles4, i % tiles2), 0)
    idx = jnp.arange(LANES)
    bd = jnp.where(idx[:, None] // HEAD_DIM == idx[None, :] // HEAD_DIM,
                   1.0 / HEAD_DIM, 0.0).astype(BF16)

    jobs = []
    for l in range(depth):
        if l > 0:
            jobs.append((("w_in", l), w_in[l][None], l - 1))
        jobs += [((name, l), w[l][None], l)
                 for name, w in (("w_oa", w_oa), ("w_ob", w_ob), ("w_o", w_o))]
        named = ((("ffn_gate", ffn_w_gate), ("ffn_up", ffn_w_up), ("ffn_down", ffn_w_down))
                 if l % 2 == 0 else
                 (("moe_gate", moe_w_gate), ("moe_up", moe_w_up), ("moe_down", moe_w_down)))
        jobs += [((name, l), w[l // 2] if l % 2 else w[l // 2][None], l) for name, w in named]
    cast_plan = _plan_casts(jobs, depth)
    bf16_w = {("w_in", 0): w_in[0].astype(BF16)}

    for l in range(depth):
        gain = norm_mix[l][None, :]
        q, k, vt, cb, u, ga, gb = _in_proj(
            xparts, t_all, n_prompt_tiles, gain, bf16_w[("w_in", l)],
            _gain_rope_tables(cos, sin, q_norm[l] * Q_SCALE),
            _gain_rope_tables(cos, sin, k_norm[l]), bd, pos_map)
        shift = (Q_SCALE * HEAD_DIM * jnp.max(jnp.abs(q_norm[l]))
                 * jnp.max(jnp.abs(k_norm[l]))).reshape(1)
        bound_ok = 2.0 * shift[0] < MAX_SHIFT_GAP

        jobs_p, jobs_s = cast_plan[l]

        def attend(exact_max, tq_p, tq_s):
            def run(shift, q, k, vt, w_p, w_s):
                attn_p, done_p = _attention(shift, q, k, vt, w_p, row0=0, n_seq=bp, seq=sp,
                                            tq=tq_p, exact_max=exact_max)
                attn_s, done_s = _attention(shift, q, k, vt, w_s, row0=n_prompt, n_seq=bs,
                                            seq=ss, tq=tq_s, exact_max=exact_max)
                return (attn_p, attn_s), done_p + done_s
            return run

        attn_parts, done = lax.cond(
            bound_ok, attend(False, 256, 256), attend(True, 128, 256), shift, q, k, vt,
            [job[1] for job in jobs_p], [job[1] for job in jobs_s])
        for job, w16 in zip(jobs_p + jobs_s, done):
            bf16_w[job[0]] = w16 if job[0][0].startswith("moe") else w16[0]
        j = l // 2
        is_moe = l % 2 == 1
        rw = (jnp.pad(jnp.tile(router_w[j], (1, 2)), ((0, 0), (0, LANES - 2 * N_EXPERTS)))
              if is_moe else None)
        outs = _mix(attn_parts, cb, u, ga, gb, xparts, conv_w[l],
                    bf16_w[("w_oa", l)], bf16_w[("w_ob", l)], bf16_w[("w_o", l)],
                    norm_ffn[l][None, :], rw, n_prompt=n_prompt, seq_p=sp, seq_s=ss)
        if not is_moe:
            x, h2 = outs
            x = _ffn(h2, x, bf16_w[("ffn_gate", l)], bf16_w[("ffn_up", l)],
                     bf16_w[("ffn_down", l)])
            xparts = (x, x)
        else:
            x, h2, logits = outs
            route, fields, counts = _router(logits)
            pos, pad, tile_expert, tile_valid = _route_plan(fields, counts, t_all)
            xs = _dispatch(*pad, tile_valid, pos, h2)
            ys = _moe_ffn(tile_expert, tile_valid, xs, bf16_w[("moe_gate", l)],
                          bf16_w[("moe_up", l)], bf16_w[("moe_down", l)])
            xparts = _combine(pos, x, route, final_norm[None, :], ys, n_prompt)

    return (xparts[0].reshape(bp, sp, D_MODEL), xparts[1].reshape(bs, ss, D_MODEL))
```

```python
import functools
import math

import jax
import jax.numpy as jnp
from jax import lax
from jax.experimental import pallas as pl
from jax.experimental.pallas import tpu as pltpu

F32 = jnp.float32
BF16 = jnp.bfloat16

D_MODEL = 1024
N_HEADS = 16
N_KV_HEADS = 4
HEAD_DIM = 64
GROUP = N_HEADS // N_KV_HEADS
KV_WIDTH = N_KV_HEADS * HEAD_DIM
AXIS_DIM = HEAD_DIM // 2
ROPE_THETA = 10000.0
GRID_W = 64
D_FF = 3584
N_EXPERTS = 8
EPS = 1e-6
LANES = 128
QKV_WIDTH = D_MODEL + 2 * KV_WIDTH
REST_WIDTH = 5 * D_MODEL
Q_EXP_WIDTH = N_HEADS * LANES
KV_EXP_WIDTH = N_KV_HEADS * LANES
Q_SCALE = math.log2(math.e) / math.sqrt(HEAD_DIM)
SUBLANES = 8
BF16_SUBLANES = 16
ROT_HALF = AXIS_DIM // 2
V_ROWS = HEAD_DIM + BF16_SUBLANES
KEY_CHUNK = 512
ROW_UNROLL = SUBLANES
MAX_SHIFT_GAP = 100.0

TM = 512
TM_FFN = 512
TF = 3584
TF_MOE = 1792
TR = 512
TM_MOE = 512
VMEM_LIMIT = 56 * 1024 * 1024


def _params(sem):
    return pltpu.CompilerParams(dimension_semantics=sem, vmem_limit_bytes=VMEM_LIMIT)


def _rms(x, gain):
    return x * lax.rsqrt(jnp.mean(x * x, axis=-1, keepdims=True) + EPS) * gain


def _pick(n_a, a_ref, b_ref):
    return jnp.where(pl.program_id(0) < n_a, a_ref[...], b_ref[...])


def _inproj_kernel(xa_ref, xb_ref, g_ref, w_ref, q_own_ref, q_other_ref, k_own_ref, k_other_ref,
                   bd_ref, q_ref, k_ref, vt_ref, cb_ref, u_ref, ga_ref, gb_ref, *, n_a):
    tm = xa_ref.shape[0]
    h = _rms(_pick(n_a, xa_ref, xb_ref), g_ref[...]).astype(BF16)
    p = jnp.dot(h, w_ref[:, :QKV_WIDTH], preferred_element_type=F32)
    rest = jnp.dot(h, w_ref[:, QKV_WIDTH:], preferred_element_type=F32)
    cb_ref[...] = rest[:, :D_MODEL].astype(BF16)
    u_ref[...] = (rest[:, D_MODEL:2 * D_MODEL] * rest[:, 2 * D_MODEL:3 * D_MODEL]).astype(BF16)
    ga_ref[...] = rest[:, 3 * D_MODEL:4 * D_MODEL].astype(BF16)
    gb_ref[...] = rest[:, 4 * D_MODEL:].astype(BF16)
    bd = bd_ref[...]
    lane = lax.broadcasted_iota(jnp.int32, (tm, LANES), 1)
    first_half = (lane & (AXIS_DIM - 1)) < ROT_HALF
    low_half = lane < HEAD_DIM

    def norm_rope(c, own, other):
        ms = jnp.dot((c * c).astype(BF16), bd, preferred_element_type=F32)
        partner = jnp.where(first_half, pltpu.roll(c, LANES - ROT_HALF, 1),
                            pltpu.roll(c, ROT_HALF, 1))
        return (c * own + partner * other) * lax.rsqrt(ms + EPS)

    is_aux = lane == HEAD_DIM
    zero = jnp.zeros((tm, LANES), F32)
    ones_aux = jnp.where(is_aux, 1.0, zero)

    q_own = q_own_ref[...]
    q_other = q_other_ref[...]
    for c in range(N_HEADS // 2):
        r = norm_rope(p[:, c * LANES:(c + 1) * LANES], q_own, q_other)
        even = jnp.where(low_half, r, zero)
        odd = jnp.where(low_half, pltpu.roll(r, HEAD_DIM, 1), zero)
        q_ref[:, (2 * c) * LANES:(2 * c + 1) * LANES] = even.astype(BF16)
        q_ref[:, (2 * c + 1) * LANES:(2 * c + 2) * LANES] = odd.astype(BF16)

    k_own = k_own_ref[...]
    k_other = k_other_ref[...]
    for c in range(KV_WIDTH // LANES):
        kn = norm_rope(p[:, D_MODEL + c * LANES:D_MODEL + (c + 1) * LANES], k_own, k_other)
        even = jnp.where(low_half, kn, ones_aux)
        odd = jnp.where(low_half, pltpu.roll(kn, HEAD_DIM, 1), ones_aux)
        k_ref[:, (2 * c) * LANES:(2 * c + 1) * LANES] = even.astype(BF16)
        k_ref[:, (2 * c + 1) * LANES:(2 * c + 2) * LANES] = odd.astype(BF16)
        vv = p[:, D_MODEL + KV_WIDTH + c * LANES:D_MODEL + KV_WIDTH + (c + 1) * LANES]
        even = jnp.where(low_half, vv, ones_aux)
        odd = jnp.where(low_half, pltpu.roll(vv, HEAD_DIM, 1), ones_aux)
        vt_ref[(2 * c) * LANES:(2 * c + 1) * LANES, :] = even.T.astype(BF16)
        vt_ref[(2 * c + 1) * LANES:(2 * c + 2) * LANES, :] = odd.T.astype(BF16)


def _attn_kernel(shift_ref, q_ref, k_ref, vt_ref, *refs, exact_max):
    n_cast = (len(refs) - 1) // 2
    o_ref = refs[n_cast]
    for src, dst in zip(refs[:n_cast], refs[n_cast + 1:]):
        dst[...] = src[...].astype(BF16)
    tq = q_ref.shape[0]
    is_aux_row = lax.broadcasted_iota(jnp.int32, (LANES, GROUP * tq), 0) == HEAD_DIM
    for j in range(N_KV_HEADS):
        qt = jnp.concatenate(
            [q_ref[:, h * LANES:(h + 1) * LANES].T for h in range(GROUP * j, GROUP * (j + 1))],
            axis=1)
        if exact_max:
            st = jnp.dot(k_ref[:, j * LANES:(j + 1) * LANES], qt,
                         preferred_element_type=F32)
            st = st - jnp.max(st, axis=0, keepdims=True)
            pt = jnp.exp2(st).astype(BF16)
            ot = jnp.dot(vt_ref[j * LANES:j * LANES + V_ROWS, :], pt,
                         preferred_element_type=F32)
        else:
            neg_shift = jnp.full(qt.shape, -shift_ref[0], F32).astype(BF16)
            qt = jnp.where(is_aux_row, neg_shift, qt)
            ot = jnp.zeros((V_ROWS, GROUP * tq), F32)
            for c in range(k_ref.shape[0] // KEY_CHUNK):
                rows = slice(c * KEY_CHUNK, (c + 1) * KEY_CHUNK)
                st = jnp.dot(k_ref[rows, j * LANES:(j + 1) * LANES], qt,
                             preferred_element_type=F32)
                pt = jnp.exp2(st).astype(BF16)
                ot = ot + jnp.dot(vt_ref[j * LANES:j * LANES + V_ROWS, rows], pt,
                                  preferred_element_type=F32)
        ot = ot[:HEAD_DIM] / ot[HEAD_DIM:HEAD_DIM + 1]
        for a in range(2):
            pair = jnp.concatenate([ot[:, (2 * a) * tq:(2 * a + 1) * tq],
                                    ot[:, (2 * a + 1) * tq:(2 * a + 2) * tq]], axis=0)
            c = 2 * j + a
            o_ref[:, c * LANES:(c + 1) * LANES] = pair.T.astype(BF16)


def _mix_kernel(attn_a_ref, attn_b_ref, cb_ref, u_ref, up_ref, un_ref, ga_ref, gb_ref,
                xa_ref, xb_ref, cw_ref, woa_ref, wob_ref, wo_ref, g_ref, *rest,
                n_prompt_tiles, tiles4, tiles2, with_router):
    if with_router:
        rw_ref, xo_ref, h2_ref, lg_ref = rest
    else:
        xo_ref, h2_ref = rest
    i = pl.program_id(0)
    tm = xa_ref.shape[0]
    is_prompt = i < n_prompt_tiles
    seq_start = jnp.where(is_prompt, i % tiles4 == 0, i % tiles2 == 0)
    seq_end = jnp.where(is_prompt, i % tiles4 == tiles4 - 1, i % tiles2 == tiles2 - 1)

    attn = _pick(n_prompt_tiles, attn_a_ref, attn_b_ref)
    ya = jnp.dot(attn, woa_ref[...], preferred_element_type=F32)

    u = u_ref[...].astype(F32)
    row = lax.broadcasted_iota(jnp.int32, u.shape, 0)
    prev_row = jnp.where(seq_start, 0.0, up_ref[SUBLANES - 1:SUBLANES, :].astype(F32))
    next_row = jnp.where(seq_end, 0.0, un_ref[0:1, :].astype(F32))
    u_prev = jnp.where(row == 0, prev_row, pltpu.roll(u, 1, 0))
    u_next = jnp.where(row == tm - 1, next_row, pltpu.roll(u, tm - 1, 0))
    cw = cw_ref[...]
    conv = cw[0:1, :] * u_prev + cw[1:2, :] * u + cw[2:3, :] * u_next
    yb_in = (cb_ref[...].astype(F32) * conv).astype(BF16)
    yb = jnp.dot(yb_in, wob_ref[...], preferred_element_type=F32)

    m = (jax.nn.sigmoid(ga_ref[...].astype(F32)) * ya
         + jax.nn.sigmoid(gb_ref[...].astype(F32)) * yb)
    xn = (_pick(n_prompt_tiles, xa_ref, xb_ref)
          + jnp.dot(m.astype(BF16), wo_ref[...], preferred_element_type=F32))
    xo_ref[...] = xn
    h2 = _rms(xn, g_ref[...])
    h2_ref[...] = h2.astype(h2_ref.dtype)
    if not with_router:
        return
    hi = h2.astype(BF16)
    lo = (h2 - hi.astype(F32)).astype(BF16)
    rw = rw_ref[...]
    rhi = rw.astype(BF16)
    rlo = (rw - rhi.astype(F32)).astype(BF16)
    lane = lax.broadcasted_iota(jnp.int32, rw.shape, 1)
    hi_terms = jnp.dot(hi, jnp.where(lane < N_EXPERTS, rhi, rlo), preferred_element_type=F32)
    lg_ref[...] = (hi_terms + pltpu.roll(hi_terms, LANES - N_EXPERTS, 1)
                   + jnp.dot(lo, rhi, preferred_element_type=F32))


def _ffn_kernel(h_ref, x_ref, wg_ref, wu_ref, wd_ref, o_ref, acc_ref):
    j = pl.program_id(1)

    @pl.when(j == 0)
    def _():
        acc_ref[...] = jnp.zeros_like(acc_ref)

    h = h_ref[...]
    g = jnp.dot(h, wg_ref[...], preferred_element_type=F32)
    u = jnp.dot(h, wu_ref[...], preferred_element_type=F32)
    a = (g * jax.nn.sigmoid(g) * u).astype(BF16)
    acc_ref[...] += jnp.dot(a, wd_ref[...], preferred_element_type=F32)

    @pl.when(j == pl.num_programs(1) - 1)
    def _():
        o_ref[...] = x_ref[...] + acc_ref[...]


ROUTE_E0, ROUTE_E1, ROUTE_R0, ROUTE_R1, ROUTE_W0, ROUTE_W1 = range(6)


def _lane_pick(x, lane, k):
    return jnp.sum(jnp.where(lane == k, x, 0.0), axis=-1, keepdims=True)


def _router_kernel(lg_ref, tri_ref, route_ref, fields_ref, count_ref, base_ref):
    i = pl.program_id(0)

    @pl.when(i == 0)
    def _():
        base_ref[...] = jnp.zeros_like(base_ref)

    lg = lg_ref[...]
    lane = lax.broadcasted_iota(jnp.int32, lg.shape, 1)
    neg = jnp.float32(-jnp.inf)
    l1 = jnp.where(lane < N_EXPERTS, lg, neg)
    m1 = jnp.max(l1, axis=-1, keepdims=True)
    i1 = jnp.min(jnp.where(l1 == m1, lane, LANES), axis=-1, keepdims=True)
    l2 = jnp.where(lane == i1, neg, l1)
    m2 = jnp.max(l2, axis=-1, keepdims=True)
    i2 = jnp.min(jnp.where(l2 == m2, lane, LANES), axis=-1, keepdims=True)
    e = jnp.exp(m2 - m1)
    w1 = 1.0 / (1.0 + e)
    w2 = e / (1.0 + e)

    hot1 = lane == i1
    hot2 = lane == i2
    onehot = jnp.where(hot1 | hot2, 1.0, 0.0)
    base = base_ref[0:1, :]
    prefix = jnp.dot(tri_ref[...], onehot.astype(BF16), preferred_element_type=F32) + base
    r1 = jnp.sum(jnp.where(hot1, prefix, 0.0), axis=-1, keepdims=True)
    r2 = jnp.sum(jnp.where(hot2, prefix, 0.0), axis=-1, keepdims=True)
    total = base + jnp.sum(onehot, axis=0, keepdims=True)
    base_ref[...] = jnp.broadcast_to(total, base_ref.shape)
    count_ref[...] = jnp.broadcast_to(total, count_ref.shape)

    rec = jnp.where(lane == ROUTE_E0, i1.astype(F32), 0.0)
    rec = jnp.where(lane == ROUTE_E1, i2.astype(F32), rec)
    rec = jnp.where(lane == ROUTE_R0, r1, rec)
    rec = jnp.where(lane == ROUTE_R1, r2, rec)
    rec = jnp.where(lane == ROUTE_W0, w1, rec)
    rec = jnp.where(lane == ROUTE_W1, w2, rec)
    route_ref[...] = rec
    fields_ref[...] = rec.T[:fields_ref.shape[0], :]


def _row_copy(src, src_row, dst, dst_row, sem):
    return pltpu.make_async_copy(src.at[pl.ds(src_row, 1)], dst.at[pl.ds(dst_row, 1)], sem)


def _dispatch_kernel(pad_start_ref, pad_count_ref, tile_valid_ref, pos_ref, h_ref, xs_ref,
                     zero_ref, sem, zsem):
    tm = h_ref.shape[0]
    n_tiles = xs_ref.shape[0] // tm

    @pl.when(pl.program_id(0) == 0)
    def _():
        zero_ref[...] = jnp.zeros_like(zero_ref)
        for e in range(N_EXPERTS):
            start = pad_start_ref[e]
            count = pad_count_ref[e]

            def zero_issue(r, carry):
                _row_copy(zero_ref, 0, xs_ref, start + r, zsem).start()
                return carry

            def zero_wait(r, carry):
                _row_copy(zero_ref, 0, xs_ref, 0, zsem).wait()
                return carry

            lax.fori_loop(0, count, zero_issue, 0)
            lax.fori_loop(0, count, zero_wait, 0)
        for g in range(n_tiles - N_EXPERTS, n_tiles):
            @pl.when(tile_valid_ref[g] == 0)
            def _():
                fill = pltpu.make_async_copy(zero_ref, xs_ref.at[pl.ds(g * tm, tm)], zsem)
                fill.start()
                fill.wait()

    def issue(i, carry):
        for k in range(ROW_UNROLL):
            r = i * ROW_UNROLL + k
            _row_copy(h_ref, r, xs_ref, pos_ref[0, 0, r], sem).start()
            _row_copy(h_ref, r, xs_ref, pos_ref[0, 0, tm + r], sem).start()
        return carry

    lax.fori_loop(0, tm // ROW_UNROLL, issue, 0)
    for _ in range(2):
        pltpu.make_async_copy(h_ref, xs_ref.at[pl.ds(0, tm)], sem).wait()


def _moe_ffn_kernel(te_ref, tv_ref, x_ref, wg_ref, wu_ref, wd_ref, y_ref, acc_ref):
    del te_ref
    g_idx = pl.program_id(0)
    j = pl.program_id(1)
    valid = tv_ref[g_idx] > 0

    @pl.when(j == 0)
    def _():
        acc_ref[...] = jnp.zeros_like(acc_ref)

    @pl.when(valid)
    def _():
        h = x_ref[...].astype(BF16)
        g = jnp.dot(h, wg_ref[0], preferred_element_type=F32)
        u = jnp.dot(h, wu_ref[0], preferred_element_type=F32)
        a = (g * jax.nn.sigmoid(g) * u).astype(BF16)
        acc_ref[...] += jnp.dot(a, wd_ref[0], preferred_element_type=F32)

    @pl.when(j == pl.num_programs(1) - 1)
    def _():
        y_ref[...] = acc_ref[...]


def _combine_kernel(pos_ref, nxt_ref, x_ref, route_ref, fg_ref, y_hbm, oa_ref, ob_ref,
                    ybuf, sem, *, n_a):
    i = pl.program_id(0)
    tc = x_ref.shape[0]
    cur = i % 2

    def issue_tile(p_ref, par):
        def issue(blk, carry):
            for k in range(ROW_UNROLL):
                r = blk * ROW_UNROLL + k
                _row_copy(y_hbm, p_ref[0, 0, r], ybuf.at[par, 0], r, sem.at[par]).start()
                _row_copy(y_hbm, p_ref[0, 0, tc + r], ybuf.at[par, 1], r, sem.at[par]).start()
            return carry
        lax.fori_loop(0, tc // ROW_UNROLL, issue, 0)

    @pl.when(i == 0)
    def _():
        issue_tile(pos_ref, 0)

    @pl.when(i + 1 < pl.num_programs(0))
    def _():
        issue_tile(nxt_ref, 1 - cur)

    for slot in range(2):
        pltpu.make_async_copy(y_hbm.at[pl.ds(0, tc)], ybuf.at[cur, slot], sem.at[cur]).wait()
    route = route_ref[...]
    lane = lax.broadcasted_iota(jnp.int32, route.shape, 1)
    w0 = _lane_pick(route, lane, ROUTE_W0)
    w1 = _lane_pick(route, lane, ROUTE_W1)
    out = _rms(x_ref[...] + (w0 * ybuf[cur, 0] + w1 * ybuf[cur, 1]), fg_ref[...])
    in_a = i < n_a

    @pl.when(in_a)
    def _():
        oa_ref[...] = out

    @pl.when(jnp.logical_not(in_a))
    def _():
        ob_ref[...] = out


def _rope_tables(max_seq):
    t = jnp.arange(max_seq, dtype=jnp.int32)
    row = (t // GRID_W).astype(F32)
    col = (t % GRID_W).astype(F32)
    inv = 1.0 / (ROPE_THETA ** (jnp.arange(0, AXIS_DIM, 2, dtype=F32) / AXIS_DIM))
    ar = row[:, None] * inv[None, :]
    ac = col[:, None] * inv[None, :]
    cos64 = jnp.concatenate([jnp.cos(ar), jnp.cos(ar), jnp.cos(ac), jnp.cos(ac)], axis=-1)
    sin64 = jnp.concatenate([-jnp.sin(ar), jnp.sin(ar), -jnp.sin(ac), jnp.sin(ac)], axis=-1)
    return cos64, sin64


def _gain_rope_tables(cos64, sin64, gain):
    half = AXIS_DIM // 2
    partner_gain = gain.reshape(-1, 2, half)[:, ::-1, :].reshape(-1)
    return (jnp.tile(cos64 * gain[None, :], (1, 2)),
            jnp.tile(sin64 * partner_gain[None, :], (1, 2)))


def _two_part_specs(block, n_a, parts):
    off = 0 if parts[0] is parts[1] else n_a
    return [pl.BlockSpec(block, lambda i: (jnp.minimum(i, n_a - 1), 0)),
            pl.BlockSpec(block, lambda i: (jnp.maximum(i, n_a) - off, 0))]


def _in_proj(xparts, t, n_a, gain, w, q_tables, k_tables, bd, pos_map):
    row = lambda i: (i, 0)
    const = lambda i: (0, 0)
    table = pl.BlockSpec((TM, LANES), pos_map)
    act = pl.BlockSpec((TM, D_MODEL), row)
    act_shape = jax.ShapeDtypeStruct((t, D_MODEL), BF16)
    return pl.pallas_call(
        functools.partial(_inproj_kernel, n_a=n_a),
        grid=(t // TM,),
        in_specs=_two_part_specs((TM, D_MODEL), n_a, xparts) + [
                  pl.BlockSpec((1, D_MODEL), const),
                  pl.BlockSpec((D_MODEL, QKV_WIDTH + REST_WIDTH), const,
                               pipeline_mode=pl.Buffered(1)),
                  table, table, table, table,
                  pl.BlockSpec((LANES, LANES), const)],
        out_specs=[pl.BlockSpec((TM, Q_EXP_WIDTH), row),
                   pl.BlockSpec((TM, KV_EXP_WIDTH), row),
                   pl.BlockSpec((KV_EXP_WIDTH, TM), lambda i: (0, i)),
                   act, act, act, act],
        out_shape=[jax.ShapeDtypeStruct((t, Q_EXP_WIDTH), BF16),
                   jax.ShapeDtypeStruct((t, KV_EXP_WIDTH), BF16),
                   jax.ShapeDtypeStruct((KV_EXP_WIDTH, t), BF16),
                   act_shape, act_shape, act_shape, act_shape],
        compiler_params=_params(("parallel",)),
        name="in_proj",
    )(*xparts, gain, w, *q_tables, *k_tables, bd)


def _attention(shift, q, k, vt, casts, *, row0, n_seq, seq, tq, exact_max):
    nq = seq // tq
    q0 = row0 // tq
    s0 = row0 // seq
    n_steps = n_seq * nq
    step_blk = lambda b, i, km: (b * nq + i, 0, 0)
    rides = [(w.shape[0] * w.shape[1]) % (n_steps * BF16_SUBLANES) == 0 for w in casts]
    views = [w.reshape(n_steps, w.shape[0] * w.shape[1] // n_steps, w.shape[2])
             for w, ok in zip(casts, rides) if ok]
    cast_specs = [pl.BlockSpec((1,) + v.shape[1:], step_blk) for v in views]
    outs = pl.pallas_call(
        functools.partial(_attn_kernel, exact_max=exact_max),
        grid_spec=pltpu.PrefetchScalarGridSpec(
            num_scalar_prefetch=1,
            grid=(n_seq, nq),
            in_specs=[pl.BlockSpec((tq, Q_EXP_WIDTH), lambda b, i, km: (q0 + b * nq + i, 0)),
                      pl.BlockSpec((seq, KV_EXP_WIDTH), lambda b, i, km: (s0 + b, 0)),
                      pl.BlockSpec((KV_EXP_WIDTH, seq), lambda b, i, km: (0, s0 + b))]
                     + cast_specs,
            out_specs=[pl.BlockSpec((tq, D_MODEL), lambda b, i, km: (b * nq + i, 0))]
                      + cast_specs),
        out_shape=[jax.ShapeDtypeStruct((n_seq * seq, D_MODEL), BF16)]
                  + [jax.ShapeDtypeStruct(v.shape, BF16) for v in views],
        compiler_params=_params(("parallel", "parallel")),
        name=f"attention_s{seq}" + ("_exact" if exact_max else ""),
    )(shift, q, k, vt, *views)
    carried = iter(outs[1:])
    return outs[0], [next(carried).reshape(w.shape) if ok else w.astype(BF16)
                     for w, ok in zip(casts, rides)]


def _plan_casts(jobs, depth, call_weights=(2.0, 1.0)):
    plan = []
    pending = sorted(jobs, key=lambda job: job[2])
    for l in range(depth):
        budget = sum(job[1].size for job in pending) / (depth - l)
        take, used = [], 0
        for job in pending:
            if job[2] == l or used + job[1].size <= budget:
                take.append(job)
                used += job[1].size
        pending = [job for job in pending if all(job is not t for t in take)]
        calls = [[] for _ in call_weights]
        loads = [0.0] * len(call_weights)
        for job in sorted(take, key=lambda job: -job[1].size):
            c = min(range(len(calls)), key=lambda c: loads[c] / call_weights[c])
            calls[c].append(job)
            loads[c] += job[1].size
        plan.append(calls)
    return plan


def _mix(attn_parts, cb, u, ga, gb, xparts, cw, woa, wob, wo, gain, rw, *, n_prompt, seq_p, seq_s):
    t = cb.shape[0]
    n_a = n_prompt // TM
    row = lambda i: (i, 0)
    const = lambda i: (0, 0)
    sub = TM // SUBLANES
    last = t // SUBLANES - 1
    act = pl.BlockSpec((TM, D_MODEL), row)
    wspec = pl.BlockSpec((D_MODEL, D_MODEL), const)
    with_router = rw is not None
    kern = functools.partial(_mix_kernel, n_prompt_tiles=n_prompt // TM,
                             tiles4=seq_p // TM, tiles2=seq_s // TM, with_router=with_router)
    in_specs = (_two_part_specs((TM, D_MODEL), n_a, attn_parts) + [
                act, act,
                pl.BlockSpec((SUBLANES, D_MODEL), lambda i: (jnp.maximum(i * sub - 1, 0), 0)),
                pl.BlockSpec((SUBLANES, D_MODEL),
                             lambda i: (jnp.minimum((i + 1) * sub, last), 0)),
                act, act]
                + _two_part_specs((TM, D_MODEL), n_a, xparts) + [
                pl.BlockSpec((3, D_MODEL), const),
                wspec, wspec, wspec,
                pl.BlockSpec((1, D_MODEL), const)])
    args = [*attn_parts, cb, u, u, u, ga, gb, *xparts, cw, woa, wob, wo, gain]
    out_specs = [act, act]
    out_shape = [jax.ShapeDtypeStruct((t, D_MODEL), F32),
                 jax.ShapeDtypeStruct((t, D_MODEL), F32 if with_router else BF16)]
    if with_router:
        in_specs.append(pl.BlockSpec((D_MODEL, LANES), const))
        args.append(rw)
        out_specs.append(pl.BlockSpec((TM, LANES), row))
        out_shape.append(jax.ShapeDtypeStruct((t, LANES), F32))
    return pl.pallas_call(
        kern,
        grid=(t // TM,),
        in_specs=in_specs,
        out_specs=out_specs,
        out_shape=out_shape,
        compiler_params=_params(("parallel",)),
        name="mix_proj_router" if with_router else "mix_proj",
    )(*args)


def _ffn(h, x, wg, wu, wd):
    t = x.shape[0]
    row = lambda i, j: (i, 0)
    return pl.pallas_call(
        _ffn_kernel,
        grid=(t // TM_FFN, D_FF // TF),
        in_specs=[pl.BlockSpec((TM_FFN, D_MODEL), row),
                  pl.BlockSpec((TM_FFN, D_MODEL), row),
                  pl.BlockSpec((D_MODEL, TF), lambda i, j: (0, j), pipeline_mode=pl.Buffered(1)),
                  pl.BlockSpec((D_MODEL, TF), lambda i, j: (0, j), pipeline_mode=pl.Buffered(1)),
                  pl.BlockSpec((TF, D_MODEL), lambda i, j: (j, 0), pipeline_mode=pl.Buffered(1))],
        out_specs=pl.BlockSpec((TM_FFN, D_MODEL), row),
        out_shape=jax.ShapeDtypeStruct((t, D_MODEL), F32),
        scratch_shapes=[pltpu.VMEM((TM_FFN, D_MODEL), F32)],
        compiler_params=_params(("parallel", "arbitrary")),
        name="ffn_dense",
    )(h, x, wg, wu, wd)


def _router(logits):
    t = logits.shape[0]
    row = lambda i: (i, 0)
    const = lambda i: (0, 0)
    r = jnp.arange(TR)
    tri = (r[None, :] < r[:, None]).astype(BF16)
    return pl.pallas_call(
        _router_kernel,
        grid=(t // TR,),
        in_specs=[pl.BlockSpec((TR, LANES), row),
                  pl.BlockSpec((TR, TR), const)],
        out_specs=[pl.BlockSpec((TR, LANES), row),
                   pl.BlockSpec((SUBLANES, TR), lambda i: (0, i)),
                   pl.BlockSpec((SUBLANES, LANES), const)],
        out_shape=[jax.ShapeDtypeStruct((t, LANES), F32),
                   jax.ShapeDtypeStruct((SUBLANES, t), F32),
                   jax.ShapeDtypeStruct((SUBLANES, LANES), F32)],
        scratch_shapes=[pltpu.VMEM((SUBLANES, LANES), F32)],
        compiler_params=_params(("arbitrary",)),
        name="router",
    )(logits, tri)


def _route_plan(fields, counts, t):
    e0 = fields[ROUTE_E0].astype(jnp.int32)
    e1 = fields[ROUTE_E1].astype(jnp.int32)
    r0 = fields[ROUTE_R0].astype(jnp.int32)
    r1 = fields[ROUTE_R1].astype(jnp.int32)
    cnt = counts[0, :N_EXPERTS].astype(jnp.int32)
    padded = ((cnt + TM_MOE - 1) // TM_MOE) * TM_MOE
    ends = jnp.cumsum(padded)
    starts = ends - padded
    experts = jnp.arange(N_EXPERTS, dtype=jnp.int32)
    pos0 = jnp.sum(jnp.where(e0[:, None] == experts[None, :], starts[None, :], 0), axis=1) + r0
    pos1 = jnp.sum(jnp.where(e1[:, None] == experts[None, :], starts[None, :], 0), axis=1) + r1
    n_tok_tiles = t // TM_MOE
    pos = jnp.concatenate([pos0.reshape(n_tok_tiles, 1, TM_MOE),
                           pos1.reshape(n_tok_tiles, 1, TM_MOE)], axis=2)
    n_tiles = 2 * t // TM_MOE + N_EXPERTS
    tile_start = jnp.arange(n_tiles, dtype=jnp.int32) * TM_MOE
    tile_valid = (tile_start < ends[-1]).astype(jnp.int32)
    tile_expert = jnp.sum((tile_start[:, None] >= ends[None, :]).astype(jnp.int32), axis=1)
    tile_expert = jnp.minimum(tile_expert, N_EXPERTS - 1)
    pad = (starts + cnt, padded - cnt)
    return pos, pad, tile_expert, tile_valid


def _dispatch(pad_start, pad_count, tile_valid, pos, h):
    t = h.shape[0]
    n_rows = tile_valid.shape[0] * TM_MOE
    return pl.pallas_call(
        _dispatch_kernel,
        grid_spec=pltpu.PrefetchScalarGridSpec(
            num_scalar_prefetch=3,
            grid=(t // TM_MOE,),
            in_specs=[pl.BlockSpec((1, 1, 2 * TM_MOE), lambda i, ps, pc, tv: (i, 0, 0),
                                   memory_space=pltpu.SMEM),
                      pl.BlockSpec((TM_MOE, D_MODEL), lambda i, ps, pc, tv: (i, 0))],
            out_specs=pl.BlockSpec(memory_space=pl.ANY),
            scratch_shapes=[pltpu.VMEM((TM_MOE, D_MODEL), F32),
                            pltpu.SemaphoreType.DMA(()),
                            pltpu.SemaphoreType.DMA(())]),
        out_shape=jax.ShapeDtypeStruct((n_rows, D_MODEL), F32),
        compiler_params=_params(("arbitrary",)),
        name="moe_dispatch",
    )(pad_start, pad_count, tile_valid, pos, h)


def _moe_ffn(tile_expert, tile_valid, xs, wg, wu, wd):
    n_rows = xs.shape[0]
    row = lambda g, j, te, tv: (g, 0)
    return pl.pallas_call(
        _moe_ffn_kernel,
        grid_spec=pltpu.PrefetchScalarGridSpec(
            num_scalar_prefetch=2,
            grid=(n_rows // TM_MOE, D_FF // TF_MOE),
            in_specs=[pl.BlockSpec((TM_MOE, D_MODEL), row),
                      pl.BlockSpec((1, D_MODEL, TF_MOE),
                                   lambda g, j, te, tv: (te[g], 0, j * tv[g])),
                      pl.BlockSpec((1, D_MODEL, TF_MOE),
                                   lambda g, j, te, tv: (te[g], 0, j * tv[g])),
                      pl.BlockSpec((1, TF_MOE, D_MODEL),
                                   lambda g, j, te, tv: (te[g], j * tv[g], 0))],
            out_specs=pl.BlockSpec((TM_MOE, D_MODEL), row),
            scratch_shapes=[pltpu.VMEM((TM_MOE, D_MODEL), F32)]),
        out_shape=jax.ShapeDtypeStruct((n_rows, D_MODEL), F32),
        compiler_params=_params(("parallel", "arbitrary")),
        name="moe_ffn",
    )(tile_expert, tile_valid, xs, wg, wu, wd)


def _combine(pos, x, route, fgain, ys, n_first):
    t = x.shape[0]
    n_a = n_first // TM_MOE
    row = lambda i: (i, 0)
    blk = (TM_MOE, D_MODEL)
    last = t // TM_MOE - 1
    pos_blk = (1, 1, 2 * TM_MOE)
    return pl.pallas_call(
        functools.partial(_combine_kernel, n_a=n_a),
        grid=(t // TM_MOE,),
        in_specs=[pl.BlockSpec(pos_blk, lambda i: (i, 0, 0), memory_space=pltpu.SMEM),
                  pl.BlockSpec(pos_blk, lambda i: (jnp.minimum(i + 1, last), 0, 0),
                               memory_space=pltpu.SMEM),
                  pl.BlockSpec(blk, row),
                  pl.BlockSpec((TM_MOE, LANES), row),
                  pl.BlockSpec((1, D_MODEL), lambda i: (0, 0)),
                  pl.BlockSpec(memory_space=pl.ANY)],
        out_specs=[pl.BlockSpec(blk, lambda i: (jnp.minimum(i, n_a - 1), 0)),
                   pl.BlockSpec(blk, lambda i: (jnp.maximum(i - n_a, 0), 0))],
        out_shape=[jax.ShapeDtypeStruct((n_first, D_MODEL), F32),
                   jax.ShapeDtypeStruct((t - n_first, D_MODEL), F32)],
        scratch_shapes=[pltpu.VMEM((2, 2, TM_MOE, D_MODEL), F32),
                        pltpu.SemaphoreType.DMA((2,))],
        compiler_params=_params(("arbitrary",)),
        name="moe_combine",
    )(pos, pos, x, route, fgain, ys)


def kernel(x_prompt, x_sample, norm_mix, w_in, q_norm, k_norm, conv_w, w_oa, w_ob, w_o, norm_ffn,
           ffn_w_gate, ffn_w_up, ffn_w_down, router_w, moe_w_gate, moe_w_up, moe_w_down, final_norm):
    bp, sp, _ = x_prompt.shape
    bs, ss, _ = x_sample.shape
    n_prompt = bp * sp
    n_sample = bs * ss
    depth = norm_mix.shape[0]
    assert sp % TM == 0 and ss % TM == 0 and TM == TM_FFN == TM_MOE
    assert depth == 2 and ffn_w_gate.shape[0] == moe_w_gate.shape[0] == 1

    t_all = n_prompt + n_sample
    xparts = (x_prompt.reshape(n_prompt, D_MODEL), x_sample.reshape(n_sample, D_MODEL))

    cos, sin = _rope_tables(max(sp, ss))
    n_prompt_tiles, tiles4, tiles2 = n_prompt // TM, sp // TM, ss // TM
    pos_map = lambda i: (jnp.where(i < n_prompt_tiles, i % tiles4, i % tiles2), 0)
    idx = jnp.arange(LANES)
    bd = jnp.where(idx[:, None] // HEAD_DIM == idx[None, :] // HEAD_DIM,
                   1.0 / HEAD_DIM, 0.0).astype(BF16)

    jobs = []
    for l in range(depth):
        if l > 0:
            jobs.append((("w_in", l), w_in[l][None], l - 1))
        jobs += [((name, l), w[l][None], l)
                 for name, w in (("w_oa", w_oa), ("w_ob", w_ob), ("w_o", w_o))]
        named = ((("ffn_gate", ffn_w_gate), ("ffn_up", ffn_w_up), ("ffn_down", ffn_w_down))
                 if l % 2 == 0 else
                 (("moe_gate", moe_w_gate), ("moe_up", moe_w_up), ("moe_down", moe_w_down)))
        jobs += [((name, l), w[l // 2] if l % 2 else w[l // 2][None], l) for name, w in named]
    cast_plan = _plan_casts(jobs, depth)
    bf16_w = {("w_in", 0): w_in[0].astype(BF16)}

    for l in range(depth):
        gain = norm_mix[l][None, :]
        q, k, vt, cb, u, ga, gb = _in_proj(
            xparts, t_all, n_prompt_tiles, gain, bf16_w[("w_in", l)],
            _gain_rope_tables(cos, sin, q_norm[l] * Q_SCALE),
            _gain_rope_tables(cos, sin, k_norm[l]), bd, pos_map)
        shift = (Q_SCALE * HEAD_DIM * jnp.max(jnp.abs(q_norm[l]))
                 * jnp.max(jnp.abs(k_norm[l]))).reshape(1)
        bound_ok = 2.0 * shift[0] < MAX_SHIFT_GAP

        jobs_p, jobs_s = cast_plan[l]

        def attend(exact_max, tq_p, tq_s):
            def run(shift, q, k, vt, w_p, w_s):
                attn_p, done_p = _attention(shift, q, k, vt, w_p, row0=0, n_seq=bp, seq=sp,
                                            tq=tq_p, exact_max=exact_max)
                attn_s, done_s = _attention(shift, q, k, vt, w_s, row0=n_prompt, n_seq=bs,
                                            seq=ss, tq=tq_s, exact_max=exact_max)
                return (attn_p, attn_s), done_p + done_s
            return run

        attn_parts, done = lax.cond(
            bound_ok, attend(False, 256, 256), attend(True, 128, 256), shift, q, k, vt,
            [job[1] for job in jobs_p], [job[1] for job in jobs_s])
        for job, w16 in zip(jobs_p + jobs_s, done):
            bf16_w[job[0]] = w16 if job[0][0].startswith("moe") else w16[0]
        j = l // 2
        is_moe = l % 2 == 1
        rw = (jnp.pad(jnp.tile(router_w[j], (1, 2)), ((0, 0), (0, LANES - 2 * N_EXPERTS)))
              if is_moe else None)
        outs = _mix(attn_parts, cb, u, ga, gb, xparts, conv_w[l],
                    bf16_w[("w_oa", l)], bf16_w[("w_ob", l)], bf16_w[("w_o", l)],
                    norm_ffn[l][None, :], rw, n_prompt=n_prompt, seq_p=sp, seq_s=ss)
        if not is_moe:
            x, h2 = outs
            x = _ffn(h2, x, bf16_w[("ffn_gate", l)], bf16_w[("ffn_up", l)],
                     bf16_w[("ffn_down", l)])
            xparts = (x, x)
        else:
            x, h2, logits = outs
            route, fields, counts = _router(logits)
            pos, pad, tile_expert, tile_valid = _route_plan(fields, counts, t_all)
            xs = _dispatch(*pad, tile_valid, pos, h2)
            ys = _moe_ffn(tile_expert, tile_valid, xs, bf16_w[("moe_gate", l)],
                          bf16_w[("moe_up", l)], bf16_w[("moe_down", l)])
            xparts = _combine(pos, x, route, final_norm[None, :], ys, n_prompt)

    return (xparts[0].reshape(bp, sp, D_MODEL), xparts[1].reshape(bs, ss, D_MODEL))
```

```python
import functools
import math

import jax
import jax.numpy as jnp
from jax import lax
from jax.experimental import pallas as pl
from jax.experimental.pallas import tpu as pltpu

F32 = jnp.float32
BF16 = jnp.bfloat16

D_MODEL = 1024
N_HEADS = 16
N_KV_HEADS = 4
HEAD_DIM = 64
GROUP = N_HEADS // N_KV_HEADS
KV_WIDTH = N_KV_HEADS * HEAD_DIM
AXIS_DIM = HEAD_DIM // 2
ROPE_THETA = 10000.0
GRID_W = 64
D_FF = 3584
N_EXPERTS = 8
EPS = 1e-6
LANES = 128
QKV_WIDTH = D_MODEL + 2 * KV_WIDTH
REST_WIDTH = 5 * D_MODEL
Q_EXP_WIDTH = N_HEADS * LANES
KV_EXP_WIDTH = N_KV_HEADS * LANES
Q_SCALE = math.log2(math.e) / math.sqrt(HEAD_DIM)
SUBLANES = 8
BF16_SUBLANES = 16
ROT_HALF = AXIS_DIM // 2
V_ROWS = HEAD_DIM + BF16_SUBLANES
KEY_CHUNK = 512
ROW_UNROLL = SUBLANES
MAX_SHIFT_GAP = 100.0

TM = 512
TM_FFN = 512
TF = 3584
TF_MOE = 1792
TR = 512
TM_MOE = 512
VMEM_LIMIT = 56 * 1024 * 1024


def _params(sem):
    return pltpu.CompilerParams(dimension_semantics=sem, vmem_limit_bytes=VMEM_LIMIT)


def _rms(x, gain):
    return x * lax.rsqrt(jnp.mean(x * x, axis=-1, keepdims=True) + EPS) * gain


def _store_row_tiles(ref, x):
    rows = x.shape[0]
    for c in range(D_MODEL // LANES):
        ref[pl.ds(c, rows, stride=SUBLANES), :] = x[:, c * LANES:(c + 1) * LANES]


def _load_row_tiles(ref):
    rows = ref.shape[0] // SUBLANES
    return jnp.concatenate([ref[pl.ds(c, rows, stride=SUBLANES), :]
                            for c in range(D_MODEL // LANES)], axis=1)


def _pick(n_a, a_ref, b_ref):
    return jnp.where(pl.program_id(0) < n_a, a_ref[...], b_ref[...])


def _inproj_kernel(xa_ref, xb_ref, g_ref, w_ref, q_own_ref, q_other_ref, k_own_ref, k_other_ref,
                   bd_ref, q_ref, k_ref, vt_ref, cb_ref, u_ref, ga_ref, gb_ref, *, n_a):
    tm = xa_ref.shape[0]
    h = _rms(_pick(n_a, xa_ref, xb_ref), g_ref[...]).astype(BF16)
    p = jnp.dot(h, w_ref[:, :QKV_WIDTH], preferred_element_type=F32)
    rest = jnp.dot(h, w_ref[:, QKV_WIDTH:], preferred_element_type=F32)
    cb_ref[...] = rest[:, :D_MODEL].astype(BF16)
    u_ref[...] = (rest[:, D_MODEL:2 * D_MODEL] * rest[:, 2 * D_MODEL:3 * D_MODEL]).astype(BF16)
    ga_ref[...] = rest[:, 3 * D_MODEL:4 * D_MODEL].astype(BF16)
    gb_ref[...] = rest[:, 4 * D_MODEL:].astype(BF16)
    bd = bd_ref[...]
    lane = lax.broadcasted_iota(jnp.int32, (tm, LANES), 1)
    first_half = (lane & (AXIS_DIM - 1)) < ROT_HALF
    low_half = lane < HEAD_DIM

    def norm_rope(c, own, other):
        ms = jnp.dot((c * c).astype(BF16), bd, preferred_element_type=F32)
        partner = jnp.where(first_half, pltpu.roll(c, LANES - ROT_HALF, 1),
                            pltpu.roll(c, ROT_HALF, 1))
        return (c * own + partner * other) * lax.rsqrt(ms + EPS)

    is_aux = lane == HEAD_DIM
    zero = jnp.zeros((tm, LANES), F32)
    ones_aux = jnp.where(is_aux, 1.0, zero)

    q_own = q_own_ref[...]
    q_other = q_other_ref[...]
    for c in range(N_HEADS // 2):
        r = norm_rope(p[:, c * LANES:(c + 1) * LANES], q_own, q_other)
        even = jnp.where(low_half, r, zero)
        odd = jnp.where(low_half, pltpu.roll(r, HEAD_DIM, 1), zero)
        q_ref[:, (2 * c) * LANES:(2 * c + 1) * LANES] = even.astype(BF16)
        q_ref[:, (2 * c + 1) * LANES:(2 * c + 2) * LANES] = odd.astype(BF16)

    k_own = k_own_ref[...]
    k_other = k_other_ref[...]
    for c in range(KV_WIDTH // LANES):
        kn = norm_rope(p[:, D_MODEL + c * LANES:D_MODEL + (c + 1) * LANES], k_own, k_other)
        even = jnp.where(low_half, kn, ones_aux)
        odd = jnp.where(low_half, pltpu.roll(kn, HEAD_DIM, 1), ones_aux)
        k_ref[:, (2 * c) * LANES:(2 * c + 1) * LANES] = even.astype(BF16)
        k_ref[:, (2 * c + 1) * LANES:(2 * c + 2) * LANES] = odd.astype(BF16)
        vv = p[:, D_MODEL + KV_WIDTH + c * LANES:D_MODEL + KV_WIDTH + (c + 1) * LANES]
        even = jnp.where(low_half, vv, ones_aux)
        odd = jnp.where(low_half, pltpu.roll(vv, HEAD_DIM, 1), ones_aux)
        vt_ref[(2 * c) * LANES:(2 * c + 1) * LANES, :] = even.T.astype(BF16)
        vt_ref[(2 * c + 1) * LANES:(2 * c + 2) * LANES, :] = odd.T.astype(BF16)


def _attn_kernel(shift_ref, q_ref, k_ref, vt_ref, *refs, exact_max):
    n_cast = (len(refs) - 1) // 2
    o_ref = refs[n_cast]
    for src, dst in zip(refs[:n_cast], refs[n_cast + 1:]):
        dst[...] = src[...].astype(BF16)
    tq = q_ref.shape[0]
    is_aux_row = lax.broadcasted_iota(jnp.int32, (LANES, GROUP * tq), 0) == HEAD_DIM
    for j in range(N_KV_HEADS):
        qt = jnp.concatenate(
            [q_ref[:, h * LANES:(h + 1) * LANES].T for h in range(GROUP * j, GROUP * (j + 1))],
            axis=1)
        if exact_max:
            st = jnp.dot(k_ref[:, j * LANES:(j + 1) * LANES], qt,
                         preferred_element_type=F32)
            st = st - jnp.max(st, axis=0, keepdims=True)
            pt = jnp.exp2(st).astype(BF16)
            ot = jnp.dot(vt_ref[j * LANES:j * LANES + V_ROWS, :], pt,
                         preferred_element_type=F32)
        else:
            neg_shift = jnp.full(qt.shape, -shift_ref[0], F32).astype(BF16)
            qt = jnp.where(is_aux_row, neg_shift, qt)
            ot = jnp.zeros((V_ROWS, GROUP * tq), F32)
            for c in range(k_ref.shape[0] // KEY_CHUNK):
                rows = slice(c * KEY_CHUNK, (c + 1) * KEY_CHUNK)
                st = jnp.dot(k_ref[rows, j * LANES:(j + 1) * LANES], qt,
                             preferred_element_type=F32)
                pt = jnp.exp2(st).astype(BF16)
                ot = ot + jnp.dot(vt_ref[j * LANES:j * LANES + V_ROWS, rows], pt,
                                  preferred_element_type=F32)
        ot = ot[:HEAD_DIM] / ot[HEAD_DIM:HEAD_DIM + 1]
        for a in range(2):
            pair = jnp.concatenate([ot[:, (2 * a) * tq:(2 * a + 1) * tq],
                                    ot[:, (2 * a + 1) * tq:(2 * a + 2) * tq]], axis=0)
            c = 2 * j + a
            o_ref[:, c * LANES:(c + 1) * LANES] = pair.T.astype(BF16)


def _mix_kernel(attn_a_ref, attn_b_ref, cb_ref, u_ref, up_ref, un_ref, ga_ref, gb_ref,
                xa_ref, xb_ref, cw_ref, woa_ref, wob_ref, wo_ref, g_ref, *rest,
                n_prompt_tiles, tiles4, tiles2, with_router):
    if with_router:
        rw_ref, xo_ref, h2_ref, lg_ref = rest
    else:
        xo_ref, h2_ref = rest
    i = pl.program_id(0)
    tm = xa_ref.shape[0]
    is_prompt = i < n_prompt_tiles
    seq_start = jnp.where(is_prompt, i % tiles4 == 0, i % tiles2 == 0)
    seq_end = jnp.where(is_prompt, i % tiles4 == tiles4 - 1, i % tiles2 == tiles2 - 1)

    attn = _pick(n_prompt_tiles, attn_a_ref, attn_b_ref)
    ya = jnp.dot(attn, woa_ref[...], preferred_element_type=F32)

    u = u_ref[...].astype(F32)
    row = lax.broadcasted_iota(jnp.int32, u.shape, 0)
    prev_row = jnp.where(seq_start, 0.0, up_ref[SUBLANES - 1:SUBLANES, :].astype(F32))
    next_row = jnp.where(seq_end, 0.0, un_ref[0:1, :].astype(F32))
    u_prev = jnp.where(row == 0, prev_row, pltpu.roll(u, 1, 0))
    u_next = jnp.where(row == tm - 1, next_row, pltpu.roll(u, tm - 1, 0))
    cw = cw_ref[...]
    conv = cw[0:1, :] * u_prev + cw[1:2, :] * u + cw[2:3, :] * u_next
    yb_in = (cb_ref[...].astype(F32) * conv).astype(BF16)
    yb = jnp.dot(yb_in, wob_ref[...], preferred_element_type=F32)

    m = (jax.nn.sigmoid(ga_ref[...].astype(F32)) * ya
         + jax.nn.sigmoid(gb_ref[...].astype(F32)) * yb)
    xn = (_pick(n_prompt_tiles, xa_ref, xb_ref)
          + jnp.dot(m.astype(BF16), wo_ref[...], preferred_element_type=F32))
    xo_ref[...] = xn
    h2 = _rms(xn, g_ref[...])
    if not with_router:
        h2_ref[...] = h2.astype(h2_ref.dtype)
        return
    _store_row_tiles(h2_ref, h2)
    hi = h2.astype(BF16)
    lo = (h2 - hi.astype(F32)).astype(BF16)
    rw = rw_ref[...]
    rhi = rw.astype(BF16)
    rlo = (rw - rhi.astype(F32)).astype(BF16)
    lane = lax.broadcasted_iota(jnp.int32, rw.shape, 1)
    hi_terms = jnp.dot(hi, jnp.where(lane < N_EXPERTS, rhi, rlo), preferred_element_type=F32)
    lg_ref[...] = (hi_terms + pltpu.roll(hi_terms, LANES - N_EXPERTS, 1)
                   + jnp.dot(lo, rhi, preferred_element_type=F32))


def _ffn_kernel(h_ref, x_ref, wg_ref, wu_ref, wd_ref, o_ref, acc_ref):
    j = pl.program_id(1)

    @pl.when(j == 0)
    def _():
        acc_ref[...] = jnp.zeros_like(acc_ref)

    h = h_ref[...]
    g = jnp.dot(h, wg_ref[...], preferred_element_type=F32)
    u = jnp.dot(h, wu_ref[...], preferred_element_type=F32)
    a = (g * jax.nn.sigmoid(g) * u).astype(BF16)
    acc_ref[...] += jnp.dot(a, wd_ref[...], preferred_element_type=F32)

    @pl.when(j == pl.num_programs(1) - 1)
    def _():
        o_ref[...] = x_ref[...] + acc_ref[...]


ROUTE_E0, ROUTE_E1, ROUTE_R0, ROUTE_R1, ROUTE_W0, ROUTE_W1 = range(6)


def _lane_pick(x, lane, k):
    return jnp.sum(jnp.where(lane == k, x, 0.0), axis=-1, keepdims=True)


def _router_kernel(lg_ref, tri_ref, route_ref, fields_ref, count_ref, base_ref):
    i = pl.program_id(0)

    @pl.when(i == 0)
    def _():
        base_ref[...] = jnp.zeros_like(base_ref)

    lg = lg_ref[...]
    lane = lax.broadcasted_iota(jnp.int32, lg.shape, 1)
    neg = jnp.float32(-jnp.inf)
    l1 = jnp.where(lane < N_EXPERTS, lg, neg)
    m1 = jnp.max(l1, axis=-1, keepdims=True)
    i1 = jnp.min(jnp.where(l1 == m1, lane, LANES), axis=-1, keepdims=True)
    l2 = jnp.where(lane == i1, neg, l1)
    m2 = jnp.max(l2, axis=-1, keepdims=True)
    i2 = jnp.min(jnp.where(l2 == m2, lane, LANES), axis=-1, keepdims=True)
    e = jnp.exp(m2 - m1)
    w1 = 1.0 / (1.0 + e)
    w2 = e / (1.0 + e)

    hot1 = lane == i1
    hot2 = lane == i2
    onehot = jnp.where(hot1 | hot2, 1.0, 0.0)
    base = base_ref[0:1, :]
    prefix = jnp.dot(tri_ref[...], onehot.astype(BF16), preferred_element_type=F32) + base
    r1 = jnp.sum(jnp.where(hot1, prefix, 0.0), axis=-1, keepdims=True)
    r2 = jnp.sum(jnp.where(hot2, prefix, 0.0), axis=-1, keepdims=True)
    total = base + jnp.sum(onehot, axis=0, keepdims=True)
    base_ref[...] = jnp.broadcast_to(total, base_ref.shape)
    count_ref[...] = jnp.broadcast_to(total, count_ref.shape)

    rec = jnp.where(lane == ROUTE_E0, i1.astype(F32), 0.0)
    rec = jnp.where(lane == ROUTE_E1, i2.astype(F32), rec)
    rec = jnp.where(lane == ROUTE_R0, r1, rec)
    rec = jnp.where(lane == ROUTE_R1, r2, rec)
    rec = jnp.where(lane == ROUTE_W0, w1, rec)
    rec = jnp.where(lane == ROUTE_W1, w2, rec)
    route_ref[...] = rec
    fields_ref[...] = rec.T[:fields_ref.shape[0], :]


def _row_copy(src, src_row, dst, dst_row, sem):
    return pltpu.make_async_copy(
        src.at[pl.ds(pl.multiple_of(src_row * SUBLANES, SUBLANES), SUBLANES)],
        dst.at[pl.ds(pl.multiple_of(dst_row * SUBLANES, SUBLANES), SUBLANES)], sem)


def _dispatch_kernel(pad_start_ref, pad_count_ref, tile_valid_ref, pos_ref, h_ref, xs_ref,
                     zero_ref, sem, zsem):
    tile = h_ref.shape[0]
    tm = tile // SUBLANES
    n_tiles = xs_ref.shape[0] // tile

    @pl.when(pl.program_id(0) == 0)
    def _():
        zero_ref[...] = jnp.zeros_like(zero_ref)
        for e in range(N_EXPERTS):
            start = pad_start_ref[e]
            count = pad_count_ref[e]

            def zero_issue(r, carry):
                _row_copy(zero_ref, 0, xs_ref, start + r, zsem).start()
                return carry

            def zero_wait(r, carry):
                _row_copy(zero_ref, 0, xs_ref, 0, zsem).wait()
                return carry

            lax.fori_loop(0, count, zero_issue, 0)
            lax.fori_loop(0, count, zero_wait, 0)
        for g in range(n_tiles - N_EXPERTS, n_tiles):
            @pl.when(tile_valid_ref[g] == 0)
            def _():
                fill = pltpu.make_async_copy(zero_ref, xs_ref.at[pl.ds(g * tile, tile)], zsem)
                fill.start()
                fill.wait()

    def issue(i, carry):
        for k in range(ROW_UNROLL):
            r = i * ROW_UNROLL + k
            _row_copy(h_ref, r, xs_ref, pos_ref[0, 0, r], sem).start()
            _row_copy(h_ref, r, xs_ref, pos_ref[0, 0, tm + r], sem).start()
        return carry

    lax.fori_loop(0, tm // ROW_UNROLL, issue, 0)
    for _ in range(2):
        pltpu.make_async_copy(h_ref, xs_ref.at[pl.ds(0, tile)], sem).wait()


def _moe_ffn_kernel(te_ref, tv_ref, x_ref, wg_ref, wu_ref, wd_ref, y_ref, acc_ref):
    del te_ref
    g_idx = pl.program_id(0)
    j = pl.program_id(1)
    valid = tv_ref[g_idx] > 0

    @pl.when(j == 0)
    def _():
        acc_ref[...] = jnp.zeros_like(acc_ref)

    @pl.when(valid)
    def _():
        h = _load_row_tiles(x_ref).astype(BF16)
        g = jnp.dot(h, wg_ref[0], preferred_element_type=F32)
        u = jnp.dot(h, wu_ref[0], preferred_element_type=F32)
        a = (g * jax.nn.sigmoid(g) * u).astype(BF16)
        acc_ref[...] += jnp.dot(a, wd_ref[0], preferred_element_type=F32)

    @pl.when(j == pl.num_programs(1) - 1)
    def _():
        _store_row_tiles(y_ref, acc_ref[...])


def _combine_kernel(pos_ref, nxt_ref, x_ref, route_ref, fg_ref, y_hbm, oa_ref, ob_ref,
                    ybuf, sem, *, n_a):
    i = pl.program_id(0)
    tc = x_ref.shape[0]
    cur = i % 2

    def issue_tile(p_ref, par):
        def issue(blk, carry):
            for k in range(ROW_UNROLL):
                r = blk * ROW_UNROLL + k
                _row_copy(y_hbm, p_ref[0, 0, r], ybuf.at[par, 0], r, sem.at[par]).start()
                _row_copy(y_hbm, p_ref[0, 0, tc + r], ybuf.at[par, 1], r, sem.at[par]).start()
            return carry
        lax.fori_loop(0, tc // ROW_UNROLL, issue, 0)

    @pl.when(i == 0)
    def _():
        issue_tile(pos_ref, 0)

    @pl.when(i + 1 < pl.num_programs(0))
    def _():
        issue_tile(nxt_ref, 1 - cur)

    for slot in range(2):
        pltpu.make_async_copy(y_hbm.at[pl.ds(0, tc * SUBLANES)], ybuf.at[cur, slot],
                              sem.at[cur]).wait()
    route = route_ref[...]
    lane = lax.broadcasted_iota(jnp.int32, route.shape, 1)
    w0 = _lane_pick(route, lane, ROUTE_W0)
    w1 = _lane_pick(route, lane, ROUTE_W1)
    y0 = _load_row_tiles(ybuf.at[cur, 0])
    y1 = _load_row_tiles(ybuf.at[cur, 1])
    out = _rms(x_ref[...] + (w0 * y0 + w1 * y1), fg_ref[...])
    in_a = i < n_a

    @pl.when(in_a)
    def _():
        oa_ref[...] = out

    @pl.when(jnp.logical_not(in_a))
    def _():
        ob_ref[...] = out


def _rope_tables(max_seq):
    t = jnp.arange(max_seq, dtype=jnp.int32)
    row = (t // GRID_W).astype(F32)
    col = (t % GRID_W).astype(F32)
    inv = 1.0 / (ROPE_THETA ** (jnp.arange(0, AXIS_DIM, 2, dtype=F32) / AXIS_DIM))
    ar = row[:, None] * inv[None, :]
    ac = col[:, None] * inv[None, :]
    cos64 = jnp.concatenate([jnp.cos(ar), jnp.cos(ar), jnp.cos(ac), jnp.cos(ac)], axis=-1)
    sin64 = jnp.concatenate([-jnp.sin(ar), jnp.sin(ar), -jnp.sin(ac), jnp.sin(ac)], axis=-1)
    return cos64, sin64


def _gain_rope_tables(cos64, sin64, gain):
    half = AXIS_DIM // 2
    partner_gain = gain.reshape(-1, 2, half)[:, ::-1, :].reshape(-1)
    return (jnp.tile(cos64 * gain[None, :], (1, 2)),
            jnp.tile(sin64 * partner_gain[None, :], (1, 2)))


def _two_part_specs(block, n_a, parts):
    off = 0 if parts[0] is parts[1] else n_a
    return [pl.BlockSpec(block, lambda i: (jnp.minimum(i, n_a - 1), 0)),
            pl.BlockSpec(block, lambda i: (jnp.maximum(i, n_a) - off, 0))]


def _in_proj(xparts, t, n_a, gain, w, q_tables, k_tables, bd, pos_map):
    row = lambda i: (i, 0)
    const = lambda i: (0, 0)
    table = pl.BlockSpec((TM, LANES), pos_map)
    act = pl.BlockSpec((TM, D_MODEL), row)
    act_shape = jax.ShapeDtypeStruct((t, D_MODEL), BF16)
    return pl.pallas_call(
        functools.partial(_inproj_kernel, n_a=n_a),
        grid=(t // TM,),
        in_specs=_two_part_specs((TM, D_MODEL), n_a, xparts) + [
                  pl.BlockSpec((1, D_MODEL), const),
                  pl.BlockSpec((D_MODEL, QKV_WIDTH + REST_WIDTH), const,
                               pipeline_mode=pl.Buffered(1)),
                  table, table, table, table,
                  pl.BlockSpec((LANES, LANES), const)],
        out_specs=[pl.BlockSpec((TM, Q_EXP_WIDTH), row),
                   pl.BlockSpec((TM, KV_EXP_WIDTH), row),
                   pl.BlockSpec((KV_EXP_WIDTH, TM), lambda i: (0, i)),
                   act, act, act, act],
        out_shape=[jax.ShapeDtypeStruct((t, Q_EXP_WIDTH), BF16),
                   jax.ShapeDtypeStruct((t, KV_EXP_WIDTH), BF16),
                   jax.ShapeDtypeStruct((KV_EXP_WIDTH, t), BF16),
                   act_shape, act_shape, act_shape, act_shape],
        compiler_params=_params(("parallel",)),
        name="in_proj",
    )(*xparts, gain, w, *q_tables, *k_tables, bd)


def _attention(shift, q, k, vt, casts, *, row0, n_seq, seq, tq, exact_max):
    nq = seq // tq
    q0 = row0 // tq
    s0 = row0 // seq
    n_steps = n_seq * nq
    step_blk = lambda b, i, km: (b * nq + i, 0, 0)
    rides = [(w.shape[0] * w.shape[1]) % (n_steps * BF16_SUBLANES) == 0 for w in casts]
    views = [w.reshape(n_steps, w.shape[0] * w.shape[1] // n_steps, w.shape[2])
             for w, ok in zip(casts, rides) if ok]
    cast_specs = [pl.BlockSpec((1,) + v.shape[1:], step_blk) for v in views]
    outs = pl.pallas_call(
        functools.partial(_attn_kernel, exact_max=exact_max),
        grid_spec=pltpu.PrefetchScalarGridSpec(
            num_scalar_prefetch=1,
            grid=(n_seq, nq),
            in_specs=[pl.BlockSpec((tq, Q_EXP_WIDTH), lambda b, i, km: (q0 + b * nq + i, 0)),
                      pl.BlockSpec((seq, KV_EXP_WIDTH), lambda b, i, km: (s0 + b, 0)),
                      pl.BlockSpec((KV_EXP_WIDTH, seq), lambda b, i, km: (0, s0 + b))]
                     + cast_specs,
            out_specs=[pl.BlockSpec((tq, D_MODEL), lambda b, i, km: (b * nq + i, 0))]
                      + cast_specs),
        out_shape=[jax.ShapeDtypeStruct((n_seq * seq, D_MODEL), BF16)]
                  + [jax.ShapeDtypeStruct(v.shape, BF16) for v in views],
        compiler_params=_params(("parallel", "parallel")),
        name=f"attention_s{seq}" + ("_exact" if exact_max else ""),
    )(shift, q, k, vt, *views)
    carried = iter(outs[1:])
    return outs[0], [next(carried).reshape(w.shape) if ok else w.astype(BF16)
                     for w, ok in zip(casts, rides)]


def _plan_casts(jobs, depth, call_weights=(2.0, 1.0)):
    plan = []
    pending = sorted(jobs, key=lambda job: job[2])
    for l in range(depth):
        budget = sum(job[1].size for job in pending) / (depth - l)
        take, used = [], 0
        for job in pending:
            if job[2] == l or used + job[1].size <= budget:
                take.append(job)
                used += job[1].size
        pending = [job for job in pending if all(job is not t for t in take)]
        calls = [[] for _ in call_weights]
        loads = [0.0] * len(call_weights)
        for job in sorted(take, key=lambda job: -job[1].size):
            c = min(range(len(calls)), key=lambda c: loads[c] / call_weights[c])
            calls[c].append(job)
            loads[c] += job[1].size
        plan.append(calls)
    return plan


def _mix(attn_parts, cb, u, ga, gb, xparts, cw, woa, wob, wo, gain, rw, *, n_prompt, seq_p, seq_s):
    t = cb.shape[0]
    n_a = n_prompt // TM
    row = lambda i: (i, 0)
    const = lambda i: (0, 0)
    sub = TM // SUBLANES
    last = t // SUBLANES - 1
    act = pl.BlockSpec((TM, D_MODEL), row)
    wspec = pl.BlockSpec((D_MODEL, D_MODEL), const)
    with_router = rw is not None
    kern = functools.partial(_mix_kernel, n_prompt_tiles=n_prompt // TM,
                             tiles4=seq_p // TM, tiles2=seq_s // TM, with_router=with_router)
    in_specs = (_two_part_specs((TM, D_MODEL), n_a, attn_parts) + [
                act, act,
                pl.BlockSpec((SUBLANES, D_MODEL), lambda i: (jnp.maximum(i * sub - 1, 0), 0)),
                pl.BlockSpec((SUBLANES, D_MODEL),
                             lambda i: (jnp.minimum((i + 1) * sub, last), 0)),
                act, act]
                + _two_part_specs((TM, D_MODEL), n_a, xparts) + [
                pl.BlockSpec((3, D_MODEL), const),
                wspec, wspec, wspec,
                pl.BlockSpec((1, D_MODEL), const)])
    args = [*attn_parts, cb, u, u, u, ga, gb, *xparts, cw, woa, wob, wo, gain]
    out_specs = [act, act]
    out_shape = [jax.ShapeDtypeStruct((t, D_MODEL), F32),
                 jax.ShapeDtypeStruct((t, D_MODEL), BF16)]
    if with_router:
        out_specs[1] = pl.BlockSpec((TM * SUBLANES, LANES), row)
        out_shape[1] = jax.ShapeDtypeStruct((t * SUBLANES, LANES), F32)
        in_specs.append(pl.BlockSpec((D_MODEL, LANES), const))
        args.append(rw)
        out_specs.append(pl.BlockSpec((TM, LANES), row))
        out_shape.append(jax.ShapeDtypeStruct((t, LANES), F32))
    return pl.pallas_call(
        kern,
        grid=(t // TM,),
        in_specs=in_specs,
        out_specs=out_specs,
        out_shape=out_shape,
        compiler_params=_params(("parallel",)),
        name="mix_proj_router" if with_router else "mix_proj",
    )(*args)


def _ffn(h, x, wg, wu, wd):
    t = x.shape[0]
    row = lambda i, j: (i, 0)
    return pl.pallas_call(
        _ffn_kernel,
        grid=(t // TM_FFN, D_FF // TF),
        in_specs=[pl.BlockSpec((TM_FFN, D_MODEL), row),
                  pl.BlockSpec((TM_FFN, D_MODEL), row),
                  pl.BlockSpec((D_MODEL, TF), lambda i, j: (0, j), pipeline_mode=pl.Buffered(1)),
                  pl.BlockSpec((D_MODEL, TF), lambda i, j: (0, j), pipeline_mode=pl.Buffered(1)),
                  pl.BlockSpec((TF, D_MODEL), lambda i, j: (j, 0), pipeline_mode=pl.Buffered(1))],
        out_specs=pl.BlockSpec((TM_FFN, D_MODEL), row),
        out_shape=jax.ShapeDtypeStruct((t, D_MODEL), F32),
        scratch_shapes=[pltpu.VMEM((TM_FFN, D_MODEL), F32)],
        compiler_params=_params(("parallel", "arbitrary")),
        name="ffn_dense",
    )(h, x, wg, wu, wd)


def _router(logits):
    t = logits.shape[0]
    row = lambda i: (i, 0)
    const = lambda i: (0, 0)
    r = jnp.arange(TR)
    tri = (r[None, :] < r[:, None]).astype(BF16)
    return pl.pallas_call(
        _router_kernel,
        grid=(t // TR,),
        in_specs=[pl.BlockSpec((TR, LANES), row),
                  pl.BlockSpec((TR, TR), const)],
        out_specs=[pl.BlockSpec((TR, LANES), row),
                   pl.BlockSpec((SUBLANES, TR), lambda i: (0, i)),
                   pl.BlockSpec((SUBLANES, LANES), const)],
        out_shape=[jax.ShapeDtypeStruct((t, LANES), F32),
                   jax.ShapeDtypeStruct((SUBLANES, t), F32),
                   jax.ShapeDtypeStruct((SUBLANES, LANES), F32)],
        scratch_shapes=[pltpu.VMEM((SUBLANES, LANES), F32)],
        compiler_params=_params(("arbitrary",)),
        name="router",
    )(logits, tri)


def _route_plan(fields, counts, t):
    e0 = fields[ROUTE_E0].astype(jnp.int32)
    e1 = fields[ROUTE_E1].astype(jnp.int32)
    r0 = fields[ROUTE_R0].astype(jnp.int32)
    r1 = fields[ROUTE_R1].astype(jnp.int32)
    cnt = counts[0, :N_EXPERTS].astype(jnp.int32)
    padded = ((cnt + TM_MOE - 1) // TM_MOE) * TM_MOE
    ends = jnp.cumsum(padded)
    starts = ends - padded
    experts = jnp.arange(N_EXPERTS, dtype=jnp.int32)
    pos0 = jnp.sum(jnp.where(e0[:, None] == experts[None, :], starts[None, :], 0), axis=1) + r0
    pos1 = jnp.sum(jnp.where(e1[:, None] == experts[None, :], starts[None, :], 0), axis=1) + r1
    n_tok_tiles = t // TM_MOE
    pos = jnp.concatenate([pos0.reshape(n_tok_tiles, 1, TM_MOE),
                           pos1.reshape(n_tok_tiles, 1, TM_MOE)], axis=2)
    n_tiles = 2 * t // TM_MOE + N_EXPERTS
    tile_start = jnp.arange(n_tiles, dtype=jnp.int32) * TM_MOE
    tile_valid = (tile_start < ends[-1]).astype(jnp.int32)
    tile_expert = jnp.sum((tile_start[:, None] >= ends[None, :]).astype(jnp.int32), axis=1)
    tile_expert = jnp.minimum(tile_expert, N_EXPERTS - 1)
    pad = (starts + cnt, padded - cnt)
    return pos, pad, tile_expert, tile_valid


ROW_TILE = (TM_MOE * SUBLANES, LANES)


def _dispatch(pad_start, pad_count, tile_valid, pos, h):
    t = h.shape[0] // SUBLANES
    n_rows = tile_valid.shape[0] * TM_MOE
    return pl.pallas_call(
        _dispatch_kernel,
        grid_spec=pltpu.PrefetchScalarGridSpec(
            num_scalar_prefetch=3,
            grid=(t // TM_MOE,),
            in_specs=[pl.BlockSpec((1, 1, 2 * TM_MOE), lambda i, ps, pc, tv: (i, 0, 0),
                                   memory_space=pltpu.SMEM),
                      pl.BlockSpec(ROW_TILE, lambda i, ps, pc, tv: (i, 0))],
            out_specs=pl.BlockSpec(memory_space=pl.ANY),
            scratch_shapes=[pltpu.VMEM(ROW_TILE, F32),
                            pltpu.SemaphoreType.DMA(()),
                            pltpu.SemaphoreType.DMA(())]),
        out_shape=jax.ShapeDtypeStruct((n_rows * SUBLANES, LANES), F32),
        compiler_params=_params(("arbitrary",)),
        name="moe_dispatch",
    )(pad_start, pad_count, tile_valid, pos, h)


def _moe_ffn(tile_expert, tile_valid, xs, wg, wu, wd):
    n_rows = xs.shape[0] // SUBLANES
    row = lambda g, j, te, tv: (g, 0)
    return pl.pallas_call(
        _moe_ffn_kernel,
        grid_spec=pltpu.PrefetchScalarGridSpec(
            num_scalar_prefetch=2,
            grid=(n_rows // TM_MOE, D_FF // TF_MOE),
            in_specs=[pl.BlockSpec(ROW_TILE, row),
                      pl.BlockSpec((1, D_MODEL, TF_MOE),
                                   lambda g, j, te, tv: (te[g], 0, j * tv[g])),
                      pl.BlockSpec((1, D_MODEL, TF_MOE),
                                   lambda g, j, te, tv: (te[g], 0, j * tv[g])),
                      pl.BlockSpec((1, TF_MOE, D_MODEL),
                                   lambda g, j, te, tv: (te[g], j * tv[g], 0))],
            out_specs=pl.BlockSpec(ROW_TILE, row),
            scratch_shapes=[pltpu.VMEM((TM_MOE, D_MODEL), F32)]),
        out_shape=jax.ShapeDtypeStruct(xs.shape, F32),
        compiler_params=_params(("parallel", "arbitrary")),
        name="moe_ffn",
    )(tile_expert, tile_valid, xs, wg, wu, wd)


def _combine(pos, x, route, fgain, ys, n_first):
    t = x.shape[0]
    n_a = n_first // TM_MOE
    row = lambda i: (i, 0)
    blk = (TM_MOE, D_MODEL)
    last = t // TM_MOE - 1
    pos_blk = (1, 1, 2 * TM_MOE)
    return pl.pallas_call(
        functools.partial(_combine_kernel, n_a=n_a),
        grid=(t // TM_MOE,),
        in_specs=[pl.BlockSpec(pos_blk, lambda i: (i, 0, 0), memory_space=pltpu.SMEM),
                  pl.BlockSpec(pos_blk, lambda i: (jnp.minimum(i + 1, last), 0, 0),
                               memory_space=pltpu.SMEM),
                  pl.BlockSpec(blk, row),
                  pl.BlockSpec((TM_MOE, LANES), row),
                  pl.BlockSpec((1, D_MODEL), lambda i: (0, 0)),
                  pl.BlockSpec(memory_space=pl.ANY)],
        out_specs=[pl.BlockSpec(blk, lambda i: (jnp.minimum(i, n_a - 1), 0)),
                   pl.BlockSpec(blk, lambda i: (jnp.maximum(i - n_a, 0), 0))],
        out_shape=[jax.ShapeDtypeStruct((n_first, D_MODEL), F32),
                   jax.ShapeDtypeStruct((t - n_first, D_MODEL), F32)],
        scratch_shapes=[pltpu.VMEM((2, 2) + ROW_TILE, F32),
                        pltpu.SemaphoreType.DMA((2,))],
        compiler_params=_params(("arbitrary",)),
        name="moe_combine",
    )(pos, pos, x, route, fgain, ys)


def kernel(x_prompt, x_sample, norm_mix, w_in, q_norm, k_norm, conv_w, w_oa, w_ob, w_o, norm_ffn,
           ffn_w_gate, ffn_w_up, ffn_w_down, router_w, moe_w_gate, moe_w_up, moe_w_down, final_norm):
    bp, sp, _ = x_prompt.shape
    bs, ss, _ = x_sample.shape
    n_prompt = bp * sp
    n_sample = bs * ss
    depth = norm_mix.shape[0]
    assert sp % TM == 0 and ss % TM == 0 and TM == TM_FFN == TM_MOE
    assert depth == 2 and ffn_w_gate.shape[0] == moe_w_gate.shape[0] == 1

    t_all = n_prompt + n_sample
    xparts = (x_prompt.reshape(n_prompt, D_MODEL), x_sample.reshape(n_sample, D_MODEL))

    cos, sin = _rope_tables(max(sp, ss))
    n_prompt_tiles, tiles4, tiles2 = n_prompt // TM, sp // TM, ss // TM
    pos_map = lambda i: (jnp.where(i < n_prompt_tiles, i % tiles4, i % tiles2), 0)
    idx = jnp.arange(LANES)
    bd = jnp.where(idx[:, None] // HEAD_DIM == idx[None, :] // HEAD_DIM,
                   1.0 / HEAD_DIM, 0.0).astype(BF16)

    jobs = []
    for l in range(depth):
        if l > 0:
            jobs.append((("w_in", l), w_in[l][None], l - 1))
        jobs += [((name, l), w[l][None], l)
                 for name, w in (("w_oa", w_oa), ("w_ob", w_ob), ("w_o", w_o))]
        named = ((("ffn_gate", ffn_w_gate), ("ffn_up", ffn_w_up), ("ffn_down", ffn_w_down))
                 if l % 2 == 0 else
                 (("moe_gate", moe_w_gate), ("moe_up", moe_w_up), ("moe_down", moe_w_down)))
        jobs += [((name, l), w[l // 2] if l % 2 else w[l // 2][None], l) for name, w in named]
    cast_plan = _plan_casts(jobs, depth)
    bf16_w = {("w_in", 0): w_in[0].astype(BF16)}

    for l in range(depth):
        gain = norm_mix[l][None, :]
        q, k, vt, cb, u, ga, gb = _in_proj(
            xparts, t_all, n_prompt_tiles, gain, bf16_w[("w_in", l)],
            _gain_rope_tables(cos, sin, q_norm[l] * Q_SCALE),
            _gain_rope_tables(cos, sin, k_norm[l]), bd, pos_map)
        shift = (Q_SCALE * HEAD_DIM * jnp.max(jnp.abs(q_norm[l]))
                 * jnp.max(jnp.abs(k_norm[l]))).reshape(1)
        bound_ok = 2.0 * shift[0] < MAX_SHIFT_GAP

        jobs_p, jobs_s = cast_plan[l]

        def attend(exact_max, tq_p, tq_s):
            def run(shift, q, k, vt, w_p, w_s):
                attn_p, done_p = _attention(shift, q, k, vt, w_p, row0=0, n_seq=bp, seq=sp,
                                            tq=tq_p, exact_max=exact_max)
                attn_s, done_s = _attention(shift, q, k, vt, w_s, row0=n_prompt, n_seq=bs,
                                            seq=ss, tq=tq_s, exact_max=exact_max)
                return (attn_p, attn_s), done_p + done_s
            return run

        attn_parts, done = lax.cond(
            bound_ok, attend(False, 256, 256), attend(True, 128, 256), shift, q, k, vt,
            [job[1] for job in jobs_p], [job[1] for job in jobs_s])
        for job, w16 in zip(jobs_p + jobs_s, done):
            bf16_w[job[0]] = w16 if job[0][0].startswith("moe") else w16[0]
        j = l // 2
        is_moe = l % 2 == 1
        rw = (jnp.pad(jnp.tile(router_w[j], (1, 2)), ((0, 0), (0, LANES - 2 * N_EXPERTS)))
              if is_moe else None)
        outs = _mix(attn_parts, cb, u, ga, gb, xparts, conv_w[l],
                    bf16_w[("w_oa", l)], bf16_w[("w_ob", l)], bf16_w[("w_o", l)],
                    norm_ffn[l][None, :], rw, n_prompt=n_prompt, seq_p=sp, seq_s=ss)
        if not is_moe:
            x, h2 = outs
            x = _ffn(h2, x, bf16_w[("ffn_gate", l)], bf16_w[("ffn_up", l)],
                     bf16_w[("ffn_down", l)])
            xparts = (x, x)
        else:
            x, h2, logits = outs
            route, fields, counts = _router(logits)
            pos, pad, tile_expert, tile_valid = _route_plan(fields, counts, t_all)
            xs = _dispatch(*pad, tile_valid, pos, h2)
            ys = _moe_ffn(tile_expert, tile_valid, xs, bf16_w[("moe_gate", l)],
                          bf16_w[("moe_up", l)], bf16_w[("moe_down", l)])
            xparts = _combine(pos, x, route, final_norm[None, :], ys, n_prompt)

    return (xparts[0].reshape(bp, sp, D_MODEL), xparts[1].reshape(bs, ss, D_MODEL))
```

```python
import functools
import math

import jax
import jax.numpy as jnp
from jax import lax
from jax.experimental import pallas as pl
from jax.experimental.pallas import tpu as pltpu

F32 = jnp.float32
BF16 = jnp.bfloat16

D_MODEL = 1024
N_HEADS = 16
N_KV_HEADS = 4
HEAD_DIM = 64
GROUP = N_HEADS // N_KV_HEADS
KV_WIDTH = N_KV_HEADS * HEAD_DIM
AXIS_DIM = HEAD_DIM // 2
ROPE_THETA = 10000.0
GRID_W = 64
D_FF = 3584
N_EXPERTS = 8
EPS = 1e-6
LANES = 128
QKV_WIDTH = D_MODEL + 2 * KV_WIDTH
REST_WIDTH = 5 * D_MODEL
Q_EXP_WIDTH = N_HEADS * LANES
KV_EXP_WIDTH = N_KV_HEADS * LANES
Q_SCALE = math.log2(math.e) / math.sqrt(HEAD_DIM)
SUBLANES = 8
BF16_SUBLANES = 16
ROT_HALF = AXIS_DIM // 2
V_ROWS = HEAD_DIM + BF16_SUBLANES
KEY_CHUNK = 512
ROW_UNROLL = SUBLANES
MAX_SHIFT_GAP = 100.0

TM = 512
TM_FFN = 512
TF = 3584
TF_MOE = 1792
TR = 512
TM_MOE = 512
VMEM_LIMIT = 56 * 1024 * 1024


def _params(sem):
    return pltpu.CompilerParams(dimension_semantics=sem, vmem_limit_bytes=VMEM_LIMIT)


def _rms(x, gain):
    return x * lax.rsqrt(jnp.mean(x * x, axis=-1, keepdims=True) + EPS) * gain


def _store_row_tiles(ref, x):
    rows = x.shape[0]
    for c in range(D_MODEL // LANES):
        ref[pl.ds(c, rows, stride=SUBLANES), :] = x[:, c * LANES:(c + 1) * LANES]


def _load_row_tiles(ref):
    rows = ref.shape[0] // SUBLANES
    return jnp.concatenate([ref[pl.ds(c, rows, stride=SUBLANES), :]
                            for c in range(D_MODEL // LANES)], axis=1)


def _pick(n_a, a_ref, b_ref):
    return jnp.where(pl.program_id(0) < n_a, a_ref[...], b_ref[...])


def _inproj_kernel(xa_ref, xb_ref, g_ref, w_ref, q_own_ref, q_other_ref, k_own_ref, k_other_ref,
                   bd_ref, q_ref, k_ref, vt_ref, cb_ref, u_ref, ga_ref, gb_ref, *, n_a):
    tm = xa_ref.shape[0]
    h = _rms(_pick(n_a, xa_ref, xb_ref), g_ref[...]).astype(BF16)
    p = jnp.dot(h, w_ref[:, :QKV_WIDTH], preferred_element_type=F32)
    rest = jnp.dot(h, w_ref[:, QKV_WIDTH:], preferred_element_type=F32)
    cb_ref[...] = rest[:, :D_MODEL].astype(BF16)
    u_ref[...] = (rest[:, D_MODEL:2 * D_MODEL] * rest[:, 2 * D_MODEL:3 * D_MODEL]).astype(BF16)
    ga_ref[...] = rest[:, 3 * D_MODEL:4 * D_MODEL].astype(BF16)
    gb_ref[...] = rest[:, 4 * D_MODEL:].astype(BF16)
    bd = bd_ref[...]
    lane = lax.broadcasted_iota(jnp.int32, (tm, LANES), 1)
    first_half = (lane & (AXIS_DIM - 1)) < ROT_HALF
    low_half = lane < HEAD_DIM

    def norm_rope(c, own, other):
        ms = jnp.dot((c * c).astype(BF16), bd, preferred_element_type=F32)
        partner = jnp.where(first_half, pltpu.roll(c, LANES - ROT_HALF, 1),
                            pltpu.roll(c, ROT_HALF, 1))
        return (c * own + partner * other) * lax.rsqrt(ms + EPS)

    is_aux = lane == HEAD_DIM
    zero = jnp.zeros((tm, LANES), F32)
    ones_aux = jnp.where(is_aux, 1.0, zero)

    q_own = q_own_ref[...]
    q_other = q_other_ref[...]
    for c in range(N_HEADS // 2):
        r = norm_rope(p[:, c * LANES:(c + 1) * LANES], q_own, q_other)
        even = jnp.where(low_half, r, zero)
        odd = jnp.where(low_half, pltpu.roll(r, HEAD_DIM, 1), zero)
        q_ref[:, (2 * c) * LANES:(2 * c + 1) * LANES] = even.astype(BF16)
        q_ref[:, (2 * c + 1) * LANES:(2 * c + 2) * LANES] = odd.astype(BF16)

    k_own = k_own_ref[...]
    k_other = k_other_ref[...]
    for c in range(KV_WIDTH // LANES):
        kn = norm_rope(p[:, D_MODEL + c * LANES:D_MODEL + (c + 1) * LANES], k_own, k_other)
        even = jnp.where(low_half, kn, ones_aux)
        odd = jnp.where(low_half, pltpu.roll(kn, HEAD_DIM, 1), ones_aux)
        k_ref[:, (2 * c) * LANES:(2 * c + 1) * LANES] = even.astype(BF16)
        k_ref[:, (2 * c + 1) * LANES:(2 * c + 2) * LANES] = odd.astype(BF16)
        vv = p[:, D_MODEL + KV_WIDTH + c * LANES:D_MODEL + KV_WIDTH + (c + 1) * LANES]
        even = jnp.where(low_half, vv, ones_aux)
        odd = jnp.where(low_half, pltpu.roll(vv, HEAD_DIM, 1), ones_aux)
        vt_ref[(2 * c) * LANES:(2 * c + 1) * LANES, :] = even.T.astype(BF16)
        vt_ref[(2 * c + 1) * LANES:(2 * c + 2) * LANES, :] = odd.T.astype(BF16)


def _attn_kernel(shift_ref, q_ref, k_ref, vt_ref, *refs, exact_max):
    n_cast = (len(refs) - 1) // 2
    o_ref = refs[n_cast]
    for src, dst in zip(refs[:n_cast], refs[n_cast + 1:]):
        dst[...] = src[...].astype(BF16)
    tq = q_ref.shape[0]
    is_aux_row = lax.broadcasted_iota(jnp.int32, (LANES, GROUP * tq), 0) == HEAD_DIM
    for j in range(N_KV_HEADS):
        qt = jnp.concatenate(
            [q_ref[:, h * LANES:(h + 1) * LANES].T for h in range(GROUP * j, GROUP * (j + 1))],
            axis=1)
        if exact_max:
            st = jnp.dot(k_ref[:, j * LANES:(j + 1) * LANES], qt,
                         preferred_element_type=F32)
            st = st - jnp.max(st, axis=0, keepdims=True)
            pt = jnp.exp2(st).astype(BF16)
            ot = jnp.dot(vt_ref[j * LANES:j * LANES + V_ROWS, :], pt,
                         preferred_element_type=F32)
        else:
            neg_shift = jnp.full(qt.shape, -shift_ref[0], F32).astype(BF16)
            qt = jnp.where(is_aux_row, neg_shift, qt)
            ot = jnp.zeros((V_ROWS, GROUP * tq), F32)
            for c in range(k_ref.shape[0] // KEY_CHUNK):
                rows = slice(c * KEY_CHUNK, (c + 1) * KEY_CHUNK)
                st = jnp.dot(k_ref[rows, j * LANES:(j + 1) * LANES], qt,
                             preferred_element_type=F32)
                pt = jnp.exp2(st).astype(BF16)
                ot = ot + jnp.dot(vt_ref[j * LANES:j * LANES + V_ROWS, rows], pt,
                                  preferred_element_type=F32)
        ot = ot[:HEAD_DIM] / ot[HEAD_DIM:HEAD_DIM + 1]
        for a in range(2):
            pair = jnp.concatenate([ot[:, (2 * a) * tq:(2 * a + 1) * tq],
                                    ot[:, (2 * a + 1) * tq:(2 * a + 2) * tq]], axis=0)
            c = 2 * j + a
            o_ref[:, c * LANES:(c + 1) * LANES] = pair.T.astype(BF16)


def _mix_kernel(attn_a_ref, attn_b_ref, cb_ref, u_ref, up_ref, un_ref, ga_ref, gb_ref,
                xa_ref, xb_ref, cw_ref, woa_ref, wob_ref, wo_ref, g_ref, *rest,
                n_prompt_tiles, tiles4, tiles2, with_router):
    if with_router:
        rw_ref, xo_ref, h2_ref, lg_ref = rest
    else:
        xo_ref, h2_ref = rest
    i = pl.program_id(0)
    tm = xa_ref.shape[0]
    is_prompt = i < n_prompt_tiles
    seq_start = jnp.where(is_prompt, i % tiles4 == 0, i % tiles2 == 0)
    seq_end = jnp.where(is_prompt, i % tiles4 == tiles4 - 1, i % tiles2 == tiles2 - 1)

    attn = _pick(n_prompt_tiles, attn_a_ref, attn_b_ref)
    ya = jnp.dot(attn, woa_ref[...], preferred_element_type=F32)

    u = u_ref[...].astype(F32)
    row = lax.broadcasted_iota(jnp.int32, u.shape, 0)
    prev_row = jnp.where(seq_start, 0.0, up_ref[SUBLANES - 1:SUBLANES, :].astype(F32))
    next_row = jnp.where(seq_end, 0.0, un_ref[0:1, :].astype(F32))
    u_prev = jnp.where(row == 0, prev_row, pltpu.roll(u, 1, 0))
    u_next = jnp.where(row == tm - 1, next_row, pltpu.roll(u, tm - 1, 0))
    cw = cw_ref[...]
    conv = cw[0:1, :] * u_prev + cw[1:2, :] * u + cw[2:3, :] * u_next
    yb_in = (cb_ref[...].astype(F32) * conv).astype(BF16)
    yb = jnp.dot(yb_in, wob_ref[...], preferred_element_type=F32)

    m = (jax.nn.sigmoid(ga_ref[...].astype(F32)) * ya
         + jax.nn.sigmoid(gb_ref[...].astype(F32)) * yb)
    xn = (_pick(n_prompt_tiles, xa_ref, xb_ref)
          + jnp.dot(m.astype(BF16), wo_ref[...], preferred_element_type=F32))
    xo_ref[...] = xn
    h2 = _rms(xn, g_ref[...])
    h2_ref[...] = h2.astype(h2_ref.dtype)
    if not with_router:
        return
    hi = h2.astype(BF16)
    lo = (h2 - hi.astype(F32)).astype(BF16)
    rw = rw_ref[...]
    rhi = rw.astype(BF16)
    rlo = (rw - rhi.astype(F32)).astype(BF16)
    lane = lax.broadcasted_iota(jnp.int32, rw.shape, 1)
    hi_terms = jnp.dot(hi, jnp.where(lane < N_EXPERTS, rhi, rlo), preferred_element_type=F32)
    lg_ref[...] = (hi_terms + pltpu.roll(hi_terms, LANES - N_EXPERTS, 1)
                   + jnp.dot(lo, rhi, preferred_element_type=F32))


def _ffn_kernel(h_ref, x_ref, wg_ref, wu_ref, wd_ref, o_ref, acc_ref):
    j = pl.program_id(1)

    @pl.when(j == 0)
    def _():
        acc_ref[...] = jnp.zeros_like(acc_ref)

    h = h_ref[...]
    g = jnp.dot(h, wg_ref[...], preferred_element_type=F32)
    u = jnp.dot(h, wu_ref[...], preferred_element_type=F32)
    a = (g * jax.nn.sigmoid(g) * u).astype(BF16)
    acc_ref[...] += jnp.dot(a, wd_ref[...], preferred_element_type=F32)

    @pl.when(j == pl.num_programs(1) - 1)
    def _():
        o_ref[...] = x_ref[...] + acc_ref[...]


ROUTE_E0, ROUTE_E1, ROUTE_R0, ROUTE_R1, ROUTE_W0, ROUTE_W1, ROUTE_S0, ROUTE_S1 = range(8)
STAT_COUNT, STAT_BASE, STAT_SEG = range(3)


def _lane_pick(x, lane, k):
    return jnp.sum(jnp.where(lane == k, x, 0.0), axis=-1, keepdims=True)


def _router_kernel(lg_ref, tri_ref, route_ref, fields_ref, stats_ref, count_ref, base_ref):
    i = pl.program_id(0)

    @pl.when(i == 0)
    def _():
        base_ref[...] = jnp.zeros_like(base_ref)

    lg = lg_ref[...]
    lane = lax.broadcasted_iota(jnp.int32, lg.shape, 1)
    neg = jnp.float32(-jnp.inf)
    l1 = jnp.where(lane < N_EXPERTS, lg, neg)
    m1 = jnp.max(l1, axis=-1, keepdims=True)
    i1 = jnp.min(jnp.where(l1 == m1, lane, LANES), axis=-1, keepdims=True)
    l2 = jnp.where(lane == i1, neg, l1)
    m2 = jnp.max(l2, axis=-1, keepdims=True)
    i2 = jnp.min(jnp.where(l2 == m2, lane, LANES), axis=-1, keepdims=True)
    e = jnp.exp(m2 - m1)
    w1 = 1.0 / (1.0 + e)
    w2 = e / (1.0 + e)

    hot1 = lane == i1
    hot2 = lane == i2
    onehot = jnp.where(hot1 | hot2, 1.0, 0.0)
    base = base_ref[...]
    local = jnp.dot(tri_ref[...], onehot.astype(BF16), preferred_element_type=F32)
    prefix = local + base[0:1, :]
    r1 = jnp.sum(jnp.where(hot1, prefix, 0.0), axis=-1, keepdims=True)
    r2 = jnp.sum(jnp.where(hot2, prefix, 0.0), axis=-1, keepdims=True)
    in_tile = jnp.broadcast_to(jnp.sum(onehot, axis=0, keepdims=True), base.shape)
    base_ref[...] = base + in_tile
    count_ref[...] = base + in_tile
    lane8 = lax.broadcasted_iota(jnp.int32, base.shape, 1)
    incl = in_tile
    for step in (1, 2, 4):
        incl = incl + jnp.where(lane8 >= step, pltpu.roll(incl, step, 1), 0.0)
    seg = incl - in_tile
    slot = local + seg[0:1, :]
    s1 = jnp.sum(jnp.where(hot1, slot, 0.0), axis=-1, keepdims=True)
    s2 = jnp.sum(jnp.where(hot2, slot, 0.0), axis=-1, keepdims=True)
    sub8 = lax.broadcasted_iota(jnp.int32, base.shape, 0)
    stats_ref[...] = jnp.where(sub8 == STAT_COUNT, in_tile,
                               jnp.where(sub8 == STAT_BASE, base,
                                         jnp.where(sub8 == STAT_SEG, seg, 0.0)))

    rec = jnp.where(lane == ROUTE_E0, i1.astype(F32), 0.0)
    rec = jnp.where(lane == ROUTE_E1, i2.astype(F32), rec)
    rec = jnp.where(lane == ROUTE_R0, r1, rec)
    rec = jnp.where(lane == ROUTE_R1, r2, rec)
    rec = jnp.where(lane == ROUTE_W0, w1, rec)
    rec = jnp.where(lane == ROUTE_W1, w2, rec)
    rec = jnp.where(lane == ROUTE_S0, s1, rec)
    rec = jnp.where(lane == ROUTE_S1, s2, rec)
    route_ref[...] = rec
    fields_ref[...] = rec.T[:fields_ref.shape[0], :]


def _row_copy(src, src_row, dst, dst_row, sem):
    return pltpu.make_async_copy(
        src.at[pl.ds(pl.multiple_of(src_row * SUBLANES, SUBLANES), SUBLANES)],
        dst.at[pl.ds(pl.multiple_of(dst_row * SUBLANES, SUBLANES), SUBLANES)], sem)


def _dispatch_kernel(pad_start_ref, pad_count_ref, tile_valid_ref, tab_ref, fields_ref, h_ref,
                     xs_ref, zero_ref, xc_ref, sem, zsem):
    tm = h_ref.shape[0]
    tile = tm * SUBLANES
    n_tiles = xs_ref.shape[0] // tile

    @pl.when(pl.program_id(0) == 0)
    def _():
        zero_ref[...] = jnp.zeros_like(zero_ref)
        for e in range(N_EXPERTS):
            start = pad_start_ref[e]
            count = pad_count_ref[e]

            def zero_issue(r, carry):
                _row_copy(zero_ref, 0, xs_ref, start + r, zsem).start()
                return carry

            def zero_wait(r, carry):
                _row_copy(zero_ref, 0, xs_ref, 0, zsem).wait()
                return carry

            lax.fori_loop(0, count, zero_issue, 0)
            lax.fori_loop(0, count, zero_wait, 0)
        for g in range(n_tiles - N_EXPERTS, n_tiles):
            @pl.when(tile_valid_ref[g] == 0)
            def _():
                fill = pltpu.make_async_copy(zero_ref, xs_ref.at[pl.ds(g * tile, tile)], zsem)
                fill.start()
                fill.wait()

    grouped_row = lax.broadcasted_iota(jnp.int32, (2 * tm, tm), 0)
    s0 = fields_ref[ROUTE_S0:ROUTE_S0 + 1, :].astype(jnp.int32)
    s1 = fields_ref[ROUTE_S1:ROUTE_S1 + 1, :].astype(jnp.int32)
    select = jnp.where((grouped_row == s0) | (grouped_row == s1), 1.0, 0.0).astype(BF16)
    _store_row_tiles(xc_ref, jnp.dot(select, h_ref[...], preferred_element_type=F32))

    for e in range(N_EXPERTS):
        count = tab_ref[0, 0, e]
        src0 = tab_ref[0, 0, N_EXPERTS + e]
        dst0 = tab_ref[0, 0, 2 * N_EXPERTS + e]
        for bit in reversed(range(tm.bit_length())):
            size = 1 << bit

            @pl.when((count & size) != 0)
            def _():
                done = count & ~(2 * size - 1)
                src = pl.multiple_of((src0 + done) * SUBLANES, SUBLANES)
                dst = pl.multiple_of((dst0 + done) * SUBLANES, SUBLANES)
                pltpu.make_async_copy(xc_ref.at[pl.ds(src, size * SUBLANES)],
                                      xs_ref.at[pl.ds(dst, size * SUBLANES)], sem).start()
    pltpu.make_async_copy(xc_ref, xs_ref.at[pl.ds(0, 2 * tile)], sem).wait()


def _moe_ffn_kernel(te_ref, tv_ref, x_ref, wg_ref, wu_ref, wd_ref, y_ref, acc_ref):
    del te_ref
    g_idx = pl.program_id(0)
    j = pl.program_id(1)
    valid = tv_ref[g_idx] > 0

    @pl.when(j == 0)
    def _():
        acc_ref[...] = jnp.zeros_like(acc_ref)

    @pl.when(valid)
    def _():
        h = _load_row_tiles(x_ref).astype(BF16)
        g = jnp.dot(h, wg_ref[0], preferred_element_type=F32)
        u = jnp.dot(h, wu_ref[0], preferred_element_type=F32)
        a = (g * jax.nn.sigmoid(g) * u).astype(BF16)
        acc_ref[...] += jnp.dot(a, wd_ref[0], preferred_element_type=F32)

    @pl.when(j == pl.num_programs(1) - 1)
    def _():
        _store_row_tiles(y_ref, acc_ref[...])


def _combine_kernel(pos_ref, nxt_ref, x_ref, route_ref, fg_ref, y_hbm, oa_ref, ob_ref,
                    ybuf, sem, *, n_a):
    i = pl.program_id(0)
    tc = x_ref.shape[0]
    cur = i % 2

    def issue_tile(p_ref, par):
        def issue(blk, carry):
            for k in range(ROW_UNROLL):
                r = blk * ROW_UNROLL + k
                _row_copy(y_hbm, p_ref[0, 0, r], ybuf.at[par, 0], r, sem.at[par]).start()
                _row_copy(y_hbm, p_ref[0, 0, tc + r], ybuf.at[par, 1], r, sem.at[par]).start()
            return carry
        lax.fori_loop(0, tc // ROW_UNROLL, issue, 0)

    @pl.when(i == 0)
    def _():
        issue_tile(pos_ref, 0)

    @pl.when(i + 1 < pl.num_programs(0))
    def _():
        issue_tile(nxt_ref, 1 - cur)

    for slot in range(2):
        pltpu.make_async_copy(y_hbm.at[pl.ds(0, tc * SUBLANES)], ybuf.at[cur, slot],
                              sem.at[cur]).wait()
    route = route_ref[...]
    lane = lax.broadcasted_iota(jnp.int32, route.shape, 1)
    w0 = _lane_pick(route, lane, ROUTE_W0)
    w1 = _lane_pick(route, lane, ROUTE_W1)
    y0 = _load_row_tiles(ybuf.at[cur, 0])
    y1 = _load_row_tiles(ybuf.at[cur, 1])
    out = _rms(x_ref[...] + (w0 * y0 + w1 * y1), fg_ref[...])
    in_a = i < n_a

    @pl.when(in_a)
    def _():
        oa_ref[...] = out

    @pl.when(jnp.logical_not(in_a))
    def _():
        ob_ref[...] = out


def _rope_tables(max_seq):
    t = jnp.arange(max_seq, dtype=jnp.int32)
    row = (t // GRID_W).astype(F32)
    col = (t % GRID_W).astype(F32)
    inv = 1.0 / (ROPE_THETA ** (jnp.arange(0, AXIS_DIM, 2, dtype=F32) / AXIS_DIM))
    ar = row[:, None] * inv[None, :]
    ac = col[:, None] * inv[None, :]
    cos64 = jnp.concatenate([jnp.cos(ar), jnp.cos(ar), jnp.cos(ac), jnp.cos(ac)], axis=-1)
    sin64 = jnp.concatenate([-jnp.sin(ar), jnp.sin(ar), -jnp.sin(ac), jnp.sin(ac)], axis=-1)
    return cos64, sin64


def _gain_rope_tables(cos64, sin64, gain):
    half = AXIS_DIM // 2
    partner_gain = gain.reshape(-1, 2, half)[:, ::-1, :].reshape(-1)
    return (jnp.tile(cos64 * gain[None, :], (1, 2)),
            jnp.tile(sin64 * partner_gain[None, :], (1, 2)))


def _two_part_specs(block, n_a, parts):
    off = 0 if parts[0] is parts[1] else n_a
    return [pl.BlockSpec(block, lambda i: (jnp.minimum(i, n_a - 1), 0)),
            pl.BlockSpec(block, lambda i: (jnp.maximum(i, n_a) - off, 0))]


def _in_proj(xparts, t, n_a, gain, w, q_tables, k_tables, bd, pos_map):
    row = lambda i: (i, 0)
    const = lambda i: (0, 0)
    table = pl.BlockSpec((TM, LANES), pos_map)
    act = pl.BlockSpec((TM, D_MODEL), row)
    act_shape = jax.ShapeDtypeStruct((t, D_MODEL), BF16)
    return pl.pallas_call(
        functools.partial(_inproj_kernel, n_a=n_a),
        grid=(t // TM,),
        in_specs=_two_part_specs((TM, D_MODEL), n_a, xparts) + [
                  pl.BlockSpec((1, D_MODEL), const),
                  pl.BlockSpec((D_MODEL, QKV_WIDTH + REST_WIDTH), const,
                               pipeline_mode=pl.Buffered(1)),
                  table, table, table, table,
                  pl.BlockSpec((LANES, LANES), const)],
        out_specs=[pl.BlockSpec((TM, Q_EXP_WIDTH), row),
                   pl.BlockSpec((TM, KV_EXP_WIDTH), row),
                   pl.BlockSpec((KV_EXP_WIDTH, TM), lambda i: (0, i)),
                   act, act, act, act],
        out_shape=[jax.ShapeDtypeStruct((t, Q_EXP_WIDTH), BF16),
                   jax.ShapeDtypeStruct((t, KV_EXP_WIDTH), BF16),
                   jax.ShapeDtypeStruct((KV_EXP_WIDTH, t), BF16),
                   act_shape, act_shape, act_shape, act_shape],
        compiler_params=_params(("parallel",)),
        name="in_proj",
    )(*xparts, gain, w, *q_tables, *k_tables, bd)


def _attention(shift, q, k, vt, casts, *, row0, n_seq, seq, tq, exact_max):
    nq = seq // tq
    q0 = row0 // tq
    s0 = row0 // seq
    n_steps = n_seq * nq
    step_blk = lambda b, i, km: (b * nq + i, 0, 0)
    rides = [(w.shape[0] * w.shape[1]) % (n_steps * BF16_SUBLANES) == 0 for w in casts]
    views = [w.reshape(n_steps, w.shape[0] * w.shape[1] // n_steps, w.shape[2])
             for w, ok in zip(casts, rides) if ok]
    cast_specs = [pl.BlockSpec((1,) + v.shape[1:], step_blk) for v in views]
    outs = pl.pallas_call(
        functools.partial(_attn_kernel, exact_max=exact_max),
        grid_spec=pltpu.PrefetchScalarGridSpec(
            num_scalar_prefetch=1,
            grid=(n_seq, nq),
            in_specs=[pl.BlockSpec((tq, Q_EXP_WIDTH), lambda b, i, km: (q0 + b * nq + i, 0)),
                      pl.BlockSpec((seq, KV_EXP_WIDTH), lambda b, i, km: (s0 + b, 0)),
                      pl.BlockSpec((KV_EXP_WIDTH, seq), lambda b, i, km: (0, s0 + b))]
                     + cast_specs,
            out_specs=[pl.BlockSpec((tq, D_MODEL), lambda b, i, km: (b * nq + i, 0))]
                      + cast_specs),
        out_shape=[jax.ShapeDtypeStruct((n_seq * seq, D_MODEL), BF16)]
                  + [jax.ShapeDtypeStruct(v.shape, BF16) for v in views],
        compiler_params=_params(("parallel", "parallel")),
        name=f"attention_s{seq}" + ("_exact" if exact_max else ""),
    )(shift, q, k, vt, *views)
    carried = iter(outs[1:])
    return outs[0], [next(carried).reshape(w.shape) if ok else w.astype(BF16)
                     for w, ok in zip(casts, rides)]


def _plan_casts(jobs, depth, call_weights=(2.0, 1.0)):
    plan = []
    pending = sorted(jobs, key=lambda job: job[2])
    for l in range(depth):
        budget = sum(job[1].size for job in pending) / (depth - l)
        take, used = [], 0
        for job in pending:
            if job[2] == l or used + job[1].size <= budget:
                take.append(job)
                used += job[1].size
        pending = [job for job in pending if all(job is not t for t in take)]
        calls = [[] for _ in call_weights]
        loads = [0.0] * len(call_weights)
        for job in sorted(take, key=lambda job: -job[1].size):
            c = min(range(len(calls)), key=lambda c: loads[c] / call_weights[c])
            calls[c].append(job)
            loads[c] += job[1].size
        plan.append(calls)
    return plan


def _mix(attn_parts, cb, u, ga, gb, xparts, cw, woa, wob, wo, gain, rw, *, n_prompt, seq_p, seq_s):
    t = cb.shape[0]
    n_a = n_prompt // TM
    row = lambda i: (i, 0)
    const = lambda i: (0, 0)
    sub = TM // SUBLANES
    last = t // SUBLANES - 1
    act = pl.BlockSpec((TM, D_MODEL), row)
    wspec = pl.BlockSpec((D_MODEL, D_MODEL), const)
    with_router = rw is not None
    kern = functools.partial(_mix_kernel, n_prompt_tiles=n_prompt // TM,
                             tiles4=seq_p // TM, tiles2=seq_s // TM, with_router=with_router)
    in_specs = (_two_part_specs((TM, D_MODEL), n_a, attn_parts) + [
                act, act,
                pl.BlockSpec((SUBLANES, D_MODEL), lambda i: (jnp.maximum(i * sub - 1, 0), 0)),
                pl.BlockSpec((SUBLANES, D_MODEL),
                             lambda i: (jnp.minimum((i + 1) * sub, last), 0)),
                act, act]
                + _two_part_specs((TM, D_MODEL), n_a, xparts) + [
                pl.BlockSpec((3, D_MODEL), const),
                wspec, wspec, wspec,
                pl.BlockSpec((1, D_MODEL), const)])
    args = [*attn_parts, cb, u, u, u, ga, gb, *xparts, cw, woa, wob, wo, gain]
    out_specs = [act, act]
    out_shape = [jax.ShapeDtypeStruct((t, D_MODEL), F32),
                 jax.ShapeDtypeStruct((t, D_MODEL), BF16)]
    if with_router:
        in_specs.append(pl.BlockSpec((D_MODEL, LANES), const))
        args.append(rw)
        out_specs.append(pl.BlockSpec((TM, LANES), row))
        out_shape.append(jax.ShapeDtypeStruct((t, LANES), F32))
    return pl.pallas_call(
        kern,
        grid=(t // TM,),
        in_specs=in_specs,
        out_specs=out_specs,
        out_shape=out_shape,
        compiler_params=_params(("parallel",)),
        name="mix_proj_router" if with_router else "mix_proj",
    )(*args)


def _ffn(h, x, wg, wu, wd):
    t = x.shape[0]
    row = lambda i, j: (i, 0)
    return pl.pallas_call(
        _ffn_kernel,
        grid=(t // TM_FFN, D_FF // TF),
        in_specs=[pl.BlockSpec((TM_FFN, D_MODEL), row),
                  pl.BlockSpec((TM_FFN, D_MODEL), row),
                  pl.BlockSpec((D_MODEL, TF), lambda i, j: (0, j), pipeline_mode=pl.Buffered(1)),
                  pl.BlockSpec((D_MODEL, TF), lambda i, j: (0, j), pipeline_mode=pl.Buffered(1)),
                  pl.BlockSpec((TF, D_MODEL), lambda i, j: (j, 0), pipeline_mode=pl.Buffered(1))],
        out_specs=pl.BlockSpec((TM_FFN, D_MODEL), row),
        out_shape=jax.ShapeDtypeStruct((t, D_MODEL), F32),
        scratch_shapes=[pltpu.VMEM((TM_FFN, D_MODEL), F32)],
        compiler_params=_params(("parallel", "arbitrary")),
        name="ffn_dense",
    )(h, x, wg, wu, wd)


def _router(logits):
    t = logits.shape[0]
    row = lambda i: (i, 0)
    const = lambda i: (0, 0)
    r = jnp.arange(TR)
    tri = (r[None, :] < r[:, None]).astype(BF16)
    return pl.pallas_call(
        _router_kernel,
        grid=(t // TR,),
        in_specs=[pl.BlockSpec((TR, LANES), row),
                  pl.BlockSpec((TR, TR), const)],
        out_specs=[pl.BlockSpec((TR, LANES), row),
                   pl.BlockSpec((SUBLANES, TR), lambda i: (0, i)),
                   pl.BlockSpec((SUBLANES, LANES), row),
                   pl.BlockSpec((SUBLANES, LANES), const)],
        out_shape=[jax.ShapeDtypeStruct((t, LANES), F32),
                   jax.ShapeDtypeStruct((SUBLANES, t), F32),
                   jax.ShapeDtypeStruct((t // TR * SUBLANES, LANES), F32),
                   jax.ShapeDtypeStruct((SUBLANES, LANES), F32)],
        scratch_shapes=[pltpu.VMEM((SUBLANES, LANES), F32)],
        compiler_params=_params(("arbitrary",)),
        name="router",
    )(logits, tri)


def _route_plan(fields, stats, counts, t):
    e0 = fields[ROUTE_E0].astype(jnp.int32)
    e1 = fields[ROUTE_E1].astype(jnp.int32)
    r0 = fields[ROUTE_R0].astype(jnp.int32)
    r1 = fields[ROUTE_R1].astype(jnp.int32)
    cnt = counts[0, :N_EXPERTS].astype(jnp.int32)
    padded = ((cnt + TM_MOE - 1) // TM_MOE) * TM_MOE
    ends = jnp.cumsum(padded)
    starts = ends - padded
    experts = jnp.arange(N_EXPERTS, dtype=jnp.int32)
    pos0 = jnp.sum(jnp.where(e0[:, None] == experts[None, :], starts[None, :], 0), axis=1) + r0
    pos1 = jnp.sum(jnp.where(e1[:, None] == experts[None, :], starts[None, :], 0), axis=1) + r1
    n_tok_tiles = t // TM_MOE
    pos = jnp.concatenate([pos0.reshape(n_tok_tiles, 1, TM_MOE),
                           pos1.reshape(n_tok_tiles, 1, TM_MOE)], axis=2)
    n_tiles = 2 * t // TM_MOE + N_EXPERTS
    tile_start = jnp.arange(n_tiles, dtype=jnp.int32) * TM_MOE
    tile_valid = (tile_start < ends[-1]).astype(jnp.int32)
    tile_expert = jnp.sum((tile_start[:, None] >= ends[None, :]).astype(jnp.int32), axis=1)
    tile_expert = jnp.minimum(tile_expert, N_EXPERTS - 1)
    pad = (starts + cnt, padded - cnt)
    stats = stats.reshape(n_tok_tiles, SUBLANES, LANES)[:, :, :N_EXPERTS].astype(jnp.int32)
    table = jnp.concatenate([stats[:, STAT_COUNT], stats[:, STAT_SEG],
                             stats[:, STAT_BASE] + starts[None, :]], axis=1)
    return pos, pad, table[:, None, :], tile_expert, tile_valid


ROW_TILE = (TM_MOE * SUBLANES, LANES)


def _dispatch(pad_start, pad_count, tile_valid, table, fields, h):
    t = h.shape[0]
    n_rows = tile_valid.shape[0] * TM_MOE
    return pl.pallas_call(
        _dispatch_kernel,
        grid_spec=pltpu.PrefetchScalarGridSpec(
            num_scalar_prefetch=3,
            grid=(t // TM_MOE,),
            in_specs=[pl.BlockSpec((1, 1, 3 * N_EXPERTS), lambda i, ps, pc, tv: (i, 0, 0),
                                   memory_space=pltpu.SMEM),
                      pl.BlockSpec((SUBLANES, TM_MOE), lambda i, ps, pc, tv: (0, i)),
                      pl.BlockSpec((TM_MOE, D_MODEL), lambda i, ps, pc, tv: (i, 0))],
            out_specs=pl.BlockSpec(memory_space=pl.ANY),
            scratch_shapes=[pltpu.VMEM(ROW_TILE, F32),
                            pltpu.VMEM((2 * ROW_TILE[0], LANES), F32),
                            pltpu.SemaphoreType.DMA(()),
                            pltpu.SemaphoreType.DMA(())]),
        out_shape=jax.ShapeDtypeStruct((n_rows * SUBLANES, LANES), F32),
        compiler_params=_params(("arbitrary",)),
        name="moe_dispatch",
    )(pad_start, pad_count, tile_valid, table, fields, h)


def _moe_ffn(tile_expert, tile_valid, xs, wg, wu, wd):
    n_rows = xs.shape[0] // SUBLANES
    row = lambda g, j, te, tv: (g, 0)
    return pl.pallas_call(
        _moe_ffn_kernel,
        grid_spec=pltpu.PrefetchScalarGridSpec(
            num_scalar_prefetch=2,
            grid=(n_rows // TM_MOE, D_FF // TF_MOE),
            in_specs=[pl.BlockSpec(ROW_TILE, row),
                      pl.BlockSpec((1, D_MODEL, TF_MOE),
                                   lambda g, j, te, tv: (te[g], 0, j * tv[g])),
                      pl.BlockSpec((1, D_MODEL, TF_MOE),
                                   lambda g, j, te, tv: (te[g], 0, j * tv[g])),
                      pl.BlockSpec((1, TF_MOE, D_MODEL),
                                   lambda g, j, te, tv: (te[g], j * tv[g], 0))],
            out_specs=pl.BlockSpec(ROW_TILE, row),
            scratch_shapes=[pltpu.VMEM((TM_MOE, D_MODEL), F32)]),
        out_shape=jax.ShapeDtypeStruct(xs.shape, F32),
        compiler_params=_params(("parallel", "arbitrary")),
        name="moe_ffn",
    )(tile_expert, tile_valid, xs, wg, wu, wd)


def _combine(pos, x, route, fgain, ys, n_first):
    t = x.shape[0]
    n_a = n_first // TM_MOE
    row = lambda i: (i, 0)
    blk = (TM_MOE, D_MODEL)
    last = t // TM_MOE - 1
    pos_blk = (1, 1, 2 * TM_MOE)
    return pl.pallas_call(
        functools.partial(_combine_kernel, n_a=n_a),
        grid=(t // TM_MOE,),
        in_specs=[pl.BlockSpec(pos_blk, lambda i: (i, 0, 0), memory_space=pltpu.SMEM),
                  pl.BlockSpec(pos_blk, lambda i: (jnp.minimum(i + 1, last), 0, 0),
                               memory_space=pltpu.SMEM),
                  pl.BlockSpec(blk, row),
                  pl.BlockSpec((TM_MOE, LANES), row),
                  pl.BlockSpec((1, D_MODEL), lambda i: (0, 0)),
                  pl.BlockSpec(memory_space=pl.ANY)],
        out_specs=[pl.BlockSpec(blk, lambda i: (jnp.minimum(i, n_a - 1), 0)),
                   pl.BlockSpec(blk, lambda i: (jnp.maximum(i - n_a, 0), 0))],
        out_shape=[jax.ShapeDtypeStruct((n_first, D_MODEL), F32),
                   jax.ShapeDtypeStruct((t - n_first, D_MODEL), F32)],
        scratch_shapes=[pltpu.VMEM((2, 2) + ROW_TILE, F32),
                        pltpu.SemaphoreType.DMA((2,))],
        compiler_params=_params(("arbitrary",)),
        name="moe_combine",
    )(pos, pos, x, route, fgain, ys)


def kernel(x_prompt, x_sample, norm_mix, w_in, q_norm, k_norm, conv_w, w_oa, w_ob, w_o, norm_ffn,
           ffn_w_gate, ffn_w_up, ffn_w_down, router_w, moe_w_gate, moe_w_up, moe_w_down, final_norm):
    bp, sp, _ = x_prompt.shape
    bs, ss, _ = x_sample.shape
    n_prompt = bp * sp
    n_sample = bs * ss
    depth = norm_mix.shape[0]
    assert sp % TM == 0 and ss % TM == 0 and TM == TM_FFN == TM_MOE
    assert depth == 2 and ffn_w_gate.shape[0] == moe_w_gate.shape[0] == 1

    t_all = n_prompt + n_sample
    xparts = (x_prompt.reshape(n_prompt, D_MODEL), x_sample.reshape(n_sample, D_MODEL))

    cos, sin = _rope_tables(max(sp, ss))
    n_prompt_tiles, tiles4, tiles2 = n_prompt // TM, sp // TM, ss // TM
    pos_map = lambda i: (jnp.where(i < n_prompt_tiles, i % tiles4, i % tiles2), 0)
    idx = jnp.arange(LANES)
    bd = jnp.where(idx[:, None] // HEAD_DIM == idx[None, :] // HEAD_DIM,
                   1.0 / HEAD_DIM, 0.0).astype(BF16)

    jobs = []
    for l in range(depth):
        if l > 0:
            jobs.append((("w_in", l), w_in[l][None], l - 1))
        jobs += [((name, l), w[l][None], l)
                 for name, w in (("w_oa", w_oa), ("w_ob", w_ob), ("w_o", w_o))]
        named = ((("ffn_gate", ffn_w_gate), ("ffn_up", ffn_w_up), ("ffn_down", ffn_w_down))
                 if l % 2 == 0 else
                 (("moe_gate", moe_w_gate), ("moe_up", moe_w_up), ("moe_down", moe_w_down)))
        jobs += [((name, l), w[l // 2] if l % 2 else w[l // 2][None], l) for name, w in named]
    cast_plan = _plan_casts(jobs, depth)
    bf16_w = {("w_in", 0): w_in[0].astype(BF16)}

    for l in range(depth):
        gain = norm_mix[l][None, :]
        q, k, vt, cb, u, ga, gb = _in_proj(
            xparts, t_all, n_prompt_tiles, gain, bf16_w[("w_in", l)],
            _gain_rope_tables(cos, sin, q_norm[l] * Q_SCALE),
            _gain_rope_tables(cos, sin, k_norm[l]), bd, pos_map)
        shift = (Q_SCALE * HEAD_DIM * jnp.max(jnp.abs(q_norm[l]))
                 * jnp.max(jnp.abs(k_norm[l]))).reshape(1)
        bound_ok = 2.0 * shift[0] < MAX_SHIFT_GAP

        jobs_p, jobs_s = cast_plan[l]

        def attend(exact_max, tq_p, tq_s):
            def run(shift, q, k, vt, w_p, w_s):
                attn_p, done_p = _attention(shift, q, k, vt, w_p, row0=0, n_seq=bp, seq=sp,
                                            tq=tq_p, exact_max=exact_max)
                attn_s, done_s = _attention(shift, q, k, vt, w_s, row0=n_prompt, n_seq=bs,
                                            seq=ss, tq=tq_s, exact_max=exact_max)
                return (attn_p, attn_s), done_p + done_s
            return run

        attn_parts, done = lax.cond(
            bound_ok, attend(False, 256, 256), attend(True, 128, 256), shift, q, k, vt,
            [job[1] for job in jobs_p], [job[1] for job in jobs_s])
        for job, w16 in zip(jobs_p + jobs_s, done):
            bf16_w[job[0]] = w16 if job[0][0].startswith("moe") else w16[0]
        j = l // 2
        is_moe = l % 2 == 1
        rw = (jnp.pad(jnp.tile(router_w[j], (1, 2)), ((0, 0), (0, LANES - 2 * N_EXPERTS)))
              if is_moe else None)
        outs = _mix(attn_parts, cb, u, ga, gb, xparts, conv_w[l],
                    bf16_w[("w_oa", l)], bf16_w[("w_ob", l)], bf16_w[("w_o", l)],
                    norm_ffn[l][None, :], rw, n_prompt=n_prompt, seq_p=sp, seq_s=ss)
        if not is_moe:
            x, h2 = outs
            x = _ffn(h2, x, bf16_w[("ffn_gate", l)], bf16_w[("ffn_up", l)],
                     bf16_w[("ffn_down", l)])
            xparts = (x, x)
        else:
            x, h2, logits = outs
            route, fields, stats, counts = _router(logits)
            pos, pad, table, tile_expert, tile_valid = _route_plan(fields, stats, counts, t_all)
            xs = _dispatch(*pad, tile_valid, table, fields, h2)
            ys = _moe_ffn(tile_expert, tile_valid, xs, bf16_w[("moe_gate", l)],
                          bf16_w[("moe_up", l)], bf16_w[("moe_down", l)])
            xparts = _combine(pos, x, route, final_norm[None, :], ys, n_prompt)

    return (xparts[0].reshape(bp, sp, D_MODEL), xparts[1].reshape(bs, ss, D_MODEL))
```

```python
import functools
import math

import jax
import jax.numpy as jnp
from jax import lax
from jax.experimental import pallas as pl
from jax.experimental.pallas import tpu as pltpu

F32 = jnp.float32
BF16 = jnp.bfloat16

D_MODEL = 1024
N_HEADS = 16
N_KV_HEADS = 4
HEAD_DIM = 64
GROUP = N_HEADS // N_KV_HEADS
KV_WIDTH = N_KV_HEADS * HEAD_DIM
AXIS_DIM = HEAD_DIM // 2
ROPE_THETA = 10000.0
GRID_W = 64
D_FF = 3584
N_EXPERTS = 8
EPS = 1e-6
LANES = 128
QKV_WIDTH = D_MODEL + 2 * KV_WIDTH
REST_WIDTH = 5 * D_MODEL
Q_EXP_WIDTH = N_HEADS * LANES
KV_EXP_WIDTH = N_KV_HEADS * LANES
Q_SCALE = math.log2(math.e) / math.sqrt(HEAD_DIM)
SUBLANES = 8
BF16_SUBLANES = 16
ROT_HALF = AXIS_DIM // 2
V_ROWS = HEAD_DIM + BF16_SUBLANES
KEY_CHUNK = 512
ROW_UNROLL = SUBLANES
MAX_SHIFT_GAP = 100.0

TM = 512
TM_FFN = 512
TF = 3584
TF_MOE = 1792
TR = 512
TM_MOE = 512
VMEM_LIMIT = 56 * 1024 * 1024


def _params(sem):
    return pltpu.CompilerParams(dimension_semantics=sem, vmem_limit_bytes=VMEM_LIMIT)


def _rms(x, gain):
    return x * lax.rsqrt(jnp.mean(x * x, axis=-1, keepdims=True) + EPS) * gain


def _store_row_tiles(ref, x):
    rows = x.shape[0]
    for c in range(D_MODEL // LANES):
        ref[pl.ds(c, rows, stride=SUBLANES), :] = x[:, c * LANES:(c + 1) * LANES]


def _load_row_tiles(ref):
    rows = ref.shape[0] // SUBLANES
    return jnp.concatenate([ref[pl.ds(c, rows, stride=SUBLANES), :]
                            for c in range(D_MODEL // LANES)], axis=1)


def _pick(n_a, a_ref, b_ref):
    return jnp.where(pl.program_id(0) < n_a, a_ref[...], b_ref[...])


def _inproj_kernel(xa_ref, xb_ref, g_ref, w_ref, q_own_ref, q_other_ref, k_own_ref, k_other_ref,
                   bd_ref, q_ref, k_ref, vt_ref, cb_ref, u_ref, ga_ref, gb_ref, *, n_a):
    tm = xa_ref.shape[0]
    h = _rms(_pick(n_a, xa_ref, xb_ref), g_ref[...]).astype(BF16)
    p = jnp.dot(h, w_ref[:, :QKV_WIDTH], preferred_element_type=F32)
    rest = jnp.dot(h, w_ref[:, QKV_WIDTH:], preferred_element_type=F32)
    cb_ref[...] = rest[:, :D_MODEL].astype(BF16)
    u_ref[...] = (rest[:, D_MODEL:2 * D_MODEL] * rest[:, 2 * D_MODEL:3 * D_MODEL]).astype(BF16)
    ga_ref[...] = rest[:, 3 * D_MODEL:4 * D_MODEL].astype(BF16)
    gb_ref[...] = rest[:, 4 * D_MODEL:].astype(BF16)
    bd = bd_ref[...]
    lane = lax.broadcasted_iota(jnp.int32, (tm, LANES), 1)
    first_half = (lane & (AXIS_DIM - 1)) < ROT_HALF
    low_half = lane < HEAD_DIM

    def norm_rope(c, own, other):
        ms = jnp.dot((c * c).astype(BF16), bd, preferred_element_type=F32)
        partner = jnp.where(first_half, pltpu.roll(c, LANES - ROT_HALF, 1),
                            pltpu.roll(c, ROT_HALF, 1))
        return (c * own + partner * other) * lax.rsqrt(ms + EPS)

    is_aux = lane == HEAD_DIM
    zero = jnp.zeros((tm, LANES), F32)
    ones_aux = jnp.where(is_aux, 1.0, zero)

    q_own = q_own_ref[...]
    q_other = q_other_ref[...]
    for c in range(N_HEADS // 2):
        r = norm_rope(p[:, c * LANES:(c + 1) * LANES], q_own, q_other)
        even = jnp.where(low_half, r, zero)
        odd = jnp.where(low_half, pltpu.roll(r, HEAD_DIM, 1), zero)
        q_ref[:, (2 * c) * LANES:(2 * c + 1) * LANES] = even.astype(BF16)
        q_ref[:, (2 * c + 1) * LANES:(2 * c + 2) * LANES] = odd.astype(BF16)

    k_own = k_own_ref[...]
    k_other = k_other_ref[...]
    for c in range(KV_WIDTH // LANES):
        kn = norm_rope(p[:, D_MODEL + c * LANES:D_MODEL + (c + 1) * LANES], k_own, k_other)
        even = jnp.where(low_half, kn, ones_aux)
        odd = jnp.where(low_half, pltpu.roll(kn, HEAD_DIM, 1), ones_aux)
        k_ref[:, (2 * c) * LANES:(2 * c + 1) * LANES] = even.astype(BF16)
        k_ref[:, (2 * c + 1) * LANES:(2 * c + 2) * LANES] = odd.astype(BF16)
        vv = p[:, D_MODEL + KV_WIDTH + c * LANES:D_MODEL + KV_WIDTH + (c + 1) * LANES]
        even = jnp.where(low_half, vv, ones_aux)
        odd = jnp.where(low_half, pltpu.roll(vv, HEAD_DIM, 1), ones_aux)
        vt_ref[(2 * c) * LANES:(2 * c + 1) * LANES, :] = even.T.astype(BF16)
        vt_ref[(2 * c + 1) * LANES:(2 * c + 2) * LANES, :] = odd.T.astype(BF16)


def _attn_kernel(shift_ref, q_ref, k_ref, vt_ref, *refs, exact_max):
    n_cast = (len(refs) - 1) // 2
    o_ref = refs[n_cast]
    for src, dst in zip(refs[:n_cast], refs[n_cast + 1:]):
        dst[...] = src[...].astype(BF16)
    tq = q_ref.shape[0]
    is_aux_row = lax.broadcasted_iota(jnp.int32, (LANES, GROUP * tq), 0) == HEAD_DIM
    for j in range(N_KV_HEADS):
        qt = jnp.concatenate(
            [q_ref[:, h * LANES:(h + 1) * LANES].T for h in range(GROUP * j, GROUP * (j + 1))],
            axis=1)
        if exact_max:
            st = jnp.dot(k_ref[:, j * LANES:(j + 1) * LANES], qt,
                         preferred_element_type=F32)
            st = st - jnp.max(st, axis=0, keepdims=True)
            pt = jnp.exp2(st).astype(BF16)
            ot = jnp.dot(vt_ref[j * LANES:j * LANES + V_ROWS, :], pt,
                         preferred_element_type=F32)
        else:
            neg_shift = jnp.full(qt.shape, -shift_ref[0], F32).astype(BF16)
            qt = jnp.where(is_aux_row, neg_shift, qt)
            ot = jnp.zeros((V_ROWS, GROUP * tq), F32)
            for c in range(k_ref.shape[0] // KEY_CHUNK):
                rows = slice(c * KEY_CHUNK, (c + 1) * KEY_CHUNK)
                st = jnp.dot(k_ref[rows, j * LANES:(j + 1) * LANES], qt,
                             preferred_element_type=F32)
                pt = jnp.exp2(st).astype(BF16)
                ot = ot + jnp.dot(vt_ref[j * LANES:j * LANES + V_ROWS, rows], pt,
                                  preferred_element_type=F32)
        ot = ot[:HEAD_DIM] / ot[HEAD_DIM:HEAD_DIM + 1]
        for a in range(2):
            pair = jnp.concatenate([ot[:, (2 * a) * tq:(2 * a + 1) * tq],
                                    ot[:, (2 * a + 1) * tq:(2 * a + 2) * tq]], axis=0)
            c = 2 * j + a
            o_ref[:, c * LANES:(c + 1) * LANES] = pair.T.astype(BF16)


def _mix_kernel(attn_a_ref, attn_b_ref, cb_ref, u_ref, up_ref, un_ref, ga_ref, gb_ref,
                xa_ref, xb_ref, cw_ref, woa_ref, wob_ref, wo_ref, g_ref, *rest,
                n_prompt_tiles, tiles4, tiles2, with_router):
    if with_router:
        rw_ref, xo_ref, h2_ref, lg_ref = rest
    else:
        xo_ref, h2_ref = rest
    i = pl.program_id(0)
    tm = xa_ref.shape[0]
    is_prompt = i < n_prompt_tiles
    seq_start = jnp.where(is_prompt, i % tiles4 == 0, i % tiles2 == 0)
    seq_end = jnp.where(is_prompt, i % tiles4 == tiles4 - 1, i % tiles2 == tiles2 - 1)

    attn = _pick(n_prompt_tiles, attn_a_ref, attn_b_ref)
    ya = jnp.dot(attn, woa_ref[...], preferred_element_type=F32)

    u = u_ref[...].astype(F32)
    row = lax.broadcasted_iota(jnp.int32, u.shape, 0)
    prev_row = jnp.where(seq_start, 0.0, up_ref[SUBLANES - 1:SUBLANES, :].astype(F32))
    next_row = jnp.where(seq_end, 0.0, un_ref[0:1, :].astype(F32))
    u_prev = jnp.where(row == 0, prev_row, pltpu.roll(u, 1, 0))
    u_next = jnp.where(row == tm - 1, next_row, pltpu.roll(u, tm - 1, 0))
    cw = cw_ref[...]
    conv = cw[0:1, :] * u_prev + cw[1:2, :] * u + cw[2:3, :] * u_next
    yb_in = (cb_ref[...].astype(F32) * conv).astype(BF16)
    yb = jnp.dot(yb_in, wob_ref[...], preferred_element_type=F32)

    m = (jax.nn.sigmoid(ga_ref[...].astype(F32)) * ya
         + jax.nn.sigmoid(gb_ref[...].astype(F32)) * yb)
    xn = (_pick(n_prompt_tiles, xa_ref, xb_ref)
          + jnp.dot(m.astype(BF16), wo_ref[...], preferred_element_type=F32))
    xo_ref[...] = xn
    h2 = _rms(xn, g_ref[...])
    h2_ref[...] = h2.astype(h2_ref.dtype)
    if not with_router:
        return
    hi = h2.astype(BF16)
    lo = (h2 - hi.astype(F32)).astype(BF16)
    rw = rw_ref[...]
    rhi = rw.astype(BF16)
    rlo = (rw - rhi.astype(F32)).astype(BF16)
    lane = lax.broadcasted_iota(jnp.int32, rw.shape, 1)
    hi_terms = jnp.dot(hi, jnp.where(lane < N_EXPERTS, rhi, rlo), preferred_element_type=F32)
    lg_ref[...] = (hi_terms + pltpu.roll(hi_terms, LANES - N_EXPERTS, 1)
                   + jnp.dot(lo, rhi, preferred_element_type=F32))


def _ffn_kernel(h_ref, x_ref, wg_ref, wu_ref, wd_ref, o_ref, acc_ref):
    j = pl.program_id(1)

    @pl.when(j == 0)
    def _():
        acc_ref[...] = jnp.zeros_like(acc_ref)

    h = h_ref[...]
    g = jnp.dot(h, wg_ref[...], preferred_element_type=F32)
    u = jnp.dot(h, wu_ref[...], preferred_element_type=F32)
    a = (g * jax.nn.sigmoid(g) * u).astype(BF16)
    acc_ref[...] += jnp.dot(a, wd_ref[...], preferred_element_type=F32)

    @pl.when(j == pl.num_programs(1) - 1)
    def _():
        o_ref[...] = x_ref[...] + acc_ref[...]


ROUTE_E0, ROUTE_E1, ROUTE_R0, ROUTE_R1, ROUTE_W0, ROUTE_W1, ROUTE_S0, ROUTE_S1 = range(8)
STAT_COUNT, STAT_BASE, STAT_SEG = range(3)


def _lane_pick(x, lane, k):
    return jnp.sum(jnp.where(lane == k, x, 0.0), axis=-1, keepdims=True)


def _router_kernel(lg_ref, tri_ref, route_ref, fields_ref, stats_ref, count_ref, base_ref):
    i = pl.program_id(0)

    @pl.when(i == 0)
    def _():
        base_ref[...] = jnp.zeros_like(base_ref)

    lg = lg_ref[...]
    lane = lax.broadcasted_iota(jnp.int32, lg.shape, 1)
    neg = jnp.float32(-jnp.inf)
    l1 = jnp.where(lane < N_EXPERTS, lg, neg)
    m1 = jnp.max(l1, axis=-1, keepdims=True)
    i1 = jnp.min(jnp.where(l1 == m1, lane, LANES), axis=-1, keepdims=True)
    l2 = jnp.where(lane == i1, neg, l1)
    m2 = jnp.max(l2, axis=-1, keepdims=True)
    i2 = jnp.min(jnp.where(l2 == m2, lane, LANES), axis=-1, keepdims=True)
    e = jnp.exp(m2 - m1)
    w1 = 1.0 / (1.0 + e)
    w2 = e / (1.0 + e)

    hot1 = lane == i1
    hot2 = lane == i2
    onehot = jnp.where(hot1 | hot2, 1.0, 0.0)
    base = base_ref[...]
    local = jnp.dot(tri_ref[...], onehot.astype(BF16), preferred_element_type=F32)
    prefix = local + base[0:1, :]
    r1 = jnp.sum(jnp.where(hot1, prefix, 0.0), axis=-1, keepdims=True)
    r2 = jnp.sum(jnp.where(hot2, prefix, 0.0), axis=-1, keepdims=True)
    in_tile = jnp.broadcast_to(jnp.sum(onehot, axis=0, keepdims=True), base.shape)
    base_ref[...] = base + in_tile
    count_ref[...] = base + in_tile
    lane8 = lax.broadcasted_iota(jnp.int32, base.shape, 1)
    incl = in_tile
    for step in (1, 2, 4):
        incl = incl + jnp.where(lane8 >= step, pltpu.roll(incl, step, 1), 0.0)
    seg = incl - in_tile
    slot = local + seg[0:1, :]
    s1 = jnp.sum(jnp.where(hot1, slot, 0.0), axis=-1, keepdims=True)
    s2 = jnp.sum(jnp.where(hot2, slot, 0.0), axis=-1, keepdims=True)
    sub8 = lax.broadcasted_iota(jnp.int32, base.shape, 0)
    stats_ref[...] = jnp.where(sub8 == STAT_COUNT, in_tile,
                               jnp.where(sub8 == STAT_BASE, base,
                                         jnp.where(sub8 == STAT_SEG, seg, 0.0)))

    rec = jnp.where(lane == ROUTE_E0, i1.astype(F32), 0.0)
    rec = jnp.where(lane == ROUTE_E1, i2.astype(F32), rec)
    rec = jnp.where(lane == ROUTE_R0, r1, rec)
    rec = jnp.where(lane == ROUTE_R1, r2, rec)
    rec = jnp.where(lane == ROUTE_W0, w1, rec)
    rec = jnp.where(lane == ROUTE_W1, w2, rec)
    rec = jnp.where(lane == ROUTE_S0, s1, rec)
    rec = jnp.where(lane == ROUTE_S1, s2, rec)
    route_ref[...] = rec
    fields_ref[...] = rec.T[:fields_ref.shape[0], :]


def _row_copy(src, src_row, dst, dst_row, sem):
    return pltpu.make_async_copy(
        src.at[pl.ds(pl.multiple_of(src_row * SUBLANES, SUBLANES), SUBLANES)],
        dst.at[pl.ds(pl.multiple_of(dst_row * SUBLANES, SUBLANES), SUBLANES)], sem)


def _dispatch_kernel(pad_start_ref, pad_count_ref, tile_valid_ref, tab_ref, fields_ref, h_ref,
                     xs_ref, zero_ref, xc_ref, sem, zsem):
    tm = h_ref.shape[0]
    tile = tm * SUBLANES
    n_tiles = xs_ref.shape[0] // tile

    @pl.when(pl.program_id(0) == 0)
    def _():
        zero_ref[...] = jnp.zeros_like(zero_ref)
        for e in range(N_EXPERTS):
            start = pad_start_ref[e]
            count = pad_count_ref[e]

            def zero_issue(r, carry):
                _row_copy(zero_ref, 0, xs_ref, start + r, zsem).start()
                return carry

            def zero_wait(r, carry):
                _row_copy(zero_ref, 0, xs_ref, 0, zsem).wait()
                return carry

            lax.fori_loop(0, count, zero_issue, 0)
            lax.fori_loop(0, count, zero_wait, 0)
        for g in range(n_tiles - N_EXPERTS, n_tiles):
            @pl.when(tile_valid_ref[g] == 0)
            def _():
                fill = pltpu.make_async_copy(zero_ref, xs_ref.at[pl.ds(g * tile, tile)], zsem)
                fill.start()
                fill.wait()

    i = pl.program_id(0)
    buf = i % 2
    xc = xc_ref.at[buf]

    def drain(b):
        pltpu.make_async_copy(xc_ref.at[b], xs_ref.at[pl.ds(0, 2 * tile)], sem.at[b]).wait()

    @pl.when(i >= 2)
    def _():
        drain(buf)

    grouped_row = lax.broadcasted_iota(jnp.int32, (2 * tm, tm), 0)
    s0 = fields_ref[ROUTE_S0:ROUTE_S0 + 1, :].astype(jnp.int32)
    s1 = fields_ref[ROUTE_S1:ROUTE_S1 + 1, :].astype(jnp.int32)
    select = jnp.where((grouped_row == s0) | (grouped_row == s1), 1.0, 0.0).astype(BF16)
    _store_row_tiles(xc, jnp.dot(select, h_ref[...], preferred_element_type=F32))

    for e in range(N_EXPERTS):
        count = tab_ref[0, 0, e]
        src0 = tab_ref[0, 0, N_EXPERTS + e]
        dst0 = tab_ref[0, 0, 2 * N_EXPERTS + e]
        for bit in reversed(range(tm.bit_length())):
            size = 1 << bit

            @pl.when((count & size) != 0)
            def _():
                done = count & ~(2 * size - 1)
                src = pl.multiple_of((src0 + done) * SUBLANES, SUBLANES)
                dst = pl.multiple_of((dst0 + done) * SUBLANES, SUBLANES)
                pltpu.make_async_copy(xc.at[pl.ds(src, size * SUBLANES)],
                                      xs_ref.at[pl.ds(dst, size * SUBLANES)],
                                      sem.at[buf]).start()

    last = pl.num_programs(0) - 1

    @pl.when((i == last) & (i >= 1))
    def _():
        drain(1 - buf)

    @pl.when(i == last)
    def _():
        drain(buf)


def _moe_ffn_kernel(te_ref, tv_ref, x_ref, wg_ref, wu_ref, wd_ref, y_ref, acc_ref):
    del te_ref
    g_idx = pl.program_id(0)
    j = pl.program_id(1)
    valid = tv_ref[g_idx] > 0

    @pl.when(j == 0)
    def _():
        acc_ref[...] = jnp.zeros_like(acc_ref)

    @pl.when(valid)
    def _():
        h = _load_row_tiles(x_ref).astype(BF16)
        g = jnp.dot(h, wg_ref[0], preferred_element_type=F32)
        u = jnp.dot(h, wu_ref[0], preferred_element_type=F32)
        a = (g * jax.nn.sigmoid(g) * u).astype(BF16)
        acc_ref[...] += jnp.dot(a, wd_ref[0], preferred_element_type=F32)

    @pl.when(j == pl.num_programs(1) - 1)
    def _():
        _store_row_tiles(y_ref, acc_ref[...])


def _combine_kernel(pos_ref, nxt_ref, x_ref, route_ref, fg_ref, y_hbm, oa_ref, ob_ref,
                    ybuf, sem, *, n_a):
    i = pl.program_id(0)
    tc = x_ref.shape[0]
    cur = i % 2

    def issue_tile(p_ref, par):
        def issue(blk, carry):
            for k in range(ROW_UNROLL):
                r = blk * ROW_UNROLL + k
                _row_copy(y_hbm, p_ref[0, 0, r], ybuf.at[par, 0], r, sem.at[par]).start()
                _row_copy(y_hbm, p_ref[0, 0, tc + r], ybuf.at[par, 1], r, sem.at[par]).start()
            return carry
        lax.fori_loop(0, tc // ROW_UNROLL, issue, 0)

    @pl.when(i == 0)
    def _():
        issue_tile(pos_ref, 0)

    @pl.when(i + 1 < pl.num_programs(0))
    def _():
        issue_tile(nxt_ref, 1 - cur)

    for slot in range(2):
        pltpu.make_async_copy(y_hbm.at[pl.ds(0, tc * SUBLANES)], ybuf.at[cur, slot],
                              sem.at[cur]).wait()
    route = route_ref[...]
    lane = lax.broadcasted_iota(jnp.int32, route.shape, 1)
    w0 = _lane_pick(route, lane, ROUTE_W0)
    w1 = _lane_pick(route, lane, ROUTE_W1)
    y0 = _load_row_tiles(ybuf.at[cur, 0])
    y1 = _load_row_tiles(ybuf.at[cur, 1])
    out = _rms(x_ref[...] + (w0 * y0 + w1 * y1), fg_ref[...])
    in_a = i < n_a

    @pl.when(in_a)
    def _():
        oa_ref[...] = out

    @pl.when(jnp.logical_not(in_a))
    def _():
        ob_ref[...] = out


def _rope_tables(max_seq):
    t = jnp.arange(max_seq, dtype=jnp.int32)
    row = (t // GRID_W).astype(F32)
    col = (t % GRID_W).astype(F32)
    inv = 1.0 / (ROPE_THETA ** (jnp.arange(0, AXIS_DIM, 2, dtype=F32) / AXIS_DIM))
    ar = row[:, None] * inv[None, :]
    ac = col[:, None] * inv[None, :]
    cos64 = jnp.concatenate([jnp.cos(ar), jnp.cos(ar), jnp.cos(ac), jnp.cos(ac)], axis=-1)
    sin64 = jnp.concatenate([-jnp.sin(ar), jnp.sin(ar), -jnp.sin(ac), jnp.sin(ac)], axis=-1)
    return cos64, sin64


def _gain_rope_tables(cos64, sin64, gain):
    half = AXIS_DIM // 2
    partner_gain = gain.reshape(-1, 2, half)[:, ::-1, :].reshape(-1)
    return (jnp.tile(cos64 * gain[None, :], (1, 2)),
            jnp.tile(sin64 * partner_gain[None, :], (1, 2)))


def _two_part_specs(block, n_a, parts):
    off = 0 if parts[0] is parts[1] else n_a
    return [pl.BlockSpec(block, lambda i: (jnp.minimum(i, n_a - 1), 0)),
            pl.BlockSpec(block, lambda i: (jnp.maximum(i, n_a) - off, 0))]


def _in_proj(xparts, t, n_a, gain, w, q_tables, k_tables, bd, pos_map):
    row = lambda i: (i, 0)
    const = lambda i: (0, 0)
    table = pl.BlockSpec((TM, LANES), pos_map)
    act = pl.BlockSpec((TM, D_MODEL), row)
    act_shape = jax.ShapeDtypeStruct((t, D_MODEL), BF16)
    return pl.pallas_call(
        functools.partial(_inproj_kernel, n_a=n_a),
        grid=(t // TM,),
        in_specs=_two_part_specs((TM, D_MODEL), n_a, xparts) + [
                  pl.BlockSpec((1, D_MODEL), const),
                  pl.BlockSpec((D_MODEL, QKV_WIDTH + REST_WIDTH), const,
                               pipeline_mode=pl.Buffered(1)),
                  table, table, table, table,
                  pl.BlockSpec((LANES, LANES), const)],
        out_specs=[pl.BlockSpec((TM, Q_EXP_WIDTH), row),
                   pl.BlockSpec((TM, KV_EXP_WIDTH), row),
                   pl.BlockSpec((KV_EXP_WIDTH, TM), lambda i: (0, i)),
                   act, act, act, act],
        out_shape=[jax.ShapeDtypeStruct((t, Q_EXP_WIDTH), BF16),
                   jax.ShapeDtypeStruct((t, KV_EXP_WIDTH), BF16),
                   jax.ShapeDtypeStruct((KV_EXP_WIDTH, t), BF16),
                   act_shape, act_shape, act_shape, act_shape],
        compiler_params=_params(("parallel",)),
        name="in_proj",
    )(*xparts, gain, w, *q_tables, *k_tables, bd)


def _attention(shift, q, k, vt, casts, *, row0, n_seq, seq, tq, exact_max):
    nq = seq // tq
    q0 = row0 // tq
    s0 = row0 // seq
    n_steps = n_seq * nq
    step_blk = lambda b, i, km: (b * nq + i, 0, 0)
    rides = [(w.shape[0] * w.shape[1]) % (n_steps * BF16_SUBLANES) == 0 for w in casts]
    views = [w.reshape(n_steps, w.shape[0] * w.shape[1] // n_steps, w.shape[2])
             for w, ok in zip(casts, rides) if ok]
    cast_specs = [pl.BlockSpec((1,) + v.shape[1:], step_blk) for v in views]
    outs = pl.pallas_call(
        functools.partial(_attn_kernel, exact_max=exact_max),
        grid_spec=pltpu.PrefetchScalarGridSpec(
            num_scalar_prefetch=1,
            grid=(n_seq, nq),
            in_specs=[pl.BlockSpec((tq, Q_EXP_WIDTH), lambda b, i, km: (q0 + b * nq + i, 0)),
                      pl.BlockSpec((seq, KV_EXP_WIDTH), lambda b, i, km: (s0 + b, 0)),
                      pl.BlockSpec((KV_EXP_WIDTH, seq), lambda b, i, km: (0, s0 + b))]
                     + cast_specs,
            out_specs=[pl.BlockSpec((tq, D_MODEL), lambda b, i, km: (b * nq + i, 0))]
                      + cast_specs),
        out_shape=[jax.ShapeDtypeStruct((n_seq * seq, D_MODEL), BF16)]
                  + [jax.ShapeDtypeStruct(v.shape, BF16) for v in views],
        compiler_params=_params(("parallel", "parallel")),
        name=f"attention_s{seq}" + ("_exact" if exact_max else ""),
    )(shift, q, k, vt, *views)
    carried = iter(outs[1:])
    return outs[0], [next(carried).reshape(w.shape) if ok else w.astype(BF16)
                     for w, ok in zip(casts, rides)]


def _plan_casts(jobs, depth, call_weights=(2.0, 1.0)):
    plan = []
    pending = sorted(jobs, key=lambda job: job[2])
    for l in range(depth):
        budget = sum(job[1].size for job in pending) / (depth - l)
        take, used = [], 0
        for job in pending:
            if job[2] == l or used + job[1].size <= budget:
                take.append(job)
                used += job[1].size
        pending = [job for job in pending if all(job is not t for t in take)]
        calls = [[] for _ in call_weights]
        loads = [0.0] * len(call_weights)
        for job in sorted(take, key=lambda job: -job[1].size):
            c = min(range(len(calls)), key=lambda c: loads[c] / call_weights[c])
            calls[c].append(job)
            loads[c] += job[1].size
        plan.append(calls)
    return plan


def _mix(attn_parts, cb, u, ga, gb, xparts, cw, woa, wob, wo, gain, rw, *, n_prompt, seq_p, seq_s):
    t = cb.shape[0]
    n_a = n_prompt // TM
    row = lambda i: (i, 0)
    const = lambda i: (0, 0)
    sub = TM // SUBLANES
    last = t // SUBLANES - 1
    act = pl.BlockSpec((TM, D_MODEL), row)
    wspec = pl.BlockSpec((D_MODEL, D_MODEL), const)
    with_router = rw is not None
    kern = functools.partial(_mix_kernel, n_prompt_tiles=n_prompt // TM,
                             tiles4=seq_p // TM, tiles2=seq_s // TM, with_router=with_router)
    in_specs = (_two_part_specs((TM, D_MODEL), n_a, attn_parts) + [
                act, act,
                pl.BlockSpec((SUBLANES, D_MODEL), lambda i: (jnp.maximum(i * sub - 1, 0), 0)),
                pl.BlockSpec((SUBLANES, D_MODEL),
                             lambda i: (jnp.minimum((i + 1) * sub, last), 0)),
                act, act]
                + _two_part_specs((TM, D_MODEL), n_a, xparts) + [
                pl.BlockSpec((3, D_MODEL), const),
                wspec, wspec, wspec,
                pl.BlockSpec((1, D_MODEL), const)])
    args = [*attn_parts, cb, u, u, u, ga, gb, *xparts, cw, woa, wob, wo, gain]
    out_specs = [act, act]
    out_shape = [jax.ShapeDtypeStruct((t, D_MODEL), F32),
                 jax.ShapeDtypeStruct((t, D_MODEL), BF16)]
    if with_router:
        in_specs.append(pl.BlockSpec((D_MODEL, LANES), const))
        args.append(rw)
        out_specs.append(pl.BlockSpec((TM, LANES), row))
        out_shape.append(jax.ShapeDtypeStruct((t, LANES), F32))
    return pl.pallas_call(
        kern,
        grid=(t // TM,),
        in_specs=in_specs,
        out_specs=out_specs,
        out_shape=out_shape,
        compiler_params=_params(("parallel",)),
        name="mix_proj_router" if with_router else "mix_proj",
    )(*args)


def _ffn(h, x, wg, wu, wd):
    t = x.shape[0]
    row = lambda i, j: (i, 0)
    return pl.pallas_call(
        _ffn_kernel,
        grid=(t // TM_FFN, D_FF // TF),
        in_specs=[pl.BlockSpec((TM_FFN, D_MODEL), row),
                  pl.BlockSpec((TM_FFN, D_MODEL), row),
                  pl.BlockSpec((D_MODEL, TF), lambda i, j: (0, j), pipeline_mode=pl.Buffered(1)),
                  pl.BlockSpec((D_MODEL, TF), lambda i, j: (0, j), pipeline_mode=pl.Buffered(1)),
                  pl.BlockSpec((TF, D_MODEL), lambda i, j: (j, 0), pipeline_mode=pl.Buffered(1))],
        out_specs=pl.BlockSpec((TM_FFN, D_MODEL), row),
        out_shape=jax.ShapeDtypeStruct((t, D_MODEL), F32),
        scratch_shapes=[pltpu.VMEM((TM_FFN, D_MODEL), F32)],
        compiler_params=_params(("parallel", "arbitrary")),
        name="ffn_dense",
    )(h, x, wg, wu, wd)


def _router(logits):
    t = logits.shape[0]
    row = lambda i: (i, 0)
    const = lambda i: (0, 0)
    r = jnp.arange(TR)
    tri = (r[None, :] < r[:, None]).astype(BF16)
    return pl.pallas_call(
        _router_kernel,
        grid=(t // TR,),
        in_specs=[pl.BlockSpec((TR, LANES), row),
                  pl.BlockSpec((TR, TR), const)],
        out_specs=[pl.BlockSpec((TR, LANES), row),
                   pl.BlockSpec((SUBLANES, TR), lambda i: (0, i)),
                   pl.BlockSpec((SUBLANES, LANES), row),
                   pl.BlockSpec((SUBLANES, LANES), const)],
        out_shape=[jax.ShapeDtypeStruct((t, LANES), F32),
                   jax.ShapeDtypeStruct((SUBLANES, t), F32),
                   jax.ShapeDtypeStruct((t // TR * SUBLANES, LANES), F32),
                   jax.ShapeDtypeStruct((SUBLANES, LANES), F32)],
        scratch_shapes=[pltpu.VMEM((SUBLANES, LANES), F32)],
        compiler_params=_params(("arbitrary",)),
        name="router",
    )(logits, tri)


def _route_plan(fields, stats, counts, t):
    e0 = fields[ROUTE_E0].astype(jnp.int32)
    e1 = fields[ROUTE_E1].astype(jnp.int32)
    r0 = fields[ROUTE_R0].astype(jnp.int32)
    r1 = fields[ROUTE_R1].astype(jnp.int32)
    cnt = counts[0, :N_EXPERTS].astype(jnp.int32)
    padded = ((cnt + TM_MOE - 1) // TM_MOE) * TM_MOE
    ends = jnp.cumsum(padded)
    starts = ends - padded
    experts = jnp.arange(N_EXPERTS, dtype=jnp.int32)
    pos0 = jnp.sum(jnp.where(e0[:, None] == experts[None, :], starts[None, :], 0), axis=1) + r0
    pos1 = jnp.sum(jnp.where(e1[:, None] == experts[None, :], starts[None, :], 0), axis=1) + r1
    n_tok_tiles = t // TM_MOE
    pos = jnp.concatenate([pos0.reshape(n_tok_tiles, 1, TM_MOE),
                           pos1.reshape(n_tok_tiles, 1, TM_MOE)], axis=2)
    n_tiles = 2 * t // TM_MOE + N_EXPERTS
    tile_start = jnp.arange(n_tiles, dtype=jnp.int32) * TM_MOE
    tile_valid = (tile_start < ends[-1]).astype(jnp.int32)
    tile_expert = jnp.sum((tile_start[:, None] >= ends[None, :]).astype(jnp.int32), axis=1)
    tile_expert = jnp.minimum(tile_expert, N_EXPERTS - 1)
    pad = (starts + cnt, padded - cnt)
    stats = stats.reshape(n_tok_tiles, SUBLANES, LANES)[:, :, :N_EXPERTS].astype(jnp.int32)
    table = jnp.concatenate([stats[:, STAT_COUNT], stats[:, STAT_SEG],
                             stats[:, STAT_BASE] + starts[None, :]], axis=1)
    return pos, pad, table[:, None, :], tile_expert, tile_valid


ROW_TILE = (TM_MOE * SUBLANES, LANES)


def _dispatch(pad_start, pad_count, tile_valid, table, fields, h):
    t = h.shape[0]
    n_rows = tile_valid.shape[0] * TM_MOE
    return pl.pallas_call(
        _dispatch_kernel,
        grid_spec=pltpu.PrefetchScalarGridSpec(
            num_scalar_prefetch=3,
            grid=(t // TM_MOE,),
            in_specs=[pl.BlockSpec((1, 1, 3 * N_EXPERTS), lambda i, ps, pc, tv: (i, 0, 0),
                                   memory_space=pltpu.SMEM),
                      pl.BlockSpec((SUBLANES, TM_MOE), lambda i, ps, pc, tv: (0, i)),
                      pl.BlockSpec((TM_MOE, D_MODEL), lambda i, ps, pc, tv: (i, 0))],
            out_specs=pl.BlockSpec(memory_space=pl.ANY),
            scratch_shapes=[pltpu.VMEM(ROW_TILE, F32),
                            pltpu.VMEM((2, 2 * ROW_TILE[0], LANES), F32),
                            pltpu.SemaphoreType.DMA((2,)),
                            pltpu.SemaphoreType.DMA(())]),
        out_shape=jax.ShapeDtypeStruct((n_rows * SUBLANES, LANES), F32),
        compiler_params=_params(("arbitrary",)),
        name="moe_dispatch",
    )(pad_start, pad_count, tile_valid, table, fields, h)


def _moe_ffn(tile_expert, tile_valid, xs, wg, wu, wd):
    n_rows = xs.shape[0] // SUBLANES
    row = lambda g, j, te, tv: (g, 0)
    return pl.pallas_call(
        _moe_ffn_kernel,
        grid_spec=pltpu.PrefetchScalarGridSpec(
            num_scalar_prefetch=2,
            grid=(n_rows // TM_MOE, D_FF // TF_MOE),
            in_specs=[pl.BlockSpec(ROW_TILE, row),
                      pl.BlockSpec((1, D_MODEL, TF_MOE),
                                   lambda g, j, te, tv: (te[g], 0, j * tv[g])),
                      pl.BlockSpec((1, D_MODEL, TF_MOE),
                                   lambda g, j, te, tv: (te[g], 0, j * tv[g])),
                      pl.BlockSpec((1, TF_MOE, D_MODEL),
                                   lambda g, j, te, tv: (te[g], j * tv[g], 0))],
            out_specs=pl.BlockSpec(ROW_TILE, row),
            scratch_shapes=[pltpu.VMEM((TM_MOE, D_MODEL), F32)]),
        out_shape=jax.ShapeDtypeStruct(xs.shape, F32),
        compiler_params=_params(("parallel", "arbitrary")),
        name="moe_ffn",
    )(tile_expert, tile_valid, xs, wg, wu, wd)


def _combine(pos, x, route, fgain, ys, n_first):
    t = x.shape[0]
    n_a = n_first // TM_MOE
    row = lambda i: (i, 0)
    blk = (TM_MOE, D_MODEL)
    last = t // TM_MOE - 1
    pos_blk = (1, 1, 2 * TM_MOE)
    return pl.pallas_call(
        functools.partial(_combine_kernel, n_a=n_a),
        grid=(t // TM_MOE,),
        in_specs=[pl.BlockSpec(pos_blk, lambda i: (i, 0, 0), memory_space=pltpu.SMEM),
                  pl.BlockSpec(pos_blk, lambda i: (jnp.minimum(i + 1, last), 0, 0),
                               memory_space=pltpu.SMEM),
                  pl.BlockSpec(blk, row),
                  pl.BlockSpec((TM_MOE, LANES), row),
                  pl.BlockSpec((1, D_MODEL), lambda i: (0, 0)),
                  pl.BlockSpec(memory_space=pl.ANY)],
        out_specs=[pl.BlockSpec(blk, lambda i: (jnp.minimum(i, n_a - 1), 0)),
                   pl.BlockSpec(blk, lambda i: (jnp.maximum(i - n_a, 0), 0))],
        out_shape=[jax.ShapeDtypeStruct((n_first, D_MODEL), F32),
                   jax.ShapeDtypeStruct((t - n_first, D_MODEL), F32)],
        scratch_shapes=[pltpu.VMEM((2, 2) + ROW_TILE, F32),
                        pltpu.SemaphoreType.DMA((2,))],
        compiler_params=_params(("arbitrary",)),
        name="moe_combine",
    )(pos, pos, x, route, fgain, ys)


def kernel(x_prompt, x_sample, norm_mix, w_in, q_norm, k_norm, conv_w, w_oa, w_ob, w_o, norm_ffn,
           ffn_w_gate, ffn_w_up, ffn_w_down, router_w, moe_w_gate, moe_w_up, moe_w_down, final_norm):
    bp, sp, _ = x_prompt.shape
    bs, ss, _ = x_sample.shape
    n_prompt = bp * sp
    n_sample = bs * ss
    depth = norm_mix.shape[0]
    assert sp % TM == 0 and ss % TM == 0 and TM == TM_FFN == TM_MOE
    assert depth == 2 and ffn_w_gate.shape[0] == moe_w_gate.shape[0] == 1

    t_all = n_prompt + n_sample
    xparts = (x_prompt.reshape(n_prompt, D_MODEL), x_sample.reshape(n_sample, D_MODEL))

    cos, sin = _rope_tables(max(sp, ss))
    n_prompt_tiles, tiles4, tiles2 = n_prompt // TM, sp // TM, ss // TM
    pos_map = lambda i: (jnp.where(i < n_prompt_tiles, i % tiles4, i % tiles2), 0)
    idx = jnp.arange(LANES)
    bd = jnp.where(idx[:, None] // HEAD_DIM == idx[None, :] // HEAD_DIM,
                   1.0 / HEAD_DIM, 0.0).astype(BF16)

    jobs = []
    for l in range(depth):
        if l > 0:
            jobs.append((("w_in", l), w_in[l][None], l - 1))
        jobs += [((name, l), w[l][None], l)
                 for name, w in (("w_oa", w_oa), ("w_ob", w_ob), ("w_o", w_o))]
        named = ((("ffn_gate", ffn_w_gate), ("ffn_up", ffn_w_up), ("ffn_down", ffn_w_down))
                 if l % 2 == 0 else
                 (("moe_gate", moe_w_gate), ("moe_up", moe_w_up), ("moe_down", moe_w_down)))
        jobs += [((name, l), w[l // 2] if l % 2 else w[l // 2][None], l) for name, w in named]
    cast_plan = _plan_casts(jobs, depth)
    bf16_w = {("w_in", 0): w_in[0].astype(BF16)}

    for l in range(depth):
        gain = norm_mix[l][None, :]
        q, k, vt, cb, u, ga, gb = _in_proj(
            xparts, t_all, n_prompt_tiles, gain, bf16_w[("w_in", l)],
            _gain_rope_tables(cos, sin, q_norm[l] * Q_SCALE),
            _gain_rope_tables(cos, sin, k_norm[l]), bd, pos_map)
        shift = (Q_SCALE * HEAD_DIM * jnp.max(jnp.abs(q_norm[l]))
                 * jnp.max(jnp.abs(k_norm[l]))).reshape(1)
        bound_ok = 2.0 * shift[0] < MAX_SHIFT_GAP

        jobs_p, jobs_s = cast_plan[l]

        def attend(exact_max, tq_p, tq_s):
            def run(shift, q, k, vt, w_p, w_s):
                attn_p, done_p = _attention(shift, q, k, vt, w_p, row0=0, n_seq=bp, seq=sp,
                                            tq=tq_p, exact_max=exact_max)
                attn_s, done_s = _attention(shift, q, k, vt, w_s, row0=n_prompt, n_seq=bs,
                                            seq=ss, tq=tq_s, exact_max=exact_max)
                return (attn_p, attn_s), done_p + done_s
            return run

        attn_parts, done = lax.cond(
            bound_ok, attend(False, 256, 256), attend(True, 128, 256), shift, q, k, vt,
            [job[1] for job in jobs_p], [job[1] for job in jobs_s])
        for job, w16 in zip(jobs_p + jobs_s, done):
            bf16_w[job[0]] = w16 if job[0][0].startswith("moe") else w16[0]
        j = l // 2
        is_moe = l % 2 == 1
        rw = (jnp.pad(jnp.tile(router_w[j], (1, 2)), ((0, 0), (0, LANES - 2 * N_EXPERTS)))
              if is_moe else None)
        outs = _mix(attn_parts, cb, u, ga, gb, xparts, conv_w[l],
                    bf16_w[("w_oa", l)], bf16_w[("w_ob", l)], bf16_w[("w_o", l)],
                    norm_ffn[l][None, :], rw, n_prompt=n_prompt, seq_p=sp, seq_s=ss)
        if not is_moe:
            x, h2 = outs
            x = _ffn(h2, x, bf16_w[("ffn_gate", l)], bf16_w[("ffn_up", l)],
                     bf16_w[("ffn_down", l)])
            xparts = (x, x)
        else:
            x, h2, logits = outs
            route, fields, stats, counts = _router(logits)
            pos, pad, table, tile_expert, tile_valid = _route_plan(fields, stats, counts, t_all)
            xs = _dispatch(*pad, tile_valid, table, fields, h2)
            ys = _moe_ffn(tile_expert, tile_valid, xs, bf16_w[("moe_gate", l)],
                          bf16_w[("moe_up", l)], bf16_w[("moe_down", l)])
            xparts = _combine(pos, x, route, final_norm[None, :], ys, n_prompt)

    return (xparts[0].reshape(bp, sp, D_MODEL), xparts[1].reshape(bs, ss, D_MODEL))
```

```python
import functools
import math

import jax
import jax.numpy as jnp
from jax import lax
from jax.experimental import pallas as pl
from jax.experimental.pallas import tpu as pltpu

F32 = jnp.float32
BF16 = jnp.bfloat16

D_MODEL = 1024
N_HEADS = 16
N_KV_HEADS = 4
HEAD_DIM = 64
GROUP = N_HEADS // N_KV_HEADS
KV_WIDTH = N_KV_HEADS * HEAD_DIM
AXIS_DIM = HEAD_DIM // 2
ROPE_THETA = 10000.0
GRID_W = 64
D_FF = 3584
N_EXPERTS = 8
EPS = 1e-6
LANES = 128
QKV_WIDTH = D_MODEL + 2 * KV_WIDTH
REST_WIDTH = 5 * D_MODEL
Q_EXP_WIDTH = N_HEADS * LANES
KV_EXP_WIDTH = N_KV_HEADS * LANES
Q_SCALE = math.log2(math.e) / math.sqrt(HEAD_DIM)
SUBLANES = 8
BF16_SUBLANES = 16
ROT_HALF = AXIS_DIM // 2
V_ROWS = HEAD_DIM + BF16_SUBLANES
KEY_CHUNK = 512
ROW_UNROLL = SUBLANES
MAX_SHIFT_GAP = 100.0

TM = 512
TM_FFN = 512
TF = 3584
TF_MOE = 1792
TR = 512
TM_MOE = 512
VMEM_LIMIT = 56 * 1024 * 1024


def _params(sem):
    return pltpu.CompilerParams(dimension_semantics=sem, vmem_limit_bytes=VMEM_LIMIT)


def _rms(x, gain):
    return x * lax.rsqrt(jnp.mean(x * x, axis=-1, keepdims=True) + EPS) * gain


def _store_row_tiles(ref, x):
    rows = x.shape[0]
    for c in range(D_MODEL // LANES):
        ref[pl.ds(c, rows, stride=SUBLANES), :] = x[:, c * LANES:(c + 1) * LANES]


def _load_row_tiles(ref):
    rows = ref.shape[0] // SUBLANES
    return jnp.concatenate([ref[pl.ds(c, rows, stride=SUBLANES), :]
                            for c in range(D_MODEL // LANES)], axis=1)


def _pick(n_a, a_ref, b_ref):
    return jnp.where(pl.program_id(0) < n_a, a_ref[...], b_ref[...])


def _inproj_kernel(xa_ref, xb_ref, g_ref, w_ref, q_own_ref, q_other_ref, k_own_ref, k_other_ref,
                   bd_ref, q_ref, k_ref, vt_ref, cb_ref, u_ref, ga_ref, gb_ref, *, n_a):
    tm = xa_ref.shape[0]
    h = _rms(_pick(n_a, xa_ref, xb_ref), g_ref[...]).astype(BF16)
    p = jnp.dot(h, w_ref[:, :QKV_WIDTH], preferred_element_type=F32)
    rest = jnp.dot(h, w_ref[:, QKV_WIDTH:], preferred_element_type=F32)
    cb_ref[...] = rest[:, :D_MODEL].astype(BF16)
    u_ref[...] = (rest[:, D_MODEL:2 * D_MODEL] * rest[:, 2 * D_MODEL:3 * D_MODEL]).astype(BF16)
    ga_ref[...] = rest[:, 3 * D_MODEL:4 * D_MODEL].astype(BF16)
    gb_ref[...] = rest[:, 4 * D_MODEL:].astype(BF16)
    bd = bd_ref[...]
    lane = lax.broadcasted_iota(jnp.int32, (tm, LANES), 1)
    first_half = (lane & (AXIS_DIM - 1)) < ROT_HALF
    low_half = lane < HEAD_DIM

    def norm_rope(c, own, other):
        ms = jnp.dot((c * c).astype(BF16), bd, preferred_element_type=F32)
        partner = jnp.where(first_half, pltpu.roll(c, LANES - ROT_HALF, 1),
                            pltpu.roll(c, ROT_HALF, 1))
        return (c * own + partner * other) * lax.rsqrt(ms + EPS)

    is_aux = lane == HEAD_DIM
    zero = jnp.zeros((tm, LANES), F32)
    ones_aux = jnp.where(is_aux, 1.0, zero)

    q_own = q_own_ref[...]
    q_other = q_other_ref[...]
    for c in range(N_HEADS // 2):
        r = norm_rope(p[:, c * LANES:(c + 1) * LANES], q_own, q_other)
        even = jnp.where(low_half, r, zero)
        odd = jnp.where(low_half, pltpu.roll(r, HEAD_DIM, 1), zero)
        q_ref[:, (2 * c) * LANES:(2 * c + 1) * LANES] = even.astype(BF16)
        q_ref[:, (2 * c + 1) * LANES:(2 * c + 2) * LANES] = odd.astype(BF16)

    k_own = k_own_ref[...]
    k_other = k_other_ref[...]
    for c in range(KV_WIDTH // LANES):
        kn = norm_rope(p[:, D_MODEL + c * LANES:D_MODEL + (c + 1) * LANES], k_own, k_other)
        even = jnp.where(low_half, kn, ones_aux)
        odd = jnp.where(low_half, pltpu.roll(kn, HEAD_DIM, 1), ones_aux)
        k_ref[:, (2 * c) * LANES:(2 * c + 1) * LANES] = even.astype(BF16)
        k_ref[:, (2 * c + 1) * LANES:(2 * c + 2) * LANES] = odd.astype(BF16)
        vv = p[:, D_MODEL + KV_WIDTH + c * LANES:D_MODEL + KV_WIDTH + (c + 1) * LANES]
        even = jnp.where(low_half, vv, ones_aux)
        odd = jnp.where(low_half, pltpu.roll(vv, HEAD_DIM, 1), ones_aux)
        vt_ref[(2 * c) * LANES:(2 * c + 1) * LANES, :] = even.T.astype(BF16)
        vt_ref[(2 * c + 1) * LANES:(2 * c + 2) * LANES, :] = odd.T.astype(BF16)


def _attn_kernel(shift_ref, q_ref, k_ref, vt_ref, *refs, exact_max):
    n_cast = (len(refs) - 1) // 2
    o_ref = refs[n_cast]
    for src, dst in zip(refs[:n_cast], refs[n_cast + 1:]):
        dst[...] = src[...].astype(BF16)
    tq = q_ref.shape[0]
    is_aux_row = lax.broadcasted_iota(jnp.int32, (LANES, GROUP * tq), 0) == HEAD_DIM
    for j in range(N_KV_HEADS):
        qt = jnp.concatenate(
            [q_ref[:, h * LANES:(h + 1) * LANES].T for h in range(GROUP * j, GROUP * (j + 1))],
            axis=1)
        if exact_max:
            st = jnp.dot(k_ref[:, j * LANES:(j + 1) * LANES], qt,
                         preferred_element_type=F32)
            st = st - jnp.max(st, axis=0, keepdims=True)
            pt = jnp.exp2(st).astype(BF16)
            ot = jnp.dot(vt_ref[j * LANES:j * LANES + V_ROWS, :], pt,
                         preferred_element_type=F32)
        else:
            neg_shift = jnp.full(qt.shape, -shift_ref[0], F32).astype(BF16)
            qt = jnp.where(is_aux_row, neg_shift, qt)
            ot = jnp.zeros((V_ROWS, GROUP * tq), F32)
            for c in range(k_ref.shape[0] // KEY_CHUNK):
                rows = slice(c * KEY_CHUNK, (c + 1) * KEY_CHUNK)
                st = jnp.dot(k_ref[rows, j * LANES:(j + 1) * LANES], qt,
                             preferred_element_type=F32)
                pt = jnp.exp2(st).astype(BF16)
                ot = ot + jnp.dot(vt_ref[j * LANES:j * LANES + V_ROWS, rows], pt,
                                  preferred_element_type=F32)
        ot = ot[:HEAD_DIM] / ot[HEAD_DIM:HEAD_DIM + 1]
        for a in range(2):
            pair = jnp.concatenate([ot[:, (2 * a) * tq:(2 * a + 1) * tq],
                                    ot[:, (2 * a + 1) * tq:(2 * a + 2) * tq]], axis=0)
            c = 2 * j + a
            o_ref[:, c * LANES:(c + 1) * LANES] = pair.T.astype(BF16)


def _mix_kernel(attn_a_ref, attn_b_ref, cb_ref, u_ref, up_ref, un_ref, ga_ref, gb_ref,
                xa_ref, xb_ref, cw_ref, woa_ref, wob_ref, wo_ref, g_ref, *rest,
                n_prompt_tiles, tiles4, tiles2, with_router):
    if with_router:
        rw_ref, xo_ref, h2_ref, lg_ref = rest
    else:
        xo_ref, h2_ref = rest
    i = pl.program_id(0)
    tm = xa_ref.shape[0]
    is_prompt = i < n_prompt_tiles
    seq_start = jnp.where(is_prompt, i % tiles4 == 0, i % tiles2 == 0)
    seq_end = jnp.where(is_prompt, i % tiles4 == tiles4 - 1, i % tiles2 == tiles2 - 1)

    attn = _pick(n_prompt_tiles, attn_a_ref, attn_b_ref)
    ya = jnp.dot(attn, woa_ref[...], preferred_element_type=F32)

    u = u_ref[...].astype(F32)
    row = lax.broadcasted_iota(jnp.int32, u.shape, 0)
    prev_row = jnp.where(seq_start, 0.0, up_ref[SUBLANES - 1:SUBLANES, :].astype(F32))
    next_row = jnp.where(seq_end, 0.0, un_ref[0:1, :].astype(F32))
    u_prev = jnp.where(row == 0, prev_row, pltpu.roll(u, 1, 0))
    u_next = jnp.where(row == tm - 1, next_row, pltpu.roll(u, tm - 1, 0))
    cw = cw_ref[...]
    conv = cw[0:1, :] * u_prev + cw[1:2, :] * u + cw[2:3, :] * u_next
    yb_in = (cb_ref[...].astype(F32) * conv).astype(BF16)
    yb = jnp.dot(yb_in, wob_ref[...], preferred_element_type=F32)

    m = (jax.nn.sigmoid(ga_ref[...].astype(F32)) * ya
         + jax.nn.sigmoid(gb_ref[...].astype(F32)) * yb)
    xn = (_pick(n_prompt_tiles, xa_ref, xb_ref)
          + jnp.dot(m.astype(BF16), wo_ref[...], preferred_element_type=F32))
    xo_ref[...] = xn
    h2 = _rms(xn, g_ref[...])
    h2_ref[...] = h2.astype(h2_ref.dtype)
    if not with_router:
        return
    hi = h2.astype(BF16)
    lo = (h2 - hi.astype(F32)).astype(BF16)
    rw = rw_ref[...]
    rhi = rw.astype(BF16)
    rlo = (rw - rhi.astype(F32)).astype(BF16)
    lane = lax.broadcasted_iota(jnp.int32, rw.shape, 1)
    hi_terms = jnp.dot(hi, jnp.where(lane < N_EXPERTS, rhi, rlo), preferred_element_type=F32)
    lg_ref[...] = (hi_terms + pltpu.roll(hi_terms, LANES - N_EXPERTS, 1)
                   + jnp.dot(lo, rhi, preferred_element_type=F32))


def _ffn_kernel(h_ref, x_ref, wg_ref, wu_ref, wd_ref, o_ref, acc_ref):
    j = pl.program_id(1)

    @pl.when(j == 0)
    def _():
        acc_ref[...] = jnp.zeros_like(acc_ref)

    h = h_ref[...]
    g = jnp.dot(h, wg_ref[...], preferred_element_type=F32)
    u = jnp.dot(h, wu_ref[...], preferred_element_type=F32)
    a = (g * jax.nn.sigmoid(g) * u).astype(BF16)
    acc_ref[...] += jnp.dot(a, wd_ref[...], preferred_element_type=F32)

    @pl.when(j == pl.num_programs(1) - 1)
    def _():
        o_ref[...] = x_ref[...] + acc_ref[...]


ROUTE_E0, ROUTE_E1, ROUTE_R0, ROUTE_R1, ROUTE_W0, ROUTE_W1, ROUTE_S0, ROUTE_S1 = range(8)
STAT_COUNT, STAT_BASE, STAT_SEG = range(3)


def _lane_pick(x, lane, k):
    return jnp.sum(jnp.where(lane == k, x, 0.0), axis=-1, keepdims=True)


def _router_kernel(lg_ref, tri_ref, route_ref, fields_ref, stats_ref, count_ref, base_ref):
    i = pl.program_id(0)

    @pl.when(i == 0)
    def _():
        base_ref[...] = jnp.zeros_like(base_ref)

    lg = lg_ref[...]
    lane = lax.broadcasted_iota(jnp.int32, lg.shape, 1)
    neg = jnp.float32(-jnp.inf)
    l1 = jnp.where(lane < N_EXPERTS, lg, neg)
    m1 = jnp.max(l1, axis=-1, keepdims=True)
    i1 = jnp.min(jnp.where(l1 == m1, lane, LANES), axis=-1, keepdims=True)
    l2 = jnp.where(lane == i1, neg, l1)
    m2 = jnp.max(l2, axis=-1, keepdims=True)
    i2 = jnp.min(jnp.where(l2 == m2, lane, LANES), axis=-1, keepdims=True)
    e = jnp.exp(m2 - m1)
    w1 = 1.0 / (1.0 + e)
    w2 = e / (1.0 + e)

    hot1 = lane == i1
    hot2 = lane == i2
    onehot = jnp.where(hot1 | hot2, 1.0, 0.0)
    base = base_ref[...]
    local = jnp.dot(tri_ref[...], onehot.astype(BF16), preferred_element_type=F32)
    prefix = local + base[0:1, :]
    r1 = jnp.sum(jnp.where(hot1, prefix, 0.0), axis=-1, keepdims=True)
    r2 = jnp.sum(jnp.where(hot2, prefix, 0.0), axis=-1, keepdims=True)
    in_tile = jnp.broadcast_to(jnp.sum(onehot, axis=0, keepdims=True), base.shape)
    base_ref[...] = base + in_tile
    count_ref[...] = base + in_tile
    lane8 = lax.broadcasted_iota(jnp.int32, base.shape, 1)
    incl = in_tile
    for step in (1, 2, 4):
        incl = incl + jnp.where(lane8 >= step, pltpu.roll(incl, step, 1), 0.0)
    seg = incl - in_tile
    slot = local + seg[0:1, :]
    s1 = jnp.sum(jnp.where(hot1, slot, 0.0), axis=-1, keepdims=True)
    s2 = jnp.sum(jnp.where(hot2, slot, 0.0), axis=-1, keepdims=True)
    sub8 = lax.broadcasted_iota(jnp.int32, base.shape, 0)
    stats_ref[...] = jnp.where(sub8 == STAT_COUNT, in_tile,
                               jnp.where(sub8 == STAT_BASE, base,
                                         jnp.where(sub8 == STAT_SEG, seg, 0.0)))

    rec = jnp.where(lane == ROUTE_E0, i1.astype(F32), 0.0)
    rec = jnp.where(lane == ROUTE_E1, i2.astype(F32), rec)
    rec = jnp.where(lane == ROUTE_R0, r1, rec)
    rec = jnp.where(lane == ROUTE_R1, r2, rec)
    rec = jnp.where(lane == ROUTE_W0, w1, rec)
    rec = jnp.where(lane == ROUTE_W1, w2, rec)
    rec = jnp.where(lane == ROUTE_S0, s1, rec)
    rec = jnp.where(lane == ROUTE_S1, s2, rec)
    route_ref[...] = rec
    fields_ref[...] = rec.T[:fields_ref.shape[0], :]


def _row_copy(src, src_row, dst, dst_row, sem):
    return pltpu.make_async_copy(
        src.at[pl.ds(pl.multiple_of(src_row * SUBLANES, SUBLANES), SUBLANES)],
        dst.at[pl.ds(pl.multiple_of(dst_row * SUBLANES, SUBLANES), SUBLANES)], sem)


def _dispatch_kernel(pad_start_ref, pad_count_ref, tile_valid_ref, tab_ref, fields_ref, h_ref,
                     xs_ref, zero_ref, xc_ref, sem, zsem):
    tm = h_ref.shape[0]
    tile = tm * SUBLANES
    n_tiles = xs_ref.shape[0] // tile

    @pl.when(pl.program_id(0) == 0)
    def _():
        zero_ref[...] = jnp.zeros_like(zero_ref)
        for e in range(N_EXPERTS):
            start = pad_start_ref[e]
            count = pad_count_ref[e]

            def zero_issue(r, carry):
                _row_copy(zero_ref, 0, xs_ref, start + r, zsem).start()
                return carry

            def zero_wait(r, carry):
                _row_copy(zero_ref, 0, xs_ref, 0, zsem).wait()
                return carry

            lax.fori_loop(0, count, zero_issue, 0)
            lax.fori_loop(0, count, zero_wait, 0)
        for g in range(n_tiles - N_EXPERTS, n_tiles):
            @pl.when(tile_valid_ref[g] == 0)
            def _():
                fill = pltpu.make_async_copy(zero_ref, xs_ref.at[pl.ds(g * tile, tile)], zsem)
                fill.start()
                fill.wait()

    i = pl.program_id(0)
    buf = i % 2
    xc = xc_ref.at[buf]

    def drain(b):
        pltpu.make_async_copy(xc_ref.at[b], xs_ref.at[pl.ds(0, 2 * tile)], sem.at[b]).wait()

    @pl.when(i >= 2)
    def _():
        drain(buf)

    grouped_row = lax.broadcasted_iota(jnp.int32, (2 * tm, tm), 0)
    s0 = fields_ref[ROUTE_S0:ROUTE_S0 + 1, :].astype(jnp.int32)
    s1 = fields_ref[ROUTE_S1:ROUTE_S1 + 1, :].astype(jnp.int32)
    select = jnp.where((grouped_row == s0) | (grouped_row == s1), 1.0, 0.0).astype(BF16)
    _store_row_tiles(xc, jnp.dot(select, h_ref[...], preferred_element_type=F32))

    for e in range(N_EXPERTS):
        count = tab_ref[0, 0, e]
        src0 = tab_ref[0, 0, N_EXPERTS + e]
        dst0 = tab_ref[0, 0, 2 * N_EXPERTS + e]
        for bit in reversed(range(tm.bit_length())):
            size = 1 << bit

            @pl.when((count & size) != 0)
            def _():
                done = count & ~(2 * size - 1)
                src = pl.multiple_of((src0 + done) * SUBLANES, SUBLANES)
                dst = pl.multiple_of((dst0 + done) * SUBLANES, SUBLANES)
                pltpu.make_async_copy(xc.at[pl.ds(src, size * SUBLANES)],
                                      xs_ref.at[pl.ds(dst, size * SUBLANES)],
                                      sem.at[buf]).start()

    last = pl.num_programs(0) - 1

    @pl.when((i == last) & (i >= 1))
    def _():
        drain(1 - buf)

    @pl.when(i == last)
    def _():
        drain(buf)


def _moe_ffn_kernel(te_ref, tv_ref, x_ref, wg_ref, wu_ref, wd_ref, y_ref, acc_ref):
    del te_ref
    g_idx = pl.program_id(0)
    j = pl.program_id(1)
    valid = tv_ref[g_idx] > 0

    @pl.when(j == 0)
    def _():
        acc_ref[...] = jnp.zeros_like(acc_ref)

    @pl.when(valid)
    def _():
        h = _load_row_tiles(x_ref).astype(BF16)
        g = jnp.dot(h, wg_ref[0], preferred_element_type=F32)
        u = jnp.dot(h, wu_ref[0], preferred_element_type=F32)
        a = (g * jax.nn.sigmoid(g) * u).astype(BF16)
        acc_ref[...] += jnp.dot(a, wd_ref[0], preferred_element_type=F32)

    @pl.when(j == pl.num_programs(1) - 1)
    def _():
        _store_row_tiles(y_ref, acc_ref[...])


def _combine_kernel(pos_ref, nxt_ref, x_ref, route_ref, fg_ref, y_hbm, oa_ref, ob_ref,
                    ybuf, sem, *, n_a):
    i = pl.program_id(0)
    tc = x_ref.shape[0]
    cur = i % 2

    def issue_tile(p_ref, par):
        def issue(blk, carry):
            for k in range(ROW_UNROLL):
                r = blk * ROW_UNROLL + k
                _row_copy(y_hbm, p_ref[0, 0, r], ybuf.at[par, 0], r,
                          sem.at[par]).start(priority=0)
                _row_copy(y_hbm, p_ref[0, 0, tc + r], ybuf.at[par, 1], r,
                          sem.at[par]).start(priority=1)
            return carry
        lax.fori_loop(0, tc // ROW_UNROLL, issue, 0)

    @pl.when(i == 0)
    def _():
        issue_tile(pos_ref, 0)

    @pl.when(i + 1 < pl.num_programs(0))
    def _():
        issue_tile(nxt_ref, 1 - cur)

    for slot in range(2):
        pltpu.make_async_copy(y_hbm.at[pl.ds(0, tc * SUBLANES)], ybuf.at[cur, slot],
                              sem.at[cur]).wait()
    route = route_ref[...]
    lane = lax.broadcasted_iota(jnp.int32, route.shape, 1)
    w0 = _lane_pick(route, lane, ROUTE_W0)
    w1 = _lane_pick(route, lane, ROUTE_W1)
    y0 = _load_row_tiles(ybuf.at[cur, 0])
    y1 = _load_row_tiles(ybuf.at[cur, 1])
    out = _rms(x_ref[...] + (w0 * y0 + w1 * y1), fg_ref[...])
    in_a = i < n_a

    @pl.when(in_a)
    def _():
        oa_ref[...] = out

    @pl.when(jnp.logical_not(in_a))
    def _():
        ob_ref[...] = out


def _rope_tables(max_seq):
    t = jnp.arange(max_seq, dtype=jnp.int32)
    row = (t // GRID_W).astype(F32)
    col = (t % GRID_W).astype(F32)
    inv = 1.0 / (ROPE_THETA ** (jnp.arange(0, AXIS_DIM, 2, dtype=F32) / AXIS_DIM))
    ar = row[:, None] * inv[None, :]
    ac = col[:, None] * inv[None, :]
    cos64 = jnp.concatenate([jnp.cos(ar), jnp.cos(ar), jnp.cos(ac), jnp.cos(ac)], axis=-1)
    sin64 = jnp.concatenate([-jnp.sin(ar), jnp.sin(ar), -jnp.sin(ac), jnp.sin(ac)], axis=-1)
    return cos64, sin64


def _gain_rope_tables(cos64, sin64, gain):
    half = AXIS_DIM // 2
    partner_gain = gain.reshape(-1, 2, half)[:, ::-1, :].reshape(-1)
    return (jnp.tile(cos64 * gain[None, :], (1, 2)),
            jnp.tile(sin64 * partner_gain[None, :], (1, 2)))


def _two_part_specs(block, n_a, parts):
    off = 0 if parts[0] is parts[1] else n_a
    return [pl.BlockSpec(block, lambda i: (jnp.minimum(i, n_a - 1), 0)),
            pl.BlockSpec(block, lambda i: (jnp.maximum(i, n_a) - off, 0))]


def _in_proj(xparts, t, n_a, gain, w, q_tables, k_tables, bd, pos_map):
    row = lambda i: (i, 0)
    const = lambda i: (0, 0)
    table = pl.BlockSpec((TM, LANES), pos_map)
    act = pl.BlockSpec((TM, D_MODEL), row)
    act_shape = jax.ShapeDtypeStruct((t, D_MODEL), BF16)
    return pl.pallas_call(
        functools.partial(_inproj_kernel, n_a=n_a),
        grid=(t // TM,),
        in_specs=_two_part_specs((TM, D_MODEL), n_a, xparts) + [
                  pl.BlockSpec((1, D_MODEL), const),
                  pl.BlockSpec((D_MODEL, QKV_WIDTH + REST_WIDTH), const,
                               pipeline_mode=pl.Buffered(1)),
                  table, table, table, table,
                  pl.BlockSpec((LANES, LANES), const)],
        out_specs=[pl.BlockSpec((TM, Q_EXP_WIDTH), row),
                   pl.BlockSpec((TM, KV_EXP_WIDTH), row),
                   pl.BlockSpec((KV_EXP_WIDTH, TM), lambda i: (0, i)),
                   act, act, act, act],
        out_shape=[jax.ShapeDtypeStruct((t, Q_EXP_WIDTH), BF16),
                   jax.ShapeDtypeStruct((t, KV_EXP_WIDTH), BF16),
                   jax.ShapeDtypeStruct((KV_EXP_WIDTH, t), BF16),
                   act_shape, act_shape, act_shape, act_shape],
        compiler_params=_params(("parallel",)),
        name="in_proj",
    )(*xparts, gain, w, *q_tables, *k_tables, bd)


def _attention(shift, q, k, vt, casts, *, row0, n_seq, seq, tq, exact_max):
    nq = seq // tq
    q0 = row0 // tq
    s0 = row0 // seq
    n_steps = n_seq * nq
    step_blk = lambda b, i, km: (b * nq + i, 0, 0)
    rides = [(w.shape[0] * w.shape[1]) % (n_steps * BF16_SUBLANES) == 0 for w in casts]
    views = [w.reshape(n_steps, w.shape[0] * w.shape[1] // n_steps, w.shape[2])
             for w, ok in zip(casts, rides) if ok]
    cast_specs = [pl.BlockSpec((1,) + v.shape[1:], step_blk) for v in views]
    outs = pl.pallas_call(
        functools.partial(_attn_kernel, exact_max=exact_max),
        grid_spec=pltpu.PrefetchScalarGridSpec(
            num_scalar_prefetch=1,
            grid=(n_seq, nq),
            in_specs=[pl.BlockSpec((tq, Q_EXP_WIDTH), lambda b, i, km: (q0 + b * nq + i, 0)),
                      pl.BlockSpec((seq, KV_EXP_WIDTH), lambda b, i, km: (s0 + b, 0)),
                      pl.BlockSpec((KV_EXP_WIDTH, seq), lambda b, i, km: (0, s0 + b))]
                     + cast_specs,
            out_specs=[pl.BlockSpec((tq, D_MODEL), lambda b, i, km: (b * nq + i, 0))]
                      + cast_specs),
        out_shape=[jax.ShapeDtypeStruct((n_seq * seq, D_MODEL), BF16)]
                  + [jax.ShapeDtypeStruct(v.shape, BF16) for v in views],
        compiler_params=_params(("parallel", "parallel")),
        name=f"attention_s{seq}" + ("_exact" if exact_max else ""),
    )(shift, q, k, vt, *views)
    carried = iter(outs[1:])
    return outs[0], [next(carried).reshape(w.shape) if ok else w.astype(BF16)
                     for w, ok in zip(casts, rides)]


def _plan_casts(jobs, depth, call_weights=(2.0, 1.0)):
    plan = []
    pending = sorted(jobs, key=lambda job: job[2])
    for l in range(depth):
        budget = sum(job[1].size for job in pending) / (depth - l)
        take, used = [], 0
        for job in pending:
            if job[2] == l or used + job[1].size <= budget:
                take.append(job)
                used += job[1].size
        pending = [job for job in pending if all(job is not t for t in take)]
        calls = [[] for _ in call_weights]
        loads = [0.0] * len(call_weights)
        for job in sorted(take, key=lambda job: -job[1].size):
            c = min(range(len(calls)), key=lambda c: loads[c] / call_weights[c])
            calls[c].append(job)
            loads[c] += job[1].size
        plan.append(calls)
    return plan


def _mix(attn_parts, cb, u, ga, gb, xparts, cw, woa, wob, wo, gain, rw, *, n_prompt, seq_p, seq_s):
    t = cb.shape[0]
    n_a = n_prompt // TM
    row = lambda i: (i, 0)
    const = lambda i: (0, 0)
    sub = TM // SUBLANES
    last = t // SUBLANES - 1
    act = pl.BlockSpec((TM, D_MODEL), row)
    wspec = pl.BlockSpec((D_MODEL, D_MODEL), const)
    with_router = rw is not None
    kern = functools.partial(_mix_kernel, n_prompt_tiles=n_prompt // TM,
                             tiles4=seq_p // TM, tiles2=seq_s // TM, with_router=with_router)
    in_specs = (_two_part_specs((TM, D_MODEL), n_a, attn_parts) + [
                act, act,
                pl.BlockSpec((SUBLANES, D_MODEL), lambda i: (jnp.maximum(i * sub - 1, 0), 0)),
                pl.BlockSpec((SUBLANES, D_MODEL),
                             lambda i: (jnp.minimum((i + 1) * sub, last), 0)),
                act, act]
                + _two_part_specs((TM, D_MODEL), n_a, xparts) + [
                pl.BlockSpec((3, D_MODEL), const),
                wspec, wspec, wspec,
                pl.BlockSpec((1, D_MODEL), const)])
    args = [*attn_parts, cb, u, u, u, ga, gb, *xparts, cw, woa, wob, wo, gain]
    out_specs = [act, act]
    out_shape = [jax.ShapeDtypeStruct((t, D_MODEL), F32),
                 jax.ShapeDtypeStruct((t, D_MODEL), BF16)]
    if with_router:
        in_specs.append(pl.BlockSpec((D_MODEL, LANES), const))
        args.append(rw)
        out_specs.append(pl.BlockSpec((TM, LANES), row))
        out_shape.append(jax.ShapeDtypeStruct((t, LANES), F32))
    return pl.pallas_call(
        kern,
        grid=(t // TM,),
        in_specs=in_specs,
        out_specs=out_specs,
        out_shape=out_shape,
        compiler_params=_params(("parallel",)),
        name="mix_proj_router" if with_router else "mix_proj",
    )(*args)


def _ffn(h, x, wg, wu, wd):
    t = x.shape[0]
    row = lambda i, j: (i, 0)
    return pl.pallas_call(
        _ffn_kernel,
        grid=(t // TM_FFN, D_FF // TF),
        in_specs=[pl.BlockSpec((TM_FFN, D_MODEL), row),
                  pl.BlockSpec((TM_FFN, D_MODEL), row),
                  pl.BlockSpec((D_MODEL, TF), lambda i, j: (0, j), pipeline_mode=pl.Buffered(1)),
                  pl.BlockSpec((D_MODEL, TF), lambda i, j: (0, j), pipeline_mode=pl.Buffered(1)),
                  pl.BlockSpec((TF, D_MODEL), lambda i, j: (j, 0), pipeline_mode=pl.Buffered(1))],
        out_specs=pl.BlockSpec((TM_FFN, D_MODEL), row),
        out_shape=jax.ShapeDtypeStruct((t, D_MODEL), F32),
        scratch_shapes=[pltpu.VMEM((TM_FFN, D_MODEL), F32)],
        compiler_params=_params(("parallel", "arbitrary")),
        name="ffn_dense",
    )(h, x, wg, wu, wd)


def _router(logits):
    t = logits.shape[0]
    row = lambda i: (i, 0)
    const = lambda i: (0, 0)
    r = jnp.arange(TR)
    tri = (r[None, :] < r[:, None]).astype(BF16)
    return pl.pallas_call(
        _router_kernel,
        grid=(t // TR,),
        in_specs=[pl.BlockSpec((TR, LANES), row),
                  pl.BlockSpec((TR, TR), const)],
        out_specs=[pl.BlockSpec((TR, LANES), row),
                   pl.BlockSpec((SUBLANES, TR), lambda i: (0, i)),
                   pl.BlockSpec((SUBLANES, LANES), row),
                   pl.BlockSpec((SUBLANES, LANES), const)],
        out_shape=[jax.ShapeDtypeStruct((t, LANES), F32),
                   jax.ShapeDtypeStruct((SUBLANES, t), F32),
                   jax.ShapeDtypeStruct((t // TR * SUBLANES, LANES), F32),
                   jax.ShapeDtypeStruct((SUBLANES, LANES), F32)],
        scratch_shapes=[pltpu.VMEM((SUBLANES, LANES), F32)],
        compiler_params=_params(("arbitrary",)),
        name="router",
    )(logits, tri)


def _route_plan(fields, stats, counts, t):
    e0 = fields[ROUTE_E0].astype(jnp.int32)
    e1 = fields[ROUTE_E1].astype(jnp.int32)
    r0 = fields[ROUTE_R0].astype(jnp.int32)
    r1 = fields[ROUTE_R1].astype(jnp.int32)
    cnt = counts[0, :N_EXPERTS].astype(jnp.int32)
    padded = ((cnt + TM_MOE - 1) // TM_MOE) * TM_MOE
    ends = jnp.cumsum(padded)
    starts = ends - padded
    experts = jnp.arange(N_EXPERTS, dtype=jnp.int32)
    pos0 = jnp.sum(jnp.where(e0[:, None] == experts[None, :], starts[None, :], 0), axis=1) + r0
    pos1 = jnp.sum(jnp.where(e1[:, None] == experts[None, :], starts[None, :], 0), axis=1) + r1
    n_tok_tiles = t // TM_MOE
    pos = jnp.concatenate([pos0.reshape(n_tok_tiles, 1, TM_MOE),
                           pos1.reshape(n_tok_tiles, 1, TM_MOE)], axis=2)
    n_tiles = 2 * t // TM_MOE + N_EXPERTS
    tile_start = jnp.arange(n_tiles, dtype=jnp.int32) * TM_MOE
    tile_valid = (tile_start < ends[-1]).astype(jnp.int32)
    tile_expert = jnp.sum((tile_start[:, None] >= ends[None, :]).astype(jnp.int32), axis=1)
    tile_expert = jnp.minimum(tile_expert, N_EXPERTS - 1)
    pad = (starts + cnt, padded - cnt)
    stats = stats.reshape(n_tok_tiles, SUBLANES, LANES)[:, :, :N_EXPERTS].astype(jnp.int32)
    table = jnp.concatenate([stats[:, STAT_COUNT], stats[:, STAT_SEG],
                             stats[:, STAT_BASE] + starts[None, :]], axis=1)
    return pos, pad, table[:, None, :], tile_expert, tile_valid


ROW_TILE = (TM_MOE * SUBLANES, LANES)


def _dispatch(pad_start, pad_count, tile_valid, table, fields, h):
    t = h.shape[0]
    n_rows = tile_valid.shape[0] * TM_MOE
    return pl.pallas_call(
        _dispatch_kernel,
        grid_spec=pltpu.PrefetchScalarGridSpec(
            num_scalar_prefetch=3,
            grid=(t // TM_MOE,),
            in_specs=[pl.BlockSpec((1, 1, 3 * N_EXPERTS), lambda i, ps, pc, tv: (i, 0, 0),
                                   memory_space=pltpu.SMEM),
                      pl.BlockSpec((SUBLANES, TM_MOE), lambda i, ps, pc, tv: (0, i)),
                      pl.BlockSpec((TM_MOE, D_MODEL), lambda i, ps, pc, tv: (i, 0))],
            out_specs=pl.BlockSpec(memory_space=pl.ANY),
            scratch_shapes=[pltpu.VMEM(ROW_TILE, F32),
                            pltpu.VMEM((2, 2 * ROW_TILE[0], LANES), F32),
                            pltpu.SemaphoreType.DMA((2,)),
                            pltpu.SemaphoreType.DMA(())]),
        out_shape=jax.ShapeDtypeStruct((n_rows * SUBLANES, LANES), F32),
        compiler_params=_params(("arbitrary",)),
        name="moe_dispatch",
    )(pad_start, pad_count, tile_valid, table, fields, h)


def _moe_ffn(tile_expert, tile_valid, xs, wg, wu, wd):
    n_rows = xs.shape[0] // SUBLANES
    row = lambda g, j, te, tv: (g, 0)
    return pl.pallas_call(
        _moe_ffn_kernel,
        grid_spec=pltpu.PrefetchScalarGridSpec(
            num_scalar_prefetch=2,
            grid=(n_rows // TM_MOE, D_FF // TF_MOE),
            in_specs=[pl.BlockSpec(ROW_TILE, row),
                      pl.BlockSpec((1, D_MODEL, TF_MOE),
                                   lambda g, j, te, tv: (te[g], 0, j * tv[g])),
                      pl.BlockSpec((1, D_MODEL, TF_MOE),
                                   lambda g, j, te, tv: (te[g], 0, j * tv[g])),
                      pl.BlockSpec((1, TF_MOE, D_MODEL),
                                   lambda g, j, te, tv: (te[g], j * tv[g], 0))],
            out_specs=pl.BlockSpec(ROW_TILE, row),
            scratch_shapes=[pltpu.VMEM((TM_MOE, D_MODEL), F32)]),
        out_shape=jax.ShapeDtypeStruct(xs.shape, F32),
        compiler_params=_params(("parallel", "arbitrary")),
        name="moe_ffn",
    )(tile_expert, tile_valid, xs, wg, wu, wd)


def _combine(pos, x, route, fgain, ys, n_first):
    t = x.shape[0]
    n_a = n_first // TM_MOE
    row = lambda i: (i, 0)
    blk = (TM_MOE, D_MODEL)
    last = t // TM_MOE - 1
    pos_blk = (1, 1, 2 * TM_MOE)
    return pl.pallas_call(
        functools.partial(_combine_kernel, n_a=n_a),
        grid=(t // TM_MOE,),
        in_specs=[pl.BlockSpec(pos_blk, lambda i: (i, 0, 0), memory_space=pltpu.SMEM),
                  pl.BlockSpec(pos_blk, lambda i: (jnp.minimum(i + 1, last), 0, 0),
                               memory_space=pltpu.SMEM),
                  pl.BlockSpec(blk, row),
                  pl.BlockSpec((TM_MOE, LANES), row),
                  pl.BlockSpec((1, D_MODEL), lambda i: (0, 0)),
                  pl.BlockSpec(memory_space=pl.ANY)],
        out_specs=[pl.BlockSpec(blk, lambda i: (jnp.minimum(i, n_a - 1), 0)),
                   pl.BlockSpec(blk, lambda i: (jnp.maximum(i - n_a, 0), 0))],
        out_shape=[jax.ShapeDtypeStruct((n_first, D_MODEL), F32),
                   jax.ShapeDtypeStruct((t - n_first, D_MODEL), F32)],
        scratch_shapes=[pltpu.VMEM((2, 2) + ROW_TILE, F32),
                        pltpu.SemaphoreType.DMA((2,))],
        compiler_params=_params(("arbitrary",)),
        name="moe_combine",
    )(pos, pos, x, route, fgain, ys)


def kernel(x_prompt, x_sample, norm_mix, w_in, q_norm, k_norm, conv_w, w_oa, w_ob, w_o, norm_ffn,
           ffn_w_gate, ffn_w_up, ffn_w_down, router_w, moe_w_gate, moe_w_up, moe_w_down, final_norm):
    bp, sp, _ = x_prompt.shape
    bs, ss, _ = x_sample.shape
    n_prompt = bp * sp
    n_sample = bs * ss
    depth = norm_mix.shape[0]
    assert sp % TM == 0 and ss % TM == 0 and TM == TM_FFN == TM_MOE
    assert depth == 2 and ffn_w_gate.shape[0] == moe_w_gate.shape[0] == 1

    t_all = n_prompt + n_sample
    xparts = (x_prompt.reshape(n_prompt, D_MODEL), x_sample.reshape(n_sample, D_MODEL))

    cos, sin = _rope_tables(max(sp, ss))
    n_prompt_tiles, tiles4, tiles2 = n_prompt // TM, sp // TM, ss // TM
    pos_map = lambda i: (jnp.where(i < n_prompt_tiles, i % tiles4, i % tiles2), 0)
    idx = jnp.arange(LANES)
    bd = jnp.where(idx[:, None] // HEAD_DIM == idx[None, :] // HEAD_DIM,
                   1.0 / HEAD_DIM, 0.0).astype(BF16)

    jobs = []
    for l in range(depth):
        if l > 0:
            jobs.append((("w_in", l), w_in[l][None], l - 1))
        jobs += [((name, l), w[l][None], l)
                 for name, w in (("w_oa", w_oa), ("w_ob", w_ob), ("w_o", w_o))]
        named = ((("ffn_gate", ffn_w_gate), ("ffn_up", ffn_w_up), ("ffn_down", ffn_w_down))
                 if l % 2 == 0 else
                 (("moe_gate", moe_w_gate), ("moe_up", moe_w_up), ("moe_down", moe_w_down)))
        jobs += [((name, l), w[l // 2] if l % 2 else w[l // 2][None], l) for name, w in named]
    cast_plan = _plan_casts(jobs, depth)
    bf16_w = {("w_in", 0): w_in[0].astype(BF16)}

    for l in range(depth):
        gain = norm_mix[l][None, :]
        q, k, vt, cb, u, ga, gb = _in_proj(
            xparts, t_all, n_prompt_tiles, gain, bf16_w[("w_in", l)],
            _gain_rope_tables(cos, sin, q_norm[l] * Q_SCALE),
            _gain_rope_tables(cos, sin, k_norm[l]), bd, pos_map)
        shift = (Q_SCALE * HEAD_DIM * jnp.max(jnp.abs(q_norm[l]))
                 * jnp.max(jnp.abs(k_norm[l]))).reshape(1)
        bound_ok = 2.0 * shift[0] < MAX_SHIFT_GAP

        jobs_p, jobs_s = cast_plan[l]

        def attend(exact_max, tq_p, tq_s):
            def run(shift, q, k, vt, w_p, w_s):
                attn_p, done_p = _attention(shift, q, k, vt, w_p, row0=0, n_seq=bp, seq=sp,
                                            tq=tq_p, exact_max=exact_max)
                attn_s, done_s = _attention(shift, q, k, vt, w_s, row0=n_prompt, n_seq=bs,
                                            seq=ss, tq=tq_s, exact_max=exact_max)
                return (attn_p, attn_s), done_p + done_s
            return run

        attn_parts, done = lax.cond(
            bound_ok, attend(False, 256, 256), attend(True, 128, 256), shift, q, k, vt,
            [job[1] for job in jobs_p], [job[1] for job in jobs_s])
        for job, w16 in zip(jobs_p + jobs_s, done):
            bf16_w[job[0]] = w16 if job[0][0].startswith("moe") else w16[0]
        j = l // 2
        is_moe = l % 2 == 1
        rw = (jnp.pad(jnp.tile(router_w[j], (1, 2)), ((0, 0), (0, LANES - 2 * N_EXPERTS)))
              if is_moe else None)
        outs = _mix(attn_parts, cb, u, ga, gb, xparts, conv_w[l],
                    bf16_w[("w_oa", l)], bf16_w[("w_ob", l)], bf16_w[("w_o", l)],
                    norm_ffn[l][None, :], rw, n_prompt=n_prompt, seq_p=sp, seq_s=ss)
        if not is_moe:
            x, h2 = outs
            x = _ffn(h2, x, bf16_w[("ffn_gate", l)], bf16_w[("ffn_up", l)],
                     bf16_w[("ffn_down", l)])
            xparts = (x, x)
        else:
            x, h2, logits = outs
            route, fields, stats, counts = _router(logits)
            pos, pad, table, tile_expert, tile_valid = _route_plan(fields, stats, counts, t_all)
            xs = _dispatch(*pad, tile_valid, table, fields, h2)
            ys = _moe_ffn(tile_expert, tile_valid, xs, bf16_w[("moe_gate", l)],
                          bf16_w[("moe_up", l)], bf16_w[("moe_down", l)])
            xparts = _combine(pos, x, route, final_norm[None, :], ys, n_prompt)

    return (xparts[0].reshape(bp, sp, D_MODEL), xparts[1].reshape(bs, ss, D_MODEL))
```
